```python
import jax
import jax.numpy as jnp
from jax import lax
import numpy as np

D_MODEL = 2048
BATCH = 8
SEQ = 2048
DEPTH = 1
DEC_BATCH = 8
DEC_SEQ = 64
PAST_LEN = 1024

CHUNK = 64
Q_BLOCK = 128
EPS = 1e-6
ROPE_THETA = 10000.0

MLA_HEADS = 8
MLA_NOPE = 128
MLA_ROPE = 64
MLA_V = 128
Q_LORA = 512
KV_LORA = 512
MLA_WIDTH = MLA_HEADS * MLA_V

RET_HEADS = 4
RET_QK = 256
RET_V = 256
RET_WIDTH = RET_HEADS * RET_V

MIX_WIDTH = MLA_WIDTH + RET_WIDTH
IN_SIZES = (Q_LORA, KV_LORA, MLA_ROPE, RET_HEADS * RET_QK, RET_HEADS * RET_QK, RET_WIDTH, RET_WIDTH)
IN_COLS = Q_LORA + KV_LORA + MLA_ROPE + 2 * RET_HEADS * RET_QK + 2 * RET_WIDTH

N_EXPERTS = 32
TOP_K = 4
D_FF = 2048
SWIGLU_LIMIT = 7.0
SWIGLU_ALPHA = 1.702
MOE_BLOCK = 128

kernel_name = 'hybrid_mla_retention_moe_stream_step'


def rmsnorm(x, g):
    xf = x.astype(jnp.float32)
    y = xf * lax.rsqrt(jnp.mean(xf * xf, axis=-1, keepdims=True) + EPS)
    return (y * g.astype(jnp.float32)).astype(x.dtype)


def rope(x, pos):
    d = x.shape[-1]
    half = d // 2
    inv_freq = 1.0 / (ROPE_THETA ** (jnp.arange(half, dtype=jnp.float32) * (2.0 / d)))
    ang = pos.astype(jnp.float32)[:, None] * inv_freq[None, :]
    cos = jnp.cos(ang)[None, :, None, :]
    sin = jnp.sin(ang)[None, :, None, :]
    xf = x.astype(jnp.float32)
    x1, x2 = xf[..., :half], xf[..., half:]
    return jnp.concatenate([x1 * cos - x2 * sin, x1 * sin + x2 * cos], axis=-1).astype(x.dtype)


def ada_modulation(c, w_ada, b_ada):
    mod = jax.nn.silu(c) @ w_ada + b_ada
    return [m[:, None, :] for m in jnp.split(mod, 6, axis=-1)]


def combined_projection(h, pos, w_in, g_q_lat, g_kv_lat):
    B, S, _ = h.shape
    z = h @ w_in
    bounds = np.cumsum(IN_SIZES)[:-1].tolist()
    q_lat, kv_lat, k_pe, r_q, r_k, r_v, r_g = jnp.split(z, bounds, axis=-1)
    q_lat = rmsnorm(q_lat, g_q_lat)
    kv_lat = rmsnorm(kv_lat, g_kv_lat)
    k_pe = rope(k_pe[:, :, None, :], pos)[:, :, 0, :]
    r_q = rope(r_q.reshape(B, S, RET_HEADS, RET_QK), pos)
    r_k = rope(r_k.reshape(B, S, RET_HEADS, RET_QK), pos) * (RET_QK ** -0.5)
    r_v = r_v.reshape(B, S, RET_HEADS, RET_V)
    return q_lat, kv_lat, k_pe, r_q, r_k, r_v, r_g


def mla_queries(q_lat, pos, w_uq):
    B, S, _ = q_lat.shape
    q = (q_lat @ w_uq).reshape(B, S, MLA_HEADS, MLA_NOPE + MLA_ROPE)
    return q[..., :MLA_NOPE], rope(q[..., MLA_NOPE:], pos)


def mla_keys(kv_lat, w_ukv):
    B, T, _ = kv_lat.shape
    kv = (kv_lat @ w_ukv).reshape(B, T, MLA_HEADS, MLA_NOPE + MLA_V)
    return kv[..., :MLA_NOPE], kv[..., MLA_NOPE:]


def mla_scores(q_nope, q_pe, k_nope, k_pe):
    s = jnp.einsum('bqhn,bkhn->bhqk', q_nope, k_nope) + jnp.einsum('bqhr,bkr->bhqk', q_pe, k_pe)
    return s.astype(jnp.float32) * ((MLA_NOPE + MLA_ROPE) ** -0.5)


def mla_attention_chunk_causal(q_nope, q_pe, k_nope, k_pe, v):
    B, S, H, _ = q_nope.shape
    nb = S // Q_BLOCK
    qn = q_nope.reshape(B, nb, Q_BLOCK, H, MLA_NOPE).transpose(1, 0, 2, 3, 4)
    qp = q_pe.reshape(B, nb, Q_BLOCK, H, MLA_ROPE).transpose(1, 0, 2, 3, 4)
    key_chunk = jnp.arange(S) // CHUNK

    def block(args):
        qn_b, qp_b, b = args
        s = mla_scores(qn_b, qp_b, k_nope, k_pe)
        q_chunk = (b * Q_BLOCK + jnp.arange(Q_BLOCK)) // CHUNK
        visible = key_chunk[None, :] <= q_chunk[:, None]
        p = jax.nn.softmax(jnp.where(visible[None, None], s, -1e30), axis=-1).astype(v.dtype)
        return jnp.einsum('bhqk,bkhv->bqhv', p, v)

    out = lax.map(block, (qn, qp, jnp.arange(nb)))
    return out.transpose(1, 0, 2, 3, 4).reshape(B, S, H * MLA_V)


def mla_attention_full(q_nope, q_pe, k_nope, k_pe, v):
    B, Q, H, _ = q_nope.shape
    p = jax.nn.softmax(mla_scores(q_nope, q_pe, k_nope, k_pe), axis=-1).astype(v.dtype)
    return jnp.einsum('bhqk,bkhv->bqhv', p, v).reshape(B, Q, H * MLA_V)


def retention_chunkwise(q, k, v, state0, chunk):
    B, S, H, dk = q.shape
    dv = v.shape[-1]
    n = S // chunk
    f32 = jnp.float32
    log_g = jnp.log1p(-jnp.exp2(-5.0 - jnp.arange(H, dtype=f32)))
    idx = jnp.arange(chunk, dtype=f32)
    diff = idx[:, None] - idx[None, :]
    decay_mask = jnp.where(diff[None] >= 0, jnp.exp(jnp.maximum(diff, 0.0)[None] * log_g[:, None, None]), 0.0)
    q_decay = jnp.exp((idx[None, :] + 1.0) * log_g[:, None])
    k_decay = jnp.exp((chunk - 1.0 - idx[None, :]) * log_g[:, None])
    chunk_decay = jnp.exp(chunk * log_g)

    def to_chunks(t):
        return t.astype(f32).reshape(B, n, chunk, H, t.shape[-1]).transpose(1, 0, 3, 2, 4)

    def step(state, inp):
        qb, kb, vb = inp
        attn = jnp.einsum('bhik,bhjk->bhij', qb, kb) * decay_mask[None]
        inner = jnp.einsum('bhij,bhjv->bhiv', attn, vb)
        cross = jnp.einsum('bhik,bhkv->bhiv', qb, state) * q_decay[None, :, :, None]
        new_state = state * chunk_decay[None, :, None, None] + jnp.einsum('bhjk,bhjv->bhkv', kb * k_decay[None, :, :, None], vb)
        return new_state, inner + cross

    state, out = lax.scan(step, state0.astype(f32), (to_chunks(q), to_chunks(k), to_chunks(v)))
    out = out.transpose(1, 0, 3, 2, 4).reshape(B, S, H, dv)
    return out, state


def mixer_output(attn, ret, r_g, g_ret, w_out):
    B, S = attn.shape[:2]
    rf = ret
    mu = jnp.mean(rf, axis=-1, keepdims=True)
    var = jnp.mean(jnp.square(rf - mu), axis=-1, keepdims=True)
    rn = ((rf - mu) * lax.rsqrt(var + EPS)).reshape(B, S, RET_WIDTH) * g_ret.astype(jnp.float32)
    rn = rn.astype(attn.dtype) * jax.nn.silu(r_g)
    return jnp.concatenate([attn, rn], axis=-1) @ w_out


def moe_ffn(h, w_router, b_router, w_gu, b_gu, w_down, b_down):
    B, S, D = h.shape
    xt = h.reshape(-1, D)
    N = xt.shape[0]
    NK = N * TOP_K
    logits = (xt @ w_router + b_router).astype(jnp.float32)
    top_val, top_idx = lax.top_k(logits, TOP_K)
    gates = jax.nn.softmax(top_val, axis=-1)
    flat_e = top_idx.reshape(-1)
    order = jnp.argsort(flat_e)
    sorted_e = flat_e[order]
    tok = (order // TOP_K).astype(jnp.int32)
    counts = jnp.bincount(flat_e, length=N_EXPERTS)
    padded = (counts + MOE_BLOCK - 1) // MOE_BLOCK * MOE_BLOCK
    pad_end = jnp.cumsum(padded)
    pad_start = pad_end - padded
    start = jnp.cumsum(counts) - counts
    dest = pad_start[sorted_e] + jnp.arange(NK) - start[sorted_e]
    n_blocks = (NK + N_EXPERTS * (MOE_BLOCK - 1)) // MOE_BLOCK + 1
    rows = n_blocks * MOE_BLOCK
    buf_tok = jnp.zeros((rows,), jnp.int32).at[dest].set(tok)
    block_e = jnp.minimum(jnp.searchsorted(pad_end, jnp.arange(n_blocks) * MOE_BLOCK, side='right'), N_EXPERTS - 1)
    xb = xt[buf_tok].reshape(n_blocks, MOE_BLOCK, D)

    def expert_block(args):
        xblk, e = args
        gu = xblk @ w_gu[e] + b_gu[e]
        gate = jnp.minimum(gu[:, :D_FF], SWIGLU_LIMIT)
        up = jnp.clip(gu[:, D_FF:], -SWIGLU_LIMIT, SWIGLU_LIMIT)
        act = (up + 1.0) * gate * jax.nn.sigmoid(SWIGLU_ALPHA * gate)
        return act @ w_down[e] + b_down[e]

    yb = lax.map(expert_block, (xb, block_e)).reshape(rows, D)
    y_sorted = yb[dest] * gates.reshape(-1)[order][:, None].astype(yb.dtype)
    out = jnp.zeros((N, D), yb.dtype).at[tok].add(y_sorted)
    return out.reshape(B, S, D)


def trunk_layer(x, c, pos, p, past_latent, past_k_pe, ret_state0, ret_chunk):
    shift1, scale1, gate1, shift2, scale2, gate2 = ada_modulation(c, p['w_ada'], p['b_ada'])
    h = rmsnorm(x, p['g_pre_mix']) * (1.0 + scale1) + shift1
    q_lat, kv_lat, k_pe, r_q, r_k, r_v, r_g = combined_projection(h, pos, p['w_in'], p['g_q_lat'], p['g_kv_lat'])
    q_nope, q_pe = mla_queries(q_lat, pos, p['w_uq'])
    if past_latent is None:
        k_nope, v = mla_keys(kv_lat, p['w_ukv'])
        attn = mla_attention_chunk_causal(q_nope, q_pe, k_nope, k_pe, v)
    else:
        lat_all = jnp.concatenate([past_latent.astype(kv_lat.dtype), kv_lat], axis=1)
        kpe_all = jnp.concatenate([past_k_pe.astype(k_pe.dtype), k_pe], axis=1)
        k_nope, v = mla_keys(lat_all, p['w_ukv'])
        attn = mla_attention_full(q_nope, q_pe, k_nope, kpe_all, v)
    ret, ret_state = retention_chunkwise(r_q, r_k, r_v, ret_state0, ret_chunk)
    mix = mixer_output(attn, ret, r_g, p['g_ret'], p['w_out'])
    x = x + gate1 * rmsnorm(mix, p['g_post_mix'])
    h2 = rmsnorm(x, p['g_pre_ffn']) * (1.0 + scale2) + shift2
    ffn = moe_ffn(h2, p['w_router'], p['b_router'], p['w_gate_up'], p['b_gate_up'], p['w_down'], p['b_down'])
    x = x + gate2 * rmsnorm(ffn, p['g_post_ffn'])
    return x, kv_lat, k_pe, ret_state


def setup_inputs(seed: int = 0) -> dict:
    key = jax.random.key(seed)
    ks = list(jax.random.split(key, 32))
    f32 = jnp.float32

    def nrm(shape, scale):
        return jax.random.normal(ks.pop(), shape, f32) * scale

    def gain(shape):
        return 1.0 + nrm(shape, 0.05)

    D = D_MODEL
    return {
        'x_prompt': nrm((BATCH, SEQ, D), 1.0),
        'x_sample': nrm((DEC_BATCH, DEC_SEQ, D), 1.0),
        'cache_kv_latent': nrm((DEPTH, DEC_BATCH, PAST_LEN, KV_LORA), 1.0),
        'cache_k_rope': nrm((DEPTH, DEC_BATCH, PAST_LEN, MLA_ROPE), 1.0),
        'state_retention': nrm((DEPTH, DEC_BATCH, RET_HEADS, RET_QK, RET_V), 0.1),
        'c_prompt': nrm((BATCH, D), 1.0),
        'c_sample': nrm((DEC_BATCH, D), 1.0),
        'w_ada': nrm((DEPTH, D, 6 * D), 0.5 * D ** -0.5),
        'b_ada': nrm((DEPTH, 6 * D), 0.02),
        'g_pre_mix': gain((DEPTH, D)),
        'g_post_mix': gain((DEPTH, D)),
        'g_pre_ffn': gain((DEPTH, D)),
        'g_post_ffn': gain((DEPTH, D)),
        'w_in': nrm((DEPTH, D, IN_COLS), D ** -0.5),
        'g_q_lat': gain((DEPTH, Q_LORA)),
        'g_kv_lat': gain((DEPTH, KV_LORA)),
        'w_uq': nrm((DEPTH, Q_LORA, MLA_HEADS * (MLA_NOPE + MLA_ROPE)), Q_LORA ** -0.5),
        'w_ukv': nrm((DEPTH, KV_LORA, MLA_HEADS * (MLA_NOPE + MLA_V)), KV_LORA ** -0.5),
        'g_ret': gain((DEPTH, RET_WIDTH)),
        'w_out': nrm((DEPTH, MIX_WIDTH, D), MIX_WIDTH ** -0.5),
        'w_router': nrm((DEPTH, D, N_EXPERTS), D ** -0.5),
        'b_router': nrm((DEPTH, N_EXPERTS), 0.01),
        'w_gate_up': nrm((DEPTH, N_EXPERTS, D, 2 * D_FF), D ** -0.5),
        'b_gate_up': nrm((DEPTH, N_EXPERTS, 2 * D_FF), 0.01),
        'w_down': nrm((DEPTH, N_EXPERTS, D_FF, D), D_FF ** -0.5),
        'b_down': nrm((DEPTH, N_EXPERTS, D), 0.01),
    }


def reference(x_prompt, x_sample, cache_kv_latent, cache_k_rope, state_retention, c_prompt, c_sample,
              w_ada, b_ada, g_pre_mix, g_post_mix, g_pre_ffn, g_post_ffn, w_in, g_q_lat, g_kv_lat,
              w_uq, w_ukv, g_ret, w_out, w_router, b_router, w_gate_up, b_gate_up, w_down, b_down):
    xp, xs = x_prompt, x_sample
    pos_p = jnp.arange(xp.shape[1])
    pos_s = cache_kv_latent.shape[2] + jnp.arange(xs.shape[1])
    lat_p, kpe_p, st_p, lat_s, kpe_s, st_s = [], [], [], [], [], []
    for l in range(DEPTH):
        p = dict(w_ada=w_ada[l], b_ada=b_ada[l], g_pre_mix=g_pre_mix[l], g_post_mix=g_post_mix[l],
                 g_pre_ffn=g_pre_ffn[l], g_post_ffn=g_post_ffn[l], w_in=w_in[l], g_q_lat=g_q_lat[l],
                 g_kv_lat=g_kv_lat[l], w_uq=w_uq[l], w_ukv=w_ukv[l], g_ret=g_ret[l], w_out=w_out[l],
                 w_router=w_router[l], b_router=b_router[l], w_gate_up=w_gate_up[l], b_gate_up=b_gate_up[l],
                 w_down=w_down[l], b_down=b_down[l])
        zero_state = jnp.zeros((xp.shape[0], RET_HEADS, RET_QK, RET_V), jnp.float32)
        xp, lat, kpe, st = trunk_layer(xp, c_prompt, pos_p, p, None, None, zero_state, CHUNK)
        lat_p.append(lat)
        kpe_p.append(kpe)
        st_p.append(st)
        xs, lat, kpe, st = trunk_layer(xs, c_sample, pos_s, p, cache_kv_latent[l], cache_k_rope[l],
                                       state_retention[l], xs.shape[1])
        lat_s.append(lat)
        kpe_s.append(kpe)
        st_s.append(st)
    return (xp, xs,
            jnp.stack(lat_p), jnp.stack(kpe_p), jnp.stack(st_p).astype(state_retention.dtype),
            jnp.stack(lat_s), jnp.stack(kpe_s), jnp.stack(st_s).astype(state_retention.dtype))
```

```python
import functools

import numpy as np
import jax
import jax.numpy as jnp
from jax import lax
from jax.experimental import pallas as pl
from jax.experimental.pallas import tpu as pltpu

CHUNK = 64
EPS = 1e-6
ROPE_THETA = 10000.0

MLA_HEADS = 8
MLA_NOPE = 128
MLA_ROPE = 64
MLA_V = 128
Q_LORA = 512
KV_LORA = 512
MLA_WIDTH = MLA_HEADS * MLA_V

RET_HEADS = 4
RET_QK = 256
RET_V = 256
RET_WIDTH = RET_HEADS * RET_V

N_EXPERTS = 32
TOP_K = 4
D_FF = 2048
SWIGLU_LIMIT = 7.0
SWIGLU_ALPHA = 1.702

LANES = 128
V7X_VMEM_BYTES = 64 * 1024 * 1024
VMEM_CAP_BYTES = V7X_VMEM_BYTES - 8 * 1024 * 1024

RET_CHUNK_PROMPT = 128
ATTN_Q_TILE = 256
MOE_ROWS = 256
COMBINE_TOKENS = 64
NEG_BIG = -1e30

bf16 = jnp.bfloat16
f32 = jnp.float32


def _params(sem, vmem_bytes):
    return pltpu.CompilerParams(dimension_semantics=sem, vmem_limit_bytes=int(min(vmem_bytes, VMEM_CAP_BYTES)))


def _const_spec(shape):
    nd = len(shape)
    return pl.BlockSpec(shape, lambda *_: (0,) * nd, pipeline_mode=pl.Buffered(1))


def _rms(x):
    return x * lax.rsqrt(jnp.mean(x * x, axis=-1, keepdims=True) + EPS)


def _silu(x):
    return x * jax.nn.sigmoid(x)


def _ada_kernel(c_ref, w_ref, b_ref, o_ref):
    s = _silu(c_ref[...]).astype(bf16)
    o_ref[...] = jnp.dot(s, w_ref[...].astype(bf16), preferred_element_type=f32) + b_ref[...]


def _ada(c, w_ada, b_ada):
    nb, d = c.shape
    n = w_ada.shape[1]
    tn = 1536
    return pl.pallas_call(
        _ada_kernel,
        grid=(n // tn,),
        in_specs=[pl.BlockSpec((nb, d), lambda j: (0, 0)),
                  pl.BlockSpec((d, tn), lambda j: (0, j)),
                  pl.BlockSpec((1, tn), lambda j: (0, j))],
        out_specs=pl.BlockSpec((nb, tn), lambda j: (0, j)),
        out_shape=jax.ShapeDtypeStruct((nb, n), f32),
        compiler_params=_params(("arbitrary",), 2 * d * tn * 4 + d * tn * 2 + (8 << 20)),
        name="ada_modulation",
    )(c, w_ada, b_ada.reshape(1, n))


_C_Q = 0
_C_KV = _C_Q + Q_LORA
_C_PE = _C_KV + KV_LORA
_C_RQ = _C_PE + 2 * MLA_ROPE
_C_RK = _C_RQ + RET_HEADS * RET_QK
_C_RV = _C_RK + RET_HEADS * RET_QK
_C_RG = _C_RV + RET_WIDTH
_C_END = _C_RG + RET_WIDTH


def _rope_pair(t):
    return t + pltpu.roll(t, MLA_ROPE, axis=1)


def _rope_heads(z, cos, sin, scale):
    outs = []
    half = RET_QK // 2
    for h in range(RET_HEADS):
        x1 = z[:, h * RET_QK:h * RET_QK + half]
        x2 = z[:, h * RET_QK + half:(h + 1) * RET_QK]
        outs.append((x1 * cos - x2 * sin) * scale)
        outs.append((x1 * sin + x2 * cos) * scale)
    return jnp.concatenate(outs, axis=-1)


def _inproj_kernel(x_ref, mod_ref, g_ref, w_ref, gq_ref, gkv_ref, t64_ref, t256_ref,
                   qlat_ref, kvlat_ref, kvlat16_ref, kpe_ref, kpe16_ref, rq_ref, rk_ref, rv_ref, rg_ref):
    m = mod_ref[0]
    h = (_rms(x_ref[...]) * g_ref[...] * (1.0 + m[1:2]) + m[0:1]).astype(bf16)

    def proj(c0, c1):
        return jnp.dot(h, w_ref[:, c0:c1], preferred_element_type=f32)

    qlat_ref[...] = (_rms(proj(_C_Q, _C_KV)) * gq_ref[...]).astype(bf16)
    kv = _rms(proj(_C_KV, _C_PE)) * gkv_ref[...]
    kvlat_ref[...] = kv
    kvlat16_ref[...] = kv.astype(bf16)
    pe = _rope_pair(proj(_C_PE, _C_RQ) * t64_ref[...])
    kpe_ref[...] = pe[:, :MLA_ROPE]
    lane = lax.broadcasted_iota(jnp.int32, pe.shape, 1)
    kpe16_ref[...] = jnp.where(lane < MLA_ROPE, pe, 0.0).astype(bf16)
    cos = t256_ref[:, :RET_QK // 2]
    sin = t256_ref[:, RET_QK // 2:]
    rq_ref[...] = _rope_heads(proj(_C_RQ, _C_RK), cos, sin, 1.0).astype(bf16)
    rk_ref[...] = _rope_heads(proj(_C_RK, _C_RV), cos, sin, RET_QK ** -0.5).astype(bf16)
    rv_ref[...] = proj(_C_RV, _C_RG).astype(bf16)
    rg_ref[...] = proj(_C_RG, _C_END).astype(bf16)


def _inproj(x2d, mod, g_pre, w_ext, g_q, g_kv, t64, t256, seq, tm):
    n, d = x2d.shape
    per_seq = seq // tm
    row = lambda w: pl.BlockSpec((tm, w), lambda i: (i, 0))
    tab = lambda w: pl.BlockSpec((tm, w), lambda i: (i % per_seq, 0))
    outs = [(Q_LORA, bf16), (KV_LORA, f32), (KV_LORA, bf16), (MLA_ROPE, f32), (2 * MLA_ROPE, bf16),
            (RET_HEADS * RET_QK, bf16), (RET_HEADS * RET_QK, bf16), (RET_WIDTH, bf16), (RET_WIDTH, bf16)]
    vmem = w_ext.size * 2 + 2 * tm * d * 4 + 4 * tm * d * 4 + sum(2 * tm * w * 4 for w, _ in outs) + (6 << 20)
    return pl.pallas_call(
        _inproj_kernel,
        grid=(n // tm,),
        in_specs=[row(d),
                  pl.BlockSpec((1, 6, d), lambda i: (i // per_seq, 0, 0)),
                  _const_spec((1, d)), _const_spec(w_ext.shape), _const_spec((1, Q_LORA)), _const_spec((1, KV_LORA)),
                  tab(2 * MLA_ROPE), tab(RET_QK)],
        out_specs=[row(w) for w, _ in outs],
        out_shape=[jax.ShapeDtypeStruct((n, w), dt) for w, dt in outs],
        compiler_params=_params(("arbitrary",), vmem),
        name="in_projection",
    )(x2d, mod, g_pre.reshape(1, d), w_ext, g_q.reshape(1, Q_LORA), g_kv.reshape(1, KV_LORA), t64, t256)


def _mla_kernel(qlat_ref, lat_ref, kpe_ref, wq_ref, wk_ref, wv_ref, tab_ref, o_ref, q_s, k_s, v_s, *, q_tiles):
    scale = (MLA_NOPE + MLA_ROPE) ** -0.5
    qh = jnp.dot(qlat_ref[0], wq_ref[0], preferred_element_type=f32)
    q_s[:, :MLA_NOPE] = (qh[:, :MLA_NOPE] * scale).astype(bf16)
    q_s[:, MLA_NOPE:] = (_rope_pair(qh[:, MLA_NOPE:] * tab_ref[...]) * scale).astype(bf16)
    lat = lat_ref[0]
    k_s[:, :MLA_NOPE] = jnp.dot(lat, wk_ref[0], preferred_element_type=f32).astype(bf16)
    k_s[:, MLA_NOPE:] = kpe_ref[0]
    v_s[...] = jnp.dot(lat, wv_ref[0], preferred_element_type=f32).astype(bf16)
    for q0, ql, kvl, masked in q_tiles:
        q = q_s[q0:q0 + ql, :]
        s = lax.dot_general(q, k_s[0:kvl, :], (((1,), (1,)), ((), ())), preferred_element_type=f32)
        if masked:
            qc = (q0 + lax.broadcasted_iota(jnp.int32, (ql, 1), 0)) // CHUNK
            kc = lax.broadcasted_iota(jnp.int32, (1, kvl), 1) // CHUNK
            s = jnp.where(kc <= qc, s, NEG_BIG)
        p = jnp.exp(s - jnp.max(s, axis=-1, keepdims=True))
        l = jnp.sum(p, axis=-1, keepdims=True)
        o = jnp.dot(p.astype(bf16), v_s[0:kvl, :], preferred_element_type=f32)
        o_ref[0, q0:q0 + ql, :] = (o / l).astype(bf16)


def _mla(qlat, lat, kpe16, wq, wk, wv, tab, q_tiles):
    b, sq, _ = qlat.shape
    skv = lat.shape[1]
    kern = functools.partial(_mla_kernel, q_tiles=q_tiles)
    max_ql = max(t[1] for t in q_tiles)
    vmem = 2 * 2 * (sq * Q_LORA + skv * KV_LORA + skv * LANES) + sq * LANES * 4 + sq * 3 * LANES * 4 \
        + (sq + skv) * 3 * LANES * 2 + 4 * max_ql * skv * 4 + (8 << 20)
    return pl.pallas_call(
        kern,
        grid=(b, MLA_HEADS),
        in_specs=[pl.BlockSpec((1, sq, Q_LORA), lambda i, h: (i, 0, 0)),
                  pl.BlockSpec((1, skv, KV_LORA), lambda i, h: (i, 0, 0)),
                  pl.BlockSpec((1, skv, 2 * MLA_ROPE), lambda i, h: (i, 0, 0)),
                  pl.BlockSpec((1, Q_LORA, MLA_NOPE + 2 * MLA_ROPE), lambda i, h: (h, 0, 0)),
                  pl.BlockSpec((1, KV_LORA, MLA_NOPE), lambda i, h: (h, 0, 0)),
                  pl.BlockSpec((1, KV_LORA, MLA_V), lambda i, h: (h, 0, 0)),
                  pl.BlockSpec((sq, 2 * MLA_ROPE), lambda i, h: (0, 0))],
        out_specs=pl.BlockSpec((1, sq, MLA_V), lambda i, h: (i, 0, h)),
        out_shape=jax.ShapeDtypeStruct((b, sq, MLA_WIDTH), bf16),
        scratch_shapes=[pltpu.VMEM((sq, MLA_NOPE + 2 * MLA_ROPE), bf16),
                        pltpu.VMEM((skv, MLA_NOPE + 2 * MLA_ROPE), bf16),
                        pltpu.VMEM((skv, MLA_V), bf16)],
        compiler_params=_params(("arbitrary", "arbitrary"), vmem),
        name="mla_attention",
    )(qlat, lat, kpe16, wq, wk, wv, tab)


def _ret_kernel(lg_ref, rq_ref, rk_ref, rv_ref, rg_ref, gret_ref, s0_ref, o_ref, sout_ref, state, *, chunk, n_chunks):
    c = chunk
    lg = lg_ref[0, 0:1, 0:1]
    ii = lax.broadcasted_iota(jnp.int32, (c, c), 0)
    jj = lax.broadcasted_iota(jnp.int32, (c, c), 1)
    diff = (ii - jj).astype(f32)
    dmask = jnp.where(diff >= 0.0, jnp.exp(jnp.maximum(diff, 0.0) * lg), 0.0)
    ic = lax.broadcasted_iota(jnp.int32, (c, 1), 0).astype(f32)
    q_decay = jnp.exp((ic + 1.0) * lg)
    k_decay = jnp.exp((c - 1.0 - ic) * lg)
    chunk_decay = jnp.exp(float(c) * lg)
    gret = gret_ref[...]
    state[...] = s0_ref[0, 0]

    def step(n, carry):
        sl = pl.ds(pl.multiple_of(n * c, c), c)
        q = rq_ref[0, sl, :]
        k = rk_ref[0, sl, :]
        v = rv_ref[0, sl, :]
        st = state[...]
        attn = lax.dot_general(q, k, (((1,), (1,)), ((), ())), preferred_element_type=f32) * dmask
        inner = jnp.dot(attn.astype(bf16), v, preferred_element_type=f32)
        cross = jnp.dot(q, st.astype(bf16), preferred_element_type=f32) * q_decay
        o = inner + cross
        kd_t = (k.astype(f32) * k_decay).T.astype(bf16)
        state[...] = st * chunk_decay + jnp.dot(kd_t, v, preferred_element_type=f32)
        mu = jnp.mean(o, axis=-1, keepdims=True)
        oc = o - mu
        rn = oc * lax.rsqrt(jnp.mean(oc * oc, axis=-1, keepdims=True) + EPS) * gret
        o_ref[0, sl, :] = (rn * _silu(rg_ref[0, sl, :].astype(f32))).astype(bf16)
        return carry

    lax.fori_loop(0, n_chunks, step, 0)
    sout_ref[0, 0] = state[...]


def _retention(rq, rk, rv, rg, g_ret, state0, log_g, chunk):
    b, s, _ = rq.shape
    kern = functools.partial(_ret_kernel, chunk=chunk, n_chunks=s // chunk)
    head = lambda w: pl.BlockSpec((1, s, w), lambda i, h: (i, 0, h))
    st = pl.BlockSpec((1, 1, RET_QK, RET_V), lambda i, h: (i, h, 0, 0))
    vmem = 2 * 5 * s * RET_QK * 2 + 5 * RET_QK * RET_V * 4 + (8 << 20)
    return pl.pallas_call(
        kern,
        grid=(b, RET_HEADS),
        in_specs=[pl.BlockSpec((1, 8, LANES), lambda i, h: (h, 0, 0)),
                  head(RET_QK), head(RET_QK), head(RET_V), head(RET_V),
                  pl.BlockSpec((1, RET_V), lambda i, h: (0, h)), st],
        out_specs=[head(RET_V), st],
        out_shape=[jax.ShapeDtypeStruct((b, s, RET_WIDTH), bf16),
                   jax.ShapeDtypeStruct((b, RET_HEADS, RET_QK, RET_V), f32)],
        scratch_shapes=[pltpu.VMEM((RET_QK, RET_V), f32)],
        compiler_params=_params(("arbitrary", "arbitrary"), vmem),
        name="retention",
    )(log_g, rq, rk, rv, rg, g_ret.reshape(1, RET_WIDTH), state0)


def _pack_pairs(lo16, hi16):
    lo = lax.bitcast_convert_type(lo16.astype(f32), jnp.uint32)
    hi = lax.bitcast_convert_type(hi16.astype(f32), jnp.uint32)
    return (hi & jnp.uint32(0xFFFF0000)) | (lo >> 16)


def _unpack_pairs(u):
    lo = lax.bitcast_convert_type(u << 16, f32)
    hi = lax.bitcast_convert_type(u & jnp.uint32(0xFFFF0000), f32)
    return lo, hi


def _mixout_kernel(attn_ref, rn_ref, x_ref, mod_ref, gpost_ref, gpre_ref, wout_ref, wr_ref, br_ref,
                   xmid_ref, h2p_ref, idx_ref, gate_ref):
    mix = (jnp.dot(attn_ref[...], wout_ref[0:MLA_WIDTH, :], preferred_element_type=f32)
           + jnp.dot(rn_ref[...], wout_ref[MLA_WIDTH:, :], preferred_element_type=f32))
    m = mod_ref[0]
    x1 = x_ref[...] + m[2:3] * (_rms(mix) * gpost_ref[...])
    xmid_ref[...] = x1
    h2 = _rms(x1) * gpre_ref[...] * (1.0 + m[4:5]) + m[3:4]
    h_hi = h2.astype(bf16)
    half = h2.shape[1] // 2
    h2p_ref[...] = _pack_pairs(h_hi[:, :half], h_hi[:, half:])
    h_lo = (h2 - h_hi.astype(f32)).astype(bf16)
    logits = (jnp.dot(h_hi, wr_ref[0], preferred_element_type=f32)
              + jnp.dot(h_lo, wr_ref[0], preferred_element_type=f32)
              + jnp.dot(h_hi, wr_ref[1], preferred_element_type=f32)) + br_ref[...]
    col = lax.broadcasted_iota(jnp.int32, logits.shape, 1).astype(f32)
    vals, idxs = [], []
    for _ in range(TOP_K):
        mx = jnp.max(logits, axis=-1, keepdims=True)
        ix = jnp.min(jnp.where(logits == mx, col, float(LANES)), axis=-1, keepdims=True)
        vals.append(mx)
        idxs.append(ix)
        logits = jnp.where(col == ix, -jnp.inf, logits)
    es = [jnp.exp(v - vals[0]) for v in vals]
    tot = es[0] + es[1] + es[2] + es[3]
    idx_out = jnp.zeros_like(col)
    gate_out = jnp.zeros_like(col)
    for k in range(TOP_K):
        idx_out = jnp.where(col == float(k), idxs[k], idx_out)
        gate_out = jnp.where(col == float(k), es[k] / tot, gate_out)
    idx_ref[...] = idx_out.astype(jnp.int32)
    gate_ref[...] = gate_out


def _mixout(attn, rn, x2d, mod, g_post, g_pre, w_out16, wr, br, seq, tm):
    n, d = x2d.shape
    per_seq = seq // tm
    row = lambda w: pl.BlockSpec((tm, w), lambda i: (i, 0))
    vmem = w_out16.size * 2 + 2 * tm * d * (4 + 4 + 2 + 2) + 6 * tm * d * 4 + (8 << 20)
    return pl.pallas_call(
        _mixout_kernel,
        grid=(n // tm,),
        in_specs=[row(MLA_WIDTH), row(RET_WIDTH), row(d),
                  pl.BlockSpec((1, 6, d), lambda i: (i // per_seq, 0, 0)),
                  _const_spec((1, d)), _const_spec((1, d)), _const_spec(w_out16.shape),
                  _const_spec(wr.shape), _const_spec((1, LANES))],
        out_specs=[row(d), row(d // 2), row(LANES), row(LANES)],
        out_shape=[jax.ShapeDtypeStruct((n, d), f32), jax.ShapeDtypeStruct((n, d // 2), jnp.uint32),
                   jax.ShapeDtypeStruct((n, LANES), jnp.int32), jax.ShapeDtypeStruct((n, LANES), f32)],
        compiler_params=_params(("arbitrary",), vmem),
        name="mixer_out_router",
    )(attn, rn, x2d, mod, g_post.reshape(1, d), g_pre.reshape(1, d), w_out16, wr, br)


def _gather_rows(idx_ref, src_hbm, dst, sem, n_rows):
    def body(r, carry):
        t = idx_ref[0, 0, r]
        pltpu.make_async_copy(src_hbm.at[pl.ds(t, 1), :], dst.at[pl.ds(r, 1), :], sem).start()
        return carry
    lax.fori_loop(0, n_rows, body, 0, unroll=8)


def _wait_rows(src_hbm, dst, sem, n_rows):
    pltpu.make_async_copy(src_hbm.at[pl.ds(0, n_rows), :], dst, sem).wait()


def _prefetched_gather(i, n_live, cur_idx_ref, next_idx_ref, src_hbm, buf, sem, n_rows):
    slot = i % 2

    @pl.when(jnp.logical_and(i == 0, n_live > 0))
    def _():
        _gather_rows(cur_idx_ref, src_hbm, buf.at[0], sem.at[0], n_rows)

    @pl.when(i + 1 < n_live)
    def _():
        _gather_rows(next_idx_ref, src_hbm, buf.at[1 - slot], sem.at[1 - slot], n_rows)

    @pl.when(i < n_live)
    def _():
        _wait_rows(src_hbm, buf.at[slot], sem.at[slot], n_rows)

    return slot


def _moe_up_kernel(be_ref, nb_ref, tok_ref, tokn_ref, h2p_hbm, wgu_ref, bgu_ref, act_ref, xbuf, sem):
    i = pl.program_id(0)
    n_live = nb_ref[0]
    slot = _prefetched_gather(i, n_live, tok_ref, tokn_ref, h2p_hbm, xbuf, sem, MOE_ROWS)

    @pl.when(i < n_live)
    def _():
        lo, hi = _unpack_pairs(xbuf[slot])
        x_lo = lo.astype(bf16)
        x_hi = hi.astype(bf16)
        half = x_lo.shape[1]
        tf = 512
        for c in range(D_FF // tf):
            def proj(c0):
                return (jnp.dot(x_lo, wgu_ref[0, 0:half, c0:c0 + tf], preferred_element_type=f32)
                        + jnp.dot(x_hi, wgu_ref[0, half:, c0:c0 + tf], preferred_element_type=f32)
                        + bgu_ref[0, :, c0:c0 + tf])
            gate = jnp.minimum(proj(c * tf), SWIGLU_LIMIT)
            up = jnp.clip(proj(D_FF + c * tf), -SWIGLU_LIMIT, SWIGLU_LIMIT)
            act = (up + 1.0) * gate * jax.nn.sigmoid(SWIGLU_ALPHA * gate)
            act_ref[:, c * tf:(c + 1) * tf] = act.astype(bf16)

    @pl.when(i >= n_live)
    def _():
        act_ref[...] = jnp.zeros_like(act_ref)


def _moe_up(block_e, n_live, buf_tok, h2p, wgu16, bgu):
    n_blocks = block_e.shape[0]
    e, d, f2 = wgu16.shape
    vmem = 2 * d * f2 * 2 + 2 * MOE_ROWS * d * 2 + 2 * MOE_ROWS * D_FF * 2 + 8 * MOE_ROWS * 512 * 4 + (8 << 20)
    grid_spec = pltpu.PrefetchScalarGridSpec(
        num_scalar_prefetch=2,
        grid=(n_blocks,),
        in_specs=[pl.BlockSpec((1, 1, MOE_ROWS), lambda i, be, nb: (i, 0, 0), memory_space=pltpu.SMEM),
                  pl.BlockSpec((1, 1, MOE_ROWS), lambda i, be, nb: (jnp.minimum(i + 1, n_blocks - 1), 0, 0),
                               memory_space=pltpu.SMEM),
                  pl.BlockSpec(memory_space=pl.ANY),
                  pl.BlockSpec((1, d, f2), lambda i, be, nb: (be[i], 0, 0)),
                  pl.BlockSpec((1, 1, f2), lambda i, be, nb: (be[i], 0, 0))],
        out_specs=pl.BlockSpec((MOE_ROWS, D_FF), lambda i, be, nb: (i, 0)),
        scratch_shapes=[pltpu.VMEM((2, MOE_ROWS, d // 2), jnp.uint32), pltpu.SemaphoreType.DMA((2,))],
    )
    return pl.pallas_call(
        _moe_up_kernel,
        grid_spec=grid_spec,
        out_shape=jax.ShapeDtypeStruct((n_blocks * MOE_ROWS, D_FF), bf16),
        compiler_params=_params(("arbitrary",), vmem),
        name="moe_gate_up",
    )(block_e, n_live, buf_tok, buf_tok, h2p, wgu16, bgu)


def _moe_down_kernel(be_ref, nb_ref, act_ref, wd_ref, bd_ref, y_ref):
    i = pl.program_id(0)

    @pl.when(i < nb_ref[0])
    def _():
        y = (jnp.dot(act_ref[...], wd_ref[0], preferred_element_type=f32) + bd_ref[0]).astype(bf16)
        half = y.shape[1] // 2
        y_ref[...] = _pack_pairs(y[:, :half], y[:, half:])

    @pl.when(i >= nb_ref[0])
    def _():
        y_ref[...] = jnp.zeros_like(y_ref)


def _moe_down(block_e, n_live, act, wd16, bd):
    n_blocks = block_e.shape[0]
    e, f, d = wd16.shape
    vmem = 2 * f * d * 2 + 2 * MOE_ROWS * f * 2 + 2 * MOE_ROWS * d * 2 + 4 * MOE_ROWS * d * 4 + (8 << 20)
    grid_spec = pltpu.PrefetchScalarGridSpec(
        num_scalar_prefetch=2,
        grid=(n_blocks,),
        in_specs=[pl.BlockSpec((MOE_ROWS, f), lambda i, be, nb: (i, 0)),
                  pl.BlockSpec((1, f, d), lambda i, be, nb: (be[i], 0, 0)),
                  pl.BlockSpec((1, 1, d), lambda i, be, nb: (be[i], 0, 0))],
        out_specs=pl.BlockSpec((MOE_ROWS, d // 2), lambda i, be, nb: (i, 0)),
    )
    return pl.pallas_call(
        _moe_down_kernel,
        grid_spec=grid_spec,
        out_shape=jax.ShapeDtypeStruct((n_blocks * MOE_ROWS, d // 2), jnp.uint32),
        compiler_params=_params(("arbitrary",), vmem),
        name="moe_down",
    )(block_e, n_live, act, wd16, bd)


def _combine_kernel(pos_ref, posn_ref, y_hbm, gates_ref, xmid_ref, mod_ref, gpost_ref, out_ref, ybuf, sem, *, n_steps):
    i = pl.program_id(0)
    tm = COMBINE_TOKENS
    slot = _prefetched_gather(i, n_steps, pos_ref, posn_ref, y_hbm, ybuf, sem, TOP_K * tm)
    g = gates_ref[...]
    lo = hi = None
    for k in range(TOP_K):
        l_k, h_k = _unpack_pairs(ybuf[slot, k * tm:(k + 1) * tm, :])
        gk = g[:, k:k + 1]
        lo = gk * l_k if lo is None else lo + gk * l_k
        hi = gk * h_k if hi is None else hi + gk * h_k
    half = lo.shape[1]
    inv = lax.rsqrt((jnp.sum(lo * lo, axis=-1, keepdims=True) + jnp.sum(hi * hi, axis=-1, keepdims=True))
                    / (2 * half) + EPS)
    gate2 = mod_ref[0][5:6]
    gp = gpost_ref[...]
    out_ref[:, :half] = xmid_ref[:, :half] + gate2[:, :half] * (lo * inv * gp[:, :half])
    out_ref[:, half:] = xmid_ref[:, half:] + gate2[:, half:] * (hi * inv * gp[:, half:])


def _combine(pos, y_sorted, gates, xmid, mod, g_post, seq):
    n, d = xmid.shape
    tm = COMBINE_TOKENS
    n_tiles = n // tm
    per_seq = seq // tm
    vmem = 2 * TOP_K * tm * d * 2 + 4 * tm * d * 4 + 6 * tm * d * 4 + (8 << 20)
    return pl.pallas_call(
        functools.partial(_combine_kernel, n_steps=n_tiles),
        grid=(n_tiles,),
        in_specs=[pl.BlockSpec((1, 1, TOP_K * tm), lambda i: (i, 0, 0), memory_space=pltpu.SMEM),
                  pl.BlockSpec((1, 1, TOP_K * tm), lambda i: (jnp.minimum(i + 1, n_tiles - 1), 0, 0),
                               memory_space=pltpu.SMEM),
                  pl.BlockSpec(memory_space=pl.ANY),
                  pl.BlockSpec((tm, TOP_K), lambda i: (i, 0)),
                  pl.BlockSpec((tm, d), lambda i: (i, 0)),
                  pl.BlockSpec((1, 6, d), lambda i: (i // per_seq, 0, 0)),
                  _const_spec((1, d))],
        out_specs=pl.BlockSpec((tm, d), lambda i: (i, 0)),
        out_shape=jax.ShapeDtypeStruct((n, d), f32),
        scratch_shapes=[pltpu.VMEM((2, TOP_K * tm, d // 2), jnp.uint32), pltpu.SemaphoreType.DMA((2,))],
        compiler_params=_params(("arbitrary",), vmem),
        name="moe_combine",
    )(pos, pos, y_sorted, gates, xmid, mod, g_post.reshape(1, d))


def _route(top_idx):
    n = top_idx.shape[0]
    nk = n * TOP_K
    flat_e = top_idx.reshape(-1)
    onehot = (flat_e[:, None] == jnp.arange(N_EXPERTS, dtype=jnp.int32)[None, :]).astype(jnp.int32)
    csum = jnp.cumsum(onehot, axis=0)
    rank = jnp.take_along_axis(csum, flat_e[:, None], axis=1)[:, 0] - 1
    counts = csum[-1]
    padded = (counts + MOE_ROWS - 1) // MOE_ROWS * MOE_ROWS
    pad_end = jnp.cumsum(padded)
    pad_start = pad_end - padded
    dest = (pad_start[flat_e] + rank).astype(jnp.int32)
    n_blocks = (nk + N_EXPERTS * (MOE_ROWS - 1)) // MOE_ROWS
    tok = jnp.arange(nk, dtype=jnp.int32) // TOP_K
    buf_tok = jnp.zeros((n_blocks * MOE_ROWS,), jnp.int32).at[dest].set(tok)
    n_live = (pad_end[-1] // MOE_ROWS).astype(jnp.int32)
    blk = jnp.arange(n_blocks, dtype=jnp.int32)
    block_e = jnp.searchsorted(pad_end, jnp.minimum(blk, n_live - 1) * MOE_ROWS, side='right').astype(jnp.int32)
    block_e = jnp.minimum(block_e, N_EXPERTS - 1)
    return dest, buf_tok.reshape(n_blocks, 1, MOE_ROWS), block_e, n_live.reshape(1)


def _rope_tables(pos):
    def cs(d):
        half = d // 2
        inv_freq = 1.0 / (ROPE_THETA ** (jnp.arange(half, dtype=f32) * (2.0 / d)))
        ang = pos.astype(f32)[:, None] * inv_freq[None, :]
        return jnp.cos(ang), jnp.sin(ang)
    c64, s64 = cs(MLA_ROPE)
    c256, s256 = cs(RET_QK)
    t64 = jnp.concatenate([c64, c64, s64, s64], axis=-1)
    t256 = jnp.concatenate([c256, s256], axis=-1)
    return t64, t256


def _rot_cols(w):
    half = w.shape[-1] // 2
    return jnp.concatenate([-w[..., half:], w[..., :half]], axis=-1)


def _prep_weights(w_in, w_uq, w_ukv, w_out, w_router, b_router, w_gate_up, w_down):
    b = np.cumsum((Q_LORA, KV_LORA, MLA_ROPE))
    w_pe = w_in[:, b[1]:b[2]]
    w_ext = jnp.concatenate([w_in[:, :b[2]], _rot_cols(w_pe), w_in[:, b[2]:]], axis=-1).astype(bf16)
    wq = w_uq.reshape(Q_LORA, MLA_HEADS, MLA_NOPE + MLA_ROPE)
    wq = jnp.concatenate([wq, _rot_cols(wq[..., MLA_NOPE:])], axis=-1).transpose(1, 0, 2).astype(bf16)
    wkv = w_ukv.reshape(KV_LORA, MLA_HEADS, MLA_NOPE + MLA_V).transpose(1, 0, 2)
    wk = wkv[..., :MLA_NOPE].astype(bf16)
    wv = wkv[..., MLA_NOPE:].astype(bf16)
    r_hi = w_router.astype(bf16)
    r_lo = (w_router - r_hi.astype(f32)).astype(bf16)
    pad = ((0, 0), (0, LANES - N_EXPERTS))
    wr = jnp.stack([jnp.pad(r_hi, pad), jnp.pad(r_lo, pad)])
    br = jnp.pad(b_router, (0, LANES - N_EXPERTS), constant_values=NEG_BIG).reshape(1, LANES)
    return w_ext, wq, wk, wv, w_out.astype(bf16), wr, br, w_gate_up.astype(bf16), w_down.astype(bf16)


def _mixer_half(x, mod, pos, wts, p, past_lat16, past_kpe16, state0, ret_chunk, q_tiles, tm):
    w_ext, wq, wk, wv, w_out16, wr, br = wts
    b, s, d = x.shape
    x2d = x.reshape(b * s, d)
    t64, t256 = _rope_tables(pos)
    qlat, kvlat, kvlat16, kpe, kpe16, rq, rk, rv, rg = _inproj(
        x2d, mod, p['g_pre_mix'], w_ext, p['g_q_lat'], p['g_kv_lat'], t64, t256, s, tm)
    lat16 = kvlat16.reshape(b, s, KV_LORA)
    kpe16 = kpe16.reshape(b, s, 2 * MLA_ROPE)
    if past_lat16 is not None:
        lat16 = jnp.concatenate([past_lat16, lat16], axis=1)
        kpe16 = jnp.concatenate([past_kpe16, kpe16], axis=1)
    attn = _mla(qlat.reshape(b, s, Q_LORA), lat16, kpe16, wq, wk, wv, t64, q_tiles)
    log_g = jnp.log1p(-jnp.exp2(-5.0 - jnp.arange(RET_HEADS, dtype=f32)))
    log_g = jnp.broadcast_to(log_g[:, None, None], (RET_HEADS, 8, LANES))
    sh = lambda t, w: t.reshape(b, s, w)
    rn, state = _retention(sh(rq, RET_HEADS * RET_QK), sh(rk, RET_HEADS * RET_QK), sh(rv, RET_WIDTH),
                           sh(rg, RET_WIDTH), p['g_ret'], state0, log_g, ret_chunk)
    xmid, h2p, idx, gates = _mixout(attn.reshape(b * s, MLA_WIDTH), rn.reshape(b * s, RET_WIDTH), x2d, mod,
                                    p['g_post_mix'], p['g_pre_ffn'], w_out16, wr, br, s, tm)
    return kvlat.reshape(b, s, KV_LORA), kpe.reshape(b, s, MLA_ROPE), state, xmid, h2p, idx[:, :TOP_K], gates[:, :TOP_K]


def _combine_order(dest, n_tokens):
    tm = COMBINE_TOKENS
    return dest.reshape(n_tokens // tm, tm, TOP_K).transpose(0, 2, 1).reshape(n_tokens // tm, 1, TOP_K * tm)


def kernel(x_prompt, x_sample, cache_kv_latent, cache_k_rope, state_retention, c_prompt, c_sample, w_ada, b_ada, g_pre_mix, g_post_mix, g_pre_ffn, g_post_ffn, w_in, g_q_lat, g_kv_lat, w_uq, w_ukv, g_ret, w_out, w_router, b_router, w_gate_up, b_gate_up, w_down, b_down):
    depth = w_in.shape[0]
    assert depth == 1, "the staged problem has a single layer"
    bp, sp, d = x_prompt.shape
    bs, ss, _ = x_sample.shape
    past = cache_kv_latent.shape[2]
    l = 0
    p = dict(g_pre_mix=g_pre_mix[l], g_post_mix=g_post_mix[l], g_pre_ffn=g_pre_ffn[l], g_post_ffn=g_post_ffn[l],
             g_q_lat=g_q_lat[l], g_kv_lat=g_kv_lat[l], g_ret=g_ret[l])
    w_ext, wq, wk, wv, w_out16, wr, br, wgu16, wd16 = _prep_weights(
        w_in[l], w_uq[l], w_ukv[l], w_out[l], w_router[l], b_router[l], w_gate_up[l], w_down[l])
    wts = (w_ext, wq, wk, wv, w_out16, wr, br)

    mod = _ada(jnp.concatenate([c_prompt, c_sample], axis=0), w_ada[l], b_ada[l]).reshape(bp + bs, 6, d)
    mod_p, mod_s = mod[:bp], mod[bp:]

    tq = ATTN_Q_TILE
    tiles_p = tuple((q0, tq, q0 + tq, True) for q0 in range(0, sp, tq))
    zero_state = jnp.zeros((bp, RET_HEADS, RET_QK, RET_V), f32)
    lat_p, kpe_p, st_p, xmid_p, h2p_p, idx_p, gates_p = _mixer_half(
        x_prompt, mod_p, jnp.arange(sp), wts, p, None, None, zero_state, RET_CHUNK_PROMPT, tiles_p, 512)

    past_lat16 = cache_kv_latent[l].astype(bf16)
    past_kpe16 = jnp.pad(cache_k_rope[l], ((0, 0), (0, 0), (0, MLA_ROPE))).astype(bf16)
    tiles_s = ((0, ss, past + ss, False),)
    lat_s, kpe_s, st_s, xmid_s, h2p_s, idx_s, gates_s = _mixer_half(
        x_sample, mod_s, past + jnp.arange(ss), wts, p, past_lat16, past_kpe16, state_retention[l], ss, tiles_s, ss)

    n_p, n_s = bp * sp, bs * ss
    h2p = jnp.concatenate([h2p_p, h2p_s], axis=0)
    top_idx = jnp.concatenate([idx_p, idx_s], axis=0)
    dest, buf_tok, block_e, n_live = _route(top_idx)
    act = _moe_up(block_e, n_live, buf_tok, h2p, wgu16, b_gate_up[l].reshape(N_EXPERTS, 1, 2 * D_FF))
    y_sorted = _moe_down(block_e, n_live, act, wd16, b_down[l].reshape(N_EXPERTS, 1, d))
    dest_p, dest_s = dest[:n_p * TOP_K], dest[n_p * TOP_K:]
    y_p = _combine(_combine_order(dest_p, n_p), y_sorted, gates_p, xmid_p, mod_p, p['g_post_ffn'], sp)
    y_s = _combine(_combine_order(dest_s, n_s), y_sorted, gates_s, xmid_s, mod_s, p['g_post_ffn'], ss)

    return (y_p.reshape(bp, sp, d), y_s.reshape(bs, ss, d),
            lat_p[None], kpe_p[None], st_p[None].astype(state_retention.dtype),
            lat_s[None], kpe_s[None], st_s[None].astype(state_retention.dtype))
```

```python
import functools

import numpy as np
import jax
import jax.numpy as jnp
from jax import lax
from jax.experimental import pallas as pl
from jax.experimental.pallas import tpu as pltpu

CHUNK = 64
EPS = 1e-6
ROPE_THETA = 10000.0

MLA_HEADS = 8
MLA_NOPE = 128
MLA_ROPE = 64
MLA_V = 128
Q_LORA = 512
KV_LORA = 512
MLA_WIDTH = MLA_HEADS * MLA_V

RET_HEADS = 4
RET_QK = 256
RET_V = 256
RET_WIDTH = RET_HEADS * RET_V

N_EXPERTS = 32
TOP_K = 4
D_FF = 2048
SWIGLU_LIMIT = 7.0
SWIGLU_ALPHA = 1.702

LANES = 128
V7X_VMEM_BYTES = 64 * 1024 * 1024
VMEM_CAP_BYTES = V7X_VMEM_BYTES - 8 * 1024 * 1024

RET_CHUNK_PROMPT = 128
ATTN_Q_TILE = 256
MOE_ROWS = 256
COMBINE_TOKENS = 64
NEG_BIG = -1e30

bf16 = jnp.bfloat16
f32 = jnp.float32


def _params(sem, vmem_bytes):
    return pltpu.CompilerParams(dimension_semantics=sem, vmem_limit_bytes=int(min(vmem_bytes, VMEM_CAP_BYTES)))


def _const_spec(shape):
    nd = len(shape)
    return pl.BlockSpec(shape, lambda *_: (0,) * nd, pipeline_mode=pl.Buffered(1))


def _rms(x):
    return x * lax.rsqrt(jnp.mean(x * x, axis=-1, keepdims=True) + EPS)


def _silu(x):
    return x * jax.nn.sigmoid(x)


def _ada_kernel(c_ref, w_ref, b_ref, o_ref):
    s = _silu(c_ref[...]).astype(bf16)
    o_ref[...] = jnp.dot(s, w_ref[...].astype(bf16), preferred_element_type=f32) + b_ref[...]


def _ada(c, w_ada, b_ada):
    nb, d = c.shape
    n = w_ada.shape[1]
    tn = 1536
    return pl.pallas_call(
        _ada_kernel,
        grid=(n // tn,),
        in_specs=[pl.BlockSpec((nb, d), lambda j: (0, 0)),
                  pl.BlockSpec((d, tn), lambda j: (0, j)),
                  pl.BlockSpec((1, tn), lambda j: (0, j))],
        out_specs=pl.BlockSpec((nb, tn), lambda j: (0, j)),
        out_shape=jax.ShapeDtypeStruct((nb, n), f32),
        compiler_params=_params(("arbitrary",), 2 * d * tn * 4 + d * tn * 2 + (8 << 20)),
        name="ada_modulation",
    )(c, w_ada, b_ada.reshape(1, n))


_C_Q = 0
_C_KV = _C_Q + Q_LORA
_C_PE = _C_KV + KV_LORA
_C_RQ = _C_PE + 2 * MLA_ROPE
_C_RK = _C_RQ + RET_HEADS * RET_QK
_C_RV = _C_RK + RET_HEADS * RET_QK
_C_RG = _C_RV + RET_WIDTH
_C_END = _C_RG + RET_WIDTH


def _rope_pair(t):
    return t + pltpu.roll(t, MLA_ROPE, axis=1)


def _rope_heads(z, cos, sin, scale):
    outs = []
    half = RET_QK // 2
    for h in range(RET_HEADS):
        x1 = z[:, h * RET_QK:h * RET_QK + half]
        x2 = z[:, h * RET_QK + half:(h + 1) * RET_QK]
        outs.append((x1 * cos - x2 * sin) * scale)
        outs.append((x1 * sin + x2 * cos) * scale)
    return jnp.concatenate(outs, axis=-1)


def _inproj_kernel(x_ref, mod_ref, g_ref, w_ref, gq_ref, gkv_ref, t64_ref, t256_ref,
                   qlat_ref, kvlat_ref, kvlat16_ref, kpe_ref, kpe16_ref, rq_ref, rk_ref, rv_ref, rg_ref):
    m = mod_ref[0]
    h = (_rms(x_ref[...]) * g_ref[...] * (1.0 + m[1:2]) + m[0:1]).astype(bf16)

    def proj(c0, c1):
        return jnp.dot(h, w_ref[:, c0:c1], preferred_element_type=f32)

    qlat_ref[...] = (_rms(proj(_C_Q, _C_KV)) * gq_ref[...]).astype(bf16)
    kv = _rms(proj(_C_KV, _C_PE)) * gkv_ref[...]
    kvlat_ref[...] = kv
    kvlat16_ref[...] = kv.astype(bf16)
    pe = _rope_pair(proj(_C_PE, _C_RQ) * t64_ref[...])
    kpe_ref[...] = pe[:, :MLA_ROPE]
    lane = lax.broadcasted_iota(jnp.int32, pe.shape, 1)
    kpe16_ref[...] = jnp.where(lane < MLA_ROPE, pe, 0.0).astype(bf16)
    cos = t256_ref[:, :RET_QK // 2]
    sin = t256_ref[:, RET_QK // 2:]
    rq_ref[...] = _rope_heads(proj(_C_RQ, _C_RK), cos, sin, 1.0).astype(bf16)
    rk_ref[...] = _rope_heads(proj(_C_RK, _C_RV), cos, sin, RET_QK ** -0.5).astype(bf16)
    rv_ref[...] = proj(_C_RV, _C_RG).astype(bf16)
    rg_ref[...] = proj(_C_RG, _C_END).astype(bf16)


def _inproj(x2d, mod, g_pre, w_ext, g_q, g_kv, t64, t256, seq, tm):
    n, d = x2d.shape
    per_seq = seq // tm
    row = lambda w: pl.BlockSpec((tm, w), lambda i: (i, 0))
    tab = lambda w: pl.BlockSpec((tm, w), lambda i: (i % per_seq, 0))
    outs = [(Q_LORA, bf16), (KV_LORA, f32), (KV_LORA, bf16), (MLA_ROPE, f32), (2 * MLA_ROPE, bf16),
            (RET_HEADS * RET_QK, bf16), (RET_HEADS * RET_QK, bf16), (RET_WIDTH, bf16), (RET_WIDTH, bf16)]
    vmem = w_ext.size * 2 + 2 * tm * d * 4 + 4 * tm * d * 4 + sum(2 * tm * w * 4 for w, _ in outs) + (6 << 20)
    return pl.pallas_call(
        _inproj_kernel,
        grid=(n // tm,),
        in_specs=[row(d),
                  pl.BlockSpec((1, 6, d), lambda i: (i // per_seq, 0, 0)),
                  _const_spec((1, d)), _const_spec(w_ext.shape), _const_spec((1, Q_LORA)), _const_spec((1, KV_LORA)),
                  tab(2 * MLA_ROPE), tab(RET_QK)],
        out_specs=[row(w) for w, _ in outs],
        out_shape=[jax.ShapeDtypeStruct((n, w), dt) for w, dt in outs],
        compiler_params=_params(("arbitrary",), vmem),
        name="in_projection",
    )(x2d, mod, g_pre.reshape(1, d), w_ext, g_q.reshape(1, Q_LORA), g_kv.reshape(1, KV_LORA), t64, t256)


def _mla_kernel(qlat_ref, lat_ref, kpe_ref, wq_ref, wk_ref, wv_ref, tab_ref, o_ref, q_s, k_s, v_s, *, q_tiles):
    scale = (MLA_NOPE + MLA_ROPE) ** -0.5
    qh = jnp.dot(qlat_ref[0], wq_ref[0], preferred_element_type=f32)
    q_s[:, :MLA_NOPE] = (qh[:, :MLA_NOPE] * scale).astype(bf16)
    q_s[:, MLA_NOPE:] = (_rope_pair(qh[:, MLA_NOPE:] * tab_ref[...]) * scale).astype(bf16)
    lat = lat_ref[0]
    k_s[:, :MLA_NOPE] = jnp.dot(lat, wk_ref[0], preferred_element_type=f32).astype(bf16)
    k_s[:, MLA_NOPE:] = kpe_ref[0]
    v_s[...] = jnp.dot(lat, wv_ref[0], preferred_element_type=f32).astype(bf16)
    for q0, ql, kvl, masked in q_tiles:
        q = q_s[q0:q0 + ql, :]
        s = lax.dot_general(q, k_s[0:kvl, :], (((1,), (1,)), ((), ())), preferred_element_type=f32)
        if masked:
            qc = (q0 + lax.broadcasted_iota(jnp.int32, (ql, 1), 0)) // CHUNK
            kc = lax.broadcasted_iota(jnp.int32, (1, kvl), 1) // CHUNK
            s = jnp.where(kc <= qc, s, NEG_BIG)
        p = jnp.exp(s - jnp.max(s, axis=-1, keepdims=True))
        l = jnp.sum(p, axis=-1, keepdims=True)
        o = jnp.dot(p.astype(bf16), v_s[0:kvl, :], preferred_element_type=f32)
        o_ref[0, q0:q0 + ql, :] = (o / l).astype(bf16)


def _mla(qlat, lat, kpe16, wq, wk, wv, tab, q_tiles):
    b, sq, _ = qlat.shape
    skv = lat.shape[1]
    kern = functools.partial(_mla_kernel, q_tiles=q_tiles)
    max_ql = max(t[1] for t in q_tiles)
    vmem = 2 * 2 * (sq * Q_LORA + skv * KV_LORA + skv * LANES) + sq * LANES * 4 + sq * 3 * LANES * 4 \
        + (sq + skv) * 3 * LANES * 2 + 4 * max_ql * skv * 4 + (8 << 20)
    return pl.pallas_call(
        kern,
        grid=(b, MLA_HEADS),
        in_specs=[pl.BlockSpec((1, sq, Q_LORA), lambda i, h: (i, 0, 0)),
                  pl.BlockSpec((1, skv, KV_LORA), lambda i, h: (i, 0, 0)),
                  pl.BlockSpec((1, skv, 2 * MLA_ROPE), lambda i, h: (i, 0, 0)),
                  pl.BlockSpec((1, Q_LORA, MLA_NOPE + 2 * MLA_ROPE), lambda i, h: (h, 0, 0)),
                  pl.BlockSpec((1, KV_LORA, MLA_NOPE), lambda i, h: (h, 0, 0)),
                  pl.BlockSpec((1, KV_LORA, MLA_V), lambda i, h: (h, 0, 0)),
                  pl.BlockSpec((sq, 2 * MLA_ROPE), lambda i, h: (0, 0))],
        out_specs=pl.BlockSpec((1, sq, MLA_V), lambda i, h: (i, 0, h)),
        out_shape=jax.ShapeDtypeStruct((b, sq, MLA_WIDTH), bf16),
        scratch_shapes=[pltpu.VMEM((sq, MLA_NOPE + 2 * MLA_ROPE), bf16),
                        pltpu.VMEM((skv, MLA_NOPE + 2 * MLA_ROPE), bf16),
                        pltpu.VMEM((skv, MLA_V), bf16)],
        compiler_params=_params(("arbitrary", "arbitrary"), vmem),
        name="mla_attention",
    )(qlat, lat, kpe16, wq, wk, wv, tab)


def _ret_kernel(lg_ref, rq_ref, rk_ref, rv_ref, rg_ref, gret_ref, s0_ref, o_ref, sout_ref, state, *, chunk, n_chunks):
    c = chunk
    lg = lg_ref[0, 0:1, 0:1]
    ii = lax.broadcasted_iota(jnp.int32, (c, c), 0)
    jj = lax.broadcasted_iota(jnp.int32, (c, c), 1)
    diff = (ii - jj).astype(f32)
    dmask = jnp.where(diff >= 0.0, jnp.exp(jnp.maximum(diff, 0.0) * lg), 0.0)
    ic = lax.broadcasted_iota(jnp.int32, (c, 1), 0).astype(f32)
    q_decay = jnp.exp((ic + 1.0) * lg)
    k_decay = jnp.exp((c - 1.0 - ic) * lg)
    chunk_decay = jnp.exp(float(c) * lg)
    gret = gret_ref[...]
    state[...] = s0_ref[0, 0]

    def step(n, carry):
        sl = pl.ds(pl.multiple_of(n * c, c), c)
        q = rq_ref[0, sl, :]
        k = rk_ref[0, sl, :]
        v = rv_ref[0, sl, :]
        st = state[...]
        attn = lax.dot_general(q, k, (((1,), (1,)), ((), ())), preferred_element_type=f32) * dmask
        inner = jnp.dot(attn.astype(bf16), v, preferred_element_type=f32)
        cross = jnp.dot(q, st.astype(bf16), preferred_element_type=f32) * q_decay
        o = inner + cross
        kd_t = (k.astype(f32) * k_decay).T.astype(bf16)
        state[...] = st * chunk_decay + jnp.dot(kd_t, v, preferred_element_type=f32)
        mu = jnp.mean(o, axis=-1, keepdims=True)
        oc = o - mu
        rn = oc * lax.rsqrt(jnp.mean(oc * oc, axis=-1, keepdims=True) + EPS) * gret
        o_ref[0, sl, :] = (rn * _silu(rg_ref[0, sl, :].astype(f32))).astype(bf16)
        return carry

    lax.fori_loop(0, n_chunks, step, 0)
    sout_ref[0, 0] = state[...]


def _retention(rq, rk, rv, rg, g_ret, state0, log_g, chunk):
    b, s, _ = rq.shape
    kern = functools.partial(_ret_kernel, chunk=chunk, n_chunks=s // chunk)
    head = lambda w: pl.BlockSpec((1, s, w), lambda i, h: (i, 0, h))
    st = pl.BlockSpec((1, 1, RET_QK, RET_V), lambda i, h: (i, h, 0, 0))
    vmem = 2 * 5 * s * RET_QK * 2 + 5 * RET_QK * RET_V * 4 + (8 << 20)
    return pl.pallas_call(
        kern,
        grid=(b, RET_HEADS),
        in_specs=[pl.BlockSpec((1, 8, LANES), lambda i, h: (h, 0, 0)),
                  head(RET_QK), head(RET_QK), head(RET_V), head(RET_V),
                  pl.BlockSpec((1, RET_V), lambda i, h: (0, h)), st],
        out_specs=[head(RET_V), st],
        out_shape=[jax.ShapeDtypeStruct((b, s, RET_WIDTH), bf16),
                   jax.ShapeDtypeStruct((b, RET_HEADS, RET_QK, RET_V), f32)],
        scratch_shapes=[pltpu.VMEM((RET_QK, RET_V), f32)],
        compiler_params=_params(("arbitrary", "arbitrary"), vmem),
        name="retention",
    )(log_g, rq, rk, rv, rg, g_ret.reshape(1, RET_WIDTH), state0)


def _pack_pairs(lo16, hi16):
    lo = lax.bitcast_convert_type(lo16.astype(f32), jnp.uint32)
    hi = lax.bitcast_convert_type(hi16.astype(f32), jnp.uint32)
    return (hi & jnp.uint32(0xFFFF0000)) | (lo >> 16)


def _unpack_pairs(u):
    lo = lax.bitcast_convert_type(u << 16, f32)
    hi = lax.bitcast_convert_type(u & jnp.uint32(0xFFFF0000), f32)
    return lo, hi


def _mixout_kernel(attn_ref, rn_ref, x_ref, mod_ref, gpost_ref, gpre_ref, wout_ref, wr_ref, br_ref, cnt0_ref,
                   xmid_ref, h2p_ref, idx_ref, gate_ref, rank_ref, cnt_ref, carry):
    @pl.when(pl.program_id(0) == 0)
    def _():
        carry[...] = cnt0_ref[...]

    mix = (jnp.dot(attn_ref[...], wout_ref[0:MLA_WIDTH, :], preferred_element_type=f32)
           + jnp.dot(rn_ref[...], wout_ref[MLA_WIDTH:, :], preferred_element_type=f32))
    m = mod_ref[0]
    x1 = x_ref[...] + m[2:3] * (_rms(mix) * gpost_ref[...])
    xmid_ref[...] = x1
    h2 = _rms(x1) * gpre_ref[...] * (1.0 + m[4:5]) + m[3:4]
    h_hi = h2.astype(bf16)
    half = h2.shape[1] // 2
    h2p_ref[...] = _pack_pairs(h_hi[:, :half], h_hi[:, half:])
    h_lo = (h2 - h_hi.astype(f32)).astype(bf16)
    logits = (jnp.dot(h_hi, wr_ref[0], preferred_element_type=f32)
              + jnp.dot(h_lo, wr_ref[0], preferred_element_type=f32)
              + jnp.dot(h_hi, wr_ref[1], preferred_element_type=f32)) + br_ref[...]
    tm = logits.shape[0]
    col = lax.broadcasted_iota(jnp.int32, logits.shape, 1).astype(f32)
    vals, idxs = [], []
    for _ in range(TOP_K):
        mx = jnp.max(logits, axis=-1, keepdims=True)
        ix = jnp.min(jnp.where(logits == mx, col, float(LANES)), axis=-1, keepdims=True)
        vals.append(mx)
        idxs.append(ix)
        logits = jnp.where(col == ix, -jnp.inf, logits)
    es = [jnp.exp(v - vals[0]) for v in vals]
    tot = es[0] + es[1] + es[2] + es[3]
    onehot = jnp.zeros_like(col)
    for k in range(TOP_K):
        onehot = jnp.where(col == idxs[k], 1.0, onehot)
    earlier = (lax.broadcasted_iota(jnp.int32, (tm, tm), 0) > lax.broadcasted_iota(jnp.int32, (tm, tm), 1))
    before = jnp.dot(jnp.where(earlier, 1.0, 0.0).astype(bf16), onehot.astype(bf16),
                     preferred_element_type=f32) + carry[...]
    idx_out = jnp.zeros_like(col)
    gate_out = jnp.zeros_like(col)
    rank_out = jnp.zeros_like(col)
    for k in range(TOP_K):
        slot = col == float(k)
        idx_out = jnp.where(slot, idxs[k], idx_out)
        gate_out = jnp.where(slot, es[k] / tot, gate_out)
        rank_k = jnp.sum(jnp.where(col == idxs[k], before, 0.0), axis=-1, keepdims=True)
        rank_out = jnp.where(slot, rank_k, rank_out)
    idx_ref[...] = idx_out.astype(jnp.int32)
    gate_ref[...] = gate_out
    rank_ref[...] = rank_out.astype(jnp.int32)
    carry[...] = carry[...] + jnp.sum(onehot, axis=0, keepdims=True)
    cnt_ref[...] = carry[...]


def _mixout(attn, rn, x2d, mod, g_post, g_pre, w_out16, wr, br, counts0, seq, tm):
    n, d = x2d.shape
    per_seq = seq // tm
    row = lambda w: pl.BlockSpec((tm, w), lambda i: (i, 0))
    vmem = w_out16.size * 2 + 2 * tm * d * (4 + 4 + 2 + 2) + 6 * tm * d * 4 + (8 << 20)
    return pl.pallas_call(
        _mixout_kernel,
        grid=(n // tm,),
        in_specs=[row(MLA_WIDTH), row(RET_WIDTH), row(d),
                  pl.BlockSpec((1, 6, d), lambda i: (i // per_seq, 0, 0)),
                  _const_spec((1, d)), _const_spec((1, d)), _const_spec(w_out16.shape),
                  _const_spec(wr.shape), _const_spec((1, LANES)), _const_spec((1, LANES))],
        out_specs=[row(d), row(d // 2), row(LANES), row(LANES), row(LANES),
                   pl.BlockSpec((1, LANES), lambda i: (0, 0))],
        out_shape=[jax.ShapeDtypeStruct((n, d), f32), jax.ShapeDtypeStruct((n, d // 2), jnp.uint32),
                   jax.ShapeDtypeStruct((n, LANES), jnp.int32), jax.ShapeDtypeStruct((n, LANES), f32),
                   jax.ShapeDtypeStruct((n, LANES), jnp.int32), jax.ShapeDtypeStruct((1, LANES), f32)],
        scratch_shapes=[pltpu.VMEM((1, LANES), f32)],
        compiler_params=_params(("arbitrary",), vmem),
        name="mixer_out_router",
    )(attn, rn, x2d, mod, g_post.reshape(1, d), g_pre.reshape(1, d), w_out16, wr, br, counts0)


def _route(top_idx, rank, counts):
    counts = counts[0, :N_EXPERTS].astype(jnp.int32)
    padded = (counts + MOE_ROWS - 1) // MOE_ROWS * MOE_ROWS
    pad_end = jnp.cumsum(padded).astype(jnp.int32)
    pad_start = pad_end - padded
    dest = pad_start[top_idx] + rank
    return dest, pad_start, pad_end, padded // MOE_ROWS


def _n_sorted_rows(n_tokens):
    return (n_tokens * TOP_K + N_EXPERTS * (MOE_ROWS - 1)) // MOE_ROWS * MOE_ROWS


def _fill_unused_chunks(zero_buf, dst_rows, sem, used_rows, total_rows):
    first = used_rows // MOE_ROWS

    def copy(c):
        return pltpu.make_async_copy(zero_buf, dst_rows(pl.ds(pl.multiple_of(c * MOE_ROWS, MOE_ROWS), MOE_ROWS)), sem)

    def start(c, carry):
        copy(c).start()
        return carry

    def wait(c, carry):
        copy(c).wait()
        return carry

    lax.fori_loop(first, total_rows // MOE_ROWS, start, 0)
    lax.fori_loop(first, total_rows // MOE_ROWS, wait, 0)


def _dispatch_kernel(pe_ref, nc_ref, dest_ref, h2p_hbm, x_hbm, zbuf, sem, zsem, *, n_steps, tokens):
    i = pl.program_id(0)
    rows = tokens * TOP_K

    @pl.when(i == 0)
    def _():
        zbuf[...] = jnp.zeros_like(zbuf)

        def tail(e):
            start = pl.multiple_of(pe_ref[e] - MOE_ROWS, MOE_ROWS)
            return pltpu.make_async_copy(zbuf, x_hbm.at[pl.ds(start, MOE_ROWS), :], zsem)

        def start_tail(e, c):
            @pl.when(nc_ref[e] > 0)
            def _():
                tail(e).start()
            return c

        def wait_tail(e, c):
            @pl.when(nc_ref[e] > 0)
            def _():
                tail(e).wait()
            return c

        lax.fori_loop(0, N_EXPERTS, start_tail, 0)
        lax.fori_loop(0, N_EXPERTS, wait_tail, 0)
        _fill_unused_chunks(zbuf, lambda rows: x_hbm.at[rows, :], zsem, pe_ref[N_EXPERTS - 1], x_hbm.shape[0])

    def body(r, c):
        d = dest_ref[0, 0, r]
        t = i * tokens + r // TOP_K
        pltpu.make_async_copy(h2p_hbm.at[pl.ds(t, 1), :], x_hbm.at[pl.ds(d, 1), :], sem.at[i % 2]).start()
        return c

    lax.fori_loop(0, rows, body, 0, unroll=8)

    def wait_step(slot):
        pltpu.make_async_copy(h2p_hbm.at[pl.ds(0, rows), :], x_hbm.at[pl.ds(0, rows), :], sem.at[slot]).wait()

    @pl.when(i > 0)
    def _():
        wait_step((i + 1) % 2)

    @pl.when(i == n_steps - 1)
    def _():
        wait_step(i % 2)


def _dispatch(pad_end, n_chunks, dest, h2p):
    n, w = h2p.shape
    tokens = 256
    n_steps = n // tokens
    grid_spec = pltpu.PrefetchScalarGridSpec(
        num_scalar_prefetch=2,
        grid=(n_steps,),
        in_specs=[pl.BlockSpec((1, 1, tokens * TOP_K), lambda i, pe, nc: (i, 0, 0), memory_space=pltpu.SMEM),
                  pl.BlockSpec(memory_space=pl.ANY)],
        out_specs=pl.BlockSpec(memory_space=pl.ANY),
        scratch_shapes=[pltpu.VMEM((MOE_ROWS, w), jnp.uint32), pltpu.SemaphoreType.DMA((2,)),
                        pltpu.SemaphoreType.DMA(())],
    )
    return pl.pallas_call(
        functools.partial(_dispatch_kernel, n_steps=n_steps, tokens=tokens),
        grid_spec=grid_spec,
        out_shape=jax.ShapeDtypeStruct((_n_sorted_rows(n), w), jnp.uint32),
        compiler_params=_params(("arbitrary",), 16 << 20),
        name="moe_dispatch",
    )(pad_end, n_chunks, dest.reshape(n_steps, 1, tokens * TOP_K), h2p)


def _chunk_rows(r0, c):
    return pl.ds(pl.multiple_of(r0 + c * MOE_ROWS, MOE_ROWS), MOE_ROWS)


def _moe_up_kernel(rs_ref, nc_ref, x_hbm, wg_ref, wu_ref, bg_ref, bu_ref, act_hbm,
                   wg16, wu16, xbuf, obuf, sem_in, sem_out):
    e = pl.program_id(0)
    j = pl.program_id(1)
    n = nc_ref[e]
    r0 = rs_ref[e]

    def in_copy(c, slot):
        return pltpu.make_async_copy(x_hbm.at[_chunk_rows(r0, c), :], xbuf.at[slot], sem_in.at[slot])

    def out_copy(c, slot):
        return pltpu.make_async_copy(obuf.at[slot], act_hbm.at[j, _chunk_rows(r0, c), :], sem_out.at[slot])

    @pl.when(n > 0)
    def _():
        in_copy(0, 0).start()
        wg16[...] = wg_ref[0].astype(bf16)
        wu16[...] = wu_ref[0].astype(bf16)
        half = wg16.shape[0] // 2

        def body(c, carry):
            slot = c % 2

            @pl.when(c + 1 < n)
            def _():
                in_copy(c + 1, 1 - slot).start()

            in_copy(c, slot).wait()
            lo, hi = _unpack_pairs(xbuf[slot])
            x_lo = lo.astype(bf16)
            x_hi = hi.astype(bf16)

            def proj(w16, b_ref):
                return (jnp.dot(x_lo, w16[0:half, :], preferred_element_type=f32)
                        + jnp.dot(x_hi, w16[half:, :], preferred_element_type=f32) + b_ref[0])

            gate = jnp.minimum(proj(wg16, bg_ref), SWIGLU_LIMIT)
            up = jnp.clip(proj(wu16, bu_ref), -SWIGLU_LIMIT, SWIGLU_LIMIT)
            act = (up + 1.0) * gate * jax.nn.sigmoid(SWIGLU_ALPHA * gate)

            @pl.when(c >= 2)
            def _():
                out_copy(c - 2, slot).wait()

            obuf[slot] = act.astype(bf16)
            out_copy(c, slot).start()
            return carry

        lax.fori_loop(0, n, body, 0)

        @pl.when(n >= 2)
        def _():
            out_copy(n - 2, n % 2).wait()

        out_copy(n - 1, (n - 1) % 2).wait()

    @pl.when(e == N_EXPERTS - 1)
    def _():
        obuf[0] = jnp.zeros(obuf.shape[1:], obuf.dtype)
        _fill_unused_chunks(obuf.at[0], lambda rows: act_hbm.at[j, rows, :], sem_out.at[0],
                            r0 + n * MOE_ROWS, act_hbm.shape[1])


def _moe_up(row_start, n_chunks, x_sorted, w_gate_up, b_gate_up):
    e, d, f2 = w_gate_up.shape
    tf = 1024
    n_j = D_FF // tf
    rows = x_sorted.shape[0]
    vmem = 2 * 2 * d * tf * 4 + 2 * d * tf * 2 + 2 * MOE_ROWS * d * 2 + 2 * MOE_ROWS * tf * 2 \
        + 6 * MOE_ROWS * tf * 4 + (6 << 20)
    grid_spec = pltpu.PrefetchScalarGridSpec(
        num_scalar_prefetch=2,
        grid=(e, n_j),
        in_specs=[pl.BlockSpec(memory_space=pl.ANY),
                  pl.BlockSpec((1, d, tf), lambda i, j, rs, nc: (i, 0, j)),
                  pl.BlockSpec((1, d, tf), lambda i, j, rs, nc: (i, 0, n_j + j)),
                  pl.BlockSpec((1, 1, tf), lambda i, j, rs, nc: (i, 0, j)),
                  pl.BlockSpec((1, 1, tf), lambda i, j, rs, nc: (i, 0, n_j + j))],
        out_specs=pl.BlockSpec(memory_space=pl.ANY),
        scratch_shapes=[pltpu.VMEM((d, tf), bf16), pltpu.VMEM((d, tf), bf16),
                        pltpu.VMEM((2, MOE_ROWS, d // 2), jnp.uint32), pltpu.VMEM((2, MOE_ROWS, tf), bf16),
                        pltpu.SemaphoreType.DMA((2,)), pltpu.SemaphoreType.DMA((2,))],
    )
    b3 = b_gate_up.reshape(e, 1, f2)
    return pl.pallas_call(
        _moe_up_kernel,
        grid_spec=grid_spec,
        out_shape=jax.ShapeDtypeStruct((n_j, rows, tf), bf16),
        compiler_params=_params(("arbitrary", "arbitrary"), vmem),
        name="moe_gate_up",
    )(row_start, n_chunks, x_sorted, w_gate_up, w_gate_up, b3, b3)


def _moe_down_kernel(rs_ref, nc_ref, act_hbm, wd_ref, bd_ref, y_hbm, wd16, abuf, ybuf, sem_in, sem_out):
    e = pl.program_id(0)
    n = nc_ref[e]
    r0 = rs_ref[e]
    n_k, tf = abuf.shape[1], abuf.shape[3]

    def in_copies(c, slot):
        return [pltpu.make_async_copy(act_hbm.at[k, _chunk_rows(r0, c), :], abuf.at[slot, k], sem_in.at[slot])
                for k in range(n_k)]

    def out_copy(c, slot):
        return pltpu.make_async_copy(ybuf.at[slot], y_hbm.at[_chunk_rows(r0, c), :], sem_out.at[slot])

    @pl.when(n > 0)
    def _():
        for cp in in_copies(0, 0):
            cp.start()
        wd16[...] = wd_ref[0].astype(bf16)

        def body(c, carry):
            slot = c % 2

            @pl.when(c + 1 < n)
            def _():
                for cp in in_copies(c + 1, 1 - slot):
                    cp.start()

            for cp in in_copies(c, slot):
                cp.wait()
            y = bd_ref[0]
            for k in range(n_k):
                y = y + jnp.dot(abuf[slot, k], wd16[k * tf:(k + 1) * tf, :], preferred_element_type=f32)
            y16 = y.astype(bf16)
            half = y16.shape[1] // 2

            @pl.when(c >= 2)
            def _():
                out_copy(c - 2, slot).wait()

            ybuf[slot] = _pack_pairs(y16[:, :half], y16[:, half:])
            out_copy(c, slot).start()
            return carry

        lax.fori_loop(0, n, body, 0)

        @pl.when(n >= 2)
        def _():
            out_copy(n - 2, n % 2).wait()

        out_copy(n - 1, (n - 1) % 2).wait()

    @pl.when(e == N_EXPERTS - 1)
    def _():
        ybuf[0] = jnp.zeros(ybuf.shape[1:], ybuf.dtype)
        _fill_unused_chunks(ybuf.at[0], lambda rows: y_hbm.at[rows, :], sem_out.at[0],
                            r0 + n * MOE_ROWS, y_hbm.shape[0])


def _moe_down(row_start, n_chunks, act, w_down, b_down):
    e, f, d = w_down.shape
    n_k, rows, tf = act.shape
    vmem = 2 * f * d * 4 + f * d * 2 + 2 * MOE_ROWS * f * 2 + 2 * MOE_ROWS * d * 2 + 4 * MOE_ROWS * d * 4 + (4 << 20)
    grid_spec = pltpu.PrefetchScalarGridSpec(
        num_scalar_prefetch=2,
        grid=(e,),
        in_specs=[pl.BlockSpec(memory_space=pl.ANY),
                  pl.BlockSpec((1, f, d), lambda i, rs, nc: (i, 0, 0)),
                  pl.BlockSpec((1, 1, d), lambda i, rs, nc: (i, 0, 0))],
        out_specs=pl.BlockSpec(memory_space=pl.ANY),
        scratch_shapes=[pltpu.VMEM((f, d), bf16), pltpu.VMEM((2, n_k, MOE_ROWS, tf), bf16),
                        pltpu.VMEM((2, MOE_ROWS, d // 2), jnp.uint32),
                        pltpu.SemaphoreType.DMA((2,)), pltpu.SemaphoreType.DMA((2,))],
    )
    return pl.pallas_call(
        _moe_down_kernel,
        grid_spec=grid_spec,
        out_shape=jax.ShapeDtypeStruct((rows, d // 2), jnp.uint32),
        compiler_params=_params(("arbitrary",), vmem),
        name="moe_down",
    )(row_start, n_chunks, act, w_down, b_down.reshape(e, 1, d))


def _gather_rows(idx_ref, src_hbm, dst, sem, n_rows):
    def body(r, carry):
        t = idx_ref[0, 0, r]
        pltpu.make_async_copy(src_hbm.at[pl.ds(t, 1), :], dst.at[pl.ds(r, 1), :], sem).start()
        return carry
    lax.fori_loop(0, n_rows, body, 0, unroll=8)


def _wait_rows(src_hbm, dst, sem, n_rows):
    pltpu.make_async_copy(src_hbm.at[pl.ds(0, n_rows), :], dst, sem).wait()


def _prefetched_gather(i, n_live, cur_idx_ref, next_idx_ref, src_hbm, buf, sem, n_rows):
    slot = i % 2

    @pl.when(jnp.logical_and(i == 0, n_live > 0))
    def _():
        _gather_rows(cur_idx_ref, src_hbm, buf.at[0], sem.at[0], n_rows)

    @pl.when(i + 1 < n_live)
    def _():
        _gather_rows(next_idx_ref, src_hbm, buf.at[1 - slot], sem.at[1 - slot], n_rows)

    @pl.when(i < n_live)
    def _():
        _wait_rows(src_hbm, buf.at[slot], sem.at[slot], n_rows)

    return slot


def _combine_kernel(pos_ref, posn_ref, y_hbm, gates_ref, xmid_ref, mod_ref, gpost_ref, out_ref, ybuf, sem, *, n_steps):
    i = pl.program_id(0)
    tm = COMBINE_TOKENS
    slot = _prefetched_gather(i, n_steps, pos_ref, posn_ref, y_hbm, ybuf, sem, TOP_K * tm)
    g = gates_ref[...]
    lo = hi = None
    for k in range(TOP_K):
        l_k, h_k = _unpack_pairs(ybuf[slot, k * tm:(k + 1) * tm, :])
        gk = g[:, k:k + 1]
        lo = gk * l_k if lo is None else lo + gk * l_k
        hi = gk * h_k if hi is None else hi + gk * h_k
    half = lo.shape[1]
    inv = lax.rsqrt((jnp.sum(lo * lo, axis=-1, keepdims=True) + jnp.sum(hi * hi, axis=-1, keepdims=True))
                    / (2 * half) + EPS)
    gate2 = mod_ref[0][5:6]
    gp = gpost_ref[...]
    out_ref[:, :half] = xmid_ref[:, :half] + gate2[:, :half] * (lo * inv * gp[:, :half])
    out_ref[:, half:] = xmid_ref[:, half:] + gate2[:, half:] * (hi * inv * gp[:, half:])


def _combine(pos, y_sorted, gates, xmid, mod, g_post, seq):
    n, d = xmid.shape
    tm = COMBINE_TOKENS
    n_tiles = n // tm
    per_seq = seq // tm
    vmem = 2 * TOP_K * tm * d * 2 + 4 * tm * d * 4 + 6 * tm * d * 4 + (8 << 20)
    return pl.pallas_call(
        functools.partial(_combine_kernel, n_steps=n_tiles),
        grid=(n_tiles,),
        in_specs=[pl.BlockSpec((1, 1, TOP_K * tm), lambda i: (i, 0, 0), memory_space=pltpu.SMEM),
                  pl.BlockSpec((1, 1, TOP_K * tm), lambda i: (jnp.minimum(i + 1, n_tiles - 1), 0, 0),
                               memory_space=pltpu.SMEM),
                  pl.BlockSpec(memory_space=pl.ANY),
                  pl.BlockSpec((tm, TOP_K), lambda i: (i, 0)),
                  pl.BlockSpec((tm, d), lambda i: (i, 0)),
                  pl.BlockSpec((1, 6, d), lambda i: (i // per_seq, 0, 0)),
                  _const_spec((1, d))],
        out_specs=pl.BlockSpec((tm, d), lambda i: (i, 0)),
        out_shape=jax.ShapeDtypeStruct((n, d), f32),
        scratch_shapes=[pltpu.VMEM((2, TOP_K * tm, d // 2), jnp.uint32), pltpu.SemaphoreType.DMA((2,))],
        compiler_params=_params(("arbitrary",), vmem),
        name="moe_combine",
    )(pos, pos, y_sorted, gates, xmid, mod, g_post.reshape(1, d))


def _rope_tables(pos):
    def cs(d):
        half = d // 2
        inv_freq = 1.0 / (ROPE_THETA ** (jnp.arange(half, dtype=f32) * (2.0 / d)))
        ang = pos.astype(f32)[:, None] * inv_freq[None, :]
        return jnp.cos(ang), jnp.sin(ang)
    c64, s64 = cs(MLA_ROPE)
    c256, s256 = cs(RET_QK)
    t64 = jnp.concatenate([c64, c64, s64, s64], axis=-1)
    t256 = jnp.concatenate([c256, s256], axis=-1)
    return t64, t256


def _rot_cols(w):
    half = w.shape[-1] // 2
    return jnp.concatenate([-w[..., half:], w[..., :half]], axis=-1)


def _prep_weights(w_in, w_uq, w_ukv, w_out, w_router, b_router):
    b = np.cumsum((Q_LORA, KV_LORA, MLA_ROPE))
    w_pe = w_in[:, b[1]:b[2]]
    w_ext = jnp.concatenate([w_in[:, :b[2]], _rot_cols(w_pe), w_in[:, b[2]:]], axis=-1).astype(bf16)
    wq = w_uq.reshape(Q_LORA, MLA_HEADS, MLA_NOPE + MLA_ROPE)
    wq = jnp.concatenate([wq, _rot_cols(wq[..., MLA_NOPE:])], axis=-1).transpose(1, 0, 2).astype(bf16)
    wkv = w_ukv.reshape(KV_LORA, MLA_HEADS, MLA_NOPE + MLA_V).transpose(1, 0, 2)
    wk = wkv[..., :MLA_NOPE].astype(bf16)
    wv = wkv[..., MLA_NOPE:].astype(bf16)
    r_hi = w_router.astype(bf16)
    r_lo = (w_router - r_hi.astype(f32)).astype(bf16)
    pad = ((0, 0), (0, LANES - N_EXPERTS))
    wr = jnp.stack([jnp.pad(r_hi, pad), jnp.pad(r_lo, pad)])
    br = jnp.pad(b_router, (0, LANES - N_EXPERTS), constant_values=NEG_BIG).reshape(1, LANES)
    return w_ext, wq, wk, wv, w_out.astype(bf16), wr, br


def _mixer_half(x, mod, pos, wts, p, past_lat16, past_kpe16, state0, counts0, ret_chunk, q_tiles, tm):
    w_ext, wq, wk, wv, w_out16, wr, br = wts
    b, s, d = x.shape
    x2d = x.reshape(b * s, d)
    t64, t256 = _rope_tables(pos)
    qlat, kvlat, kvlat16, kpe, kpe16, rq, rk, rv, rg = _inproj(
        x2d, mod, p['g_pre_mix'], w_ext, p['g_q_lat'], p['g_kv_lat'], t64, t256, s, tm)
    lat16 = kvlat16.reshape(b, s, KV_LORA)
    kpe16 = kpe16.reshape(b, s, 2 * MLA_ROPE)
    if past_lat16 is not None:
        lat16 = jnp.concatenate([past_lat16, lat16], axis=1)
        kpe16 = jnp.concatenate([past_kpe16, kpe16], axis=1)
    attn = _mla(qlat.reshape(b, s, Q_LORA), lat16, kpe16, wq, wk, wv, t64, q_tiles)
    log_g = jnp.log1p(-jnp.exp2(-5.0 - jnp.arange(RET_HEADS, dtype=f32)))
    log_g = jnp.broadcast_to(log_g[:, None, None], (RET_HEADS, 8, LANES))
    sh = lambda t, w: t.reshape(b, s, w)
    rn, state = _retention(sh(rq, RET_HEADS * RET_QK), sh(rk, RET_HEADS * RET_QK), sh(rv, RET_WIDTH),
                           sh(rg, RET_WIDTH), p['g_ret'], state0, log_g, ret_chunk)
    xmid, h2p, idx, gates, rank, counts = _mixout(
        attn.reshape(b * s, MLA_WIDTH), rn.reshape(b * s, RET_WIDTH), x2d, mod,
        p['g_post_mix'], p['g_pre_ffn'], w_out16, wr, br, counts0, s, tm)
    route = (idx[:, :TOP_K], gates[:, :TOP_K], rank[:, :TOP_K], counts)
    return kvlat.reshape(b, s, KV_LORA), kpe.reshape(b, s, MLA_ROPE), state, xmid, h2p, route


def _combine_order(dest):
    tm = COMBINE_TOKENS
    n_tiles = dest.shape[0] // tm
    return dest.reshape(n_tiles, tm, TOP_K).transpose(0, 2, 1).reshape(n_tiles, 1, TOP_K * tm)


def kernel(x_prompt, x_sample, cache_kv_latent, cache_k_rope, state_retention, c_prompt, c_sample, w_ada, b_ada, g_pre_mix, g_post_mix, g_pre_ffn, g_post_ffn, w_in, g_q_lat, g_kv_lat, w_uq, w_ukv, g_ret, w_out, w_router, b_router, w_gate_up, b_gate_up, w_down, b_down):
    depth = w_in.shape[0]
    assert depth == 1, "the staged problem has a single layer"
    bp, sp, d = x_prompt.shape
    bs, ss, _ = x_sample.shape
    past = cache_kv_latent.shape[2]
    l = 0
    p = dict(g_pre_mix=g_pre_mix[l], g_post_mix=g_post_mix[l], g_pre_ffn=g_pre_ffn[l], g_post_ffn=g_post_ffn[l],
             g_q_lat=g_q_lat[l], g_kv_lat=g_kv_lat[l], g_ret=g_ret[l])
    wts = _prep_weights(w_in[l], w_uq[l], w_ukv[l], w_out[l], w_router[l], b_router[l])

    mod = _ada(jnp.concatenate([c_prompt, c_sample], axis=0), w_ada[l], b_ada[l]).reshape(bp + bs, 6, d)
    mod_p, mod_s = mod[:bp], mod[bp:]

    tq = ATTN_Q_TILE
    tiles_p = tuple((q0, tq, q0 + tq, True) for q0 in range(0, sp, tq))
    zero_state = jnp.zeros((bp, RET_HEADS, RET_QK, RET_V), f32)
    lat_p, kpe_p, st_p, xmid_p, h2p_p, (idx_p, gates_p, rank_p, counts_p) = _mixer_half(
        x_prompt, mod_p, jnp.arange(sp), wts, p, None, None, zero_state, jnp.zeros((1, LANES), f32),
        RET_CHUNK_PROMPT, tiles_p, 512)

    past_lat16 = cache_kv_latent[l].astype(bf16)
    past_kpe16 = jnp.pad(cache_k_rope[l], ((0, 0), (0, 0), (0, MLA_ROPE))).astype(bf16)
    tiles_s = ((0, ss, past + ss, False),)
    lat_s, kpe_s, st_s, xmid_s, h2p_s, (idx_s, gates_s, rank_s, counts) = _mixer_half(
        x_sample, mod_s, past + jnp.arange(ss), wts, p, past_lat16, past_kpe16, state_retention[l], counts_p,
        ss, tiles_s, ss)

    n_p = bp * sp
    h2p = jnp.concatenate([h2p_p, h2p_s], axis=0)
    dest, row_start, pad_end, n_chunks = _route(
        jnp.concatenate([idx_p, idx_s], axis=0), jnp.concatenate([rank_p, rank_s], axis=0), counts)
    x_sorted = _dispatch(pad_end, n_chunks, dest, h2p)
    act = _moe_up(row_start, n_chunks, x_sorted, w_gate_up[l], b_gate_up[l])
    y_sorted = _moe_down(row_start, n_chunks, act, w_down[l], b_down[l])
    y_p = _combine(_combine_order(dest[:n_p]), y_sorted, gates_p, xmid_p, mod_p, p['g_post_ffn'], sp)
    y_s = _combine(_combine_order(dest[n_p:]), y_sorted, gates_s, xmid_s, mod_s, p['g_post_ffn'], ss)

    return (y_p.reshape(bp, sp, d), y_s.reshape(bs, ss, d),
            lat_p[None], kpe_p[None], st_p[None].astype(state_retention.dtype),
            lat_s[None], kpe_s[None], st_s[None].astype(state_retention.dtype))
```

```python
import functools

import numpy as np
import jax
import jax.numpy as jnp
from jax import lax
from jax.experimental import pallas as pl
from jax.experimental.pallas import tpu as pltpu

CHUNK = 64
EPS = 1e-6
ROPE_THETA = 10000.0

MLA_HEADS = 8
MLA_NOPE = 128
MLA_ROPE = 64
MLA_V = 128
Q_LORA = 512
KV_LORA = 512
MLA_WIDTH = MLA_HEADS * MLA_V

RET_HEADS = 4
RET_QK = 256
RET_V = 256
RET_WIDTH = RET_HEADS * RET_V

N_EXPERTS = 32
TOP_K = 4
D_FF = 2048
SWIGLU_LIMIT = 7.0
SWIGLU_ALPHA = 1.702

LANES = 128
V7X_VMEM_BYTES = 64 * 1024 * 1024
VMEM_CAP_BYTES = V7X_VMEM_BYTES - 8 * 1024 * 1024

RET_CHUNK_PROMPT = 128
RET_UNROLL = 4
ATTN_Q_TILE = 256
MOE_ROWS = 256
COMBINE_TOKENS = 64
NEG_BIG = -1e30

bf16 = jnp.bfloat16
f32 = jnp.float32


def _params(sem, vmem_bytes):
    return pltpu.CompilerParams(dimension_semantics=sem, vmem_limit_bytes=int(min(vmem_bytes, VMEM_CAP_BYTES)))


def _const_spec(shape):
    nd = len(shape)
    return pl.BlockSpec(shape, lambda *_: (0,) * nd, pipeline_mode=pl.Buffered(1))


def _rms(x):
    return x * lax.rsqrt(jnp.mean(x * x, axis=-1, keepdims=True) + EPS)


def _silu(x):
    return x * jax.nn.sigmoid(x)


def _ada_kernel(c_ref, w_ref, b_ref, o_ref):
    s = _silu(c_ref[...]).astype(bf16)
    o_ref[...] = jnp.dot(s, w_ref[...].astype(bf16), preferred_element_type=f32) + b_ref[...]


def _ada(c, w_ada, b_ada):
    nb, d = c.shape
    n = w_ada.shape[1]
    tn = 1536
    return pl.pallas_call(
        _ada_kernel,
        grid=(n // tn,),
        in_specs=[pl.BlockSpec((nb, d), lambda j: (0, 0)),
                  pl.BlockSpec((d, tn), lambda j: (0, j)),
                  pl.BlockSpec((1, tn), lambda j: (0, j))],
        out_specs=pl.BlockSpec((nb, tn), lambda j: (0, j)),
        out_shape=jax.ShapeDtypeStruct((nb, n), f32),
        compiler_params=_params(("arbitrary",), 2 * d * tn * 4 + d * tn * 2 + (8 << 20)),
        name="ada_modulation",
    )(c, w_ada, b_ada.reshape(1, n))


_C_Q = 0
_C_KV = _C_Q + Q_LORA
_C_PE = _C_KV + KV_LORA
_C_RQ = _C_PE + 2 * MLA_ROPE
_C_RK = _C_RQ + RET_HEADS * RET_QK
_C_RV = _C_RK + RET_HEADS * RET_QK
_C_RG = _C_RV + RET_WIDTH
_C_END = _C_RG + RET_WIDTH


def _rope_pair(t):
    return t + pltpu.roll(t, MLA_ROPE, axis=1)


def _rope_heads(z, cos, sin, scale):
    outs = []
    half = RET_QK // 2
    for h in range(RET_HEADS):
        x1 = z[:, h * RET_QK:h * RET_QK + half]
        x2 = z[:, h * RET_QK + half:(h + 1) * RET_QK]
        outs.append((x1 * cos - x2 * sin) * scale)
        outs.append((x1 * sin + x2 * cos) * scale)
    return jnp.concatenate(outs, axis=-1)


def _inproj_kernel(x_ref, mod_ref, g_ref, w_ref, gq_ref, gkv_ref, t64_ref, t256_ref,
                   qlat_ref, kvlat_ref, kvlat16_ref, kpe_ref, kpe16_ref, rq_ref, rk_ref, rv_ref, rg_ref):
    m = mod_ref[0]
    h = (_rms(x_ref[...]) * g_ref[...] * (1.0 + m[1:2]) + m[0:1]).astype(bf16)

    def proj(c0, c1):
        return jnp.dot(h, w_ref[:, c0:c1], preferred_element_type=f32)

    qlat_ref[...] = (_rms(proj(_C_Q, _C_KV)) * gq_ref[...]).astype(bf16)
    kv = _rms(proj(_C_KV, _C_PE)) * gkv_ref[...]
    kvlat_ref[...] = kv
    kvlat16_ref[...] = kv.astype(bf16)
    pe = _rope_pair(proj(_C_PE, _C_RQ) * t64_ref[...])
    kpe_ref[...] = pe[:, :MLA_ROPE]
    lane = lax.broadcasted_iota(jnp.int32, pe.shape, 1)
    kpe16_ref[...] = jnp.where(lane < MLA_ROPE, pe, 0.0).astype(bf16)
    cos = t256_ref[:, :RET_QK // 2]
    sin = t256_ref[:, RET_QK // 2:]
    rq_ref[...] = _rope_heads(proj(_C_RQ, _C_RK), cos, sin, 1.0).astype(bf16)
    rk_ref[...] = _rope_heads(proj(_C_RK, _C_RV), cos, sin, RET_QK ** -0.5).astype(bf16)
    rv_ref[...] = proj(_C_RV, _C_RG).astype(bf16)
    rg_ref[...] = proj(_C_RG, _C_END).astype(bf16)


def _inproj(x2d, mod, g_pre, w_ext, g_q, g_kv, t64, t256, seq, tm):
    n, d = x2d.shape
    per_seq = seq // tm
    row = lambda w: pl.BlockSpec((tm, w), lambda i: (i, 0))
    tab = lambda w: pl.BlockSpec((tm, w), lambda i: (i % per_seq, 0))
    outs = [(Q_LORA, bf16), (KV_LORA, f32), (KV_LORA, bf16), (MLA_ROPE, f32), (2 * MLA_ROPE, bf16),
            (RET_HEADS * RET_QK, bf16), (RET_HEADS * RET_QK, bf16), (RET_WIDTH, bf16), (RET_WIDTH, bf16)]
    vmem = w_ext.size * 2 + 2 * tm * d * 4 + 4 * tm * d * 4 + sum(2 * tm * w * 4 for w, _ in outs) + (6 << 20)
    return pl.pallas_call(
        _inproj_kernel,
        grid=(n // tm,),
        in_specs=[row(d),
                  pl.BlockSpec((1, 6, d), lambda i: (i // per_seq, 0, 0)),
                  _const_spec((1, d)), _const_spec(w_ext.shape), _const_spec((1, Q_LORA)), _const_spec((1, KV_LORA)),
                  tab(2 * MLA_ROPE), tab(RET_QK)],
        out_specs=[row(w) for w, _ in outs],
        out_shape=[jax.ShapeDtypeStruct((n, w), dt) for w, dt in outs],
        compiler_params=_params(("arbitrary",), vmem),
        name="in_projection",
    )(x2d, mod, g_pre.reshape(1, d), w_ext, g_q.reshape(1, Q_LORA), g_kv.reshape(1, KV_LORA), t64, t256)


def _mla_kernel(qlat_ref, lat_ref, kpe_ref, wq_ref, wk_ref, wv_ref, tab_ref, o_ref, q_s, k_s, v_s, *, q_tiles):
    scale = (MLA_NOPE + MLA_ROPE) ** -0.5
    qh = jnp.dot(qlat_ref[0], wq_ref[0], preferred_element_type=f32)
    q_s[:, :MLA_NOPE] = (qh[:, :MLA_NOPE] * scale).astype(bf16)
    q_s[:, MLA_NOPE:] = (_rope_pair(qh[:, MLA_NOPE:] * tab_ref[...]) * scale).astype(bf16)
    lat = lat_ref[0]
    k_s[:, :MLA_NOPE] = jnp.dot(lat, wk_ref[0], preferred_element_type=f32).astype(bf16)
    k_s[:, MLA_NOPE:] = kpe_ref[0]
    v_s[...] = jnp.dot(lat, wv_ref[0], preferred_element_type=f32).astype(bf16)
    for q0, ql, kvl, masked in q_tiles:
        q = q_s[q0:q0 + ql, :]
        s = lax.dot_general(q, k_s[0:kvl, :], (((1,), (1,)), ((), ())), preferred_element_type=f32)
        if masked:
            qc = (q0 + lax.broadcasted_iota(jnp.int32, (ql, 1), 0)) // CHUNK
            kc = lax.broadcasted_iota(jnp.int32, (1, kvl), 1) // CHUNK
            s = jnp.where(kc <= qc, s, NEG_BIG)
        p = jnp.exp(s - jnp.max(s, axis=-1, keepdims=True))
        l = jnp.sum(p, axis=-1, keepdims=True)
        o = jnp.dot(p.astype(bf16), v_s[0:kvl, :], preferred_element_type=f32)
        o_ref[0, q0:q0 + ql, :] = (o / l).astype(bf16)


def _mla(qlat, lat, kpe16, wq, wk, wv, tab, q_tiles):
    b, sq, _ = qlat.shape
    skv = lat.shape[1]
    kern = functools.partial(_mla_kernel, q_tiles=q_tiles)
    max_ql = max(t[1] for t in q_tiles)
    vmem = 2 * 2 * (sq * Q_LORA + skv * KV_LORA + skv * LANES) + sq * LANES * 4 + sq * 3 * LANES * 4 \
        + (sq + skv) * 3 * LANES * 2 + 4 * max_ql * skv * 4 + (8 << 20)
    return pl.pallas_call(
        kern,
        grid=(b, MLA_HEADS),
        in_specs=[pl.BlockSpec((1, sq, Q_LORA), lambda i, h: (i, 0, 0)),
                  pl.BlockSpec((1, skv, KV_LORA), lambda i, h: (i, 0, 0)),
                  pl.BlockSpec((1, skv, 2 * MLA_ROPE), lambda i, h: (i, 0, 0)),
                  pl.BlockSpec((1, Q_LORA, MLA_NOPE + 2 * MLA_ROPE), lambda i, h: (h, 0, 0)),
                  pl.BlockSpec((1, KV_LORA, MLA_NOPE), lambda i, h: (h, 0, 0)),
                  pl.BlockSpec((1, KV_LORA, MLA_V), lambda i, h: (h, 0, 0)),
                  pl.BlockSpec((sq, 2 * MLA_ROPE), lambda i, h: (0, 0))],
        out_specs=pl.BlockSpec((1, sq, MLA_V), lambda i, h: (i, 0, h)),
        out_shape=jax.ShapeDtypeStruct((b, sq, MLA_WIDTH), bf16),
        scratch_shapes=[pltpu.VMEM((sq, MLA_NOPE + 2 * MLA_ROPE), bf16),
                        pltpu.VMEM((skv, MLA_NOPE + 2 * MLA_ROPE), bf16),
                        pltpu.VMEM((skv, MLA_V), bf16)],
        compiler_params=_params(("arbitrary", "arbitrary"), vmem),
        name="mla_attention",
    )(qlat, lat, kpe16, wq, wk, wv, tab)


def _ret_kernel(lg_ref, rq_ref, rk_ref, rv_ref, rg_ref, gret_ref, s0_ref, o_ref, sout_ref, state, *, chunk, n_chunks):
    c = chunk
    lg = lg_ref[0, 0:1, 0:1]
    ii = lax.broadcasted_iota(jnp.int32, (c, c), 0)
    jj = lax.broadcasted_iota(jnp.int32, (c, c), 1)
    diff = (ii - jj).astype(f32)
    dmask = jnp.where(diff >= 0.0, jnp.exp(jnp.maximum(diff, 0.0) * lg), 0.0)
    ic = lax.broadcasted_iota(jnp.int32, (c, 1), 0).astype(f32)
    q_decay = jnp.exp((ic + 1.0) * lg)
    k_decay = jnp.exp((c - 1.0 - ic) * lg)
    chunk_decay = jnp.exp(float(c) * lg)
    gret = gret_ref[...]
    state[...] = s0_ref[0, 0]

    def step(n, carry):
        sl = pl.ds(pl.multiple_of(n * c, c), c)
        q = rq_ref[0, sl, :]
        k = rk_ref[0, sl, :]
        v = rv_ref[0, sl, :]
        st = state[...]
        attn = lax.dot_general(q, k, (((1,), (1,)), ((), ())), preferred_element_type=f32) * dmask
        inner = jnp.dot(attn.astype(bf16), v, preferred_element_type=f32)
        cross = jnp.dot(q, st.astype(bf16), preferred_element_type=f32) * q_decay
        o = inner + cross
        kd_t = (k.astype(f32) * k_decay).T.astype(bf16)
        state[...] = st * chunk_decay + jnp.dot(kd_t, v, preferred_element_type=f32)
        mu = jnp.mean(o, axis=-1, keepdims=True)
        oc = o - mu
        rn = oc * lax.rsqrt(jnp.mean(oc * oc, axis=-1, keepdims=True) + EPS) * gret
        o_ref[0, sl, :] = (rn * _silu(rg_ref[0, sl, :].astype(f32))).astype(bf16)
        return carry

    lax.fori_loop(0, n_chunks, step, 0, unroll=min(n_chunks, RET_UNROLL))
    sout_ref[0, 0] = state[...]


def _retention(rq, rk, rv, rg, g_ret, state0, log_g, chunk):
    b, s, _ = rq.shape
    kern = functools.partial(_ret_kernel, chunk=chunk, n_chunks=s // chunk)
    head = lambda w: pl.BlockSpec((1, s, w), lambda i, h: (i, 0, h))
    st = pl.BlockSpec((1, 1, RET_QK, RET_V), lambda i, h: (i, h, 0, 0))
    vmem = 2 * 5 * s * RET_QK * 2 + 5 * RET_QK * RET_V * 4 + (8 << 20)
    return pl.pallas_call(
        kern,
        grid=(b, RET_HEADS),
        in_specs=[pl.BlockSpec((1, 8, LANES), lambda i, h: (h, 0, 0)),
                  head(RET_QK), head(RET_QK), head(RET_V), head(RET_V),
                  pl.BlockSpec((1, RET_V), lambda i, h: (0, h)), st],
        out_specs=[head(RET_V), st],
        out_shape=[jax.ShapeDtypeStruct((b, s, RET_WIDTH), bf16),
                   jax.ShapeDtypeStruct((b, RET_HEADS, RET_QK, RET_V), f32)],
        scratch_shapes=[pltpu.VMEM((RET_QK, RET_V), f32)],
        compiler_params=_params(("arbitrary", "arbitrary"), vmem),
        name="retention",
    )(log_g, rq, rk, rv, rg, g_ret.reshape(1, RET_WIDTH), state0)


def _pack_pairs(lo16, hi16):
    lo = lax.bitcast_convert_type(lo16.astype(f32), jnp.uint32)
    hi = lax.bitcast_convert_type(hi16.astype(f32), jnp.uint32)
    return (hi & jnp.uint32(0xFFFF0000)) | (lo >> 16)


def _unpack_pairs(u):
    lo = lax.bitcast_convert_type(u << 16, f32)
    hi = lax.bitcast_convert_type(u & jnp.uint32(0xFFFF0000), f32)
    return lo, hi


def _mixout_kernel(attn_ref, rn_ref, x_ref, mod_ref, gpost_ref, gpre_ref, wout_ref, wr_ref, br_ref, cnt0_ref,
                   xmid_ref, h2p_ref, idx_ref, gate_ref, rank_ref, cnt_ref, carry):
    @pl.when(pl.program_id(0) == 0)
    def _():
        carry[...] = cnt0_ref[...]

    mix = (jnp.dot(attn_ref[...], wout_ref[0:MLA_WIDTH, :], preferred_element_type=f32)
           + jnp.dot(rn_ref[...], wout_ref[MLA_WIDTH:, :], preferred_element_type=f32))
    m = mod_ref[0]
    x1 = x_ref[...] + m[2:3] * (_rms(mix) * gpost_ref[...])
    xmid_ref[...] = x1
    h2 = _rms(x1) * gpre_ref[...] * (1.0 + m[4:5]) + m[3:4]
    h_hi = h2.astype(bf16)
    half = h2.shape[1] // 2
    h2p_ref[...] = _pack_pairs(h_hi[:, :half], h_hi[:, half:])
    h_lo = (h2 - h_hi.astype(f32)).astype(bf16)
    logits = (jnp.dot(h_hi, wr_ref[0], preferred_element_type=f32)
              + jnp.dot(h_lo, wr_ref[0], preferred_element_type=f32)
              + jnp.dot(h_hi, wr_ref[1], preferred_element_type=f32)) + br_ref[...]
    tm = logits.shape[0]
    col = lax.broadcasted_iota(jnp.int32, logits.shape, 1).astype(f32)
    vals, idxs = [], []
    for _ in range(TOP_K):
        mx = jnp.max(logits, axis=-1, keepdims=True)
        ix = jnp.min(jnp.where(logits == mx, col, float(LANES)), axis=-1, keepdims=True)
        vals.append(mx)
        idxs.append(ix)
        logits = jnp.where(col == ix, -jnp.inf, logits)
    es = [jnp.exp(v - vals[0]) for v in vals]
    tot = es[0] + es[1] + es[2] + es[3]
    onehot = jnp.zeros_like(col)
    for k in range(TOP_K):
        onehot = jnp.where(col == idxs[k], 1.0, onehot)
    earlier = (lax.broadcasted_iota(jnp.int32, (tm, tm), 0) > lax.broadcasted_iota(jnp.int32, (tm, tm), 1))
    before = jnp.dot(jnp.where(earlier, 1.0, 0.0).astype(bf16), onehot.astype(bf16),
                     preferred_element_type=f32) + carry[...]
    idx_out = jnp.zeros_like(col)
    gate_out = jnp.zeros_like(col)
    rank_out = jnp.zeros_like(col)
    for k in range(TOP_K):
        slot = col == float(k)
        idx_out = jnp.where(slot, idxs[k], idx_out)
        gate_out = jnp.where(slot, es[k] / tot, gate_out)
        rank_k = jnp.sum(jnp.where(col == idxs[k], before, 0.0), axis=-1, keepdims=True)
        rank_out = jnp.where(slot, rank_k, rank_out)
    idx_ref[...] = idx_out.astype(jnp.int32)
    gate_ref[...] = gate_out
    rank_ref[...] = rank_out.astype(jnp.int32)
    carry[...] = carry[...] + jnp.sum(onehot, axis=0, keepdims=True)
    cnt_ref[...] = carry[...]


def _mixout(attn, rn, x2d, mod, g_post, g_pre, w_out16, wr, br, counts0, seq, tm):
    n, d = x2d.shape
    per_seq = seq // tm
    row = lambda w: pl.BlockSpec((tm, w), lambda i: (i, 0))
    vmem = w_out16.size * 2 + 2 * tm * d * (4 + 4 + 2 + 2) + 6 * tm * d * 4 + (8 << 20)
    return pl.pallas_call(
        _mixout_kernel,
        grid=(n // tm,),
        in_specs=[row(MLA_WIDTH), row(RET_WIDTH), row(d),
                  pl.BlockSpec((1, 6, d), lambda i: (i // per_seq, 0, 0)),
                  _const_spec((1, d)), _const_spec((1, d)), _const_spec(w_out16.shape),
                  _const_spec(wr.shape), _const_spec((1, LANES)), _const_spec((1, LANES))],
        out_specs=[row(d), row(d // 2), row(LANES), row(LANES), row(LANES),
                   pl.BlockSpec((1, LANES), lambda i: (0, 0))],
        out_shape=[jax.ShapeDtypeStruct((n, d), f32), jax.ShapeDtypeStruct((n, d // 2), jnp.uint32),
                   jax.ShapeDtypeStruct((n, LANES), jnp.int32), jax.ShapeDtypeStruct((n, LANES), f32),
                   jax.ShapeDtypeStruct((n, LANES), jnp.int32), jax.ShapeDtypeStruct((1, LANES), f32)],
        scratch_shapes=[pltpu.VMEM((1, LANES), f32)],
        compiler_params=_params(("arbitrary",), vmem),
        name="mixer_out_router",
    )(attn, rn, x2d, mod, g_post.reshape(1, d), g_pre.reshape(1, d), w_out16, wr, br, counts0)


def _route(top_idx, rank, counts):
    counts = counts[0, :N_EXPERTS].astype(jnp.int32)
    padded = (counts + MOE_ROWS - 1) // MOE_ROWS * MOE_ROWS
    pad_end = jnp.cumsum(padded).astype(jnp.int32)
    pad_start = pad_end - padded
    dest = pad_start[top_idx] + rank
    return dest, pad_start, pad_end, padded // MOE_ROWS


def _n_sorted_rows(n_tokens):
    return (n_tokens * TOP_K + N_EXPERTS * (MOE_ROWS - 1)) // MOE_ROWS * MOE_ROWS


def _fill_unused_chunks(zero_buf, dst_rows, sem, used_rows, total_rows):
    first = used_rows // MOE_ROWS

    def copy(c):
        return pltpu.make_async_copy(zero_buf, dst_rows(pl.ds(pl.multiple_of(c * MOE_ROWS, MOE_ROWS), MOE_ROWS)), sem)

    def start(c, carry):
        copy(c).start()
        return carry

    def wait(c, carry):
        copy(c).wait()
        return carry

    lax.fori_loop(first, total_rows // MOE_ROWS, start, 0)
    lax.fori_loop(first, total_rows // MOE_ROWS, wait, 0)


def _dispatch_kernel(pe_ref, nc_ref, dest_ref, h2p_ref, x_hbm, zbuf, sem, zsem, *, tokens):
    i = pl.program_id(0)

    @pl.when(i == 0)
    def _():
        zbuf[...] = jnp.zeros_like(zbuf)

        def tail(e):
            start = pl.multiple_of(pe_ref[e] - MOE_ROWS, MOE_ROWS)
            return pltpu.make_async_copy(zbuf, x_hbm.at[pl.ds(start, MOE_ROWS), :], zsem)

        def start_tail(e, c):
            @pl.when(nc_ref[e] > 0)
            def _():
                tail(e).start()
            return c

        def wait_tail(e, c):
            @pl.when(nc_ref[e] > 0)
            def _():
                tail(e).wait()
            return c

        lax.fori_loop(0, N_EXPERTS, start_tail, 0)
        lax.fori_loop(0, N_EXPERTS, wait_tail, 0)
        _fill_unused_chunks(zbuf, lambda rows: x_hbm.at[rows, :], zsem, pe_ref[N_EXPERTS - 1], x_hbm.shape[0])

    def body(t, c):
        for k in range(TOP_K):
            d = dest_ref[0, 0, t * TOP_K + k]
            pltpu.make_async_copy(h2p_ref.at[pl.ds(t, 1), :], x_hbm.at[pl.ds(d, 1), :], sem).start(priority=k % 2)
        return c

    lax.fori_loop(0, tokens, body, 0, unroll=2)
    for _ in range(TOP_K):
        pltpu.make_async_copy(h2p_ref, x_hbm.at[pl.ds(0, tokens), :], sem).wait()


def _dispatch(pad_end, n_chunks, dest, h2p):
    n, w = h2p.shape
    tokens = 512
    n_steps = n // tokens
    grid_spec = pltpu.PrefetchScalarGridSpec(
        num_scalar_prefetch=2,
        grid=(n_steps,),
        in_specs=[pl.BlockSpec((1, 1, tokens * TOP_K), lambda i, pe, nc: (i, 0, 0), memory_space=pltpu.SMEM),
                  pl.BlockSpec((tokens, w), lambda i, pe, nc: (i, 0))],
        out_specs=pl.BlockSpec(memory_space=pl.ANY),
        scratch_shapes=[pltpu.VMEM((MOE_ROWS, w), jnp.uint32), pltpu.SemaphoreType.DMA(()),
                        pltpu.SemaphoreType.DMA(())],
    )
    return pl.pallas_call(
        functools.partial(_dispatch_kernel, tokens=tokens),
        grid_spec=grid_spec,
        out_shape=jax.ShapeDtypeStruct((_n_sorted_rows(n), w), jnp.uint32),
        compiler_params=_params(("arbitrary",), 24 << 20),
        name="moe_dispatch",
    )(pad_end, n_chunks, dest.reshape(n_steps, 1, tokens * TOP_K), h2p)


def _chunk_rows(r0, c):
    return pl.ds(pl.multiple_of(r0 + c * MOE_ROWS, MOE_ROWS), MOE_ROWS)


CHUNK_READ_PRIORITY = 1


def _moe_up_kernel(rs_ref, nc_ref, x_hbm, wg_ref, wu_ref, bg_ref, bu_ref, act_hbm,
                   wg16, wu16, xbuf, obuf, sem_in, sem_out):
    e = pl.program_id(0)
    j = pl.program_id(1)
    n = nc_ref[e]
    r0 = rs_ref[e]

    def in_copy(c, slot):
        return pltpu.make_async_copy(x_hbm.at[_chunk_rows(r0, c), :], xbuf.at[slot], sem_in.at[slot])

    def out_copy(c, slot):
        return pltpu.make_async_copy(obuf.at[slot], act_hbm.at[j, _chunk_rows(r0, c), :], sem_out.at[slot])

    @pl.when(n > 0)
    def _():
        in_copy(0, 0).start(priority=CHUNK_READ_PRIORITY)
        wg16[...] = wg_ref[0].astype(bf16)
        wu16[...] = wu_ref[0].astype(bf16)
        half = wg16.shape[0] // 2

        def body(c, carry):
            slot = c % 2

            @pl.when(c + 1 < n)
            def _():
                in_copy(c + 1, 1 - slot).start(priority=CHUNK_READ_PRIORITY)

            in_copy(c, slot).wait()
            lo, hi = _unpack_pairs(xbuf[slot])
            x_lo = lo.astype(bf16)
            x_hi = hi.astype(bf16)

            def proj(w16, b_ref):
                return (jnp.dot(x_lo, w16[0:half, :], preferred_element_type=f32)
                        + jnp.dot(x_hi, w16[half:, :], preferred_element_type=f32) + b_ref[0])

            gate = jnp.minimum(proj(wg16, bg_ref), SWIGLU_LIMIT)
            up = jnp.clip(proj(wu16, bu_ref), -SWIGLU_LIMIT, SWIGLU_LIMIT)
            act = (up + 1.0) * gate * jax.nn.sigmoid(SWIGLU_ALPHA * gate)

            @pl.when(c >= 2)
            def _():
                out_copy(c - 2, slot).wait()

            obuf[slot] = act.astype(bf16)
            out_copy(c, slot).start()
            return carry

        lax.fori_loop(0, n, body, 0)

        @pl.when(n >= 2)
        def _():
            out_copy(n - 2, n % 2).wait()

        out_copy(n - 1, (n - 1) % 2).wait()

    @pl.when(e == N_EXPERTS - 1)
    def _():
        obuf[0] = jnp.zeros(obuf.shape[1:], obuf.dtype)
        _fill_unused_chunks(obuf.at[0], lambda rows: act_hbm.at[j, rows, :], sem_out.at[0],
                            r0 + n * MOE_ROWS, act_hbm.shape[1])


def _moe_up(row_start, n_chunks, x_sorted, w_gate_up, b_gate_up):
    e, d, f2 = w_gate_up.shape
    tf = 1024
    n_j = D_FF // tf
    rows = x_sorted.shape[0]
    vmem = 2 * 2 * d * tf * 4 + 2 * d * tf * 2 + 2 * MOE_ROWS * d * 2 + 2 * MOE_ROWS * tf * 2 \
        + 6 * MOE_ROWS * tf * 4 + (6 << 20)
    grid_spec = pltpu.PrefetchScalarGridSpec(
        num_scalar_prefetch=2,
        grid=(e, n_j),
        in_specs=[pl.BlockSpec(memory_space=pl.ANY),
                  pl.BlockSpec((1, d, tf), lambda i, j, rs, nc: (i, 0, j)),
                  pl.BlockSpec((1, d, tf), lambda i, j, rs, nc: (i, 0, n_j + j)),
                  pl.BlockSpec((1, 1, tf), lambda i, j, rs, nc: (i, 0, j)),
                  pl.BlockSpec((1, 1, tf), lambda i, j, rs, nc: (i, 0, n_j + j))],
        out_specs=pl.BlockSpec(memory_space=pl.ANY),
        scratch_shapes=[pltpu.VMEM((d, tf), bf16), pltpu.VMEM((d, tf), bf16),
                        pltpu.VMEM((2, MOE_ROWS, d // 2), jnp.uint32), pltpu.VMEM((2, MOE_ROWS, tf), bf16),
                        pltpu.SemaphoreType.DMA((2,)), pltpu.SemaphoreType.DMA((2,))],
    )
    b3 = b_gate_up.reshape(e, 1, f2)
    return pl.pallas_call(
        _moe_up_kernel,
        grid_spec=grid_spec,
        out_shape=jax.ShapeDtypeStruct((n_j, rows, tf), bf16),
        compiler_params=_params(("arbitrary", "arbitrary"), vmem),
        name="moe_gate_up",
    )(row_start, n_chunks, x_sorted, w_gate_up, w_gate_up, b3, b3)


def _moe_down_kernel(rs_ref, nc_ref, act_hbm, wd_ref, bd_ref, y_hbm, wd16, abuf, ybuf, sem_in, sem_out):
    e = pl.program_id(0)
    n = nc_ref[e]
    r0 = rs_ref[e]
    n_k, tf = abuf.shape[1], abuf.shape[3]

    def in_copies(c, slot):
        return [pltpu.make_async_copy(act_hbm.at[k, _chunk_rows(r0, c), :], abuf.at[slot, k], sem_in.at[slot])
                for k in range(n_k)]

    def out_copy(c, slot):
        return pltpu.make_async_copy(ybuf.at[slot], y_hbm.at[_chunk_rows(r0, c), :], sem_out.at[slot])

    @pl.when(n > 0)
    def _():
        for cp in in_copies(0, 0):
            cp.start(priority=CHUNK_READ_PRIORITY)
        wd16[...] = wd_ref[0].astype(bf16)

        def body(c, carry):
            slot = c % 2

            @pl.when(c + 1 < n)
            def _():
                for cp in in_copies(c + 1, 1 - slot):
                    cp.start(priority=CHUNK_READ_PRIORITY)

            for cp in in_copies(c, slot):
                cp.wait()
            y = bd_ref[0]
            for k in range(n_k):
                y = y + jnp.dot(abuf[slot, k], wd16[k * tf:(k + 1) * tf, :], preferred_element_type=f32)
            y16 = y.astype(bf16)
            half = y16.shape[1] // 2

            @pl.when(c >= 2)
            def _():
                out_copy(c - 2, slot).wait()

            ybuf[slot] = _pack_pairs(y16[:, :half], y16[:, half:])
            out_copy(c, slot).start()
            return carry

        lax.fori_loop(0, n, body, 0)

        @pl.when(n >= 2)
        def _():
            out_copy(n - 2, n % 2).wait()

        out_copy(n - 1, (n - 1) % 2).wait()

    @pl.when(e == N_EXPERTS - 1)
    def _():
        ybuf[0] = jnp.zeros(ybuf.shape[1:], ybuf.dtype)
        _fill_unused_chunks(ybuf.at[0], lambda rows: y_hbm.at[rows, :], sem_out.at[0],
                            r0 + n * MOE_ROWS, y_hbm.shape[0])


def _moe_down(row_start, n_chunks, act, w_down, b_down):
    e, f, d = w_down.shape
    n_k, rows, tf = act.shape
    vmem = 2 * f * d * 4 + f * d * 2 + 2 * MOE_ROWS * f * 2 + 2 * MOE_ROWS * d * 2 + 4 * MOE_ROWS * d * 4 + (4 << 20)
    grid_spec = pltpu.PrefetchScalarGridSpec(
        num_scalar_prefetch=2,
        grid=(e,),
        in_specs=[pl.BlockSpec(memory_space=pl.ANY),
                  pl.BlockSpec((1, f, d), lambda i, rs, nc: (i, 0, 0)),
                  pl.BlockSpec((1, 1, d), lambda i, rs, nc: (i, 0, 0))],
        out_specs=pl.BlockSpec(memory_space=pl.ANY),
        scratch_shapes=[pltpu.VMEM((f, d), bf16), pltpu.VMEM((2, n_k, MOE_ROWS, tf), bf16),
                        pltpu.VMEM((2, MOE_ROWS, d // 2), jnp.uint32),
                        pltpu.SemaphoreType.DMA((2,)), pltpu.SemaphoreType.DMA((2,))],
    )
    return pl.pallas_call(
        _moe_down_kernel,
        grid_spec=grid_spec,
        out_shape=jax.ShapeDtypeStruct((rows, d // 2), jnp.uint32),
        compiler_params=_params(("arbitrary",), vmem),
        name="moe_down",
    )(row_start, n_chunks, act, w_down, b_down.reshape(e, 1, d))


def _gather_rows(idx_ref, src_hbm, dst, sem, n_rows):
    def body(r2, carry):
        for u in range(2):
            r = 2 * r2 + u
            t = idx_ref[0, 0, r]
            pltpu.make_async_copy(src_hbm.at[pl.ds(t, 1), :], dst.at[pl.ds(r, 1), :], sem).start(priority=u)
        return carry
    lax.fori_loop(0, n_rows // 2, body, 0, unroll=4)


def _wait_rows(src_hbm, dst, sem, n_rows):
    pltpu.make_async_copy(src_hbm.at[pl.ds(0, n_rows), :], dst, sem).wait()


def _prefetched_gather(i, n_live, cur_idx_ref, next_idx_ref, src_hbm, buf, sem, n_rows):
    slot = i % 2

    @pl.when(jnp.logical_and(i == 0, n_live > 0))
    def _():
        _gather_rows(cur_idx_ref, src_hbm, buf.at[0], sem.at[0], n_rows)

    @pl.when(i + 1 < n_live)
    def _():
        _gather_rows(next_idx_ref, src_hbm, buf.at[1 - slot], sem.at[1 - slot], n_rows)

    @pl.when(i < n_live)
    def _():
        _wait_rows(src_hbm, buf.at[slot], sem.at[slot], n_rows)

    return slot


def _combine_kernel(pos_ref, posn_ref, y_hbm, gates_ref, xmid_ref, mod_ref, gpost_ref, out_ref, ybuf, sem, *, n_steps):
    i = pl.program_id(0)
    tm = COMBINE_TOKENS
    slot = _prefetched_gather(i, n_steps, pos_ref, posn_ref, y_hbm, ybuf, sem, TOP_K * tm)
    g = gates_ref[...]
    lo = hi = None
    for k in range(TOP_K):
        l_k, h_k = _unpack_pairs(ybuf[slot, k * tm:(k + 1) * tm, :])
        gk = g[:, k:k + 1]
        lo = gk * l_k if lo is None else lo + gk * l_k
        hi = gk * h_k if hi is None else hi + gk * h_k
    half = lo.shape[1]
    inv = lax.rsqrt((jnp.sum(lo * lo, axis=-1, keepdims=True) + jnp.sum(hi * hi, axis=-1, keepdims=True))
                    / (2 * half) + EPS)
    gate2 = mod_ref[0][5:6]
    gp = gpost_ref[...]
    out_ref[:, :half] = xmid_ref[:, :half] + gate2[:, :half] * (lo * inv * gp[:, :half])
    out_ref[:, half:] = xmid_ref[:, half:] + gate2[:, half:] * (hi * inv * gp[:, half:])


def _combine(pos, y_sorted, gates, xmid, mod, g_post, seq):
    n, d = xmid.shape
    tm = COMBINE_TOKENS
    n_tiles = n // tm
    per_seq = seq // tm
    vmem = 2 * TOP_K * tm * d * 2 + 4 * tm * d * 4 + 6 * tm * d * 4 + (8 << 20)
    return pl.pallas_call(
        functools.partial(_combine_kernel, n_steps=n_tiles),
        grid=(n_tiles,),
        in_specs=[pl.BlockSpec((1, 1, TOP_K * tm), lambda i: (i, 0, 0), memory_space=pltpu.SMEM),
                  pl.BlockSpec((1, 1, TOP_K * tm), lambda i: (jnp.minimum(i + 1, n_tiles - 1), 0, 0),
                               memory_space=pltpu.SMEM),
                  pl.BlockSpec(memory_space=pl.ANY),
                  pl.BlockSpec((tm, TOP_K), lambda i: (i, 0)),
                  pl.BlockSpec((tm, d), lambda i: (i, 0)),
                  pl.BlockSpec((1, 6, d), lambda i: (i // per_seq, 0, 0)),
                  _const_spec((1, d))],
        out_specs=pl.BlockSpec((tm, d), lambda i: (i, 0)),
        out_shape=jax.ShapeDtypeStruct((n, d), f32),
        scratch_shapes=[pltpu.VMEM((2, TOP_K * tm, d // 2), jnp.uint32), pltpu.SemaphoreType.DMA((2,))],
        compiler_params=_params(("arbitrary",), vmem),
        name="moe_combine",
    )(pos, pos, y_sorted, gates, xmid, mod, g_post.reshape(1, d))


def _rope_tables(pos):
    def cs(d):
        half = d // 2
        inv_freq = 1.0 / (ROPE_THETA ** (jnp.arange(half, dtype=f32) * (2.0 / d)))
        ang = pos.astype(f32)[:, None] * inv_freq[None, :]
        return jnp.cos(ang), jnp.sin(ang)
    c64, s64 = cs(MLA_ROPE)
    c256, s256 = cs(RET_QK)
    t64 = jnp.concatenate([c64, c64, s64, s64], axis=-1)
    t256 = jnp.concatenate([c256, s256], axis=-1)
    return t64, t256


def _rot_cols(w):
    half = w.shape[-1] // 2
    return jnp.concatenate([-w[..., half:], w[..., :half]], axis=-1)


def _prep_weights(w_in, w_uq, w_ukv, w_out, w_router, b_router):
    b = np.cumsum((Q_LORA, KV_LORA, MLA_ROPE))
    w_pe = w_in[:, b[1]:b[2]]
    w_ext = jnp.concatenate([w_in[:, :b[2]], _rot_cols(w_pe), w_in[:, b[2]:]], axis=-1).astype(bf16)
    wq = w_uq.reshape(Q_LORA, MLA_HEADS, MLA_NOPE + MLA_ROPE)
    wq = jnp.concatenate([wq, _rot_cols(wq[..., MLA_NOPE:])], axis=-1).transpose(1, 0, 2).astype(bf16)
    wkv = w_ukv.reshape(KV_LORA, MLA_HEADS, MLA_NOPE + MLA_V).transpose(1, 0, 2)
    wk = wkv[..., :MLA_NOPE].astype(bf16)
    wv = wkv[..., MLA_NOPE:].astype(bf16)
    r_hi = w_router.astype(bf16)
    r_lo = (w_router - r_hi.astype(f32)).astype(bf16)
    pad = ((0, 0), (0, LANES - N_EXPERTS))
    wr = jnp.stack([jnp.pad(r_hi, pad), jnp.pad(r_lo, pad)])
    br = jnp.pad(b_router, (0, LANES - N_EXPERTS), constant_values=NEG_BIG).reshape(1, LANES)
    return w_ext, wq, wk, wv, w_out.astype(bf16), wr, br


def _mixer_half(x, mod, pos, wts, p, past_lat16, past_kpe16, state0, counts0, ret_chunk, q_tiles, tm):
    w_ext, wq, wk, wv, w_out16, wr, br = wts
    b, s, d = x.shape
    x2d = x.reshape(b * s, d)
    t64, t256 = _rope_tables(pos)
    qlat, kvlat, kvlat16, kpe, kpe16, rq, rk, rv, rg = _inproj(
        x2d, mod, p['g_pre_mix'], w_ext, p['g_q_lat'], p['g_kv_lat'], t64, t256, s, tm)
    lat16 = kvlat16.reshape(b, s, KV_LORA)
    kpe16 = kpe16.reshape(b, s, 2 * MLA_ROPE)
    if past_lat16 is not None:
        lat16 = jnp.concatenate([past_lat16, lat16], axis=1)
        kpe16 = jnp.concatenate([past_kpe16, kpe16], axis=1)
    attn = _mla(qlat.reshape(b, s, Q_LORA), lat16, kpe16, wq, wk, wv, t64, q_tiles)
    log_g = jnp.log1p(-jnp.exp2(-5.0 - jnp.arange(RET_HEADS, dtype=f32)))
    log_g = jnp.broadcast_to(log_g[:, None, None], (RET_HEADS, 8, LANES))
    sh = lambda t, w: t.reshape(b, s, w)
    rn, state = _retention(sh(rq, RET_HEADS * RET_QK), sh(rk, RET_HEADS * RET_QK), sh(rv, RET_WIDTH),
                           sh(rg, RET_WIDTH), p['g_ret'], state0, log_g, ret_chunk)
    xmid, h2p, idx, gates, rank, counts = _mixout(
        attn.reshape(b * s, MLA_WIDTH), rn.reshape(b * s, RET_WIDTH), x2d, mod,
        p['g_post_mix'], p['g_pre_ffn'], w_out16, wr, br, counts0, s, tm)
    route = (idx[:, :TOP_K], gates[:, :TOP_K], rank[:, :TOP_K], counts)
    return kvlat.reshape(b, s, KV_LORA), kpe.reshape(b, s, MLA_ROPE), state, xmid, h2p, route


def _combine_order(dest):
    tm = COMBINE_TOKENS
    n_tiles = dest.shape[0] // tm
    return dest.reshape(n_tiles, tm, TOP_K).transpose(0, 2, 1).reshape(n_tiles, 1, TOP_K * tm)


def kernel(x_prompt, x_sample, cache_kv_latent, cache_k_rope, state_retention, c_prompt, c_sample, w_ada, b_ada, g_pre_mix, g_post_mix, g_pre_ffn, g_post_ffn, w_in, g_q_lat, g_kv_lat, w_uq, w_ukv, g_ret, w_out, w_router, b_router, w_gate_up, b_gate_up, w_down, b_down):
    depth = w_in.shape[0]
    assert depth == 1, "the staged problem has a single layer"
    bp, sp, d = x_prompt.shape
    bs, ss, _ = x_sample.shape
    past = cache_kv_latent.shape[2]
    l = 0
    p = dict(g_pre_mix=g_pre_mix[l], g_post_mix=g_post_mix[l], g_pre_ffn=g_pre_ffn[l], g_post_ffn=g_post_ffn[l],
             g_q_lat=g_q_lat[l], g_kv_lat=g_kv_lat[l], g_ret=g_ret[l])
    wts = _prep_weights(w_in[l], w_uq[l], w_ukv[l], w_out[l], w_router[l], b_router[l])

    mod = _ada(jnp.concatenate([c_prompt, c_sample], axis=0), w_ada[l], b_ada[l]).reshape(bp + bs, 6, d)
    mod_p, mod_s = mod[:bp], mod[bp:]

    tq = ATTN_Q_TILE
    tiles_p = tuple((q0, tq, q0 + tq, True) for q0 in range(0, sp, tq))
    zero_state = jnp.zeros((bp, RET_HEADS, RET_QK, RET_V), f32)
    lat_p, kpe_p, st_p, xmid_p, h2p_p, (idx_p, gates_p, rank_p, counts_p) = _mixer_half(
        x_prompt, mod_p, jnp.arange(sp), wts, p, None, None, zero_state, jnp.zeros((1, LANES), f32),
        RET_CHUNK_PROMPT, tiles_p, 512)

    past_lat16 = cache_kv_latent[l].astype(bf16)
    past_kpe16 = jnp.pad(cache_k_rope[l], ((0, 0), (0, 0), (0, MLA_ROPE))).astype(bf16)
    tiles_s = ((0, ss, past + ss, False),)
    lat_s, kpe_s, st_s, xmid_s, h2p_s, (idx_s, gates_s, rank_s, counts) = _mixer_half(
        x_sample, mod_s, past + jnp.arange(ss), wts, p, past_lat16, past_kpe16, state_retention[l], counts_p,
        ss, tiles_s, ss)

    n_p = bp * sp
    h2p = jnp.concatenate([h2p_p, h2p_s], axis=0)
    dest, row_start, pad_end, n_chunks = _route(
        jnp.concatenate([idx_p, idx_s], axis=0), jnp.concatenate([rank_p, rank_s], axis=0), counts)
    x_sorted = _dispatch(pad_end, n_chunks, dest, h2p)
    act = _moe_up(row_start, n_chunks, x_sorted, w_gate_up[l], b_gate_up[l])
    y_sorted = _moe_down(row_start, n_chunks, act, w_down[l], b_down[l])
    y_p = _combine(_combine_order(dest[:n_p]), y_sorted, gates_p, xmid_p, mod_p, p['g_post_ffn'], sp)
    y_s = _combine(_combine_order(dest[n_p:]), y_sorted, gates_s, xmid_s, mod_s, p['g_post_ffn'], ss)

    return (y_p.reshape(bp, sp, d), y_s.reshape(bs, ss, d),
            lat_p[None], kpe_p[None], st_p[None].astype(state_retention.dtype),
            lat_s[None], kpe_s[None], st_s[None].astype(state_retention.dtype))
```

```python
import functools

import numpy as np
import jax
import jax.numpy as jnp
from jax import lax
from jax.experimental import pallas as pl
from jax.experimental.pallas import tpu as pltpu

CHUNK = 64
EPS = 1e-6
ROPE_THETA = 10000.0

MLA_HEADS = 8
MLA_NOPE = 128
MLA_ROPE = 64
MLA_V = 128
Q_LORA = 512
KV_LORA = 512
MLA_WIDTH = MLA_HEADS * MLA_V

RET_HEADS = 4
RET_QK = 256
RET_V = 256
RET_WIDTH = RET_HEADS * RET_V

N_EXPERTS = 32
TOP_K = 4
D_FF = 2048
SWIGLU_LIMIT = 7.0
SWIGLU_ALPHA = 1.702

LANES = 128
V7X_VMEM_BYTES = 64 * 1024 * 1024
VMEM_CAP_BYTES = V7X_VMEM_BYTES - 8 * 1024 * 1024

RET_CHUNK_PROMPT = 128
RET_UNROLL = 4
ATTN_Q_TILE = 256
MOE_ROWS = 512
DOWN_TILE = 1024
COMBINE_TOKENS = 64
NEG_BIG = -1e30

bf16 = jnp.bfloat16
f32 = jnp.float32


def _params(sem, vmem_bytes):
    return pltpu.CompilerParams(dimension_semantics=sem, vmem_limit_bytes=int(min(vmem_bytes, VMEM_CAP_BYTES)))


def _const_spec(shape):
    nd = len(shape)
    return pl.BlockSpec(shape, lambda *_: (0,) * nd, pipeline_mode=pl.Buffered(1))


def _rms(x):
    return x * lax.rsqrt(jnp.mean(x * x, axis=-1, keepdims=True) + EPS)


def _silu(x):
    return x * jax.nn.sigmoid(x)


def _ada_kernel(c_ref, w_ref, b_ref, o_ref):
    s = _silu(c_ref[...]).astype(bf16)
    o_ref[...] = jnp.dot(s, w_ref[...].astype(bf16), preferred_element_type=f32) + b_ref[...]


def _ada(c, w_ada, b_ada):
    nb, d = c.shape
    n = w_ada.shape[1]
    tn = 1536
    return pl.pallas_call(
        _ada_kernel,
        grid=(n // tn,),
        in_specs=[pl.BlockSpec((nb, d), lambda j: (0, 0)),
                  pl.BlockSpec((d, tn), lambda j: (0, j)),
                  pl.BlockSpec((1, tn), lambda j: (0, j))],
        out_specs=pl.BlockSpec((nb, tn), lambda j: (0, j)),
        out_shape=jax.ShapeDtypeStruct((nb, n), f32),
        compiler_params=_params(("arbitrary",), 2 * d * tn * 4 + d * tn * 2 + (8 << 20)),
        name="ada_modulation",
    )(c, w_ada, b_ada.reshape(1, n))


_C_Q = 0
_C_KV = _C_Q + Q_LORA
_C_PE = _C_KV + KV_LORA
_C_RQ = _C_PE + 2 * MLA_ROPE
_C_RK = _C_RQ + RET_HEADS * RET_QK
_C_RV = _C_RK + RET_HEADS * RET_QK
_C_RG = _C_RV + RET_WIDTH
_C_END = _C_RG + RET_WIDTH


def _rope_pair(t):
    return t + pltpu.roll(t, MLA_ROPE, axis=1)


def _rope_heads(z, cos, sin, scale):
    outs = []
    half = RET_QK // 2
    for h in range(RET_HEADS):
        x1 = z[:, h * RET_QK:h * RET_QK + half]
        x2 = z[:, h * RET_QK + half:(h + 1) * RET_QK]
        outs.append((x1 * cos - x2 * sin) * scale)
        outs.append((x1 * sin + x2 * cos) * scale)
    return jnp.concatenate(outs, axis=-1)


def _inproj_kernel(x_ref, mod_ref, g_ref, w_ref, gq_ref, gkv_ref, t64_ref, t256_ref,
                   qlat_ref, kvlat_ref, kvlat16_ref, kpe_ref, kpe16_ref, rq_ref, rk_ref, rv_ref, rg_ref):
    m = mod_ref[0]
    h = (_rms(x_ref[...]) * g_ref[...] * (1.0 + m[1:2]) + m[0:1]).astype(bf16)

    def proj(c0, c1):
        return jnp.dot(h, w_ref[:, c0:c1], preferred_element_type=f32)

    qlat_ref[...] = (_rms(proj(_C_Q, _C_KV)) * gq_ref[...]).astype(bf16)
    kv = _rms(proj(_C_KV, _C_PE)) * gkv_ref[...]
    kvlat_ref[...] = kv
    kvlat16_ref[...] = kv.astype(bf16)
    pe = _rope_pair(proj(_C_PE, _C_RQ) * t64_ref[...])
    kpe_ref[...] = pe[:, :MLA_ROPE]
    lane = lax.broadcasted_iota(jnp.int32, pe.shape, 1)
    kpe16_ref[...] = jnp.where(lane < MLA_ROPE, pe, 0.0).astype(bf16)
    cos = t256_ref[:, :RET_QK // 2]
    sin = t256_ref[:, RET_QK // 2:]
    rq_ref[...] = _rope_heads(proj(_C_RQ, _C_RK), cos, sin, 1.0).astype(bf16)
    rk_ref[...] = _rope_heads(proj(_C_RK, _C_RV), cos, sin, RET_QK ** -0.5).astype(bf16)
    rv_ref[...] = proj(_C_RV, _C_RG).astype(bf16)
    rg_ref[...] = proj(_C_RG, _C_END).astype(bf16)


def _inproj(x2d, mod, g_pre, w_ext, g_q, g_kv, t64, t256, seq, tm):
    n, d = x2d.shape
    per_seq = seq // tm
    row = lambda w: pl.BlockSpec((tm, w), lambda i: (i, 0))
    tab = lambda w: pl.BlockSpec((tm, w), lambda i: (i % per_seq, 0))
    outs = [(Q_LORA, bf16), (KV_LORA, f32), (KV_LORA, bf16), (MLA_ROPE, f32), (2 * MLA_ROPE, bf16),
            (RET_HEADS * RET_QK, bf16), (RET_HEADS * RET_QK, bf16), (RET_WIDTH, bf16), (RET_WIDTH, bf16)]
    vmem = w_ext.size * 2 + 2 * tm * d * 4 + 4 * tm * d * 4 + sum(2 * tm * w * 4 for w, _ in outs) + (6 << 20)
    return pl.pallas_call(
        _inproj_kernel,
        grid=(n // tm,),
        in_specs=[row(d),
                  pl.BlockSpec((1, 6, d), lambda i: (i // per_seq, 0, 0)),
                  _const_spec((1, d)), _const_spec(w_ext.shape), _const_spec((1, Q_LORA)), _const_spec((1, KV_LORA)),
                  tab(2 * MLA_ROPE), tab(RET_QK)],
        out_specs=[row(w) for w, _ in outs],
        out_shape=[jax.ShapeDtypeStruct((n, w), dt) for w, dt in outs],
        compiler_params=_params(("arbitrary",), vmem),
        name="in_projection",
    )(x2d, mod, g_pre.reshape(1, d), w_ext, g_q.reshape(1, Q_LORA), g_kv.reshape(1, KV_LORA), t64, t256)


def _mla_kernel(qlat_ref, lat_ref, kpe_ref, wq_ref, wk_ref, wv_ref, tab_ref, o_ref, q_s, k_s, v_s, *, q_tiles):
    scale = (MLA_NOPE + MLA_ROPE) ** -0.5
    qh = jnp.dot(qlat_ref[0], wq_ref[0], preferred_element_type=f32)
    q_s[:, :MLA_NOPE] = (qh[:, :MLA_NOPE] * scale).astype(bf16)
    q_s[:, MLA_NOPE:] = (_rope_pair(qh[:, MLA_NOPE:] * tab_ref[...]) * scale).astype(bf16)
    lat = lat_ref[0]
    k_s[:, :MLA_NOPE] = jnp.dot(lat, wk_ref[0], preferred_element_type=f32).astype(bf16)
    k_s[:, MLA_NOPE:] = kpe_ref[0]
    v_s[...] = jnp.dot(lat, wv_ref[0], preferred_element_type=f32).astype(bf16)
    for q0, ql, kvl, masked in q_tiles:
        q = q_s[q0:q0 + ql, :]
        s = lax.dot_general(q, k_s[0:kvl, :], (((1,), (1,)), ((), ())), preferred_element_type=f32)
        if masked:
            qc = (q0 + lax.broadcasted_iota(jnp.int32, (ql, 1), 0)) // CHUNK
            kc = lax.broadcasted_iota(jnp.int32, (1, kvl), 1) // CHUNK
            s = jnp.where(kc <= qc, s, NEG_BIG)
        p = jnp.exp(s - jnp.max(s, axis=-1, keepdims=True))
        l = jnp.sum(p, axis=-1, keepdims=True)
        o = jnp.dot(p.astype(bf16), v_s[0:kvl, :], preferred_element_type=f32)
        o_ref[0, q0:q0 + ql, :] = (o / l).astype(bf16)


def _mla(qlat, lat, kpe16, wq, wk, wv, tab, q_tiles):
    b, sq, _ = qlat.shape
    skv = lat.shape[1]
    kern = functools.partial(_mla_kernel, q_tiles=q_tiles)
    max_ql = max(t[1] for t in q_tiles)
    vmem = 2 * 2 * (sq * Q_LORA + skv * KV_LORA + skv * LANES) + sq * LANES * 4 + sq * 3 * LANES * 4 \
        + (sq + skv) * 3 * LANES * 2 + 4 * max_ql * skv * 4 + (8 << 20)
    return pl.pallas_call(
        kern,
        grid=(b, MLA_HEADS),
        in_specs=[pl.BlockSpec((1, sq, Q_LORA), lambda i, h: (i, 0, 0)),
                  pl.BlockSpec((1, skv, KV_LORA), lambda i, h: (i, 0, 0)),
                  pl.BlockSpec((1, skv, 2 * MLA_ROPE), lambda i, h: (i, 0, 0)),
                  pl.BlockSpec((1, Q_LORA, MLA_NOPE + 2 * MLA_ROPE), lambda i, h: (h, 0, 0)),
                  pl.BlockSpec((1, KV_LORA, MLA_NOPE), lambda i, h: (h, 0, 0)),
                  pl.BlockSpec((1, KV_LORA, MLA_V), lambda i, h: (h, 0, 0)),
                  pl.BlockSpec((sq, 2 * MLA_ROPE), lambda i, h: (0, 0))],
        out_specs=pl.BlockSpec((1, sq, MLA_V), lambda i, h: (i, 0, h)),
        out_shape=jax.ShapeDtypeStruct((b, sq, MLA_WIDTH), bf16),
        scratch_shapes=[pltpu.VMEM((sq, MLA_NOPE + 2 * MLA_ROPE), bf16),
                        pltpu.VMEM((skv, MLA_NOPE + 2 * MLA_ROPE), bf16),
                        pltpu.VMEM((skv, MLA_V), bf16)],
        compiler_params=_params(("arbitrary", "arbitrary"), vmem),
        name="mla_attention",
    )(qlat, lat, kpe16, wq, wk, wv, tab)


def _ret_kernel(lg_ref, rq_ref, rk_ref, rv_ref, rg_ref, gret_ref, s0_ref, o_ref, sout_ref, state, *, chunk, n_chunks):
    c = chunk
    lg = lg_ref[0, 0:1, 0:1]
    ii = lax.broadcasted_iota(jnp.int32, (c, c), 0)
    jj = lax.broadcasted_iota(jnp.int32, (c, c), 1)
    diff = (ii - jj).astype(f32)
    dmask = jnp.where(diff >= 0.0, jnp.exp(jnp.maximum(diff, 0.0) * lg), 0.0)
    ic = lax.broadcasted_iota(jnp.int32, (c, 1), 0).astype(f32)
    q_decay = jnp.exp((ic + 1.0) * lg)
    k_decay = jnp.exp((c - 1.0 - ic) * lg)
    chunk_decay = jnp.exp(float(c) * lg)
    gret = gret_ref[...]
    state[...] = s0_ref[0, 0]

    def step(n, carry):
        sl = pl.ds(pl.multiple_of(n * c, c), c)
        q = rq_ref[0, sl, :]
        k = rk_ref[0, sl, :]
        v = rv_ref[0, sl, :]
        st = state[...]
        attn = lax.dot_general(q, k, (((1,), (1,)), ((), ())), preferred_element_type=f32) * dmask
        inner = jnp.dot(attn.astype(bf16), v, preferred_element_type=f32)
        cross = jnp.dot(q, st.astype(bf16), preferred_element_type=f32) * q_decay
        o = inner + cross
        kd_t = (k.astype(f32) * k_decay).T.astype(bf16)
        state[...] = st * chunk_decay + jnp.dot(kd_t, v, preferred_element_type=f32)
        mu = jnp.mean(o, axis=-1, keepdims=True)
        oc = o - mu
        rn = oc * lax.rsqrt(jnp.mean(oc * oc, axis=-1, keepdims=True) + EPS) * gret
        o_ref[0, sl, :] = (rn * _silu(rg_ref[0, sl, :].astype(f32))).astype(bf16)
        return carry

    lax.fori_loop(0, n_chunks, step, 0, unroll=min(n_chunks, RET_UNROLL))
    sout_ref[0, 0] = state[...]


def _retention(rq, rk, rv, rg, g_ret, state0, log_g, chunk):
    b, s, _ = rq.shape
    kern = functools.partial(_ret_kernel, chunk=chunk, n_chunks=s // chunk)
    head = lambda w: pl.BlockSpec((1, s, w), lambda i, h: (i, 0, h))
    st = pl.BlockSpec((1, 1, RET_QK, RET_V), lambda i, h: (i, h, 0, 0))
    vmem = 2 * 5 * s * RET_QK * 2 + 5 * RET_QK * RET_V * 4 + (8 << 20)
    return pl.pallas_call(
        kern,
        grid=(b, RET_HEADS),
        in_specs=[pl.BlockSpec((1, 8, LANES), lambda i, h: (h, 0, 0)),
                  head(RET_QK), head(RET_QK), head(RET_V), head(RET_V),
                  pl.BlockSpec((1, RET_V), lambda i, h: (0, h)), st],
        out_specs=[head(RET_V), st],
        out_shape=[jax.ShapeDtypeStruct((b, s, RET_WIDTH), bf16),
                   jax.ShapeDtypeStruct((b, RET_HEADS, RET_QK, RET_V), f32)],
        scratch_shapes=[pltpu.VMEM((RET_QK, RET_V), f32)],
        compiler_params=_params(("arbitrary", "arbitrary"), vmem),
        name="retention",
    )(log_g, rq, rk, rv, rg, g_ret.reshape(1, RET_WIDTH), state0)


def _pack_pairs(lo16, hi16):
    lo = lax.bitcast_convert_type(lo16.astype(f32), jnp.uint32)
    hi = lax.bitcast_convert_type(hi16.astype(f32), jnp.uint32)
    return (hi & jnp.uint32(0xFFFF0000)) | (lo >> 16)


def _unpack_pairs(u):
    lo = lax.bitcast_convert_type(u << 16, f32)
    hi = lax.bitcast_convert_type(u & jnp.uint32(0xFFFF0000), f32)
    return lo, hi


def _mixout_kernel(attn_ref, rn_ref, x_ref, mod_ref, gpost_ref, gpre_ref, wout_ref, wr_ref, br_ref, cnt0_ref,
                   xmid_ref, h2p_ref, idx_ref, gate_ref, rank_ref, cnt_ref, carry):
    @pl.when(pl.program_id(0) == 0)
    def _():
        carry[...] = cnt0_ref[...]

    mix = (jnp.dot(attn_ref[...], wout_ref[0:MLA_WIDTH, :], preferred_element_type=f32)
           + jnp.dot(rn_ref[...], wout_ref[MLA_WIDTH:, :], preferred_element_type=f32))
    m = mod_ref[0]
    x1 = x_ref[...] + m[2:3] * (_rms(mix) * gpost_ref[...])
    xmid_ref[...] = x1
    h2 = _rms(x1) * gpre_ref[...] * (1.0 + m[4:5]) + m[3:4]
    h_hi = h2.astype(bf16)
    half = h2.shape[1] // 2
    h2p_ref[...] = _pack_pairs(h_hi[:, :half], h_hi[:, half:])
    h_lo = (h2 - h_hi.astype(f32)).astype(bf16)
    logits = (jnp.dot(h_hi, wr_ref[0], preferred_element_type=f32)
              + jnp.dot(h_lo, wr_ref[0], preferred_element_type=f32)
              + jnp.dot(h_hi, wr_ref[1], preferred_element_type=f32)) + br_ref[...]
    tm = logits.shape[0]
    col = lax.broadcasted_iota(jnp.int32, logits.shape, 1).astype(f32)
    vals, idxs = [], []
    for _ in range(TOP_K):
        mx = jnp.max(logits, axis=-1, keepdims=True)
        ix = jnp.min(jnp.where(logits == mx, col, float(LANES)), axis=-1, keepdims=True)
        vals.append(mx)
        idxs.append(ix)
        logits = jnp.where(col == ix, -jnp.inf, logits)
    es = [jnp.exp(v - vals[0]) for v in vals]
    tot = es[0] + es[1] + es[2] + es[3]
    onehot = jnp.zeros_like(col)
    for k in range(TOP_K):
        onehot = jnp.where(col == idxs[k], 1.0, onehot)
    earlier = (lax.broadcasted_iota(jnp.int32, (tm, tm), 0) > lax.broadcasted_iota(jnp.int32, (tm, tm), 1))
    before = jnp.dot(jnp.where(earlier, 1.0, 0.0).astype(bf16), onehot.astype(bf16),
                     preferred_element_type=f32) + carry[...]
    idx_out = jnp.zeros_like(col)
    gate_out = jnp.zeros_like(col)
    rank_out = jnp.zeros_like(col)
    for k in range(TOP_K):
        slot = col == float(k)
        idx_out = jnp.where(slot, idxs[k], idx_out)
        gate_out = jnp.where(slot, es[k] / tot, gate_out)
        rank_k = jnp.sum(jnp.where(col == idxs[k], before, 0.0), axis=-1, keepdims=True)
        rank_out = jnp.where(slot, rank_k, rank_out)
    idx_ref[...] = idx_out.astype(jnp.int32)
    gate_ref[...] = gate_out
    rank_ref[...] = rank_out.astype(jnp.int32)
    carry[...] = carry[...] + jnp.sum(onehot, axis=0, keepdims=True)
    cnt_ref[...] = carry[...]


def _mixout(attn, rn, x2d, mod, g_post, g_pre, w_out16, wr, br, counts0, seq, tm):
    n, d = x2d.shape
    per_seq = seq // tm
    row = lambda w: pl.BlockSpec((tm, w), lambda i: (i, 0))
    vmem = w_out16.size * 2 + 2 * tm * d * (4 + 4 + 2 + 2) + 6 * tm * d * 4 + (8 << 20)
    return pl.pallas_call(
        _mixout_kernel,
        grid=(n // tm,),
        in_specs=[row(MLA_WIDTH), row(RET_WIDTH), row(d),
                  pl.BlockSpec((1, 6, d), lambda i: (i // per_seq, 0, 0)),
                  _const_spec((1, d)), _const_spec((1, d)), _const_spec(w_out16.shape),
                  _const_spec(wr.shape), _const_spec((1, LANES)), _const_spec((1, LANES))],
        out_specs=[row(d), row(d // 2), row(LANES), row(LANES), row(LANES),
                   pl.BlockSpec((1, LANES), lambda i: (0, 0))],
        out_shape=[jax.ShapeDtypeStruct((n, d), f32), jax.ShapeDtypeStruct((n, d // 2), jnp.uint32),
                   jax.ShapeDtypeStruct((n, LANES), jnp.int32), jax.ShapeDtypeStruct((n, LANES), f32),
                   jax.ShapeDtypeStruct((n, LANES), jnp.int32), jax.ShapeDtypeStruct((1, LANES), f32)],
        scratch_shapes=[pltpu.VMEM((1, LANES), f32)],
        compiler_params=_params(("arbitrary",), vmem),
        name="mixer_out_router",
    )(attn, rn, x2d, mod, g_post.reshape(1, d), g_pre.reshape(1, d), w_out16, wr, br, counts0)


def _route(top_idx, rank, counts):
    counts = counts[0, :N_EXPERTS].astype(jnp.int32)
    padded = (counts + MOE_ROWS - 1) // MOE_ROWS * MOE_ROWS
    pad_end = jnp.cumsum(padded).astype(jnp.int32)
    pad_start = pad_end - padded
    dest = pad_start[top_idx] + rank
    return dest, pad_start, pad_end, padded // MOE_ROWS


def _n_sorted_rows(n_tokens):
    return (n_tokens * TOP_K + N_EXPERTS * (MOE_ROWS - 1)) // MOE_ROWS * MOE_ROWS


def _fill_unused_chunks(zero_buf, dst_rows, sem, used_rows, total_rows):
    first = used_rows // MOE_ROWS

    def copy(c):
        return pltpu.make_async_copy(zero_buf, dst_rows(pl.ds(pl.multiple_of(c * MOE_ROWS, MOE_ROWS), MOE_ROWS)), sem)

    def start(c, carry):
        copy(c).start()
        return carry

    def wait(c, carry):
        copy(c).wait()
        return carry

    lax.fori_loop(first, total_rows // MOE_ROWS, start, 0)
    lax.fori_loop(first, total_rows // MOE_ROWS, wait, 0)


def _dispatch_kernel(pe_ref, nc_ref, dest_ref, h2p_ref, x_hbm, zbuf, sem, zsem, *, tokens):
    i = pl.program_id(0)

    @pl.when(i == 0)
    def _():
        zbuf[...] = jnp.zeros_like(zbuf)

        def tail(e):
            start = pl.multiple_of(pe_ref[e] - MOE_ROWS, MOE_ROWS)
            return pltpu.make_async_copy(zbuf, x_hbm.at[pl.ds(start, MOE_ROWS), :], zsem)

        def start_tail(e, c):
            @pl.when(nc_ref[e] > 0)
            def _():
                tail(e).start()
            return c

        def wait_tail(e, c):
            @pl.when(nc_ref[e] > 0)
            def _():
                tail(e).wait()
            return c

        lax.fori_loop(0, N_EXPERTS, start_tail, 0)
        lax.fori_loop(0, N_EXPERTS, wait_tail, 0)
        _fill_unused_chunks(zbuf, lambda rows: x_hbm.at[rows, :], zsem, pe_ref[N_EXPERTS - 1], x_hbm.shape[0])

    def body(t, c):
        for k in range(TOP_K):
            d = dest_ref[0, 0, t * TOP_K + k]
            pltpu.make_async_copy(h2p_ref.at[pl.ds(t, 1), :], x_hbm.at[pl.ds(d, 1), :], sem).start(priority=k % 2)
        return c

    lax.fori_loop(0, tokens, body, 0, unroll=2)
    for _ in range(TOP_K):
        pltpu.make_async_copy(h2p_ref, x_hbm.at[pl.ds(0, tokens), :], sem).wait()


def _dispatch(pad_end, n_chunks, dest, h2p):
    n, w = h2p.shape
    tokens = 512
    n_steps = n // tokens
    grid_spec = pltpu.PrefetchScalarGridSpec(
        num_scalar_prefetch=2,
        grid=(n_steps,),
        in_specs=[pl.BlockSpec((1, 1, tokens * TOP_K), lambda i, pe, nc: (i, 0, 0), memory_space=pltpu.SMEM),
                  pl.BlockSpec((tokens, w), lambda i, pe, nc: (i, 0))],
        out_specs=pl.BlockSpec(memory_space=pl.ANY),
        scratch_shapes=[pltpu.VMEM((MOE_ROWS, w), jnp.uint32), pltpu.SemaphoreType.DMA(()),
                        pltpu.SemaphoreType.DMA(())],
    )
    return pl.pallas_call(
        functools.partial(_dispatch_kernel, tokens=tokens),
        grid_spec=grid_spec,
        out_shape=jax.ShapeDtypeStruct((_n_sorted_rows(n), w), jnp.uint32),
        compiler_params=_params(("arbitrary",), 24 << 20),
        name="moe_dispatch",
    )(pad_end, n_chunks, dest.reshape(n_steps, 1, tokens * TOP_K), h2p)


def _chunk_rows(r0, c):
    return pl.ds(pl.multiple_of(r0 + c * MOE_ROWS, MOE_ROWS), MOE_ROWS)


CHUNK_READ_PRIORITY = 1


def _moe_up_kernel(rs_ref, nc_ref, x_hbm, wg_ref, wu_ref, bg_ref, bu_ref, act_hbm,
                   wg16, wu16, xbuf, obuf, sem_in, sem_out):
    e = pl.program_id(0)
    j = pl.program_id(1)
    n = nc_ref[e]
    r0 = rs_ref[e]

    def in_copy(c, slot):
        return pltpu.make_async_copy(x_hbm.at[_chunk_rows(r0, c), :], xbuf.at[slot], sem_in.at[slot])

    def out_copy(c, slot):
        return pltpu.make_async_copy(obuf.at[slot], act_hbm.at[j, _chunk_rows(r0, c), :], sem_out.at[slot])

    @pl.when(n > 0)
    def _():
        in_copy(0, 0).start(priority=CHUNK_READ_PRIORITY)
        wg16[...] = wg_ref[0].astype(bf16)
        wu16[...] = wu_ref[0].astype(bf16)
        half = wg16.shape[0] // 2

        def body(c, carry):
            slot = c % 2

            @pl.when(c + 1 < n)
            def _():
                in_copy(c + 1, 1 - slot).start(priority=CHUNK_READ_PRIORITY)

            in_copy(c, slot).wait()
            lo, hi = _unpack_pairs(xbuf[slot])
            x_lo = lo.astype(bf16)
            x_hi = hi.astype(bf16)

            def proj(w16, b_ref):
                return (jnp.dot(x_lo, w16[0:half, :], preferred_element_type=f32)
                        + jnp.dot(x_hi, w16[half:, :], preferred_element_type=f32) + b_ref[0])

            gate = jnp.minimum(proj(wg16, bg_ref), SWIGLU_LIMIT)
            up = jnp.clip(proj(wu16, bu_ref), -SWIGLU_LIMIT, SWIGLU_LIMIT)
            act = (up + 1.0) * gate * jax.nn.sigmoid(SWIGLU_ALPHA * gate)

            @pl.when(c >= 2)
            def _():
                out_copy(c - 2, slot).wait()

            obuf[slot] = act.astype(bf16)
            out_copy(c, slot).start()
            return carry

        lax.fori_loop(0, n, body, 0)

        @pl.when(n >= 2)
        def _():
            out_copy(n - 2, n % 2).wait()

        out_copy(n - 1, (n - 1) % 2).wait()

    @pl.when(e == N_EXPERTS - 1)
    def _():
        obuf[0] = jnp.zeros(obuf.shape[1:], obuf.dtype)
        _fill_unused_chunks(obuf.at[0], lambda rows: act_hbm.at[j, rows, :], sem_out.at[0],
                            r0 + n * MOE_ROWS, act_hbm.shape[1])


def _moe_up(row_start, n_chunks, x_sorted, w_gate_up, b_gate_up):
    e, d, f2 = w_gate_up.shape
    tf = 512
    n_j = D_FF // tf
    rows = x_sorted.shape[0]
    vmem = 2 * 2 * d * tf * 4 + 2 * d * tf * 2 + 2 * MOE_ROWS * d * 2 + 2 * MOE_ROWS * tf * 2 \
        + 6 * MOE_ROWS * tf * 4 + (6 << 20)
    grid_spec = pltpu.PrefetchScalarGridSpec(
        num_scalar_prefetch=2,
        grid=(e, n_j),
        in_specs=[pl.BlockSpec(memory_space=pl.ANY),
                  pl.BlockSpec((1, d, tf), lambda i, j, rs, nc: (i, 0, j)),
                  pl.BlockSpec((1, d, tf), lambda i, j, rs, nc: (i, 0, n_j + j)),
                  pl.BlockSpec((1, 1, tf), lambda i, j, rs, nc: (i, 0, j)),
                  pl.BlockSpec((1, 1, tf), lambda i, j, rs, nc: (i, 0, n_j + j))],
        out_specs=pl.BlockSpec(memory_space=pl.ANY),
        scratch_shapes=[pltpu.VMEM((d, tf), bf16), pltpu.VMEM((d, tf), bf16),
                        pltpu.VMEM((2, MOE_ROWS, d // 2), jnp.uint32), pltpu.VMEM((2, MOE_ROWS, tf), bf16),
                        pltpu.SemaphoreType.DMA((2,)), pltpu.SemaphoreType.DMA((2,))],
    )
    b3 = b_gate_up.reshape(e, 1, f2)
    return pl.pallas_call(
        _moe_up_kernel,
        grid_spec=grid_spec,
        out_shape=jax.ShapeDtypeStruct((n_j, rows, tf), bf16),
        compiler_params=_params(("arbitrary", "arbitrary"), vmem),
        name="moe_gate_up",
    )(row_start, n_chunks, x_sorted, w_gate_up, w_gate_up, b3, b3)


def _moe_down_kernel(rs_ref, nc_ref, act_hbm, wd_ref, bd_ref, y_hbm, wd16, abuf, ybuf, sem_in, sem_out):
    e = pl.program_id(0)
    nt = pl.program_id(1)
    n = nc_ref[e]
    r0 = rs_ref[e]
    n_k, tf = abuf.shape[1], abuf.shape[3]
    tw = ybuf.shape[2]
    n_t = y_hbm.shape[1] // tw

    def on_tile(fn):
        for t in range(n_t):
            @pl.when(nt == t)
            def _():
                fn(t)

    def in_copies(c, slot):
        return [pltpu.make_async_copy(act_hbm.at[k, _chunk_rows(r0, c), :], abuf.at[slot, k], sem_in.at[slot])
                for k in range(n_k)]

    def out_copy(c, slot, t):
        return pltpu.make_async_copy(ybuf.at[slot], y_hbm.at[_chunk_rows(r0, c), t * tw:(t + 1) * tw],
                                     sem_out.at[slot])

    @pl.when(n > 0)
    def _():
        for cp in in_copies(0, 0):
            cp.start(priority=CHUNK_READ_PRIORITY)
        wd16[...] = wd_ref[0].astype(bf16)

        def body(c, carry):
            slot = c % 2

            @pl.when(c + 1 < n)
            def _():
                for cp in in_copies(c + 1, 1 - slot):
                    cp.start(priority=CHUNK_READ_PRIORITY)

            for cp in in_copies(c, slot):
                cp.wait()
            y = bd_ref[0]
            for k in range(n_k):
                y = y + jnp.dot(abuf[slot, k], wd16[k * tf:(k + 1) * tf, :], preferred_element_type=f32)
            y16 = y.astype(bf16)

            @pl.when(c >= 2)
            def _():
                on_tile(lambda t: out_copy(c - 2, slot, t).wait())

            ybuf[slot] = _pack_pairs(y16[:, :tw], y16[:, tw:])
            on_tile(lambda t: out_copy(c, slot, t).start())
            return carry

        lax.fori_loop(0, n, body, 0)

        @pl.when(n >= 2)
        def _():
            on_tile(lambda t: out_copy(n - 2, n % 2, t).wait())

        on_tile(lambda t: out_copy(n - 1, (n - 1) % 2, t).wait())

    @pl.when(e == N_EXPERTS - 1)
    def _():
        ybuf[0] = jnp.zeros(ybuf.shape[1:], ybuf.dtype)
        on_tile(lambda t: _fill_unused_chunks(ybuf.at[0], lambda rows: y_hbm.at[rows, t * tw:(t + 1) * tw],
                                              sem_out.at[0], r0 + n * MOE_ROWS, y_hbm.shape[0]))


def _moe_down(row_start, n_chunks, act, w_down, b_down):
    e, f, d = w_down.shape
    n_k, rows, tf = act.shape
    tn = DOWN_TILE
    vmem = 2 * f * tn * 4 + f * tn * 2 + 2 * MOE_ROWS * f * 2 + 2 * MOE_ROWS * tn * 2 + 4 * MOE_ROWS * tn * 4 + (6 << 20)
    grid_spec = pltpu.PrefetchScalarGridSpec(
        num_scalar_prefetch=2,
        grid=(e, d // tn),
        in_specs=[pl.BlockSpec(memory_space=pl.ANY),
                  pl.BlockSpec((1, f, tn), lambda i, t, rs, nc: (i, 0, t)),
                  pl.BlockSpec((1, 1, tn), lambda i, t, rs, nc: (i, 0, t))],
        out_specs=pl.BlockSpec(memory_space=pl.ANY),
        scratch_shapes=[pltpu.VMEM((f, tn), bf16), pltpu.VMEM((2, n_k, MOE_ROWS, tf), bf16),
                        pltpu.VMEM((2, MOE_ROWS, tn // 2), jnp.uint32),
                        pltpu.SemaphoreType.DMA((2,)), pltpu.SemaphoreType.DMA((2,))],
    )
    return pl.pallas_call(
        _moe_down_kernel,
        grid_spec=grid_spec,
        out_shape=jax.ShapeDtypeStruct((rows, d // 2), jnp.uint32),
        compiler_params=_params(("arbitrary", "arbitrary"), vmem),
        name="moe_down",
    )(row_start, n_chunks, act, w_down, b_down.reshape(e, 1, d))


def _gather_rows(idx_ref, src_hbm, dst, sem, n_rows):
    def body(r2, carry):
        for u in range(2):
            r = 2 * r2 + u
            t = idx_ref[0, 0, r]
            pltpu.make_async_copy(src_hbm.at[pl.ds(t, 1), :], dst.at[pl.ds(r, 1), :], sem).start(priority=u)
        return carry
    lax.fori_loop(0, n_rows // 2, body, 0, unroll=4)


def _wait_rows(src_hbm, dst, sem, n_rows):
    pltpu.make_async_copy(src_hbm.at[pl.ds(0, n_rows), :], dst, sem).wait()


def _prefetched_gather(i, n_live, cur_idx_ref, next_idx_ref, src_hbm, buf, sem, n_rows):
    slot = i % 2

    @pl.when(jnp.logical_and(i == 0, n_live > 0))
    def _():
        _gather_rows(cur_idx_ref, src_hbm, buf.at[0], sem.at[0], n_rows)

    @pl.when(i + 1 < n_live)
    def _():
        _gather_rows(next_idx_ref, src_hbm, buf.at[1 - slot], sem.at[1 - slot], n_rows)

    @pl.when(i < n_live)
    def _():
        _wait_rows(src_hbm, buf.at[slot], sem.at[slot], n_rows)

    return slot


def _combine_kernel(pos_ref, posn_ref, y_hbm, gates_ref, xmid_ref, mod_ref, gpost_ref, out_ref, ybuf, sem, *, n_steps):
    i = pl.program_id(0)
    tm = COMBINE_TOKENS
    slot = _prefetched_gather(i, n_steps, pos_ref, posn_ref, y_hbm, ybuf, sem, TOP_K * tm)
    g = gates_ref[...]
    lo = hi = None
    for k in range(TOP_K):
        l_k, h_k = _unpack_pairs(ybuf[slot, k * tm:(k + 1) * tm, :])
        gk = g[:, k:k + 1]
        lo = gk * l_k if lo is None else lo + gk * l_k
        hi = gk * h_k if hi is None else hi + gk * h_k
    half = lo.shape[1]
    inv = lax.rsqrt((jnp.sum(lo * lo, axis=-1, keepdims=True) + jnp.sum(hi * hi, axis=-1, keepdims=True))
                    / (2 * half) + EPS)
    gate2 = mod_ref[0][5:6]
    gp = gpost_ref[...]
    q = DOWN_TILE // 2
    for t in range(half // q):
        for src, c0 in ((lo, t * DOWN_TILE), (hi, t * DOWN_TILE + q)):
            cols = slice(c0, c0 + q)
            out_ref[:, cols] = xmid_ref[:, cols] + gate2[:, cols] * (src[:, t * q:(t + 1) * q] * inv * gp[:, cols])


def _combine(pos, y_sorted, gates, xmid, mod, g_post, seq):
    n, d = xmid.shape
    tm = COMBINE_TOKENS
    n_tiles = n // tm
    per_seq = seq // tm
    vmem = 2 * TOP_K * tm * d * 2 + 4 * tm * d * 4 + 6 * tm * d * 4 + (8 << 20)
    return pl.pallas_call(
        functools.partial(_combine_kernel, n_steps=n_tiles),
        grid=(n_tiles,),
        in_specs=[pl.BlockSpec((1, 1, TOP_K * tm), lambda i: (i, 0, 0), memory_space=pltpu.SMEM),
                  pl.BlockSpec((1, 1, TOP_K * tm), lambda i: (jnp.minimum(i + 1, n_tiles - 1), 0, 0),
                               memory_space=pltpu.SMEM),
                  pl.BlockSpec(memory_space=pl.ANY),
                  pl.BlockSpec((tm, TOP_K), lambda i: (i, 0)),
                  pl.BlockSpec((tm, d), lambda i: (i, 0)),
                  pl.BlockSpec((1, 6, d), lambda i: (i // per_seq, 0, 0)),
                  _const_spec((1, d))],
        out_specs=pl.BlockSpec((tm, d), lambda i: (i, 0)),
        out_shape=jax.ShapeDtypeStruct((n, d), f32),
        scratch_shapes=[pltpu.VMEM((2, TOP_K * tm, d // 2), jnp.uint32), pltpu.SemaphoreType.DMA((2,))],
        compiler_params=_params(("arbitrary",), vmem),
        name="moe_combine",
    )(pos, pos, y_sorted, gates, xmid, mod, g_post.reshape(1, d))


def _rope_tables(pos):
    def cs(d):
        half = d // 2
        inv_freq = 1.0 / (ROPE_THETA ** (jnp.arange(half, dtype=f32) * (2.0 / d)))
        ang = pos.astype(f32)[:, None] * inv_freq[None, :]
        return jnp.cos(ang), jnp.sin(ang)
    c64, s64 = cs(MLA_ROPE)
    c256, s256 = cs(RET_QK)
    t64 = jnp.concatenate([c64, c64, s64, s64], axis=-1)
    t256 = jnp.concatenate([c256, s256], axis=-1)
    return t64, t256


def _rot_cols(w):
    half = w.shape[-1] // 2
    return jnp.concatenate([-w[..., half:], w[..., :half]], axis=-1)


def _prep_weights(w_in, w_uq, w_ukv, w_out, w_router, b_router):
    b = np.cumsum((Q_LORA, KV_LORA, MLA_ROPE))
    w_pe = w_in[:, b[1]:b[2]]
    w_ext = jnp.concatenate([w_in[:, :b[2]], _rot_cols(w_pe), w_in[:, b[2]:]], axis=-1).astype(bf16)
    wq = w_uq.reshape(Q_LORA, MLA_HEADS, MLA_NOPE + MLA_ROPE)
    wq = jnp.concatenate([wq, _rot_cols(wq[..., MLA_NOPE:])], axis=-1).transpose(1, 0, 2).astype(bf16)
    wkv = w_ukv.reshape(KV_LORA, MLA_HEADS, MLA_NOPE + MLA_V).transpose(1, 0, 2)
    wk = wkv[..., :MLA_NOPE].astype(bf16)
    wv = wkv[..., MLA_NOPE:].astype(bf16)
    r_hi = w_router.astype(bf16)
    r_lo = (w_router - r_hi.astype(f32)).astype(bf16)
    pad = ((0, 0), (0, LANES - N_EXPERTS))
    wr = jnp.stack([jnp.pad(r_hi, pad), jnp.pad(r_lo, pad)])
    br = jnp.pad(b_router, (0, LANES - N_EXPERTS), constant_values=NEG_BIG).reshape(1, LANES)
    return w_ext, wq, wk, wv, w_out.astype(bf16), wr, br


def _mixer_half(x, mod, pos, wts, p, past_lat16, past_kpe16, state0, counts0, ret_chunk, q_tiles, tm):
    w_ext, wq, wk, wv, w_out16, wr, br = wts
    b, s, d = x.shape
    x2d = x.reshape(b * s, d)
    t64, t256 = _rope_tables(pos)
    qlat, kvlat, kvlat16, kpe, kpe16, rq, rk, rv, rg = _inproj(
        x2d, mod, p['g_pre_mix'], w_ext, p['g_q_lat'], p['g_kv_lat'], t64, t256, s, tm)
    lat16 = kvlat16.reshape(b, s, KV_LORA)
    kpe16 = kpe16.reshape(b, s, 2 * MLA_ROPE)
    if past_lat16 is not None:
        lat16 = jnp.concatenate([past_lat16, lat16], axis=1)
        kpe16 = jnp.concatenate([past_kpe16, kpe16], axis=1)
    attn = _mla(qlat.reshape(b, s, Q_LORA), lat16, kpe16, wq, wk, wv, t64, q_tiles)
    log_g = jnp.log1p(-jnp.exp2(-5.0 - jnp.arange(RET_HEADS, dtype=f32)))
    log_g = jnp.broadcast_to(log_g[:, None, None], (RET_HEADS, 8, LANES))
    sh = lambda t, w: t.reshape(b, s, w)
    rn, state = _retention(sh(rq, RET_HEADS * RET_QK), sh(rk, RET_HEADS * RET_QK), sh(rv, RET_WIDTH),
                           sh(rg, RET_WIDTH), p['g_ret'], state0, log_g, ret_chunk)
    xmid, h2p, idx, gates, rank, counts = _mixout(
        attn.reshape(b * s, MLA_WIDTH), rn.reshape(b * s, RET_WIDTH), x2d, mod,
        p['g_post_mix'], p['g_pre_ffn'], w_out16, wr, br, counts0, s, tm)
    route = (idx[:, :TOP_K], gates[:, :TOP_K], rank[:, :TOP_K], counts)
    return kvlat.reshape(b, s, KV_LORA), kpe.reshape(b, s, MLA_ROPE), state, xmid, h2p, route


def _combine_order(dest):
    tm = COMBINE_TOKENS
    n_tiles = dest.shape[0] // tm
    return dest.reshape(n_tiles, tm, TOP_K).transpose(0, 2, 1).reshape(n_tiles, 1, TOP_K * tm)


def kernel(x_prompt, x_sample, cache_kv_latent, cache_k_rope, state_retention, c_prompt, c_sample, w_ada, b_ada, g_pre_mix, g_post_mix, g_pre_ffn, g_post_ffn, w_in, g_q_lat, g_kv_lat, w_uq, w_ukv, g_ret, w_out, w_router, b_router, w_gate_up, b_gate_up, w_down, b_down):
    depth = w_in.shape[0]
    assert depth == 1, "the staged problem has a single layer"
    bp, sp, d = x_prompt.shape
    bs, ss, _ = x_sample.shape
    past = cache_kv_latent.shape[2]
    l = 0
    p = dict(g_pre_mix=g_pre_mix[l], g_post_mix=g_post_mix[l], g_pre_ffn=g_pre_ffn[l], g_post_ffn=g_post_ffn[l],
             g_q_lat=g_q_lat[l], g_kv_lat=g_kv_lat[l], g_ret=g_ret[l])
    wts = _prep_weights(w_in[l], w_uq[l], w_ukv[l], w_out[l], w_router[l], b_router[l])

    mod = _ada(jnp.concatenate([c_prompt, c_sample], axis=0), w_ada[l], b_ada[l]).reshape(bp + bs, 6, d)
    mod_p, mod_s = mod[:bp], mod[bp:]

    tq = ATTN_Q_TILE
    tiles_p = tuple((q0, tq, q0 + tq, True) for q0 in range(0, sp, tq))
    zero_state = jnp.zeros((bp, RET_HEADS, RET_QK, RET_V), f32)
    lat_p, kpe_p, st_p, xmid_p, h2p_p, (idx_p, gates_p, rank_p, counts_p) = _mixer_half(
        x_prompt, mod_p, jnp.arange(sp), wts, p, None, None, zero_state, jnp.zeros((1, LANES), f32),
        RET_CHUNK_PROMPT, tiles_p, 512)

    past_lat16 = cache_kv_latent[l].astype(bf16)
    past_kpe16 = jnp.pad(cache_k_rope[l], ((0, 0), (0, 0), (0, MLA_ROPE))).astype(bf16)
    tiles_s = ((0, ss, past + ss, False),)
    lat_s, kpe_s, st_s, xmid_s, h2p_s, (idx_s, gates_s, rank_s, counts) = _mixer_half(
        x_sample, mod_s, past + jnp.arange(ss), wts, p, past_lat16, past_kpe16, state_retention[l], counts_p,
        ss, tiles_s, ss)

    n_p = bp * sp
    h2p = jnp.concatenate([h2p_p, h2p_s], axis=0)
    dest, row_start, pad_end, n_chunks = _route(
        jnp.concatenate([idx_p, idx_s], axis=0), jnp.concatenate([rank_p, rank_s], axis=0), counts)
    x_sorted = _dispatch(pad_end, n_chunks, dest, h2p)
    act = _moe_up(row_start, n_chunks, x_sorted, w_gate_up[l], b_gate_up[l])
    y_sorted = _moe_down(row_start, n_chunks, act, w_down[l], b_down[l])
    y_p = _combine(_combine_order(dest[:n_p]), y_sorted, gates_p, xmid_p, mod_p, p['g_post_ffn'], sp)
    y_s = _combine(_combine_order(dest[n_p:]), y_sorted, gates_s, xmid_s, mod_s, p['g_post_ffn'], ss)

    return (y_p.reshape(bp, sp, d), y_s.reshape(bs, ss, d),
            lat_p[None], kpe_p[None], st_p[None].astype(state_retention.dtype),
            lat_s[None], kpe_s[None], st_s[None].astype(state_retention.dtype))
```

```python
import functools

import numpy as np
import jax
import jax.numpy as jnp
from jax import lax
from jax.experimental import pallas as pl
from jax.experimental.pallas import tpu as pltpu

CHUNK = 64
EPS = 1e-6
ROPE_THETA = 10000.0

MLA_HEADS = 8
MLA_NOPE = 128
MLA_ROPE = 64
MLA_V = 128
Q_LORA = 512
KV_LORA = 512
MLA_WIDTH = MLA_HEADS * MLA_V

RET_HEADS = 4
RET_QK = 256
RET_V = 256
RET_WIDTH = RET_HEADS * RET_V

N_EXPERTS = 32
TOP_K = 4
D_FF = 2048
SWIGLU_LIMIT = 7.0
SWIGLU_ALPHA = 1.702

LANES = 128
V7X_VMEM_BYTES = 64 * 1024 * 1024
VMEM_CAP_BYTES = V7X_VMEM_BYTES - 8 * 1024 * 1024

RET_CHUNK_PROMPT = 128
RET_UNROLL = 4
ATTN_Q_TILE = 256
MOE_ROWS = 256
UP_TILE = 1024
DOWN_TILE = 2048
COMBINE_TOKENS = 64
NEG_BIG = -1e30

bf16 = jnp.bfloat16
f32 = jnp.float32


def _params(sem, vmem_bytes):
    return pltpu.CompilerParams(dimension_semantics=sem, vmem_limit_bytes=int(min(vmem_bytes, VMEM_CAP_BYTES)))


def _const_spec(shape):
    nd = len(shape)
    return pl.BlockSpec(shape, lambda *_: (0,) * nd, pipeline_mode=pl.Buffered(1))


def _rms(x):
    return x * lax.rsqrt(jnp.mean(x * x, axis=-1, keepdims=True) + EPS)


def _silu(x):
    return x * jax.nn.sigmoid(x)


def _ada_kernel(c_ref, w_ref, b_ref, o_ref):
    s = _silu(c_ref[...]).astype(bf16)
    o_ref[...] = jnp.dot(s, w_ref[...].astype(bf16), preferred_element_type=f32) + b_ref[...]


def _ada(c, w_ada, b_ada):
    nb, d = c.shape
    n = w_ada.shape[1]
    tn = 1536
    return pl.pallas_call(
        _ada_kernel,
        grid=(n // tn,),
        in_specs=[pl.BlockSpec((nb, d), lambda j: (0, 0)),
                  pl.BlockSpec((d, tn), lambda j: (0, j)),
                  pl.BlockSpec((1, tn), lambda j: (0, j))],
        out_specs=pl.BlockSpec((nb, tn), lambda j: (0, j)),
        out_shape=jax.ShapeDtypeStruct((nb, n), f32),
        compiler_params=_params(("arbitrary",), 2 * d * tn * 4 + d * tn * 2 + (8 << 20)),
        name="ada_modulation",
    )(c, w_ada, b_ada.reshape(1, n))


_C_Q = 0
_C_KV = _C_Q + Q_LORA
_C_PE = _C_KV + KV_LORA
_C_RQ = _C_PE + 2 * MLA_ROPE
_C_RK = _C_RQ + RET_HEADS * RET_QK
_C_RV = _C_RK + RET_HEADS * RET_QK
_C_RG = _C_RV + RET_WIDTH
_C_END = _C_RG + RET_WIDTH


def _rope_pair(t):
    return t + pltpu.roll(t, MLA_ROPE, axis=1)


def _rope_heads(z, cos, sin, scale):
    outs = []
    half = RET_QK // 2
    for h in range(RET_HEADS):
        x1 = z[:, h * RET_QK:h * RET_QK + half]
        x2 = z[:, h * RET_QK + half:(h + 1) * RET_QK]
        outs.append((x1 * cos - x2 * sin) * scale)
        outs.append((x1 * sin + x2 * cos) * scale)
    return jnp.concatenate(outs, axis=-1)


def _inproj_kernel(x_ref, mod_ref, g_ref, w_ref, gq_ref, gkv_ref, t64_ref, t256_ref,
                   qlat_ref, kvlat_ref, kvlat16_ref, kpe_ref, kpe16_ref, rq_ref, rk_ref, rv_ref, rg_ref):
    m = mod_ref[0]
    h = (_rms(x_ref[...]) * g_ref[...] * (1.0 + m[1:2]) + m[0:1]).astype(bf16)

    def proj(c0, c1):
        return jnp.dot(h, w_ref[:, c0:c1], preferred_element_type=f32)

    qlat_ref[...] = (_rms(proj(_C_Q, _C_KV)) * gq_ref[...]).astype(bf16)
    kv = _rms(proj(_C_KV, _C_PE)) * gkv_ref[...]
    kvlat_ref[...] = kv
    kvlat16_ref[...] = kv.astype(bf16)
    pe = _rope_pair(proj(_C_PE, _C_RQ) * t64_ref[...])
    kpe_ref[...] = pe[:, :MLA_ROPE]
    lane = lax.broadcasted_iota(jnp.int32, pe.shape, 1)
    kpe16_ref[...] = jnp.where(lane < MLA_ROPE, pe, 0.0).astype(bf16)
    cos = t256_ref[:, :RET_QK // 2]
    sin = t256_ref[:, RET_QK // 2:]
    rq_ref[...] = _rope_heads(proj(_C_RQ, _C_RK), cos, sin, 1.0).astype(bf16)
    rk_ref[...] = _rope_heads(proj(_C_RK, _C_RV), cos, sin, RET_QK ** -0.5).astype(bf16)
    rv_ref[...] = proj(_C_RV, _C_RG).astype(bf16)
    rg_ref[...] = proj(_C_RG, _C_END).astype(bf16)


def _inproj(x2d, mod, g_pre, w_ext, g_q, g_kv, t64, t256, seq, tm):
    n, d = x2d.shape
    per_seq = seq // tm
    row = lambda w: pl.BlockSpec((tm, w), lambda i: (i, 0))
    tab = lambda w: pl.BlockSpec((tm, w), lambda i: (i % per_seq, 0))
    outs = [(Q_LORA, bf16), (KV_LORA, f32), (KV_LORA, bf16), (MLA_ROPE, f32), (2 * MLA_ROPE, bf16),
            (RET_HEADS * RET_QK, bf16), (RET_HEADS * RET_QK, bf16), (RET_WIDTH, bf16), (RET_WIDTH, bf16)]
    vmem = w_ext.size * 2 + 2 * tm * d * 4 + 4 * tm * d * 4 + sum(2 * tm * w * 4 for w, _ in outs) + (6 << 20)
    return pl.pallas_call(
        _inproj_kernel,
        grid=(n // tm,),
        in_specs=[row(d),
                  pl.BlockSpec((1, 6, d), lambda i: (i // per_seq, 0, 0)),
                  _const_spec((1, d)), _const_spec(w_ext.shape), _const_spec((1, Q_LORA)), _const_spec((1, KV_LORA)),
                  tab(2 * MLA_ROPE), tab(RET_QK)],
        out_specs=[row(w) for w, _ in outs],
        out_shape=[jax.ShapeDtypeStruct((n, w), dt) for w, dt in outs],
        compiler_params=_params(("arbitrary",), vmem),
        name="in_projection",
    )(x2d, mod, g_pre.reshape(1, d), w_ext, g_q.reshape(1, Q_LORA), g_kv.reshape(1, KV_LORA), t64, t256)


def _mla_kernel(qlat_ref, lat_ref, kpe_ref, wq_ref, wkv_ref, tab_ref, o_ref, q_s, k_s, v_s, *, q_tiles):
    scale = (MLA_NOPE + MLA_ROPE) ** -0.5
    qh = jnp.dot(qlat_ref[0], wq_ref[0], preferred_element_type=f32)
    q_s[:, :MLA_NOPE] = (qh[:, :MLA_NOPE] * scale).astype(bf16)
    q_s[:, MLA_NOPE:] = (_rope_pair(qh[:, MLA_NOPE:] * tab_ref[...]) * scale).astype(bf16)
    kv = jnp.dot(lat_ref[0], wkv_ref[0], preferred_element_type=f32)
    k_s[:, :MLA_NOPE] = kv[:, :MLA_NOPE].astype(bf16)
    k_s[:, MLA_NOPE:] = kpe_ref[0]
    v_s[...] = kv[:, MLA_NOPE:].astype(bf16)
    for q0, ql, kvl, masked in q_tiles:
        q = q_s[q0:q0 + ql, :]
        s = lax.dot_general(q, k_s[0:kvl, :], (((1,), (1,)), ((), ())), preferred_element_type=f32)
        if masked:
            assert kvl == q0 + ql and q0 % CHUNK == 0
            qc = lax.broadcasted_iota(jnp.int32, (ql, 1), 0) // CHUNK
            kc = lax.broadcasted_iota(jnp.int32, (1, ql), 1) // CHUNK
            diag = jnp.where(kc <= qc, s[:, q0:], NEG_BIG)
            s = diag if q0 == 0 else jnp.concatenate([s[:, :q0], diag], axis=1)
        p = jnp.exp(s - jnp.max(s, axis=-1, keepdims=True))
        l = jnp.sum(p, axis=-1, keepdims=True)
        o = jnp.dot(p.astype(bf16), v_s[0:kvl, :], preferred_element_type=f32)
        o_ref[0, q0:q0 + ql, :] = (o / l).astype(bf16)


def _mla(qlat, lat, kpe16, wq, wkv, tab, q_tiles):
    b, sq, _ = qlat.shape
    skv = lat.shape[1]
    kern = functools.partial(_mla_kernel, q_tiles=q_tiles)
    max_ql = max(t[1] for t in q_tiles)
    vmem = 2 * 2 * (sq * Q_LORA + skv * KV_LORA + skv * LANES) + sq * LANES * 4 + sq * 3 * LANES * 4 \
        + (sq + skv) * 3 * LANES * 2 + 4 * max_ql * skv * 4 + (8 << 20)
    return pl.pallas_call(
        kern,
        grid=(b, MLA_HEADS),
        in_specs=[pl.BlockSpec((1, sq, Q_LORA), lambda i, h: (i, 0, 0)),
                  pl.BlockSpec((1, skv, KV_LORA), lambda i, h: (i, 0, 0)),
                  pl.BlockSpec((1, skv, 2 * MLA_ROPE), lambda i, h: (i, 0, 0)),
                  pl.BlockSpec((1, Q_LORA, MLA_NOPE + 2 * MLA_ROPE), lambda i, h: (h, 0, 0)),
                  pl.BlockSpec((1, KV_LORA, MLA_NOPE + MLA_V), lambda i, h: (h, 0, 0)),
                  pl.BlockSpec((sq, 2 * MLA_ROPE), lambda i, h: (0, 0))],
        out_specs=pl.BlockSpec((1, sq, MLA_V), lambda i, h: (i, 0, h)),
        out_shape=jax.ShapeDtypeStruct((b, sq, MLA_WIDTH), bf16),
        scratch_shapes=[pltpu.VMEM((sq, MLA_NOPE + 2 * MLA_ROPE), bf16),
                        pltpu.VMEM((skv, MLA_NOPE + 2 * MLA_ROPE), bf16),
                        pltpu.VMEM((skv, MLA_V), bf16)],
        compiler_params=_params(("arbitrary", "arbitrary"), vmem),
        name="mla_attention",
    )(qlat, lat, kpe16, wq, wkv, tab)


def _ret_kernel(lg_ref, rq_ref, rk_ref, rv_ref, rg_ref, gret_ref, s0_ref, o_ref, sout_ref, state, *, chunk, n_chunks):
    c = chunk
    lg = lg_ref[0, 0:1, 0:1]
    ii = lax.broadcasted_iota(jnp.int32, (c, c), 0)
    jj = lax.broadcasted_iota(jnp.int32, (c, c), 1)
    diff = (ii - jj).astype(f32)
    dmask = jnp.where(diff >= 0.0, jnp.exp(jnp.maximum(diff, 0.0) * lg), 0.0)
    ic = lax.broadcasted_iota(jnp.int32, (c, 1), 0).astype(f32)
    q_decay = jnp.exp((ic + 1.0) * lg)
    k_decay = jnp.exp((c - 1.0 - ic) * lg)
    chunk_decay = jnp.exp(float(c) * lg)
    gret = gret_ref[...]
    state[...] = s0_ref[0, 0]

    def step(n, carry):
        sl = pl.ds(pl.multiple_of(n * c, c), c)
        q = rq_ref[0, sl, :]
        k = rk_ref[0, sl, :]
        v = rv_ref[0, sl, :]
        st = state[...]
        attn = lax.dot_general(q, k, (((1,), (1,)), ((), ())), preferred_element_type=f32) * dmask
        inner = jnp.dot(attn.astype(bf16), v, preferred_element_type=f32)
        cross = jnp.dot(q, st.astype(bf16), preferred_element_type=f32) * q_decay
        o = inner + cross
        kd_t = (k.astype(f32) * k_decay).T.astype(bf16)
        state[...] = st * chunk_decay + jnp.dot(kd_t, v, preferred_element_type=f32)
        mu = jnp.mean(o, axis=-1, keepdims=True)
        oc = o - mu
        rn = oc * lax.rsqrt(jnp.mean(oc * oc, axis=-1, keepdims=True) + EPS) * gret
        o_ref[0, sl, :] = (rn * _silu(rg_ref[0, sl, :].astype(f32))).astype(bf16)
        return carry

    lax.fori_loop(0, n_chunks, step, 0, unroll=min(n_chunks, RET_UNROLL))
    sout_ref[0, 0] = state[...]


def _retention(rq, rk, rv, rg, g_ret, state0, log_g, chunk):
    b, s, _ = rq.shape
    kern = functools.partial(_ret_kernel, chunk=chunk, n_chunks=s // chunk)
    head = lambda w: pl.BlockSpec((1, s, w), lambda i, h: (i, 0, h))
    st = pl.BlockSpec((1, 1, RET_QK, RET_V), lambda i, h: (i, h, 0, 0))
    vmem = 2 * 5 * s * RET_QK * 2 + 5 * RET_QK * RET_V * 4 + (8 << 20)
    return pl.pallas_call(
        kern,
        grid=(b, RET_HEADS),
        in_specs=[pl.BlockSpec((1, 8, LANES), lambda i, h: (h, 0, 0)),
                  head(RET_QK), head(RET_QK), head(RET_V), head(RET_V),
                  pl.BlockSpec((1, RET_V), lambda i, h: (0, h)), st],
        out_specs=[head(RET_V), st],
        out_shape=[jax.ShapeDtypeStruct((b, s, RET_WIDTH), bf16),
                   jax.ShapeDtypeStruct((b, RET_HEADS, RET_QK, RET_V), f32)],
        scratch_shapes=[pltpu.VMEM((RET_QK, RET_V), f32)],
        compiler_params=_params(("arbitrary", "arbitrary"), vmem),
        name="retention",
    )(log_g, rq, rk, rv, rg, g_ret.reshape(1, RET_WIDTH), state0)


def _pack_pairs(lo16, hi16):
    lo = lax.bitcast_convert_type(lo16.astype(f32), jnp.uint32)
    hi = lax.bitcast_convert_type(hi16.astype(f32), jnp.uint32)
    return (hi & jnp.uint32(0xFFFF0000)) | (lo >> 16)


def _unpack_pairs(u):
    lo = lax.bitcast_convert_type(u << 16, f32)
    hi = lax.bitcast_convert_type(u & jnp.uint32(0xFFFF0000), f32)
    return lo, hi


def _mixout_kernel(attn_ref, rn_ref, x_ref, mod_ref, gpost_ref, gpre_ref, wout_ref, wr_ref, br_ref, cnt0_ref,
                   xmid_ref, h2p_ref, idx_ref, gate_ref, rank_ref, cnt_ref, carry):
    @pl.when(pl.program_id(0) == 0)
    def _():
        carry[...] = cnt0_ref[...]

    mix = (jnp.dot(attn_ref[...], wout_ref[0:MLA_WIDTH, :], preferred_element_type=f32)
           + jnp.dot(rn_ref[...], wout_ref[MLA_WIDTH:, :], preferred_element_type=f32))
    m = mod_ref[0]
    x1 = x_ref[...] + m[2:3] * (_rms(mix) * gpost_ref[...])
    xmid_ref[...] = x1
    h2 = _rms(x1) * gpre_ref[...] * (1.0 + m[4:5]) + m[3:4]
    h_hi = h2.astype(bf16)
    half = h2.shape[1] // 2
    h2p_ref[...] = _pack_pairs(h_hi[:, :half], h_hi[:, half:])
    h_lo = (h2 - h_hi.astype(f32)).astype(bf16)
    logits = (jnp.dot(h_hi, wr_ref[0], preferred_element_type=f32)
              + jnp.dot(h_lo, wr_ref[0], preferred_element_type=f32)
              + jnp.dot(h_hi, wr_ref[1], preferred_element_type=f32)) + br_ref[...]
    tm = logits.shape[0]
    col = lax.broadcasted_iota(jnp.int32, logits.shape, 1).astype(f32)
    vals, idxs = [], []
    for _ in range(TOP_K):
        mx = jnp.max(logits, axis=-1, keepdims=True)
        ix = jnp.min(jnp.where(logits == mx, col, float(LANES)), axis=-1, keepdims=True)
        vals.append(mx)
        idxs.append(ix)
        logits = jnp.where(col == ix, -jnp.inf, logits)
    es = [jnp.exp(v - vals[0]) for v in vals]
    tot = es[0] + es[1] + es[2] + es[3]
    onehot = jnp.zeros_like(col)
    for k in range(TOP_K):
        onehot = jnp.where(col == idxs[k], 1.0, onehot)
    earlier = (lax.broadcasted_iota(jnp.int32, (tm, tm), 0) > lax.broadcasted_iota(jnp.int32, (tm, tm), 1))
    before = jnp.dot(jnp.where(earlier, 1.0, 0.0).astype(bf16), onehot.astype(bf16),
                     preferred_element_type=f32) + carry[...]
    idx_out = jnp.zeros_like(col)
    gate_out = jnp.zeros_like(col)
    rank_out = jnp.zeros_like(col)
    for k in range(TOP_K):
        slot = col == float(k)
        idx_out = jnp.where(slot, idxs[k], idx_out)
        gate_out = jnp.where(slot, es[k] / tot, gate_out)
        rank_k = jnp.sum(jnp.where(col == idxs[k], before, 0.0), axis=-1, keepdims=True)
        rank_out = jnp.where(slot, rank_k, rank_out)
    idx_ref[...] = idx_out.astype(jnp.int32)
    gate_ref[...] = gate_out
    rank_ref[...] = rank_out.astype(jnp.int32)
    carry[...] = carry[...] + jnp.sum(onehot, axis=0, keepdims=True)
    cnt_ref[...] = carry[...]


def _mixout(attn, rn, x2d, mod, g_post, g_pre, w_out16, wr, br, counts0, seq, tm):
    n, d = x2d.shape
    per_seq = seq // tm
    row = lambda w: pl.BlockSpec((tm, w), lambda i: (i, 0))
    vmem = w_out16.size * 2 + 2 * tm * d * (4 + 4 + 2 + 2) + 6 * tm * d * 4 + (8 << 20)
    return pl.pallas_call(
        _mixout_kernel,
        grid=(n // tm,),
        in_specs=[row(MLA_WIDTH), row(RET_WIDTH), row(d),
                  pl.BlockSpec((1, 6, d), lambda i: (i // per_seq, 0, 0)),
                  _const_spec((1, d)), _const_spec((1, d)), _const_spec(w_out16.shape),
                  _const_spec(wr.shape), _const_spec((1, LANES)), _const_spec((1, LANES))],
        out_specs=[row(d), row(d // 2), row(LANES), row(LANES), row(LANES),
                   pl.BlockSpec((1, LANES), lambda i: (0, 0))],
        out_shape=[jax.ShapeDtypeStruct((n, d), f32), jax.ShapeDtypeStruct((n, d // 2), jnp.uint32),
                   jax.ShapeDtypeStruct((n, LANES), jnp.int32), jax.ShapeDtypeStruct((n, LANES), f32),
                   jax.ShapeDtypeStruct((n, LANES), jnp.int32), jax.ShapeDtypeStruct((1, LANES), f32)],
        scratch_shapes=[pltpu.VMEM((1, LANES), f32)],
        compiler_params=_params(("arbitrary",), vmem),
        name="mixer_out_router",
    )(attn, rn, x2d, mod, g_post.reshape(1, d), g_pre.reshape(1, d), w_out16, wr, br, counts0)


def _route(top_idx, rank, counts):
    counts = counts[0, :N_EXPERTS].astype(jnp.int32)
    padded = (counts + MOE_ROWS - 1) // MOE_ROWS * MOE_ROWS
    pad_end = jnp.cumsum(padded).astype(jnp.int32)
    pad_start = pad_end - padded
    dest = pad_start[top_idx] + rank
    return dest, pad_start, pad_end, padded // MOE_ROWS


def _n_sorted_rows(n_tokens):
    return (n_tokens * TOP_K + N_EXPERTS * (MOE_ROWS - 1)) // MOE_ROWS * MOE_ROWS


def _fill_unused_chunks(zero_buf, dst_rows, sem, used_rows, total_rows):
    first = used_rows // MOE_ROWS

    def copy(c):
        return pltpu.make_async_copy(zero_buf, dst_rows(pl.ds(pl.multiple_of(c * MOE_ROWS, MOE_ROWS), MOE_ROWS)), sem)

    def start(c, carry):
        copy(c).start()
        return carry

    def wait(c, carry):
        copy(c).wait()
        return carry

    lax.fori_loop(first, total_rows // MOE_ROWS, start, 0)
    lax.fori_loop(first, total_rows // MOE_ROWS, wait, 0)


def _dispatch_kernel(pe_ref, nc_ref, dest_ref, h2p_a_ref, h2p_b_ref, x_hbm, zbuf, sem, zsem, *, tokens, n_first):
    i = pl.program_id(0)

    @pl.when(i == 0)
    def _():
        zbuf[...] = jnp.zeros_like(zbuf)

        def tail(e):
            start = pl.multiple_of(pe_ref[e] - MOE_ROWS, MOE_ROWS)
            return pltpu.make_async_copy(zbuf, x_hbm.at[pl.ds(start, MOE_ROWS), :], zsem)

        def start_tail(e, c):
            @pl.when(nc_ref[e] > 0)
            def _():
                tail(e).start()
            return c

        def wait_tail(e, c):
            @pl.when(nc_ref[e] > 0)
            def _():
                tail(e).wait()
            return c

        lax.fori_loop(0, N_EXPERTS, start_tail, 0)
        lax.fori_loop(0, N_EXPERTS, wait_tail, 0)
        _fill_unused_chunks(zbuf, lambda rows: x_hbm.at[rows, :], zsem, pe_ref[N_EXPERTS - 1], x_hbm.shape[0])

    def scatter(h2p_ref):
        def body(t, c):
            for k in range(TOP_K):
                d = dest_ref[0, 0, t * TOP_K + k]
                pltpu.make_async_copy(h2p_ref.at[pl.ds(t, 1), :], x_hbm.at[pl.ds(d, 1), :], sem).start(priority=k % 2)
            return c

        lax.fori_loop(0, tokens, body, 0, unroll=2)
        for _ in range(TOP_K):
            pltpu.make_async_copy(h2p_ref, x_hbm.at[pl.ds(0, tokens), :], sem).wait()

    @pl.when(i < n_first)
    def _():
        scatter(h2p_a_ref)

    @pl.when(i >= n_first)
    def _():
        scatter(h2p_b_ref)


def _dispatch(pad_end, n_chunks, dest, h2p_a, h2p_b):
    w = h2p_a.shape[1]
    tokens = 512
    n_a, n_b = h2p_a.shape[0] // tokens, h2p_b.shape[0] // tokens
    assert n_a * tokens == h2p_a.shape[0] and n_b * tokens == h2p_b.shape[0]
    n_steps = n_a + n_b
    grid_spec = pltpu.PrefetchScalarGridSpec(
        num_scalar_prefetch=2,
        grid=(n_steps,),
        in_specs=[pl.BlockSpec((1, 1, tokens * TOP_K), lambda i, pe, nc: (i, 0, 0), memory_space=pltpu.SMEM),
                  pl.BlockSpec((tokens, w), lambda i, pe, nc: (jnp.minimum(i, n_a - 1), 0)),
                  pl.BlockSpec((tokens, w), lambda i, pe, nc: (jnp.maximum(i - n_a, 0), 0))],
        out_specs=pl.BlockSpec(memory_space=pl.ANY),
        scratch_shapes=[pltpu.VMEM((MOE_ROWS, w), jnp.uint32), pltpu.SemaphoreType.DMA(()),
                        pltpu.SemaphoreType.DMA(())],
    )
    return pl.pallas_call(
        functools.partial(_dispatch_kernel, tokens=tokens, n_first=n_a),
        grid_spec=grid_spec,
        out_shape=jax.ShapeDtypeStruct((_n_sorted_rows(n_steps * tokens), w), jnp.uint32),
        compiler_params=_params(("arbitrary",), 24 << 20),
        name="moe_dispatch",
    )(pad_end, n_chunks, dest.reshape(n_steps, 1, tokens * TOP_K), h2p_a, h2p_b)


def _chunk_rows(r0, c):
    return pl.ds(pl.multiple_of(r0 + c * MOE_ROWS, MOE_ROWS), MOE_ROWS)


CHUNK_READ_PRIORITY = 1
READ_SLOTS = 3


def _start_first_reads(copies, n):
    for a in range(READ_SLOTS - 1):
        @pl.when(a < n)
        def _():
            for cp in copies(a, a):
                cp.start(priority=CHUNK_READ_PRIORITY)


def _read_ahead(copies, c, n):
    ahead = c + (READ_SLOTS - 1)

    @pl.when(ahead < n)
    def _():
        for cp in copies(ahead, ahead % READ_SLOTS):
            cp.start(priority=CHUNK_READ_PRIORITY)

    rslot = c % READ_SLOTS
    for cp in copies(c, rslot):
        cp.wait()
    return rslot


def _moe_up_kernel(rs_ref, nc_ref, x_hbm, wg_ref, wu_ref, bg_ref, bu_ref, act_hbm,
                   wg16, wu16, xbuf, obuf, sem_in, sem_out):
    e = pl.program_id(0)
    j = pl.program_id(1)
    n = nc_ref[e]
    r0 = rs_ref[e]

    def in_copy(c, slot):
        return [pltpu.make_async_copy(x_hbm.at[_chunk_rows(r0, c), :], xbuf.at[slot], sem_in.at[slot])]

    def out_copy(c, slot):
        return pltpu.make_async_copy(obuf.at[slot], act_hbm.at[j, _chunk_rows(r0, c), :], sem_out.at[slot])

    @pl.when(n > 0)
    def _():
        _start_first_reads(in_copy, n)
        wg16[...] = wg_ref[0].astype(bf16)
        wu16[...] = wu_ref[0].astype(bf16)
        half = wg16.shape[0] // 2

        def body(c, carry):
            slot = c % 2
            rslot = _read_ahead(in_copy, c, n)
            lo, hi = _unpack_pairs(xbuf[rslot])
            x_lo = lo.astype(bf16)
            x_hi = hi.astype(bf16)

            def proj(w16, b_ref):
                return (jnp.dot(x_lo, w16[0:half, :], preferred_element_type=f32)
                        + jnp.dot(x_hi, w16[half:, :], preferred_element_type=f32) + b_ref[0])

            gate = jnp.minimum(proj(wg16, bg_ref), SWIGLU_LIMIT)
            up = jnp.clip(proj(wu16, bu_ref), -SWIGLU_LIMIT, SWIGLU_LIMIT)
            act = (up + 1.0) * gate * jax.nn.sigmoid(SWIGLU_ALPHA * gate)

            @pl.when(c >= 2)
            def _():
                out_copy(c - 2, slot).wait()

            obuf[slot] = act.astype(bf16)
            out_copy(c, slot).start()
            return carry

        lax.fori_loop(0, n, body, 0)

        @pl.when(n >= 2)
        def _():
            out_copy(n - 2, n % 2).wait()

        out_copy(n - 1, (n - 1) % 2).wait()

    @pl.when(e == N_EXPERTS - 1)
    def _():
        obuf[0] = jnp.zeros(obuf.shape[1:], obuf.dtype)
        _fill_unused_chunks(obuf.at[0], lambda rows: act_hbm.at[j, rows, :], sem_out.at[0],
                            r0 + n * MOE_ROWS, act_hbm.shape[1])


def _moe_up(row_start, n_chunks, x_sorted, w_gate_up, b_gate_up):
    e, d, f2 = w_gate_up.shape
    tf = UP_TILE
    n_j = D_FF // tf
    rows = x_sorted.shape[0]
    vmem = 2 * 2 * d * tf * 4 + 2 * d * tf * 2 + 2 * MOE_ROWS * d * 2 + 2 * MOE_ROWS * tf * 2 \
        + 6 * MOE_ROWS * tf * 4 + (6 << 20)
    grid_spec = pltpu.PrefetchScalarGridSpec(
        num_scalar_prefetch=2,
        grid=(e, n_j),
        in_specs=[pl.BlockSpec(memory_space=pl.ANY),
                  pl.BlockSpec((1, d, tf), lambda i, j, rs, nc: (i, 0, j)),
                  pl.BlockSpec((1, d, tf), lambda i, j, rs, nc: (i, 0, n_j + j)),
                  pl.BlockSpec((1, 1, tf), lambda i, j, rs, nc: (i, 0, j)),
                  pl.BlockSpec((1, 1, tf), lambda i, j, rs, nc: (i, 0, n_j + j))],
        out_specs=pl.BlockSpec(memory_space=pl.ANY),
        scratch_shapes=[pltpu.VMEM((d, tf), bf16), pltpu.VMEM((d, tf), bf16),
                        pltpu.VMEM((READ_SLOTS, MOE_ROWS, d // 2), jnp.uint32), pltpu.VMEM((2, MOE_ROWS, tf), bf16),
                        pltpu.SemaphoreType.DMA((READ_SLOTS,)), pltpu.SemaphoreType.DMA((2,))],
    )
    b3 = b_gate_up.reshape(e, 1, f2)
    return pl.pallas_call(
        _moe_up_kernel,
        grid_spec=grid_spec,
        out_shape=jax.ShapeDtypeStruct((n_j, rows, tf), bf16),
        compiler_params=_params(("arbitrary", "arbitrary"), vmem),
        name="moe_gate_up",
    )(row_start, n_chunks, x_sorted, w_gate_up, w_gate_up, b3, b3)


def _moe_down_kernel(rs_ref, nc_ref, act_hbm, wd_ref, bd_ref, y_hbm, wd16, abuf, ybuf, sem_in, sem_out):
    e = pl.program_id(0)
    nt = pl.program_id(1)
    n = nc_ref[e]
    r0 = rs_ref[e]
    n_k, tf = abuf.shape[1], abuf.shape[3]
    tw = ybuf.shape[2]
    n_t = y_hbm.shape[1] // tw

    def on_tile(fn):
        for t in range(n_t):
            @pl.when(nt == t)
            def _():
                fn(t)

    def in_copies(c, slot):
        return [pltpu.make_async_copy(act_hbm.at[k, _chunk_rows(r0, c), :], abuf.at[slot, k], sem_in.at[slot])
                for k in range(n_k)]

    def out_copy(c, slot, t):
        return pltpu.make_async_copy(ybuf.at[slot], y_hbm.at[_chunk_rows(r0, c), t * tw:(t + 1) * tw],
                                     sem_out.at[slot])

    @pl.when(n > 0)
    def _():
        _start_first_reads(in_copies, n)
        wd16[...] = wd_ref[0].astype(bf16)

        def body(c, carry):
            slot = c % 2
            rslot = _read_ahead(in_copies, c, n)
            y = bd_ref[0]
            for k in range(n_k):
                y = y + jnp.dot(abuf[rslot, k], wd16[k * tf:(k + 1) * tf, :], preferred_element_type=f32)
            y16 = y.astype(bf16)

            @pl.when(c >= 2)
            def _():
                on_tile(lambda t: out_copy(c - 2, slot, t).wait())

            ybuf[slot] = _pack_pairs(y16[:, :tw], y16[:, tw:])
            on_tile(lambda t: out_copy(c, slot, t).start())
            return carry

        lax.fori_loop(0, n, body, 0)

        @pl.when(n >= 2)
        def _():
            on_tile(lambda t: out_copy(n - 2, n % 2, t).wait())

        on_tile(lambda t: out_copy(n - 1, (n - 1) % 2, t).wait())

    @pl.when(e == N_EXPERTS - 1)
    def _():
        ybuf[0] = jnp.zeros(ybuf.shape[1:], ybuf.dtype)
        on_tile(lambda t: _fill_unused_chunks(ybuf.at[0], lambda rows: y_hbm.at[rows, t * tw:(t + 1) * tw],
                                              sem_out.at[0], r0 + n * MOE_ROWS, y_hbm.shape[0]))


def _moe_down(row_start, n_chunks, act, w_down, b_down):
    e, f, d = w_down.shape
    n_k, rows, tf = act.shape
    tn = DOWN_TILE
    vmem = 2 * f * tn * 4 + f * tn * 2 + 2 * MOE_ROWS * f * 2 + 2 * MOE_ROWS * tn * 2 + 4 * MOE_ROWS * tn * 4 + (6 << 20)
    grid_spec = pltpu.PrefetchScalarGridSpec(
        num_scalar_prefetch=2,
        grid=(e, d // tn),
        in_specs=[pl.BlockSpec(memory_space=pl.ANY),
                  pl.BlockSpec((1, f, tn), lambda i, t, rs, nc: (i, 0, t)),
                  pl.BlockSpec((1, 1, tn), lambda i, t, rs, nc: (i, 0, t))],
        out_specs=pl.BlockSpec(memory_space=pl.ANY),
        scratch_shapes=[pltpu.VMEM((f, tn), bf16), pltpu.VMEM((READ_SLOTS, n_k, MOE_ROWS, tf), bf16),
                        pltpu.VMEM((2, MOE_ROWS, tn // 2), jnp.uint32),
                        pltpu.SemaphoreType.DMA((READ_SLOTS,)), pltpu.SemaphoreType.DMA((2,))],
    )
    return pl.pallas_call(
        _moe_down_kernel,
        grid_spec=grid_spec,
        out_shape=jax.ShapeDtypeStruct((rows, d // 2), jnp.uint32),
        compiler_params=_params(("arbitrary", "arbitrary"), vmem),
        name="moe_down",
    )(row_start, n_chunks, act, w_down, b_down.reshape(e, 1, d))


def _gather_rows(idx_ref, src_hbm, dst, sem, n_rows):
    def body(r2, carry):
        for u in range(2):
            r = 2 * r2 + u
            t = idx_ref[0, 0, r]
            pltpu.make_async_copy(src_hbm.at[pl.ds(t, 1), :], dst.at[pl.ds(r, 1), :], sem).start(priority=u)
        return carry
    lax.fori_loop(0, n_rows // 2, body, 0, unroll=4)


def _wait_rows(src_hbm, dst, sem, n_rows):
    pltpu.make_async_copy(src_hbm.at[pl.ds(0, n_rows), :], dst, sem).wait()


def _prefetched_gather(i, n_live, cur_idx_ref, next_idx_ref, src_hbm, buf, sem, n_rows):
    slot = i % 2

    @pl.when(jnp.logical_and(i == 0, n_live > 0))
    def _():
        _gather_rows(cur_idx_ref, src_hbm, buf.at[0], sem.at[0], n_rows)

    @pl.when(i + 1 < n_live)
    def _():
        _gather_rows(next_idx_ref, src_hbm, buf.at[1 - slot], sem.at[1 - slot], n_rows)

    @pl.when(i < n_live)
    def _():
        _wait_rows(src_hbm, buf.at[slot], sem.at[slot], n_rows)

    return slot


def _combine_kernel(pos_ref, posn_ref, y_hbm, gates_ref, xmid_ref, mod_ref, gpost_ref, out_ref, ybuf, sem, *, n_steps):
    i = pl.program_id(0)
    tm = COMBINE_TOKENS
    slot = _prefetched_gather(i, n_steps, pos_ref, posn_ref, y_hbm, ybuf, sem, TOP_K * tm)
    g = gates_ref[...]
    lo = hi = None
    for k in range(TOP_K):
        l_k, h_k = _unpack_pairs(ybuf[slot, k * tm:(k + 1) * tm, :])
        gk = g[:, k:k + 1]
        lo = gk * l_k if lo is None else lo + gk * l_k
        hi = gk * h_k if hi is None else hi + gk * h_k
    half = lo.shape[1]
    inv = lax.rsqrt((jnp.sum(lo * lo, axis=-1, keepdims=True) + jnp.sum(hi * hi, axis=-1, keepdims=True))
                    / (2 * half) + EPS)
    gate2 = mod_ref[0][5:6]
    gp = gpost_ref[...]
    q = DOWN_TILE // 2
    for t in range(half // q):
        for src, c0 in ((lo, t * DOWN_TILE), (hi, t * DOWN_TILE + q)):
            cols = slice(c0, c0 + q)
            out_ref[:, cols] = xmid_ref[:, cols] + gate2[:, cols] * (src[:, t * q:(t + 1) * q] * inv * gp[:, cols])


def _combine(pos, y_sorted, gates, xmid, mod, g_post, seq):
    n, d = xmid.shape
    tm = COMBINE_TOKENS
    n_tiles = n // tm
    per_seq = seq // tm
    vmem = 2 * TOP_K * tm * d * 2 + 4 * tm * d * 4 + 6 * tm * d * 4 + (8 << 20)
    return pl.pallas_call(
        functools.partial(_combine_kernel, n_steps=n_tiles),
        grid=(n_tiles,),
        in_specs=[pl.BlockSpec((1, 1, TOP_K * tm), lambda i: (i, 0, 0), memory_space=pltpu.SMEM),
                  pl.BlockSpec((1, 1, TOP_K * tm), lambda i: (jnp.minimum(i + 1, n_tiles - 1), 0, 0),
                               memory_space=pltpu.SMEM),
                  pl.BlockSpec(memory_space=pl.ANY),
                  pl.BlockSpec((tm, TOP_K), lambda i: (i, 0)),
                  pl.BlockSpec((tm, d), lambda i: (i, 0)),
                  pl.BlockSpec((1, 6, d), lambda i: (i // per_seq, 0, 0)),
                  _const_spec((1, d))],
        out_specs=pl.BlockSpec((tm, d), lambda i: (i, 0)),
        out_shape=jax.ShapeDtypeStruct((n, d), f32),
        scratch_shapes=[pltpu.VMEM((2, TOP_K * tm, d // 2), jnp.uint32), pltpu.SemaphoreType.DMA((2,))],
        compiler_params=_params(("arbitrary",), vmem),
        name="moe_combine",
    )(pos, pos, y_sorted, gates, xmid, mod, g_post.reshape(1, d))


def _rope_tables(pos):
    def cs(d):
        half = d // 2
        inv_freq = 1.0 / (ROPE_THETA ** (jnp.arange(half, dtype=f32) * (2.0 / d)))
        ang = pos.astype(f32)[:, None] * inv_freq[None, :]
        return jnp.cos(ang), jnp.sin(ang)
    c64, s64 = cs(MLA_ROPE)
    c256, s256 = cs(RET_QK)
    t64 = jnp.concatenate([c64, c64, s64, s64], axis=-1)
    t256 = jnp.concatenate([c256, s256], axis=-1)
    return t64, t256


def _rot_cols(w):
    half = w.shape[-1] // 2
    return jnp.concatenate([-w[..., half:], w[..., :half]], axis=-1)


def _prep_weights(w_in, w_uq, w_ukv, w_out, w_router, b_router):
    b = np.cumsum((Q_LORA, KV_LORA, MLA_ROPE))
    w_pe = w_in[:, b[1]:b[2]]
    w_ext = jnp.concatenate([w_in[:, :b[2]], _rot_cols(w_pe), w_in[:, b[2]:]], axis=-1).astype(bf16)
    wq = w_uq.reshape(Q_LORA, MLA_HEADS, MLA_NOPE + MLA_ROPE)
    wq = jnp.concatenate([wq, _rot_cols(wq[..., MLA_NOPE:])], axis=-1).transpose(1, 0, 2).astype(bf16)
    wkv = w_ukv.reshape(KV_LORA, MLA_HEADS, MLA_NOPE + MLA_V).transpose(1, 0, 2).astype(bf16)
    r_hi = w_router.astype(bf16)
    r_lo = (w_router - r_hi.astype(f32)).astype(bf16)
    pad = ((0, 0), (0, LANES - N_EXPERTS))
    wr = jnp.stack([jnp.pad(r_hi, pad), jnp.pad(r_lo, pad)])
    br = jnp.pad(b_router, (0, LANES - N_EXPERTS), constant_values=NEG_BIG).reshape(1, LANES)
    return w_ext, wq, wkv, w_out.astype(bf16), wr, br


def _mixer_half(x, mod, pos, wts, p, past_lat16, past_kpe16, state0, counts0, ret_chunk, q_tiles, tm):
    w_ext, wq, wkv, w_out16, wr, br = wts
    b, s, d = x.shape
    x2d = x.reshape(b * s, d)
    t64, t256 = _rope_tables(pos)
    qlat, kvlat, kvlat16, kpe, kpe16, rq, rk, rv, rg = _inproj(
        x2d, mod, p['g_pre_mix'], w_ext, p['g_q_lat'], p['g_kv_lat'], t64, t256, s, tm)
    lat16 = kvlat16.reshape(b, s, KV_LORA)
    kpe16 = kpe16.reshape(b, s, 2 * MLA_ROPE)
    if past_lat16 is not None:
        lat16 = jnp.concatenate([past_lat16, lat16], axis=1)
        kpe16 = jnp.concatenate([past_kpe16, kpe16], axis=1)
    attn = _mla(qlat.reshape(b, s, Q_LORA), lat16, kpe16, wq, wkv, t64, q_tiles)
    log_g = jnp.log1p(-jnp.exp2(-5.0 - jnp.arange(RET_HEADS, dtype=f32)))
    log_g = jnp.broadcast_to(log_g[:, None, None], (RET_HEADS, 8, LANES))
    sh = lambda t, w: t.reshape(b, s, w)
    rn, state = _retention(sh(rq, RET_HEADS * RET_QK), sh(rk, RET_HEADS * RET_QK), sh(rv, RET_WIDTH),
                           sh(rg, RET_WIDTH), p['g_ret'], state0, log_g, ret_chunk)
    xmid, h2p, idx, gates, rank, counts = _mixout(
        attn.reshape(b * s, MLA_WIDTH), rn.reshape(b * s, RET_WIDTH), x2d, mod,
        p['g_post_mix'], p['g_pre_ffn'], w_out16, wr, br, counts0, s, tm)
    route = (idx[:, :TOP_K], gates[:, :TOP_K], rank[:, :TOP_K], counts)
    return kvlat.reshape(b, s, KV_LORA), kpe.reshape(b, s, MLA_ROPE), state, xmid, h2p, route


def _combine_order(dest):
    tm = COMBINE_TOKENS
    n_tiles = dest.shape[0] // tm
    return dest.reshape(n_tiles, tm, TOP_K).transpose(0, 2, 1).reshape(n_tiles, 1, TOP_K * tm)


def kernel(x_prompt, x_sample, cache_kv_latent, cache_k_rope, state_retention, c_prompt, c_sample, w_ada, b_ada, g_pre_mix, g_post_mix, g_pre_ffn, g_post_ffn, w_in, g_q_lat, g_kv_lat, w_uq, w_ukv, g_ret, w_out, w_router, b_router, w_gate_up, b_gate_up, w_down, b_down):
    depth = w_in.shape[0]
    assert depth == 1, "the staged problem has a single layer"
    bp, sp, d = x_prompt.shape
    bs, ss, _ = x_sample.shape
    past = cache_kv_latent.shape[2]
    l = 0
    p = dict(g_pre_mix=g_pre_mix[l], g_post_mix=g_post_mix[l], g_pre_ffn=g_pre_ffn[l], g_post_ffn=g_post_ffn[l],
             g_q_lat=g_q_lat[l], g_kv_lat=g_kv_lat[l], g_ret=g_ret[l])
    wts = _prep_weights(w_in[l], w_uq[l], w_ukv[l], w_out[l], w_router[l], b_router[l])

    mod = _ada(jnp.concatenate([c_prompt, c_sample], axis=0), w_ada[l], b_ada[l]).reshape(bp + bs, 6, d)
    mod_p, mod_s = mod[:bp], mod[bp:]

    tq = ATTN_Q_TILE
    tiles_p = tuple((q0, tq, q0 + tq, True) for q0 in range(0, sp, tq))
    zero_state = jnp.zeros((bp, RET_HEADS, RET_QK, RET_V), f32)
    lat_p, kpe_p, st_p, xmid_p, h2p_p, (idx_p, gates_p, rank_p, counts_p) = _mixer_half(
        x_prompt, mod_p, jnp.arange(sp), wts, p, None, None, zero_state, jnp.zeros((1, LANES), f32),
        RET_CHUNK_PROMPT, tiles_p, 512)

    past_lat16 = cache_kv_latent[l].astype(bf16)
    past_kpe16 = jnp.pad(cache_k_rope[l], ((0, 0), (0, 0), (0, MLA_ROPE))).astype(bf16)
    tiles_s = ((0, ss, past + ss, False),)
    lat_s, kpe_s, st_s, xmid_s, h2p_s, (idx_s, gates_s, rank_s, counts) = _mixer_half(
        x_sample, mod_s, past + jnp.arange(ss), wts, p, past_lat16, past_kpe16, state_retention[l], counts_p,
        ss, tiles_s, ss)

    n_p = bp * sp
    dest, row_start, pad_end, n_chunks = _route(
        jnp.concatenate([idx_p, idx_s], axis=0), jnp.concatenate([rank_p, rank_s], axis=0), counts)
    x_sorted = _dispatch(pad_end, n_chunks, dest, h2p_p, h2p_s)
    act = _moe_up(row_start, n_chunks, x_sorted, w_gate_up[l], b_gate_up[l])
    y_sorted = _moe_down(row_start, n_chunks, act, w_down[l], b_down[l])
    y_p = _combine(_combine_order(dest[:n_p]), y_sorted, gates_p, xmid_p, mod_p, p['g_post_ffn'], sp)
    y_s = _combine(_combine_order(dest[n_p:]), y_sorted, gates_s, xmid_s, mod_s, p['g_post_ffn'], ss)

    return (y_p.reshape(bp, sp, d), y_s.reshape(bs, ss, d),
            lat_p[None], kpe_p[None], st_p[None].astype(state_retention.dtype),
            lat_s[None], kpe_s[None], st_s[None].astype(state_retention.dtype))
```

```python
import functools

import numpy as np
import jax
import jax.numpy as jnp
from jax import lax
from jax.experimental import pallas as pl
from jax.experimental.pallas import tpu as pltpu

CHUNK = 64
EPS = 1e-6
ROPE_THETA = 10000.0

MLA_HEADS = 8
MLA_NOPE = 128
MLA_ROPE = 64
MLA_V = 128
Q_LORA = 512
KV_LORA = 512
MLA_WIDTH = MLA_HEADS * MLA_V

RET_HEADS = 4
RET_QK = 256
RET_V = 256
RET_WIDTH = RET_HEADS * RET_V

N_EXPERTS = 32
TOP_K = 4
D_FF = 2048
SWIGLU_LIMIT = 7.0
SWIGLU_ALPHA = 1.702

LANES = 128
V7X_VMEM_BYTES = 64 * 1024 * 1024
VMEM_CAP_BYTES = V7X_VMEM_BYTES - 8 * 1024 * 1024

RET_CHUNK_PROMPT = 128
RET_UNROLL = 4
ATTN_Q_TILE = 256
MOE_ROWS = 256
UP_TILE = 1024
DOWN_TILE = 2048
COMBINE_TOKENS = 64
NEG_BIG = -1e30

bf16 = jnp.bfloat16
f32 = jnp.float32


def _params(sem, vmem_bytes):
    return pltpu.CompilerParams(dimension_semantics=sem, vmem_limit_bytes=int(min(vmem_bytes, VMEM_CAP_BYTES)))


def _const_spec(shape):
    nd = len(shape)
    return pl.BlockSpec(shape, lambda *_: (0,) * nd, pipeline_mode=pl.Buffered(1))


def _rms(x):
    return x * lax.rsqrt(jnp.mean(x * x, axis=-1, keepdims=True) + EPS)


def _silu(x):
    return x * jax.nn.sigmoid(x)


def _ada_kernel(c_ref, w_ref, b_ref, o_ref):
    s = _silu(c_ref[...]).astype(bf16)
    o_ref[...] = jnp.dot(s, w_ref[...].astype(bf16), preferred_element_type=f32) + b_ref[...]


def _ada(c, w_ada, b_ada):
    nb, d = c.shape
    n = w_ada.shape[1]
    tn = 1536
    return pl.pallas_call(
        _ada_kernel,
        grid=(n // tn,),
        in_specs=[pl.BlockSpec((nb, d), lambda j: (0, 0)),
                  pl.BlockSpec((d, tn), lambda j: (0, j)),
                  pl.BlockSpec((1, tn), lambda j: (0, j))],
        out_specs=pl.BlockSpec((nb, tn), lambda j: (0, j)),
        out_shape=jax.ShapeDtypeStruct((nb, n), f32),
        compiler_params=_params(("arbitrary",), 2 * d * tn * 4 + d * tn * 2 + (8 << 20)),
        name="ada_modulation",
    )(c, w_ada, b_ada.reshape(1, n))


_C_Q = 0
_C_KV = _C_Q + Q_LORA
_C_PE = _C_KV + KV_LORA
_C_RQ = _C_PE + 2 * MLA_ROPE
_C_RK = _C_RQ + RET_HEADS * RET_QK
_C_RV = _C_RK + RET_HEADS * RET_QK
_C_RG = _C_RV + RET_WIDTH
_C_END = _C_RG + RET_WIDTH


def _rope_pair(t):
    return t + pltpu.roll(t, MLA_ROPE, axis=1)


def _rope_heads(z, cos, sin, scale):
    outs = []
    half = RET_QK // 2
    for h in range(RET_HEADS):
        x1 = z[:, h * RET_QK:h * RET_QK + half]
        x2 = z[:, h * RET_QK + half:(h + 1) * RET_QK]
        outs.append((x1 * cos - x2 * sin) * scale)
        outs.append((x1 * sin + x2 * cos) * scale)
    return jnp.concatenate(outs, axis=-1)


def _inproj_kernel(x_ref, mod_ref, g_ref, w_ref, gq_ref, gkv_ref, t64_ref, t256_ref,
                   qlat_ref, kvlat_ref, kvlat16_ref, kpe_ref, kpe16_ref, rq_ref, rk_ref, rv_ref, rg_ref):
    m = mod_ref[0]
    h = (_rms(x_ref[...]) * g_ref[...] * (1.0 + m[1:2]) + m[0:1]).astype(bf16)

    def proj(c0, c1):
        return jnp.dot(h, w_ref[:, c0:c1], preferred_element_type=f32)

    qlat_ref[...] = (_rms(proj(_C_Q, _C_KV)) * gq_ref[...]).astype(bf16)
    kv = _rms(proj(_C_KV, _C_PE)) * gkv_ref[...]
    kvlat_ref[...] = kv
    kvlat16_ref[...] = kv.astype(bf16)
    pe = _rope_pair(proj(_C_PE, _C_RQ) * t64_ref[...])
    kpe_ref[...] = pe[:, :MLA_ROPE]
    lane = lax.broadcasted_iota(jnp.int32, pe.shape, 1)
    kpe16_ref[...] = jnp.where(lane < MLA_ROPE, pe, 0.0).astype(bf16)
    cos = t256_ref[:, :RET_QK // 2]
    sin = t256_ref[:, RET_QK // 2:]
    rq_ref[...] = _rope_heads(proj(_C_RQ, _C_RK), cos, sin, 1.0).astype(bf16)
    rk_ref[...] = _rope_heads(proj(_C_RK, _C_RV), cos, sin, RET_QK ** -0.5).astype(bf16)
    rv_ref[...] = proj(_C_RV, _C_RG).astype(bf16)
    rg_ref[...] = proj(_C_RG, _C_END).astype(bf16)


def _inproj(x2d, mod, g_pre, w_ext, g_q, g_kv, t64, t256, seq, tm):
    n, d = x2d.shape
    per_seq = seq // tm
    row = lambda w: pl.BlockSpec((tm, w), lambda i: (i, 0))
    tab = lambda w: pl.BlockSpec((tm, w), lambda i: (i % per_seq, 0))
    outs = [(Q_LORA, bf16), (KV_LORA, f32), (KV_LORA, bf16), (MLA_ROPE, f32), (2 * MLA_ROPE, bf16),
            (RET_HEADS * RET_QK, bf16), (RET_HEADS * RET_QK, bf16), (RET_WIDTH, bf16), (RET_WIDTH, bf16)]
    vmem = w_ext.size * 2 + 2 * tm * d * 4 + 4 * tm * d * 4 + sum(2 * tm * w * 4 for w, _ in outs) + (6 << 20)
    return pl.pallas_call(
        _inproj_kernel,
        grid=(n // tm,),
        in_specs=[row(d),
                  pl.BlockSpec((1, 6, d), lambda i: (i // per_seq, 0, 0)),
                  _const_spec((1, d)), _const_spec(w_ext.shape), _const_spec((1, Q_LORA)), _const_spec((1, KV_LORA)),
                  tab(2 * MLA_ROPE), tab(RET_QK)],
        out_specs=[row(w) for w, _ in outs],
        out_shape=[jax.ShapeDtypeStruct((n, w), dt) for w, dt in outs],
        compiler_params=_params(("arbitrary",), vmem),
        name="in_projection",
    )(x2d, mod, g_pre.reshape(1, d), w_ext, g_q.reshape(1, Q_LORA), g_kv.reshape(1, KV_LORA), t64, t256)


def _mla_kernel(qlat_ref, lat_ref, kpe_ref, wq_ref, wkv_ref, tab_ref, o_ref, q_s, k_s, v_s, *, q_tiles):
    scale = (MLA_NOPE + MLA_ROPE) ** -0.5
    qh = jnp.dot(qlat_ref[0], wq_ref[0], preferred_element_type=f32)
    q_s[:, :MLA_NOPE] = (qh[:, :MLA_NOPE] * scale).astype(bf16)
    q_s[:, MLA_NOPE:] = (_rope_pair(qh[:, MLA_NOPE:] * tab_ref[...]) * scale).astype(bf16)
    kv = jnp.dot(lat_ref[0], wkv_ref[0], preferred_element_type=f32)
    k_s[:, :MLA_NOPE] = kv[:, :MLA_NOPE].astype(bf16)
    k_s[:, MLA_NOPE:] = kpe_ref[0]
    v_s[...] = kv[:, MLA_NOPE:].astype(bf16)
    for q0, ql, kvl, masked in q_tiles:
        q = q_s[q0:q0 + ql, :]
        s = lax.dot_general(q, k_s[0:kvl, :], (((1,), (1,)), ((), ())), preferred_element_type=f32)
        if masked:
            assert kvl == q0 + ql and q0 % CHUNK == 0
            qc = lax.broadcasted_iota(jnp.int32, (ql, 1), 0) // CHUNK
            kc = lax.broadcasted_iota(jnp.int32, (1, ql), 1) // CHUNK
            diag = jnp.where(kc <= qc, s[:, q0:], NEG_BIG)
            s = diag if q0 == 0 else jnp.concatenate([s[:, :q0], diag], axis=1)
        p = jnp.exp(s - jnp.max(s, axis=-1, keepdims=True))
        l = jnp.sum(p, axis=-1, keepdims=True)
        o = jnp.dot(p.astype(bf16), v_s[0:kvl, :], preferred_element_type=f32)
        o_ref[0, q0:q0 + ql, :] = (o / l).astype(bf16)


def _mla(qlat, lat, kpe16, wq, wkv, tab, q_tiles):
    b, sq, _ = qlat.shape
    skv = lat.shape[1]
    kern = functools.partial(_mla_kernel, q_tiles=q_tiles)
    max_ql = max(t[1] for t in q_tiles)
    vmem = 2 * 2 * (sq * Q_LORA + skv * KV_LORA + skv * LANES) + sq * LANES * 4 + sq * 3 * LANES * 4 \
        + (sq + skv) * 3 * LANES * 2 + 4 * max_ql * skv * 4 + (8 << 20)
    return pl.pallas_call(
        kern,
        grid=(b, MLA_HEADS),
        in_specs=[pl.BlockSpec((1, sq, Q_LORA), lambda i, h: (i, 0, 0)),
                  pl.BlockSpec((1, skv, KV_LORA), lambda i, h: (i, 0, 0)),
                  pl.BlockSpec((1, skv, 2 * MLA_ROPE), lambda i, h: (i, 0, 0)),
                  pl.BlockSpec((1, Q_LORA, MLA_NOPE + 2 * MLA_ROPE), lambda i, h: (h, 0, 0)),
                  pl.BlockSpec((1, KV_LORA, MLA_NOPE + MLA_V), lambda i, h: (h, 0, 0)),
                  pl.BlockSpec((sq, 2 * MLA_ROPE), lambda i, h: (0, 0))],
        out_specs=pl.BlockSpec((1, sq, MLA_V), lambda i, h: (i, 0, h)),
        out_shape=jax.ShapeDtypeStruct((b, sq, MLA_WIDTH), bf16),
        scratch_shapes=[pltpu.VMEM((sq, MLA_NOPE + 2 * MLA_ROPE), bf16),
                        pltpu.VMEM((skv, MLA_NOPE + 2 * MLA_ROPE), bf16),
                        pltpu.VMEM((skv, MLA_V), bf16)],
        compiler_params=_params(("arbitrary", "arbitrary"), vmem),
        name="mla_attention",
    )(qlat, lat, kpe16, wq, wkv, tab)


def _ret_kernel(lg_ref, rq_ref, rk_ref, rv_ref, rg_ref, gret_ref, s0_ref, o_ref, sout_ref, state, *, chunk, n_chunks):
    c = chunk
    lg = lg_ref[0, 0:1, 0:1]
    ii = lax.broadcasted_iota(jnp.int32, (c, c), 0)
    jj = lax.broadcasted_iota(jnp.int32, (c, c), 1)
    diff = (ii - jj).astype(f32)
    dmask = jnp.where(diff >= 0.0, jnp.exp(jnp.maximum(diff, 0.0) * lg), 0.0)
    ic = lax.broadcasted_iota(jnp.int32, (c, 1), 0).astype(f32)
    q_decay = jnp.exp((ic + 1.0) * lg)
    k_decay = jnp.exp((c - 1.0 - ic) * lg)
    chunk_decay = jnp.exp(float(c) * lg)
    gret = gret_ref[...]
    state[...] = s0_ref[0, 0]

    def step(n, carry):
        sl = pl.ds(pl.multiple_of(n * c, c), c)
        q = rq_ref[0, sl, :]
        k = rk_ref[0, sl, :]
        v = rv_ref[0, sl, :]
        st = state[...]
        attn = lax.dot_general(q, k, (((1,), (1,)), ((), ())), preferred_element_type=f32) * dmask
        inner = jnp.dot(attn.astype(bf16), v, preferred_element_type=f32)
        cross = jnp.dot(q, st.astype(bf16), preferred_element_type=f32) * q_decay
        o = inner + cross
        kd_t = (k.astype(f32) * k_decay).T.astype(bf16)
        state[...] = st * chunk_decay + jnp.dot(kd_t, v, preferred_element_type=f32)
        mu = jnp.mean(o, axis=-1, keepdims=True)
        oc = o - mu
        rn = oc * lax.rsqrt(jnp.mean(oc * oc, axis=-1, keepdims=True) + EPS) * gret
        o_ref[0, sl, :] = (rn * _silu(rg_ref[0, sl, :].astype(f32))).astype(bf16)
        return carry

    lax.fori_loop(0, n_chunks, step, 0, unroll=min(n_chunks, RET_UNROLL))
    sout_ref[0, 0] = state[...]


def _retention(rq, rk, rv, rg, g_ret, state0, log_g, chunk):
    b, s, _ = rq.shape
    kern = functools.partial(_ret_kernel, chunk=chunk, n_chunks=s // chunk)
    head = lambda w: pl.BlockSpec((1, s, w), lambda i, h: (i, 0, h))
    st = pl.BlockSpec((1, 1, RET_QK, RET_V), lambda i, h: (i, h, 0, 0))
    vmem = 2 * 5 * s * RET_QK * 2 + 5 * RET_QK * RET_V * 4 + (8 << 20)
    return pl.pallas_call(
        kern,
        grid=(b, RET_HEADS),
        in_specs=[pl.BlockSpec((1, 8, LANES), lambda i, h: (h, 0, 0)),
                  head(RET_QK), head(RET_QK), head(RET_V), head(RET_V),
                  pl.BlockSpec((1, RET_V), lambda i, h: (0, h)), st],
        out_specs=[head(RET_V), st],
        out_shape=[jax.ShapeDtypeStruct((b, s, RET_WIDTH), bf16),
                   jax.ShapeDtypeStruct((b, RET_HEADS, RET_QK, RET_V), f32)],
        scratch_shapes=[pltpu.VMEM((RET_QK, RET_V), f32)],
        compiler_params=_params(("arbitrary", "arbitrary"), vmem),
        name="retention",
    )(log_g, rq, rk, rv, rg, g_ret.reshape(1, RET_WIDTH), state0)


def _pack_pairs(lo16, hi16):
    lo = lax.bitcast_convert_type(lo16.astype(f32), jnp.uint32)
    hi = lax.bitcast_convert_type(hi16.astype(f32), jnp.uint32)
    return (hi & jnp.uint32(0xFFFF0000)) | (lo >> 16)


def _unpack_pairs(u):
    lo = lax.bitcast_convert_type(u << 16, f32)
    hi = lax.bitcast_convert_type(u & jnp.uint32(0xFFFF0000), f32)
    return lo, hi


def _mixout_kernel(attn_ref, rn_ref, x_ref, mod_ref, gpost_ref, gpre_ref, wout_ref, wr_ref, br_ref, cnt0_ref,
                   xmid_ref, h2p_ref, idx_ref, gate_ref, rank_ref, cnt_ref, carry):
    @pl.when(pl.program_id(0) == 0)
    def _():
        carry[...] = cnt0_ref[...]

    mix = (jnp.dot(attn_ref[...], wout_ref[0:MLA_WIDTH, :], preferred_element_type=f32)
           + jnp.dot(rn_ref[...], wout_ref[MLA_WIDTH:, :], preferred_element_type=f32))
    m = mod_ref[0]
    x1 = x_ref[...] + m[2:3] * (_rms(mix) * gpost_ref[...])
    xmid_ref[...] = x1
    h2 = _rms(x1) * gpre_ref[...] * (1.0 + m[4:5]) + m[3:4]
    h_hi = h2.astype(bf16)
    half = h2.shape[1] // 2
    h2p_ref[...] = _pack_pairs(h_hi[:, :half], h_hi[:, half:])
    h_lo = (h2 - h_hi.astype(f32)).astype(bf16)
    logits = (jnp.dot(h_hi, wr_ref[0], preferred_element_type=f32)
              + jnp.dot(h_lo, wr_ref[0], preferred_element_type=f32)
              + jnp.dot(h_hi, wr_ref[1], preferred_element_type=f32)) + br_ref[...]
    tm = logits.shape[0]
    col = lax.broadcasted_iota(jnp.int32, logits.shape, 1).astype(f32)
    vals, idxs = [], []
    for _ in range(TOP_K):
        mx = jnp.max(logits, axis=-1, keepdims=True)
        ix = jnp.min(jnp.where(logits == mx, col, float(LANES)), axis=-1, keepdims=True)
        vals.append(mx)
        idxs.append(ix)
        logits = jnp.where(col == ix, -jnp.inf, logits)
    es = [jnp.exp(v - vals[0]) for v in vals]
    tot = es[0] + es[1] + es[2] + es[3]
    onehot = jnp.zeros_like(col)
    for k in range(TOP_K):
        onehot = jnp.where(col == idxs[k], 1.0, onehot)
    earlier = (lax.broadcasted_iota(jnp.int32, (tm, tm), 0) > lax.broadcasted_iota(jnp.int32, (tm, tm), 1))
    before = jnp.dot(jnp.where(earlier, 1.0, 0.0).astype(bf16), onehot.astype(bf16),
                     preferred_element_type=f32) + carry[...]
    idx_out = jnp.zeros_like(col)
    gate_out = jnp.zeros_like(col)
    rank_out = jnp.zeros_like(col)
    for k in range(TOP_K):
        slot = col == float(k)
        idx_out = jnp.where(slot, idxs[k], idx_out)
        gate_out = jnp.where(slot, es[k] / tot, gate_out)
        rank_k = jnp.sum(jnp.where(col == idxs[k], before, 0.0), axis=-1, keepdims=True)
        rank_out = jnp.where(slot, rank_k, rank_out)
    idx_ref[...] = idx_out.astype(jnp.int32)
    gate_ref[...] = gate_out
    rank_ref[...] = rank_out.astype(jnp.int32)
    carry[...] = carry[...] + jnp.sum(onehot, axis=0, keepdims=True)
    cnt_ref[...] = carry[...]


def _mixout(attn, rn, x2d, mod, g_post, g_pre, w_out16, wr, br, counts0, seq, tm):
    n, d = x2d.shape
    per_seq = seq // tm
    row = lambda w: pl.BlockSpec((tm, w), lambda i: (i, 0))
    vmem = w_out16.size * 2 + 2 * tm * d * (4 + 4 + 2 + 2) + 6 * tm * d * 4 + (8 << 20)
    return pl.pallas_call(
        _mixout_kernel,
        grid=(n // tm,),
        in_specs=[row(MLA_WIDTH), row(RET_WIDTH), row(d),
                  pl.BlockSpec((1, 6, d), lambda i: (i // per_seq, 0, 0)),
                  _const_spec((1, d)), _const_spec((1, d)), _const_spec(w_out16.shape),
                  _const_spec(wr.shape), _const_spec((1, LANES)), _const_spec((1, LANES))],
        out_specs=[row(d), row(d // 2), row(LANES), row(LANES), row(LANES),
                   pl.BlockSpec((1, LANES), lambda i: (0, 0))],
        out_shape=[jax.ShapeDtypeStruct((n, d), f32), jax.ShapeDtypeStruct((n, d // 2), jnp.uint32),
                   jax.ShapeDtypeStruct((n, LANES), jnp.int32), jax.ShapeDtypeStruct((n, LANES), f32),
                   jax.ShapeDtypeStruct((n, LANES), jnp.int32), jax.ShapeDtypeStruct((1, LANES), f32)],
        scratch_shapes=[pltpu.VMEM((1, LANES), f32)],
        compiler_params=_params(("arbitrary",), vmem),
        name="mixer_out_router",
    )(attn, rn, x2d, mod, g_post.reshape(1, d), g_pre.reshape(1, d), w_out16, wr, br, counts0)


def _route(top_idx, rank, counts):
    counts = counts[0, :N_EXPERTS].astype(jnp.int32)
    padded = (counts + MOE_ROWS - 1) // MOE_ROWS * MOE_ROWS
    pad_end = jnp.cumsum(padded).astype(jnp.int32)
    pad_start = pad_end - padded
    dest = pad_start[top_idx] + rank
    return dest, pad_start, pad_end, padded // MOE_ROWS


def _n_sorted_rows(n_tokens):
    return (n_tokens * TOP_K + N_EXPERTS * (MOE_ROWS - 1)) // MOE_ROWS * MOE_ROWS


def _fill_unused_chunks(zero_buf, dst_rows, sem, used_rows, total_rows):
    first = used_rows // MOE_ROWS

    def copy(c):
        return pltpu.make_async_copy(zero_buf, dst_rows(pl.ds(pl.multiple_of(c * MOE_ROWS, MOE_ROWS), MOE_ROWS)), sem)

    def start(c, carry):
        copy(c).start()
        return carry

    def wait(c, carry):
        copy(c).wait()
        return carry

    lax.fori_loop(first, total_rows // MOE_ROWS, start, 0)
    lax.fori_loop(first, total_rows // MOE_ROWS, wait, 0)


def _dispatch_kernel(pe_ref, nc_ref, dest_ref, h2p_a_ref, h2p_b_ref, x_hbm, zbuf, sem, zsem, *, tokens, n_first):
    i = pl.program_id(0)

    @pl.when(i == 0)
    def _():
        zbuf[...] = jnp.zeros_like(zbuf)

        def tail(e):
            start = pl.multiple_of(pe_ref[e] - MOE_ROWS, MOE_ROWS)
            return pltpu.make_async_copy(zbuf, x_hbm.at[pl.ds(start, MOE_ROWS), :], zsem)

        def start_tail(e, c):
            @pl.when(nc_ref[e] > 0)
            def _():
                tail(e).start()
            return c

        def wait_tail(e, c):
            @pl.when(nc_ref[e] > 0)
            def _():
                tail(e).wait()
            return c

        lax.fori_loop(0, N_EXPERTS, start_tail, 0)
        lax.fori_loop(0, N_EXPERTS, wait_tail, 0)
        _fill_unused_chunks(zbuf, lambda rows: x_hbm.at[rows, :], zsem, pe_ref[N_EXPERTS - 1], x_hbm.shape[0])

    def scatter(h2p_ref):
        def body(t, c):
            for k in range(TOP_K):
                d = dest_ref[0, 0, t * TOP_K + k]
                pltpu.make_async_copy(h2p_ref.at[pl.ds(t, 1), :], x_hbm.at[pl.ds(d, 1), :], sem).start(priority=k % 2)
            return c

        lax.fori_loop(0, tokens, body, 0, unroll=2)
        for _ in range(TOP_K):
            pltpu.make_async_copy(h2p_ref, x_hbm.at[pl.ds(0, tokens), :], sem).wait()

    @pl.when(i < n_first)
    def _():
        scatter(h2p_a_ref)

    @pl.when(i >= n_first)
    def _():
        scatter(h2p_b_ref)


def _dispatch(pad_end, n_chunks, dest, h2p_a, h2p_b):
    w = h2p_a.shape[1]
    tokens = 512
    n_a, n_b = h2p_a.shape[0] // tokens, h2p_b.shape[0] // tokens
    assert n_a * tokens == h2p_a.shape[0] and n_b * tokens == h2p_b.shape[0]
    n_steps = n_a + n_b
    grid_spec = pltpu.PrefetchScalarGridSpec(
        num_scalar_prefetch=2,
        grid=(n_steps,),
        in_specs=[pl.BlockSpec((1, 1, tokens * TOP_K), lambda i, pe, nc: (i, 0, 0), memory_space=pltpu.SMEM),
                  pl.BlockSpec((tokens, w), lambda i, pe, nc: (jnp.minimum(i, n_a - 1), 0)),
                  pl.BlockSpec((tokens, w), lambda i, pe, nc: (jnp.maximum(i - n_a, 0), 0))],
        out_specs=pl.BlockSpec(memory_space=pl.ANY),
        scratch_shapes=[pltpu.VMEM((MOE_ROWS, w), jnp.uint32), pltpu.SemaphoreType.DMA(()),
                        pltpu.SemaphoreType.DMA(())],
    )
    return pl.pallas_call(
        functools.partial(_dispatch_kernel, tokens=tokens, n_first=n_a),
        grid_spec=grid_spec,
        out_shape=jax.ShapeDtypeStruct((_n_sorted_rows(n_steps * tokens), w), jnp.uint32),
        compiler_params=_params(("arbitrary",), 24 << 20),
        name="moe_dispatch",
    )(pad_end, n_chunks, dest.reshape(n_steps, 1, tokens * TOP_K), h2p_a, h2p_b)


def _chunk_rows(r0, c):
    return pl.ds(pl.multiple_of(r0 + c * MOE_ROWS, MOE_ROWS), MOE_ROWS)


CHUNK_READ_PRIORITY = 1
READ_SLOTS = 4


def _start_first_reads(copies, n):
    for a in range(READ_SLOTS - 1):
        @pl.when(a < n)
        def _():
            for cp in copies(a, a):
                cp.start(priority=CHUNK_READ_PRIORITY)


def _next_step_rows(rs_ref, nc_ref, e, j, n_j):
    last_tile = j == n_j - 1
    has_next = jnp.logical_or(e < N_EXPERTS - 1, jnp.logical_not(last_tile))
    e_next = jnp.minimum(jnp.where(last_tile, e + 1, e), N_EXPERTS - 1)
    return rs_ref[e_next], jnp.where(has_next, nc_ref[e_next], 0)


def _read_ahead(copies, c, n):
    ahead = c + (READ_SLOTS - 1)

    @pl.when(ahead < n)
    def _():
        for cp in copies(ahead, ahead % READ_SLOTS):
            cp.start(priority=CHUNK_READ_PRIORITY)

    rslot = c % READ_SLOTS
    for cp in copies(c, rslot):
        cp.wait()
    return rslot


def _moe_up_kernel(rs_ref, nc_ref, x_hbm, wg_ref, wu_ref, bg_ref, bu_ref, act_hbm,
                   wg16, wu16, xbuf, obuf, sem_in, sem_out):
    e = pl.program_id(0)
    j = pl.program_id(1)
    n = nc_ref[e]
    r0 = rs_ref[e]

    def in_copy(c, slot, base=r0):
        return [pltpu.make_async_copy(x_hbm.at[_chunk_rows(base, c), :], xbuf.at[slot], sem_in.at[slot])]

    def out_copy(c, slot):
        return pltpu.make_async_copy(obuf.at[slot], act_hbm.at[j, _chunk_rows(r0, c), :], sem_out.at[slot])

    @pl.when(n > 0)
    def _():
        @pl.when(jnp.logical_and(e == 0, j == 0))
        def _():
            _start_first_reads(in_copy, n)

        wg16[...] = wg_ref[0].astype(bf16)
        wu16[...] = wu_ref[0].astype(bf16)
        half = wg16.shape[0] // 2

        def body(c, carry):
            slot = c % 2
            rslot = _read_ahead(in_copy, c, n)
            lo, hi = _unpack_pairs(xbuf[rslot])
            x_lo = lo.astype(bf16)
            x_hi = hi.astype(bf16)

            def proj(w16, b_ref):
                return (jnp.dot(x_lo, w16[0:half, :], preferred_element_type=f32)
                        + jnp.dot(x_hi, w16[half:, :], preferred_element_type=f32) + b_ref[0])

            gate = jnp.minimum(proj(wg16, bg_ref), SWIGLU_LIMIT)
            up = jnp.clip(proj(wu16, bu_ref), -SWIGLU_LIMIT, SWIGLU_LIMIT)
            act = (up + 1.0) * gate * jax.nn.sigmoid(SWIGLU_ALPHA * gate)

            @pl.when(c >= 2)
            def _():
                out_copy(c - 2, slot).wait()

            obuf[slot] = act.astype(bf16)
            out_copy(c, slot).start()
            return carry

        lax.fori_loop(0, n, body, 0)

        @pl.when(n >= 2)
        def _():
            out_copy(n - 2, n % 2).wait()

        out_copy(n - 1, (n - 1) % 2).wait()

    r0_next, n_next = _next_step_rows(rs_ref, nc_ref, e, j, act_hbm.shape[0])
    _start_first_reads(lambda c, slot: in_copy(c, slot, r0_next), n_next)

    @pl.when(e == N_EXPERTS - 1)
    def _():
        obuf[0] = jnp.zeros(obuf.shape[1:], obuf.dtype)
        _fill_unused_chunks(obuf.at[0], lambda rows: act_hbm.at[j, rows, :], sem_out.at[0],
                            r0 + n * MOE_ROWS, act_hbm.shape[1])


def _moe_up(row_start, n_chunks, x_sorted, w_gate_up, b_gate_up):
    e, d, f2 = w_gate_up.shape
    tf = UP_TILE
    n_j = D_FF // tf
    rows = x_sorted.shape[0]
    vmem = 2 * 2 * d * tf * 4 + 2 * d * tf * 2 + 2 * MOE_ROWS * d * 2 + 2 * MOE_ROWS * tf * 2 \
        + 6 * MOE_ROWS * tf * 4 + (6 << 20)
    grid_spec = pltpu.PrefetchScalarGridSpec(
        num_scalar_prefetch=2,
        grid=(e, n_j),
        in_specs=[pl.BlockSpec(memory_space=pl.ANY),
                  pl.BlockSpec((1, d, tf), lambda i, j, rs, nc: (i, 0, j)),
                  pl.BlockSpec((1, d, tf), lambda i, j, rs, nc: (i, 0, n_j + j)),
                  pl.BlockSpec((1, 1, tf), lambda i, j, rs, nc: (i, 0, j)),
                  pl.BlockSpec((1, 1, tf), lambda i, j, rs, nc: (i, 0, n_j + j))],
        out_specs=pl.BlockSpec(memory_space=pl.ANY),
        scratch_shapes=[pltpu.VMEM((d, tf), bf16), pltpu.VMEM((d, tf), bf16),
                        pltpu.VMEM((READ_SLOTS, MOE_ROWS, d // 2), jnp.uint32), pltpu.VMEM((2, MOE_ROWS, tf), bf16),
                        pltpu.SemaphoreType.DMA((READ_SLOTS,)), pltpu.SemaphoreType.DMA((2,))],
    )
    b3 = b_gate_up.reshape(e, 1, f2)
    return pl.pallas_call(
        _moe_up_kernel,
        grid_spec=grid_spec,
        out_shape=jax.ShapeDtypeStruct((n_j, rows, tf), bf16),
        compiler_params=_params(("arbitrary", "arbitrary"), vmem),
        name="moe_gate_up",
    )(row_start, n_chunks, x_sorted, w_gate_up, w_gate_up, b3, b3)


def _moe_down_kernel(rs_ref, nc_ref, act_hbm, wd_ref, bd_ref, y_hbm, wd16, abuf, ybuf, sem_in, sem_out):
    e = pl.program_id(0)
    nt = pl.program_id(1)
    n = nc_ref[e]
    r0 = rs_ref[e]
    n_k, tf = abuf.shape[1], abuf.shape[3]
    tw = ybuf.shape[2]
    n_t = y_hbm.shape[1] // tw

    def on_tile(fn):
        for t in range(n_t):
            @pl.when(nt == t)
            def _():
                fn(t)

    def in_copies(c, slot, base=r0):
        return [pltpu.make_async_copy(act_hbm.at[k, _chunk_rows(base, c), :], abuf.at[slot, k], sem_in.at[slot])
                for k in range(n_k)]

    def out_copy(c, slot, t):
        return pltpu.make_async_copy(ybuf.at[slot], y_hbm.at[_chunk_rows(r0, c), t * tw:(t + 1) * tw],
                                     sem_out.at[slot])

    @pl.when(n > 0)
    def _():
        @pl.when(jnp.logical_and(e == 0, nt == 0))
        def _():
            _start_first_reads(in_copies, n)

        wd16[...] = wd_ref[0].astype(bf16)

        def body(c, carry):
            slot = c % 2
            rslot = _read_ahead(in_copies, c, n)
            y = bd_ref[0]
            for k in range(n_k):
                y = y + jnp.dot(abuf[rslot, k], wd16[k * tf:(k + 1) * tf, :], preferred_element_type=f32)
            y16 = y.astype(bf16)

            @pl.when(c >= 2)
            def _():
                on_tile(lambda t: out_copy(c - 2, slot, t).wait())

            ybuf[slot] = _pack_pairs(y16[:, :tw], y16[:, tw:])
            on_tile(lambda t: out_copy(c, slot, t).start())
            return carry

        lax.fori_loop(0, n, body, 0)

        @pl.when(n >= 2)
        def _():
            on_tile(lambda t: out_copy(n - 2, n % 2, t).wait())

        on_tile(lambda t: out_copy(n - 1, (n - 1) % 2, t).wait())

    r0_next, n_next = _next_step_rows(rs_ref, nc_ref, e, nt, n_t)
    _start_first_reads(lambda c, slot: in_copies(c, slot, r0_next), n_next)

    @pl.when(e == N_EXPERTS - 1)
    def _():
        ybuf[0] = jnp.zeros(ybuf.shape[1:], ybuf.dtype)
        on_tile(lambda t: _fill_unused_chunks(ybuf.at[0], lambda rows: y_hbm.at[rows, t * tw:(t + 1) * tw],
                                              sem_out.at[0], r0 + n * MOE_ROWS, y_hbm.shape[0]))


def _moe_down(row_start, n_chunks, act, w_down, b_down):
    e, f, d = w_down.shape
    n_k, rows, tf = act.shape
    tn = DOWN_TILE
    vmem = 2 * f * tn * 4 + f * tn * 2 + 2 * MOE_ROWS * f * 2 + 2 * MOE_ROWS * tn * 2 + 4 * MOE_ROWS * tn * 4 + (6 << 20)
    grid_spec = pltpu.PrefetchScalarGridSpec(
        num_scalar_prefetch=2,
        grid=(e, d // tn),
        in_specs=[pl.BlockSpec(memory_space=pl.ANY),
                  pl.BlockSpec((1, f, tn), lambda i, t, rs, nc: (i, 0, t)),
                  pl.BlockSpec((1, 1, tn), lambda i, t, rs, nc: (i, 0, t))],
        out_specs=pl.BlockSpec(memory_space=pl.ANY),
        scratch_shapes=[pltpu.VMEM((f, tn), bf16), pltpu.VMEM((READ_SLOTS, n_k, MOE_ROWS, tf), bf16),
                        pltpu.VMEM((2, MOE_ROWS, tn // 2), jnp.uint32),
                        pltpu.SemaphoreType.DMA((READ_SLOTS,)), pltpu.SemaphoreType.DMA((2,))],
    )
    return pl.pallas_call(
        _moe_down_kernel,
        grid_spec=grid_spec,
        out_shape=jax.ShapeDtypeStruct((rows, d // 2), jnp.uint32),
        compiler_params=_params(("arbitrary", "arbitrary"), vmem),
        name="moe_down",
    )(row_start, n_chunks, act, w_down, b_down.reshape(e, 1, d))


def _gather_rows(idx_ref, src_hbm, dst, sem, n_rows):
    def body(r2, carry):
        for u in range(2):
            r = 2 * r2 + u
            t = idx_ref[0, 0, r]
            pltpu.make_async_copy(src_hbm.at[pl.ds(t, 1), :], dst.at[pl.ds(r, 1), :], sem).start(priority=u)
        return carry
    lax.fori_loop(0, n_rows // 2, body, 0, unroll=4)


def _wait_rows(src_hbm, dst, sem, n_rows):
    pltpu.make_async_copy(src_hbm.at[pl.ds(0, n_rows), :], dst, sem).wait()


def _prefetched_gather(i, n_live, cur_idx_ref, next_idx_ref, src_hbm, buf, sem, n_rows):
    slot = i % 2

    @pl.when(jnp.logical_and(i == 0, n_live > 0))
    def _():
        _gather_rows(cur_idx_ref, src_hbm, buf.at[0], sem.at[0], n_rows)

    @pl.when(i + 1 < n_live)
    def _():
        _gather_rows(next_idx_ref, src_hbm, buf.at[1 - slot], sem.at[1 - slot], n_rows)

    @pl.when(i < n_live)
    def _():
        _wait_rows(src_hbm, buf.at[slot], sem.at[slot], n_rows)

    return slot


def _combine_kernel(pos_ref, posn_ref, y_hbm, gates_ref, xmid_ref, mod_ref, gpost_ref, out_ref, ybuf, sem, *, n_steps):
    i = pl.program_id(0)
    tm = COMBINE_TOKENS
    slot = _prefetched_gather(i, n_steps, pos_ref, posn_ref, y_hbm, ybuf, sem, TOP_K * tm)
    g = gates_ref[...]
    lo = hi = None
    for k in range(TOP_K):
        l_k, h_k = _unpack_pairs(ybuf[slot, k * tm:(k + 1) * tm, :])
        gk = g[:, k:k + 1]
        lo = gk * l_k if lo is None else lo + gk * l_k
        hi = gk * h_k if hi is None else hi + gk * h_k
    half = lo.shape[1]
    inv = lax.rsqrt((jnp.sum(lo * lo, axis=-1, keepdims=True) + jnp.sum(hi * hi, axis=-1, keepdims=True))
                    / (2 * half) + EPS)
    gate2 = mod_ref[0][5:6]
    gp = gpost_ref[...]
    q = DOWN_TILE // 2
    for t in range(half // q):
        for src, c0 in ((lo, t * DOWN_TILE), (hi, t * DOWN_TILE + q)):
            cols = slice(c0, c0 + q)
            out_ref[:, cols] = xmid_ref[:, cols] + gate2[:, cols] * (src[:, t * q:(t + 1) * q] * inv * gp[:, cols])


def _combine(pos, y_sorted, gates, xmid, mod, g_post, seq):
    n, d = xmid.shape
    tm = COMBINE_TOKENS
    n_tiles = n // tm
    per_seq = seq // tm
    vmem = 2 * TOP_K * tm * d * 2 + 4 * tm * d * 4 + 6 * tm * d * 4 + (8 << 20)
    return pl.pallas_call(
        functools.partial(_combine_kernel, n_steps=n_tiles),
        grid=(n_tiles,),
        in_specs=[pl.BlockSpec((1, 1, TOP_K * tm), lambda i: (i, 0, 0), memory_space=pltpu.SMEM),
                  pl.BlockSpec((1, 1, TOP_K * tm), lambda i: (jnp.minimum(i + 1, n_tiles - 1), 0, 0),
                               memory_space=pltpu.SMEM),
                  pl.BlockSpec(memory_space=pl.ANY),
                  pl.BlockSpec((tm, TOP_K), lambda i: (i, 0)),
                  pl.BlockSpec((tm, d), lambda i: (i, 0)),
                  pl.BlockSpec((1, 6, d), lambda i: (i // per_seq, 0, 0)),
                  _const_spec((1, d))],
        out_specs=pl.BlockSpec((tm, d), lambda i: (i, 0)),
        out_shape=jax.ShapeDtypeStruct((n, d), f32),
        scratch_shapes=[pltpu.VMEM((2, TOP_K * tm, d // 2), jnp.uint32), pltpu.SemaphoreType.DMA((2,))],
        compiler_params=_params(("arbitrary",), vmem),
        name="moe_combine",
    )(pos, pos, y_sorted, gates, xmid, mod, g_post.reshape(1, d))


def _rope_tables(pos):
    def cs(d):
        half = d // 2
        inv_freq = 1.0 / (ROPE_THETA ** (jnp.arange(half, dtype=f32) * (2.0 / d)))
        ang = pos.astype(f32)[:, None] * inv_freq[None, :]
        return jnp.cos(ang), jnp.sin(ang)
    c64, s64 = cs(MLA_ROPE)
    c256, s256 = cs(RET_QK)
    t64 = jnp.concatenate([c64, c64, s64, s64], axis=-1)
    t256 = jnp.concatenate([c256, s256], axis=-1)
    return t64, t256


def _rot_cols(w):
    half = w.shape[-1] // 2
    return jnp.concatenate([-w[..., half:], w[..., :half]], axis=-1)


def _prep_weights(w_in, w_uq, w_ukv, w_out, w_router, b_router):
    b = np.cumsum((Q_LORA, KV_LORA, MLA_ROPE))
    w_pe = w_in[:, b[1]:b[2]]
    w_ext = jnp.concatenate([w_in[:, :b[2]], _rot_cols(w_pe), w_in[:, b[2]:]], axis=-1).astype(bf16)
    wq = w_uq.reshape(Q_LORA, MLA_HEADS, MLA_NOPE + MLA_ROPE)
    wq = jnp.concatenate([wq, _rot_cols(wq[..., MLA_NOPE:])], axis=-1).transpose(1, 0, 2).astype(bf16)
    wkv = w_ukv.reshape(KV_LORA, MLA_HEADS, MLA_NOPE + MLA_V).transpose(1, 0, 2).astype(bf16)
    r_hi = w_router.astype(bf16)
    r_lo = (w_router - r_hi.astype(f32)).astype(bf16)
    pad = ((0, 0), (0, LANES - N_EXPERTS))
    wr = jnp.stack([jnp.pad(r_hi, pad), jnp.pad(r_lo, pad)])
    br = jnp.pad(b_router, (0, LANES - N_EXPERTS), constant_values=NEG_BIG).reshape(1, LANES)
    return w_ext, wq, wkv, w_out.astype(bf16), wr, br


def _mixer_half(x, mod, pos, wts, p, past_lat16, past_kpe16, state0, counts0, ret_chunk, q_tiles, tm):
    w_ext, wq, wkv, w_out16, wr, br = wts
    b, s, d = x.shape
    x2d = x.reshape(b * s, d)
    t64, t256 = _rope_tables(pos)
    qlat, kvlat, kvlat16, kpe, kpe16, rq, rk, rv, rg = _inproj(
        x2d, mod, p['g_pre_mix'], w_ext, p['g_q_lat'], p['g_kv_lat'], t64, t256, s, tm)
    lat16 = kvlat16.reshape(b, s, KV_LORA)
    kpe16 = kpe16.reshape(b, s, 2 * MLA_ROPE)
    if past_lat16 is not None:
        lat16 = jnp.concatenate([past_lat16, lat16], axis=1)
        kpe16 = jnp.concatenate([past_kpe16, kpe16], axis=1)
    attn = _mla(qlat.reshape(b, s, Q_LORA), lat16, kpe16, wq, wkv, t64, q_tiles)
    log_g = jnp.log1p(-jnp.exp2(-5.0 - jnp.arange(RET_HEADS, dtype=f32)))
    log_g = jnp.broadcast_to(log_g[:, None, None], (RET_HEADS, 8, LANES))
    sh = lambda t, w: t.reshape(b, s, w)
    rn, state = _retention(sh(rq, RET_HEADS * RET_QK), sh(rk, RET_HEADS * RET_QK), sh(rv, RET_WIDTH),
                           sh(rg, RET_WIDTH), p['g_ret'], state0, log_g, ret_chunk)
    xmid, h2p, idx, gates, rank, counts = _mixout(
        attn.reshape(b * s, MLA_WIDTH), rn.reshape(b * s, RET_WIDTH), x2d, mod,
        p['g_post_mix'], p['g_pre_ffn'], w_out16, wr, br, counts0, s, tm)
    route = (idx[:, :TOP_K], gates[:, :TOP_K], rank[:, :TOP_K], counts)
    return kvlat.reshape(b, s, KV_LORA), kpe.reshape(b, s, MLA_ROPE), state, xmid, h2p, route


def _combine_order(dest):
    tm = COMBINE_TOKENS
    n_tiles = dest.shape[0] // tm
    return dest.reshape(n_tiles, tm, TOP_K).transpose(0, 2, 1).reshape(n_tiles, 1, TOP_K * tm)


def kernel(x_prompt, x_sample, cache_kv_latent, cache_k_rope, state_retention, c_prompt, c_sample, w_ada, b_ada, g_pre_mix, g_post_mix, g_pre_ffn, g_post_ffn, w_in, g_q_lat, g_kv_lat, w_uq, w_ukv, g_ret, w_out, w_router, b_router, w_gate_up, b_gate_up, w_down, b_down):
    depth = w_in.shape[0]
    assert depth == 1, "the staged problem has a single layer"
    bp, sp, d = x_prompt.shape
    bs, ss, _ = x_sample.shape
    past = cache_kv_latent.shape[2]
    l = 0
    p = dict(g_pre_mix=g_pre_mix[l], g_post_mix=g_post_mix[l], g_pre_ffn=g_pre_ffn[l], g_post_ffn=g_post_ffn[l],
             g_q_lat=g_q_lat[l], g_kv_lat=g_kv_lat[l], g_ret=g_ret[l])
    wts = _prep_weights(w_in[l], w_uq[l], w_ukv[l], w_out[l], w_router[l], b_router[l])

    mod = _ada(jnp.concatenate([c_prompt, c_sample], axis=0), w_ada[l], b_ada[l]).reshape(bp + bs, 6, d)
    mod_p, mod_s = mod[:bp], mod[bp:]

    tq = ATTN_Q_TILE
    tiles_p = tuple((q0, tq, q0 + tq, True) for q0 in range(0, sp, tq))
    zero_state = jnp.zeros((bp, RET_HEADS, RET_QK, RET_V), f32)
    lat_p, kpe_p, st_p, xmid_p, h2p_p, (idx_p, gates_p, rank_p, counts_p) = _mixer_half(
        x_prompt, mod_p, jnp.arange(sp), wts, p, None, None, zero_state, jnp.zeros((1, LANES), f32),
        RET_CHUNK_PROMPT, tiles_p, 512)

    past_lat16 = cache_kv_latent[l].astype(bf16)
    past_kpe16 = jnp.pad(cache_k_rope[l], ((0, 0), (0, 0), (0, MLA_ROPE))).astype(bf16)
    tiles_s = ((0, ss, past + ss, False),)
    lat_s, kpe_s, st_s, xmid_s, h2p_s, (idx_s, gates_s, rank_s, counts) = _mixer_half(
        x_sample, mod_s, past + jnp.arange(ss), wts, p, past_lat16, past_kpe16, state_retention[l], counts_p,
        ss, tiles_s, ss)

    n_p = bp * sp
    dest, row_start, pad_end, n_chunks = _route(
        jnp.concatenate([idx_p, idx_s], axis=0), jnp.concatenate([rank_p, rank_s], axis=0), counts)
    x_sorted = _dispatch(pad_end, n_chunks, dest, h2p_p, h2p_s)
    act = _moe_up(row_start, n_chunks, x_sorted, w_gate_up[l], b_gate_up[l])
    y_sorted = _moe_down(row_start, n_chunks, act, w_down[l], b_down[l])
    y_p = _combine(_combine_order(dest[:n_p]), y_sorted, gates_p, xmid_p, mod_p, p['g_post_ffn'], sp)
    y_s = _combine(_combine_order(dest[n_p:]), y_sorted, gates_s, xmid_s, mod_s, p['g_post_ffn'], ss)

    return (y_p.reshape(bp, sp, d), y_s.reshape(bs, ss, d),
            lat_p[None], kpe_p[None], st_p[None].astype(state_retention.dtype),
            lat_s[None], kpe_s[None], st_s[None].astype(state_retention.dtype))
```

```python
import functools

import numpy as np
import jax
import jax.numpy as jnp
from jax import lax
from jax.experimental import pallas as pl
from jax.experimental.pallas import tpu as pltpu

CHUNK = 64
EPS = 1e-6
ROPE_THETA = 10000.0

MLA_HEADS = 8
MLA_NOPE = 128
MLA_ROPE = 64
MLA_V = 128
Q_LORA = 512
KV_LORA = 512
MLA_WIDTH = MLA_HEADS * MLA_V

RET_HEADS = 4
RET_QK = 256
RET_V = 256
RET_WIDTH = RET_HEADS * RET_V

N_EXPERTS = 32
TOP_K = 4
D_FF = 2048
SWIGLU_LIMIT = 7.0
SWIGLU_ALPHA = 1.702

LANES = 128
V7X_VMEM_BYTES = 64 * 1024 * 1024
VMEM_CAP_BYTES = V7X_VMEM_BYTES - 8 * 1024 * 1024

RET_CHUNK_PROMPT = 128
RET_UNROLL = 4
ATTN_Q_TILE = 256
MOE_ROWS = 256
UP_TILE = 1024
DOWN_TILE = 2048
COMBINE_TOKENS = 64
NEG_BIG = -1e30

bf16 = jnp.bfloat16
f32 = jnp.float32


def _params(sem, vmem_bytes):
    return pltpu.CompilerParams(dimension_semantics=sem, vmem_limit_bytes=int(min(vmem_bytes, VMEM_CAP_BYTES)))


def _const_spec(shape):
    nd = len(shape)
    return pl.BlockSpec(shape, lambda *_: (0,) * nd, pipeline_mode=pl.Buffered(1))


def _rms(x):
    return x * lax.rsqrt(jnp.mean(x * x, axis=-1, keepdims=True) + EPS)


def _silu(x):
    return x * jax.nn.sigmoid(x)


def _ada_kernel(c_ref, w_ref, b_ref, o_ref):
    s = _silu(c_ref[...]).astype(bf16)
    o_ref[...] = jnp.dot(s, w_ref[...].astype(bf16), preferred_element_type=f32) + b_ref[...]


def _ada(c, w_ada, b_ada):
    nb, d = c.shape
    n = w_ada.shape[1]
    tn = 1536
    return pl.pallas_call(
        _ada_kernel,
        grid=(n // tn,),
        in_specs=[pl.BlockSpec((nb, d), lambda j: (0, 0)),
                  pl.BlockSpec((d, tn), lambda j: (0, j)),
                  pl.BlockSpec((1, tn), lambda j: (0, j))],
        out_specs=pl.BlockSpec((nb, tn), lambda j: (0, j)),
        out_shape=jax.ShapeDtypeStruct((nb, n), f32),
        compiler_params=_params(("arbitrary",), 2 * d * tn * 4 + d * tn * 2 + (8 << 20)),
        name="ada_modulation",
    )(c, w_ada, b_ada.reshape(1, n))


_C_Q = 0
_C_KV = _C_Q + Q_LORA
_C_PE = _C_KV + KV_LORA
_C_RQ = _C_PE + 2 * MLA_ROPE
_C_RK = _C_RQ + RET_HEADS * RET_QK
_C_RV = _C_RK + RET_HEADS * RET_QK
_C_RG = _C_RV + RET_WIDTH
_C_END = _C_RG + RET_WIDTH


def _rope_pair(t):
    return t + pltpu.roll(t, MLA_ROPE, axis=1)


def _rope_heads(z, cos, sin, scale):
    outs = []
    half = RET_QK // 2
    for h in range(RET_HEADS):
        x1 = z[:, h * RET_QK:h * RET_QK + half]
        x2 = z[:, h * RET_QK + half:(h + 1) * RET_QK]
        outs.append((x1 * cos - x2 * sin) * scale)
        outs.append((x1 * sin + x2 * cos) * scale)
    return jnp.concatenate(outs, axis=-1)


def _inproj_kernel(x_ref, mod_ref, g_ref, w_ref, gq_ref, gkv_ref, t64_ref, t256_ref,
                   qlat_ref, kvlat_ref, kvlat16_ref, kpe_ref, kpe16_ref, rq_ref, rk_ref, rv_ref, rg_ref):
    m = mod_ref[0]
    h = (_rms(x_ref[...]) * g_ref[...] * (1.0 + m[1:2]) + m[0:1]).astype(bf16)

    def proj(c0, c1):
        return jnp.dot(h, w_ref[:, c0:c1], preferred_element_type=f32)

    qlat_ref[...] = (_rms(proj(_C_Q, _C_KV)) * gq_ref[...]).astype(bf16)
    kv = _rms(proj(_C_KV, _C_PE)) * gkv_ref[...]
    kvlat_ref[...] = kv
    kvlat16_ref[...] = kv.astype(bf16)
    pe = _rope_pair(proj(_C_PE, _C_RQ) * t64_ref[...])
    kpe_ref[...] = pe[:, :MLA_ROPE]
    lane = lax.broadcasted_iota(jnp.int32, pe.shape, 1)
    kpe16_ref[...] = jnp.where(lane < MLA_ROPE, pe, 0.0).astype(bf16)
    cos = t256_ref[:, :RET_QK // 2]
    sin = t256_ref[:, RET_QK // 2:]
    rq_ref[...] = _rope_heads(proj(_C_RQ, _C_RK), cos, sin, 1.0).astype(bf16)
    rk_ref[...] = _rope_heads(proj(_C_RK, _C_RV), cos, sin, RET_QK ** -0.5).astype(bf16)
    rv_ref[...] = proj(_C_RV, _C_RG).astype(bf16)
    rg_ref[...] = proj(_C_RG, _C_END).astype(bf16)


def _inproj(x2d, mod, g_pre, w_ext, g_q, g_kv, t64, t256, seq, tm):
    n, d = x2d.shape
    per_seq = seq // tm
    row = lambda w: pl.BlockSpec((tm, w), lambda i: (i, 0))
    tab = lambda w: pl.BlockSpec((tm, w), lambda i: (i % per_seq, 0))
    outs = [(Q_LORA, bf16), (KV_LORA, f32), (KV_LORA, bf16), (MLA_ROPE, f32), (2 * MLA_ROPE, bf16),
            (RET_HEADS * RET_QK, bf16), (RET_HEADS * RET_QK, bf16), (RET_WIDTH, bf16), (RET_WIDTH, bf16)]
    vmem = w_ext.size * 2 + 2 * tm * d * 4 + 4 * tm * d * 4 + sum(2 * tm * w * 4 for w, _ in outs) + (6 << 20)
    return pl.pallas_call(
        _inproj_kernel,
        grid=(n // tm,),
        in_specs=[row(d),
                  pl.BlockSpec((1, 6, d), lambda i: (i // per_seq, 0, 0)),
                  _const_spec((1, d)), _const_spec(w_ext.shape), _const_spec((1, Q_LORA)), _const_spec((1, KV_LORA)),
                  tab(2 * MLA_ROPE), tab(RET_QK)],
        out_specs=[row(w) for w, _ in outs],
        out_shape=[jax.ShapeDtypeStruct((n, w), dt) for w, dt in outs],
        compiler_params=_params(("arbitrary",), vmem),
        name="in_projection",
    )(x2d, mod, g_pre.reshape(1, d), w_ext, g_q.reshape(1, Q_LORA), g_kv.reshape(1, KV_LORA), t64, t256)


def _mla_kernel(qlat_ref, lat_ref, kpe_ref, wq_ref, wkv_ref, tab_ref, o_ref, q_s, k_s, v_s, *, q_tiles):
    scale = (MLA_NOPE + MLA_ROPE) ** -0.5
    qh = jnp.dot(qlat_ref[0], wq_ref[0], preferred_element_type=f32)
    q_s[:, :MLA_NOPE] = (qh[:, :MLA_NOPE] * scale).astype(bf16)
    q_s[:, MLA_NOPE:] = (_rope_pair(qh[:, MLA_NOPE:] * tab_ref[...]) * scale).astype(bf16)
    kv = jnp.dot(lat_ref[0], wkv_ref[0], preferred_element_type=f32)
    k_s[:, :MLA_NOPE] = kv[:, :MLA_NOPE].astype(bf16)
    k_s[:, MLA_NOPE:] = kpe_ref[0]
    v_s[...] = kv[:, MLA_NOPE:].astype(bf16)
    for q0, ql, kvl, masked in q_tiles:
        q = q_s[q0:q0 + ql, :]
        s = lax.dot_general(q, k_s[0:kvl, :], (((1,), (1,)), ((), ())), preferred_element_type=f32)
        if masked:
            assert kvl == q0 + ql and q0 % CHUNK == 0
            qc = lax.broadcasted_iota(jnp.int32, (ql, 1), 0) // CHUNK
            kc = lax.broadcasted_iota(jnp.int32, (1, ql), 1) // CHUNK
            diag = jnp.where(kc <= qc, s[:, q0:], NEG_BIG)
            s = diag if q0 == 0 else jnp.concatenate([s[:, :q0], diag], axis=1)
        p = jnp.exp(s - jnp.max(s, axis=-1, keepdims=True))
        l = jnp.sum(p, axis=-1, keepdims=True)
        o = jnp.dot(p.astype(bf16), v_s[0:kvl, :], preferred_element_type=f32)
        o_ref[0, q0:q0 + ql, :] = (o / l).astype(bf16)


def _mla(qlat, lat, kpe16, wq, wkv, tab, q_tiles):
    b, sq, _ = qlat.shape
    skv = lat.shape[1]
    kern = functools.partial(_mla_kernel, q_tiles=q_tiles)
    max_ql = max(t[1] for t in q_tiles)
    vmem = 2 * 2 * (sq * Q_LORA + skv * KV_LORA + skv * LANES) + sq * LANES * 4 + sq * 3 * LANES * 4 \
        + (sq + skv) * 3 * LANES * 2 + 4 * max_ql * skv * 4 + (8 << 20)
    return pl.pallas_call(
        kern,
        grid=(b, MLA_HEADS),
        in_specs=[pl.BlockSpec((1, sq, Q_LORA), lambda i, h: (i, 0, 0)),
                  pl.BlockSpec((1, skv, KV_LORA), lambda i, h: (i, 0, 0)),
                  pl.BlockSpec((1, skv, 2 * MLA_ROPE), lambda i, h: (i, 0, 0)),
                  pl.BlockSpec((1, Q_LORA, MLA_NOPE + 2 * MLA_ROPE), lambda i, h: (h, 0, 0)),
                  pl.BlockSpec((1, KV_LORA, MLA_NOPE + MLA_V), lambda i, h: (h, 0, 0)),
                  pl.BlockSpec((sq, 2 * MLA_ROPE), lambda i, h: (0, 0))],
        out_specs=pl.BlockSpec((1, sq, MLA_V), lambda i, h: (i, 0, h)),
        out_shape=jax.ShapeDtypeStruct((b, sq, MLA_WIDTH), bf16),
        scratch_shapes=[pltpu.VMEM((sq, MLA_NOPE + 2 * MLA_ROPE), bf16),
                        pltpu.VMEM((skv, MLA_NOPE + 2 * MLA_ROPE), bf16),
                        pltpu.VMEM((skv, MLA_V), bf16)],
        compiler_params=_params(("arbitrary", "arbitrary"), vmem),
        name="mla_attention",
    )(qlat, lat, kpe16, wq, wkv, tab)


def _ret_kernel(lg_ref, rq_ref, rk_ref, rv_ref, rg_ref, gret_ref, s0_ref, o_ref, sout_ref, state, *, chunk, n_chunks):
    c = chunk
    lg = lg_ref[0, 0:1, 0:1]
    ii = lax.broadcasted_iota(jnp.int32, (c, c), 0)
    jj = lax.broadcasted_iota(jnp.int32, (c, c), 1)
    diff = (ii - jj).astype(f32)
    dmask = jnp.where(diff >= 0.0, jnp.exp(jnp.maximum(diff, 0.0) * lg), 0.0)
    ic = lax.broadcasted_iota(jnp.int32, (c, 1), 0).astype(f32)
    q_decay = jnp.exp((ic + 1.0) * lg)
    k_decay = jnp.exp((c - 1.0 - ic) * lg)
    chunk_decay = jnp.exp(float(c) * lg)
    gret = gret_ref[...]
    state[...] = s0_ref[0, 0]

    def step(n, carry):
        sl = pl.ds(pl.multiple_of(n * c, c), c)
        q = rq_ref[0, sl, :]
        k = rk_ref[0, sl, :]
        v = rv_ref[0, sl, :]
        st = state[...]
        attn = lax.dot_general(q, k, (((1,), (1,)), ((), ())), preferred_element_type=f32) * dmask
        inner = jnp.dot(attn.astype(bf16), v, preferred_element_type=f32)
        cross = jnp.dot(q, st.astype(bf16), preferred_element_type=f32) * q_decay
        o = inner + cross
        kd_t = (k.astype(f32) * k_decay).T.astype(bf16)
        state[...] = st * chunk_decay + jnp.dot(kd_t, v, preferred_element_type=f32)
        mu = jnp.mean(o, axis=-1, keepdims=True)
        oc = o - mu
        rn = oc * lax.rsqrt(jnp.mean(oc * oc, axis=-1, keepdims=True) + EPS) * gret
        o_ref[0, sl, :] = (rn * _silu(rg_ref[0, sl, :].astype(f32))).astype(bf16)
        return carry

    lax.fori_loop(0, n_chunks, step, 0, unroll=min(n_chunks, RET_UNROLL))
    sout_ref[0, 0] = state[...]


def _retention(rq, rk, rv, rg, g_ret, state0, log_g, chunk):
    b, s, _ = rq.shape
    kern = functools.partial(_ret_kernel, chunk=chunk, n_chunks=s // chunk)
    head = lambda w: pl.BlockSpec((1, s, w), lambda i, h: (i, 0, h))
    st = pl.BlockSpec((1, 1, RET_QK, RET_V), lambda i, h: (i, h, 0, 0))
    vmem = 2 * 5 * s * RET_QK * 2 + 5 * RET_QK * RET_V * 4 + (8 << 20)
    return pl.pallas_call(
        kern,
        grid=(b, RET_HEADS),
        in_specs=[pl.BlockSpec((1, 8, LANES), lambda i, h: (h, 0, 0)),
                  head(RET_QK), head(RET_QK), head(RET_V), head(RET_V),
                  pl.BlockSpec((1, RET_V), lambda i, h: (0, h)), st],
        out_specs=[head(RET_V), st],
        out_shape=[jax.ShapeDtypeStruct((b, s, RET_WIDTH), bf16),
                   jax.ShapeDtypeStruct((b, RET_HEADS, RET_QK, RET_V), f32)],
        scratch_shapes=[pltpu.VMEM((RET_QK, RET_V), f32)],
        compiler_params=_params(("arbitrary", "arbitrary"), vmem),
        name="retention",
    )(log_g, rq, rk, rv, rg, g_ret.reshape(1, RET_WIDTH), state0)


def _pack_pairs(lo16, hi16):
    lo = lax.bitcast_convert_type(lo16.astype(f32), jnp.uint32)
    hi = lax.bitcast_convert_type(hi16.astype(f32), jnp.uint32)
    return (hi & jnp.uint32(0xFFFF0000)) | (lo >> 16)


def _unpack_pairs(u):
    lo = lax.bitcast_convert_type(u << 16, f32)
    hi = lax.bitcast_convert_type(u & jnp.uint32(0xFFFF0000), f32)
    return lo, hi


def _mixout_kernel(attn_ref, rn_ref, x_ref, mod_ref, gpost_ref, gpre_ref, wout_ref, wr_ref, br_ref, cnt0_ref,
                   xmid_ref, h2p_ref, idx_ref, gate_ref, rank_ref, cnt_ref, carry):
    @pl.when(pl.program_id(0) == 0)
    def _():
        carry[...] = cnt0_ref[...]

    mix = (jnp.dot(attn_ref[...], wout_ref[0:MLA_WIDTH, :], preferred_element_type=f32)
           + jnp.dot(rn_ref[...], wout_ref[MLA_WIDTH:, :], preferred_element_type=f32))
    m = mod_ref[0]
    x1 = x_ref[...] + m[2:3] * (_rms(mix) * gpost_ref[...])
    xmid_ref[...] = x1
    h2 = _rms(x1) * gpre_ref[...] * (1.0 + m[4:5]) + m[3:4]
    h_hi = h2.astype(bf16)
    half = h2.shape[1] // 2
    h2p_ref[...] = _pack_pairs(h_hi[:, :half], h_hi[:, half:])
    h_lo = (h2 - h_hi.astype(f32)).astype(bf16)
    logits = (jnp.dot(h_hi, wr_ref[0], preferred_element_type=f32)
              + jnp.dot(h_lo, wr_ref[0], preferred_element_type=f32)
              + jnp.dot(h_hi, wr_ref[1], preferred_element_type=f32)) + br_ref[...]
    tm = logits.shape[0]
    col = lax.broadcasted_iota(jnp.int32, logits.shape, 1).astype(f32)
    vals, idxs = [], []
    for _ in range(TOP_K):
        mx = jnp.max(logits, axis=-1, keepdims=True)
        ix = jnp.min(jnp.where(logits == mx, col, float(LANES)), axis=-1, keepdims=True)
        vals.append(mx)
        idxs.append(ix)
        logits = jnp.where(col == ix, -jnp.inf, logits)
    es = [jnp.exp(v - vals[0]) for v in vals]
    tot = es[0] + es[1] + es[2] + es[3]
    onehot = jnp.zeros_like(col)
    for k in range(TOP_K):
        onehot = jnp.where(col == idxs[k], 1.0, onehot)
    earlier = (lax.broadcasted_iota(jnp.int32, (tm, tm), 0) > lax.broadcasted_iota(jnp.int32, (tm, tm), 1))
    before = jnp.dot(jnp.where(earlier, 1.0, 0.0).astype(bf16), onehot.astype(bf16),
                     preferred_element_type=f32) + carry[...]
    idx_out = jnp.zeros_like(col)
    gate_out = jnp.zeros_like(col)
    rank_out = jnp.zeros_like(col)
    for k in range(TOP_K):
        slot = col == float(k)
        idx_out = jnp.where(slot, idxs[k], idx_out)
        gate_out = jnp.where(slot, es[k] / tot, gate_out)
        rank_k = jnp.sum(jnp.where(col == idxs[k], before, 0.0), axis=-1, keepdims=True)
        rank_out = jnp.where(slot, rank_k, rank_out)
    idx_ref[...] = idx_out.astype(jnp.int32)
    gate_ref[...] = gate_out
    rank_ref[...] = rank_out.astype(jnp.int32)
    carry[...] = carry[...] + jnp.sum(onehot, axis=0, keepdims=True)
    cnt_ref[...] = carry[...]


def _mixout(attn, rn, x2d, mod, g_post, g_pre, w_out16, wr, br, counts0, seq, tm):
    n, d = x2d.shape
    per_seq = seq // tm
    row = lambda w: pl.BlockSpec((tm, w), lambda i: (i, 0))
    vmem = w_out16.size * 2 + 2 * tm * d * (4 + 4 + 2 + 2) + 6 * tm * d * 4 + (8 << 20)
    return pl.pallas_call(
        _mixout_kernel,
        grid=(n // tm,),
        in_specs=[row(MLA_WIDTH), row(RET_WIDTH), row(d),
                  pl.BlockSpec((1, 6, d), lambda i: (i // per_seq, 0, 0)),
                  _const_spec((1, d)), _const_spec((1, d)), _const_spec(w_out16.shape),
                  _const_spec(wr.shape), _const_spec((1, LANES)), _const_spec((1, LANES))],
        out_specs=[row(d), row(d // 2), row(LANES), row(LANES), row(LANES),
                   pl.BlockSpec((1, LANES), lambda i: (0, 0))],
        out_shape=[jax.ShapeDtypeStruct((n, d), f32), jax.ShapeDtypeStruct((n, d // 2), jnp.uint32),
                   jax.ShapeDtypeStruct((n, LANES), jnp.int32), jax.ShapeDtypeStruct((n, LANES), f32),
                   jax.ShapeDtypeStruct((n, LANES), jnp.int32), jax.ShapeDtypeStruct((1, LANES), f32)],
        scratch_shapes=[pltpu.VMEM((1, LANES), f32)],
        compiler_params=_params(("arbitrary",), vmem),
        name="mixer_out_router",
    )(attn, rn, x2d, mod, g_post.reshape(1, d), g_pre.reshape(1, d), w_out16, wr, br, counts0)


def _route(top_idx, rank, counts):
    counts = counts[0, :N_EXPERTS].astype(jnp.int32)
    padded = (counts + MOE_ROWS - 1) // MOE_ROWS * MOE_ROWS
    pad_end = jnp.cumsum(padded).astype(jnp.int32)
    pad_start = pad_end - padded
    dest = pad_start[top_idx] + rank
    return dest, pad_start, pad_end, padded // MOE_ROWS


def _n_sorted_rows(n_tokens):
    return (n_tokens * TOP_K + N_EXPERTS * (MOE_ROWS - 1)) // MOE_ROWS * MOE_ROWS


def _fill_unused_chunks(zero_buf, dst_rows, sem, used_rows, total_rows):
    first = used_rows // MOE_ROWS

    def copy(c):
        return pltpu.make_async_copy(zero_buf, dst_rows(pl.ds(pl.multiple_of(c * MOE_ROWS, MOE_ROWS), MOE_ROWS)), sem)

    def start(c, carry):
        copy(c).start()
        return carry

    def wait(c, carry):
        copy(c).wait()
        return carry

    lax.fori_loop(first, total_rows // MOE_ROWS, start, 0)
    lax.fori_loop(first, total_rows // MOE_ROWS, wait, 0)


def _dispatch_kernel(pe_ref, nc_ref, dest_ref, h2p_a_ref, h2p_b_ref, x_hbm, zbuf, sem, zsem, *, tokens, n_first):
    i = pl.program_id(0)

    @pl.when(i == 0)
    def _():
        zbuf[...] = jnp.zeros_like(zbuf)

        def tail(e):
            start = pl.multiple_of(pe_ref[e] - MOE_ROWS, MOE_ROWS)
            return pltpu.make_async_copy(zbuf, x_hbm.at[pl.ds(start, MOE_ROWS), :], zsem)

        def start_tail(e, c):
            @pl.when(nc_ref[e] > 0)
            def _():
                tail(e).start()
            return c

        def wait_tail(e, c):
            @pl.when(nc_ref[e] > 0)
            def _():
                tail(e).wait()
            return c

        lax.fori_loop(0, N_EXPERTS, start_tail, 0)
        lax.fori_loop(0, N_EXPERTS, wait_tail, 0)
        _fill_unused_chunks(zbuf, lambda rows: x_hbm.at[rows, :], zsem, pe_ref[N_EXPERTS - 1], x_hbm.shape[0])

    def scatter(h2p_ref):
        def body(t, c):
            for k in range(TOP_K):
                d = dest_ref[0, 0, t * TOP_K + k]
                pltpu.make_async_copy(h2p_ref.at[pl.ds(t, 1), :], x_hbm.at[pl.ds(d, 1), :], sem).start(priority=k % 2)
            return c

        lax.fori_loop(0, tokens, body, 0, unroll=2)
        for _ in range(TOP_K):
            pltpu.make_async_copy(h2p_ref, x_hbm.at[pl.ds(0, tokens), :], sem).wait()

    @pl.when(i < n_first)
    def _():
        scatter(h2p_a_ref)

    @pl.when(i >= n_first)
    def _():
        scatter(h2p_b_ref)


def _dispatch(pad_end, n_chunks, dest, h2p_a, h2p_b):
    w = h2p_a.shape[1]
    tokens = 512
    n_a, n_b = h2p_a.shape[0] // tokens, h2p_b.shape[0] // tokens
    assert n_a * tokens == h2p_a.shape[0] and n_b * tokens == h2p_b.shape[0]
    n_steps = n_a + n_b
    grid_spec = pltpu.PrefetchScalarGridSpec(
        num_scalar_prefetch=2,
        grid=(n_steps,),
        in_specs=[pl.BlockSpec((1, 1, tokens * TOP_K), lambda i, pe, nc: (i, 0, 0), memory_space=pltpu.SMEM),
                  pl.BlockSpec((tokens, w), lambda i, pe, nc: (jnp.minimum(i, n_a - 1), 0)),
                  pl.BlockSpec((tokens, w), lambda i, pe, nc: (jnp.maximum(i - n_a, 0), 0))],
        out_specs=pl.BlockSpec(memory_space=pl.ANY),
        scratch_shapes=[pltpu.VMEM((MOE_ROWS, w), jnp.uint32), pltpu.SemaphoreType.DMA(()),
                        pltpu.SemaphoreType.DMA(())],
    )
    return pl.pallas_call(
        functools.partial(_dispatch_kernel, tokens=tokens, n_first=n_a),
        grid_spec=grid_spec,
        out_shape=jax.ShapeDtypeStruct((_n_sorted_rows(n_steps * tokens), w), jnp.uint32),
        compiler_params=_params(("arbitrary",), 24 << 20),
        name="moe_dispatch",
    )(pad_end, n_chunks, dest.reshape(n_steps, 1, tokens * TOP_K), h2p_a, h2p_b)


def _chunk_rows(r0, c):
    return pl.ds(pl.multiple_of(r0 + c * MOE_ROWS, MOE_ROWS), MOE_ROWS)


CHUNK_READ_PRIORITY = 1
READ_SLOTS = 4


def _start_first_reads(copies, n):
    for a in range(READ_SLOTS - 1):
        @pl.when(a < n)
        def _():
            for cp in copies(a, a):
                cp.start(priority=CHUNK_READ_PRIORITY)


def _next_step_rows(rs_ref, nc_ref, e, j, n_j):
    last_tile = j == n_j - 1
    has_next = jnp.logical_or(e < N_EXPERTS - 1, jnp.logical_not(last_tile))
    e_next = jnp.minimum(jnp.where(last_tile, e + 1, e), N_EXPERTS - 1)
    return rs_ref[e_next], jnp.where(has_next, nc_ref[e_next], 0)


def _read_ahead(copies, c, n):
    ahead = c + (READ_SLOTS - 1)

    @pl.when(ahead < n)
    def _():
        for cp in copies(ahead, ahead % READ_SLOTS):
            cp.start(priority=CHUNK_READ_PRIORITY)

    rslot = c % READ_SLOTS
    for cp in copies(c, rslot):
        cp.wait()
    return rslot


def _moe_up_kernel(rs_ref, nc_ref, x_hbm, wg_ref, wu_ref, bg_ref, bu_ref, act_hbm,
                   wg16, wu16, xbuf, obuf, sem_in, sem_out):
    e = pl.program_id(0)
    j = pl.program_id(1)
    n = nc_ref[e]
    r0 = rs_ref[e]

    def in_copy(c, slot, base=r0):
        return [pltpu.make_async_copy(x_hbm.at[_chunk_rows(base, c), :], xbuf.at[slot], sem_in.at[slot])]

    def out_copy(c, slot):
        return pltpu.make_async_copy(obuf.at[slot], act_hbm.at[j, _chunk_rows(r0, c), :], sem_out.at[slot])

    @pl.when(n > 0)
    def _():
        @pl.when(jnp.logical_and(e == 0, j == 0))
        def _():
            _start_first_reads(in_copy, n)

        wg16[...] = wg_ref[0].astype(bf16)
        wu16[...] = wu_ref[0].astype(bf16)
        half = wg16.shape[0] // 2

        def body(c, carry):
            slot = c % 2
            rslot = _read_ahead(in_copy, c, n)
            lo, hi = _unpack_pairs(xbuf[rslot])
            x_lo = lo.astype(bf16)
            x_hi = hi.astype(bf16)

            def proj(w16, b_ref):
                return (jnp.dot(x_lo, w16[0:half, :], preferred_element_type=f32)
                        + jnp.dot(x_hi, w16[half:, :], preferred_element_type=f32) + b_ref[0])

            gate = jnp.minimum(proj(wg16, bg_ref), SWIGLU_LIMIT)
            up = jnp.clip(proj(wu16, bu_ref), -SWIGLU_LIMIT, SWIGLU_LIMIT)
            act = (up + 1.0) * gate * jax.nn.sigmoid(SWIGLU_ALPHA * gate)

            @pl.when(c >= 2)
            def _():
                out_copy(c - 2, slot).wait()

            obuf[slot] = act.astype(bf16)
            out_copy(c, slot).start()
            return carry

        lax.fori_loop(0, n, body, 0)

        @pl.when(n >= 2)
        def _():
            out_copy(n - 2, n % 2).wait()

        out_copy(n - 1, (n - 1) % 2).wait()

    r0_next, n_next = _next_step_rows(rs_ref, nc_ref, e, j, act_hbm.shape[0])
    _start_first_reads(lambda c, slot: in_copy(c, slot, r0_next), n_next)

    @pl.when(e == N_EXPERTS - 1)
    def _():
        obuf[0] = jnp.zeros(obuf.shape[1:], obuf.dtype)
        _fill_unused_chunks(obuf.at[0], lambda rows: act_hbm.at[j, rows, :], sem_out.at[0],
                            r0 + n * MOE_ROWS, act_hbm.shape[1])


def _moe_up(row_start, n_chunks, x_sorted, w_gate_up, b_gate_up):
    e, d, f2 = w_gate_up.shape
    tf = UP_TILE
    n_j = D_FF // tf
    rows = x_sorted.shape[0]
    vmem = 2 * 2 * d * tf * 4 + 2 * d * tf * 2 + 2 * MOE_ROWS * d * 2 + 2 * MOE_ROWS * tf * 2 \
        + 6 * MOE_ROWS * tf * 4 + (6 << 20)
    grid_spec = pltpu.PrefetchScalarGridSpec(
        num_scalar_prefetch=2,
        grid=(e, n_j),
        in_specs=[pl.BlockSpec(memory_space=pl.ANY),
                  pl.BlockSpec((1, d, tf), lambda i, j, rs, nc: (i, 0, j)),
                  pl.BlockSpec((1, d, tf), lambda i, j, rs, nc: (i, 0, n_j + j)),
                  pl.BlockSpec((1, 1, tf), lambda i, j, rs, nc: (i, 0, j)),
                  pl.BlockSpec((1, 1, tf), lambda i, j, rs, nc: (i, 0, n_j + j))],
        out_specs=pl.BlockSpec(memory_space=pl.ANY),
        scratch_shapes=[pltpu.VMEM((d, tf), bf16), pltpu.VMEM((d, tf), bf16),
                        pltpu.VMEM((READ_SLOTS, MOE_ROWS, d // 2), jnp.uint32), pltpu.VMEM((2, MOE_ROWS, tf), bf16),
                        pltpu.SemaphoreType.DMA((READ_SLOTS,)), pltpu.SemaphoreType.DMA((2,))],
    )
    b3 = b_gate_up.reshape(e, 1, f2)
    return pl.pallas_call(
        _moe_up_kernel,
        grid_spec=grid_spec,
        out_shape=jax.ShapeDtypeStruct((n_j, rows, tf), bf16),
        compiler_params=_params(("arbitrary", "arbitrary"), vmem),
        name="moe_gate_up",
    )(row_start, n_chunks, x_sorted, w_gate_up, w_gate_up, b3, b3)


def _moe_down_kernel(rs_ref, nc_ref, act_hbm, wd_ref, bd_ref, y_hbm, wd16, abuf, ybuf, sem_in, sem_out):
    e = pl.program_id(0)
    nt = pl.program_id(1)
    n = nc_ref[e]
    r0 = rs_ref[e]
    n_k, tf = abuf.shape[1], abuf.shape[3]
    tw = ybuf.shape[2]
    n_t = y_hbm.shape[1] // tw

    def on_tile(fn):
        for t in range(n_t):
            @pl.when(nt == t)
            def _():
                fn(t)

    def in_copies(c, slot, base=r0):
        return [pltpu.make_async_copy(act_hbm.at[k, _chunk_rows(base, c), :], abuf.at[slot, k], sem_in.at[slot])
                for k in range(n_k)]

    def out_copy(c, slot, t):
        return pltpu.make_async_copy(ybuf.at[slot], y_hbm.at[_chunk_rows(r0, c), t * tw:(t + 1) * tw],
                                     sem_out.at[slot])

    @pl.when(n > 0)
    def _():
        @pl.when(jnp.logical_and(e == 0, nt == 0))
        def _():
            _start_first_reads(in_copies, n)

        wd16[...] = wd_ref[0].astype(bf16)

        def body(c, carry):
            slot = c % 2
            rslot = _read_ahead(in_copies, c, n)
            y = bd_ref[0]
            for k in range(n_k):
                y = y + jnp.dot(abuf[rslot, k], wd16[k * tf:(k + 1) * tf, :], preferred_element_type=f32)
            y16 = y.astype(bf16)

            @pl.when(c >= 2)
            def _():
                on_tile(lambda t: out_copy(c - 2, slot, t).wait())

            ybuf[slot] = _pack_pairs(y16[:, :tw], y16[:, tw:])
            on_tile(lambda t: out_copy(c, slot, t).start())
            return carry

        lax.fori_loop(0, n, body, 0)

        @pl.when(n >= 2)
        def _():
            on_tile(lambda t: out_copy(n - 2, n % 2, t).wait())

        on_tile(lambda t: out_copy(n - 1, (n - 1) % 2, t).wait())

    r0_next, n_next = _next_step_rows(rs_ref, nc_ref, e, nt, n_t)
    _start_first_reads(lambda c, slot: in_copies(c, slot, r0_next), n_next)

    @pl.when(e == N_EXPERTS - 1)
    def _():
        ybuf[0] = jnp.zeros(ybuf.shape[1:], ybuf.dtype)
        on_tile(lambda t: _fill_unused_chunks(ybuf.at[0], lambda rows: y_hbm.at[rows, t * tw:(t + 1) * tw],
                                              sem_out.at[0], r0 + n * MOE_ROWS, y_hbm.shape[0]))


def _moe_down(row_start, n_chunks, act, w_down, b_down):
    e, f, d = w_down.shape
    n_k, rows, tf = act.shape
    tn = DOWN_TILE
    vmem = 2 * f * tn * 4 + f * tn * 2 + 2 * MOE_ROWS * f * 2 + 2 * MOE_ROWS * tn * 2 + 4 * MOE_ROWS * tn * 4 + (6 << 20)
    grid_spec = pltpu.PrefetchScalarGridSpec(
        num_scalar_prefetch=2,
        grid=(e, d // tn),
        in_specs=[pl.BlockSpec(memory_space=pl.ANY),
                  pl.BlockSpec((1, f, tn), lambda i, t, rs, nc: (i, 0, t)),
                  pl.BlockSpec((1, 1, tn), lambda i, t, rs, nc: (i, 0, t))],
        out_specs=pl.BlockSpec(memory_space=pl.ANY),
        scratch_shapes=[pltpu.VMEM((f, tn), bf16), pltpu.VMEM((READ_SLOTS, n_k, MOE_ROWS, tf), bf16),
                        pltpu.VMEM((2, MOE_ROWS, tn // 2), jnp.uint32),
                        pltpu.SemaphoreType.DMA((READ_SLOTS,)), pltpu.SemaphoreType.DMA((2,))],
    )
    return pl.pallas_call(
        _moe_down_kernel,
        grid_spec=grid_spec,
        out_shape=jax.ShapeDtypeStruct((rows, d // 2), jnp.uint32),
        compiler_params=_params(("arbitrary", "arbitrary"), vmem),
        name="moe_down",
    )(row_start, n_chunks, act, w_down, b_down.reshape(e, 1, d))


def _block_schedule(n_chunks, n_blocks):
    ends = jnp.cumsum(n_chunks)
    n_live = ends[-1]
    blk = jnp.arange(n_blocks, dtype=jnp.int32)
    block_e = jnp.searchsorted(ends, jnp.minimum(blk, n_live - 1), side='right').astype(jnp.int32)
    is_first = jnp.logical_and(blk == (ends - n_chunks)[block_e], blk < n_live).astype(jnp.int32)
    ids = jnp.arange(N_EXPERTS, dtype=jnp.int32)
    cand = jnp.where(n_chunks > 0, ids, N_EXPERTS)
    from_here = lax.cummin(cand[::-1])[::-1]
    after = jnp.concatenate([from_here[1:], jnp.full((1,), N_EXPERTS, jnp.int32)])
    wraps = (after >= N_EXPERTS).astype(jnp.int32)
    next_e = jnp.where(after < N_EXPERTS, after, from_here[0]).astype(jnp.int32)
    return block_e, is_first, next_e, wraps, n_live.reshape(1).astype(jnp.int32)


def _on_static(value, n, fn):
    for v in range(n):
        @pl.when(value == v)
        def _():
            fn(v)


def _moe_up_kernel(be_ref, first_ref, next_ref, wrap_ref, nl_ref, x_ref, w_hbm, bg_ref, bu_ref, act_ref,
                   stage, w16, sem, *, n_j):
    j = pl.program_id(0)
    b = pl.program_id(1)
    e = be_ref[b]
    live = b < nl_ref[0]
    tf = w16.shape[2]

    def weight_copies(ee, jj):
        return [pltpu.make_async_copy(w_hbm.at[ee, :, g * D_FF + jj * tf:g * D_FF + (jj + 1) * tf], stage.at[g], sem)
                for g in range(2)]

    def start_weights(ee, jd):
        _on_static(jd, n_j, lambda jj: [cp.start() for cp in weight_copies(ee, jj)])

    @pl.when(jnp.logical_and(j == 0, b == 0))
    def _():
        start_weights(e, j)

    @pl.when(jnp.logical_and(live, first_ref[b] == 1))
    def _():
        _on_static(j, n_j, lambda jj: [cp.wait() for cp in weight_copies(e, jj)])
        w16[...] = stage[...].astype(bf16)
        j_next = j + wrap_ref[e]

        @pl.when(j_next < n_j)
        def _():
            start_weights(next_ref[e], j_next)

    @pl.when(live)
    def _():
        lo, hi = _unpack_pairs(x_ref[...])
        x_lo = lo.astype(bf16)
        x_hi = hi.astype(bf16)
        half = x_lo.shape[1]

        def proj(g, b_ref):
            return (jnp.dot(x_lo, w16[g, 0:half, :], preferred_element_type=f32)
                    + jnp.dot(x_hi, w16[g, half:, :], preferred_element_type=f32) + b_ref[0])

        gate = jnp.minimum(proj(0, bg_ref), SWIGLU_LIMIT)
        up = jnp.clip(proj(1, bu_ref), -SWIGLU_LIMIT, SWIGLU_LIMIT)
        act_ref[0] = ((up + 1.0) * gate * jax.nn.sigmoid(SWIGLU_ALPHA * gate)).astype(bf16)

    @pl.when(jnp.logical_not(live))
    def _():
        act_ref[...] = jnp.zeros_like(act_ref)


def _moe_up(sched, x_sorted, w_gate_up, b_gate_up):
    e, d, f2 = w_gate_up.shape
    tf = UP_TILE
    n_j = D_FF // tf
    rows = x_sorted.shape[0]
    n_blocks = rows // MOE_ROWS
    vmem = 2 * d * tf * 4 + 2 * d * tf * 2 + 2 * MOE_ROWS * d * 2 + 2 * MOE_ROWS * tf * 2 \
        + 8 * MOE_ROWS * tf * 4 + (6 << 20)
    grid_spec = pltpu.PrefetchScalarGridSpec(
        num_scalar_prefetch=5,
        grid=(n_j, n_blocks),
        in_specs=[pl.BlockSpec((MOE_ROWS, d // 2), lambda j, b, be, *_: (b, 0)),
                  pl.BlockSpec(memory_space=pl.ANY),
                  pl.BlockSpec((1, 1, tf), lambda j, b, be, *_: (be[b], 0, j)),
                  pl.BlockSpec((1, 1, tf), lambda j, b, be, *_: (be[b], 0, n_j + j))],
        out_specs=pl.BlockSpec((1, MOE_ROWS, tf), lambda j, b, be, *_: (j, b, 0)),
        scratch_shapes=[pltpu.VMEM((2, d, tf), f32), pltpu.VMEM((2, d, tf), bf16), pltpu.SemaphoreType.DMA(())],
    )
    b3 = b_gate_up.reshape(e, 1, f2)
    return pl.pallas_call(
        functools.partial(_moe_up_kernel, n_j=n_j),
        grid_spec=grid_spec,
        out_shape=jax.ShapeDtypeStruct((n_j, rows, tf), bf16),
        compiler_params=_params(("arbitrary", "arbitrary"), vmem),
        name="moe_gate_up",
    )(*sched, x_sorted, w_gate_up, b3, b3)


def _moe_down_kernel(be_ref, first_ref, next_ref, wrap_ref, nl_ref, act_ref, w_hbm, bd_ref, y_ref, stage, w16, sem):
    b = pl.program_id(0)
    e = be_ref[b]
    live = b < nl_ref[0]
    n_k, _, tf = act_ref.shape

    def weight_copy(ee):
        return pltpu.make_async_copy(w_hbm.at[ee], stage, sem)

    @pl.when(b == 0)
    def _():
        weight_copy(e).start()

    @pl.when(jnp.logical_and(live, first_ref[b] == 1))
    def _():
        weight_copy(e).wait()
        w16[...] = stage[...].astype(bf16)

        @pl.when(wrap_ref[e] == 0)
        def _():
            weight_copy(next_ref[e]).start()

    @pl.when(live)
    def _():
        y = bd_ref[0]
        for k in range(n_k):
            y = y + jnp.dot(act_ref[k], w16[k * tf:(k + 1) * tf, :], preferred_element_type=f32)
        y16 = y.astype(bf16)
        half = y16.shape[1] // 2
        y_ref[...] = _pack_pairs(y16[:, :half], y16[:, half:])

    @pl.when(jnp.logical_not(live))
    def _():
        y_ref[...] = jnp.zeros_like(y_ref)


def _moe_down(sched, act, w_down, b_down):
    e, f, d = w_down.shape
    n_k, rows, tf = act.shape
    n_blocks = rows // MOE_ROWS
    vmem = f * d * 4 + f * d * 2 + 2 * MOE_ROWS * f * 2 + 2 * MOE_ROWS * d * 2 + 4 * MOE_ROWS * d * 4 + (6 << 20)
    grid_spec = pltpu.PrefetchScalarGridSpec(
        num_scalar_prefetch=5,
        grid=(n_blocks,),
        in_specs=[pl.BlockSpec((n_k, MOE_ROWS, tf), lambda b, be, *_: (0, b, 0)),
                  pl.BlockSpec(memory_space=pl.ANY),
                  pl.BlockSpec((1, 1, d), lambda b, be, *_: (be[b], 0, 0))],
        out_specs=pl.BlockSpec((MOE_ROWS, d // 2), lambda b, be, *_: (b, 0)),
        scratch_shapes=[pltpu.VMEM((f, d), f32), pltpu.VMEM((f, d), bf16), pltpu.SemaphoreType.DMA(())],
    )
    return pl.pallas_call(
        _moe_down_kernel,
        grid_spec=grid_spec,
        out_shape=jax.ShapeDtypeStruct((rows, d // 2), jnp.uint32),
        compiler_params=_params(("arbitrary",), vmem),
        name="moe_down",
    )(*sched, act, w_down, b_down.reshape(e, 1, d))


def _gather_rows(idx_ref, src_hbm, dst, sem, n_rows):
    def body(r2, carry):
        for u in range(2):
            r = 2 * r2 + u
            t = idx_ref[0, 0, r]
            pltpu.make_async_copy(src_hbm.at[pl.ds(t, 1), :], dst.at[pl.ds(r, 1), :], sem).start(priority=u)
        return carry
    lax.fori_loop(0, n_rows // 2, body, 0, unroll=4)


def _wait_rows(src_hbm, dst, sem, n_rows):
    pltpu.make_async_copy(src_hbm.at[pl.ds(0, n_rows), :], dst, sem).wait()


def _prefetched_gather(i, n_live, cur_idx_ref, next_idx_ref, src_hbm, buf, sem, n_rows):
    slot = i % 2

    @pl.when(jnp.logical_and(i == 0, n_live > 0))
    def _():
        _gather_rows(cur_idx_ref, src_hbm, buf.at[0], sem.at[0], n_rows)

    @pl.when(i + 1 < n_live)
    def _():
        _gather_rows(next_idx_ref, src_hbm, buf.at[1 - slot], sem.at[1 - slot], n_rows)

    @pl.when(i < n_live)
    def _():
        _wait_rows(src_hbm, buf.at[slot], sem.at[slot], n_rows)

    return slot


def _combine_kernel(pos_ref, posn_ref, y_hbm, gates_ref, xmid_ref, mod_ref, gpost_ref, out_ref, ybuf, sem, *, n_steps):
    i = pl.program_id(0)
    tm = COMBINE_TOKENS
    slot = _prefetched_gather(i, n_steps, pos_ref, posn_ref, y_hbm, ybuf, sem, TOP_K * tm)
    g = gates_ref[...]
    lo = hi = None
    for k in range(TOP_K):
        l_k, h_k = _unpack_pairs(ybuf[slot, k * tm:(k + 1) * tm, :])
        gk = g[:, k:k + 1]
        lo = gk * l_k if lo is None else lo + gk * l_k
        hi = gk * h_k if hi is None else hi + gk * h_k
    half = lo.shape[1]
    inv = lax.rsqrt((jnp.sum(lo * lo, axis=-1, keepdims=True) + jnp.sum(hi * hi, axis=-1, keepdims=True))
                    / (2 * half) + EPS)
    gate2 = mod_ref[0][5:6]
    gp = gpost_ref[...]
    out_ref[:, :half] = xmid_ref[:, :half] + gate2[:, :half] * (lo * inv * gp[:, :half])
    out_ref[:, half:] = xmid_ref[:, half:] + gate2[:, half:] * (hi * inv * gp[:, half:])


def _combine(pos, y_sorted, gates, xmid, mod, g_post, seq):
    n, d = xmid.shape
    tm = COMBINE_TOKENS
    n_tiles = n // tm
    per_seq = seq // tm
    vmem = 2 * TOP_K * tm * d * 2 + 4 * tm * d * 4 + 6 * tm * d * 4 + (8 << 20)
    return pl.pallas_call(
        functools.partial(_combine_kernel, n_steps=n_tiles),
        grid=(n_tiles,),
        in_specs=[pl.BlockSpec((1, 1, TOP_K * tm), lambda i: (i, 0, 0), memory_space=pltpu.SMEM),
                  pl.BlockSpec((1, 1, TOP_K * tm), lambda i: (jnp.minimum(i + 1, n_tiles - 1), 0, 0),
                               memory_space=pltpu.SMEM),
                  pl.BlockSpec(memory_space=pl.ANY),
                  pl.BlockSpec((tm, TOP_K), lambda i: (i, 0)),
                  pl.BlockSpec((tm, d), lambda i: (i, 0)),
                  pl.BlockSpec((1, 6, d), lambda i: (i // per_seq, 0, 0)),
                  _const_spec((1, d))],
        out_specs=pl.BlockSpec((tm, d), lambda i: (i, 0)),
        out_shape=jax.ShapeDtypeStruct((n, d), f32),
        scratch_shapes=[pltpu.VMEM((2, TOP_K * tm, d // 2), jnp.uint32), pltpu.SemaphoreType.DMA((2,))],
        compiler_params=_params(("arbitrary",), vmem),
        name="moe_combine",
    )(pos, pos, y_sorted, gates, xmid, mod, g_post.reshape(1, d))


def _rope_tables(pos):
    def cs(d):
        half = d // 2
        inv_freq = 1.0 / (ROPE_THETA ** (jnp.arange(half, dtype=f32) * (2.0 / d)))
        ang = pos.astype(f32)[:, None] * inv_freq[None, :]
        return jnp.cos(ang), jnp.sin(ang)
    c64, s64 = cs(MLA_ROPE)
    c256, s256 = cs(RET_QK)
    t64 = jnp.concatenate([c64, c64, s64, s64], axis=-1)
    t256 = jnp.concatenate([c256, s256], axis=-1)
    return t64, t256


def _rot_cols(w):
    half = w.shape[-1] // 2
    return jnp.concatenate([-w[..., half:], w[..., :half]], axis=-1)


def _prep_weights(w_in, w_uq, w_ukv, w_out, w_router, b_router):
    b = np.cumsum((Q_LORA, KV_LORA, MLA_ROPE))
    w_pe = w_in[:, b[1]:b[2]]
    w_ext = jnp.concatenate([w_in[:, :b[2]], _rot_cols(w_pe), w_in[:, b[2]:]], axis=-1).astype(bf16)
    wq = w_uq.reshape(Q_LORA, MLA_HEADS, MLA_NOPE + MLA_ROPE)
    wq = jnp.concatenate([wq, _rot_cols(wq[..., MLA_NOPE:])], axis=-1).transpose(1, 0, 2).astype(bf16)
    wkv = w_ukv.reshape(KV_LORA, MLA_HEADS, MLA_NOPE + MLA_V).transpose(1, 0, 2).astype(bf16)
    r_hi = w_router.astype(bf16)
    r_lo = (w_router - r_hi.astype(f32)).astype(bf16)
    pad = ((0, 0), (0, LANES - N_EXPERTS))
    wr = jnp.stack([jnp.pad(r_hi, pad), jnp.pad(r_lo, pad)])
    br = jnp.pad(b_router, (0, LANES - N_EXPERTS), constant_values=NEG_BIG).reshape(1, LANES)
    return w_ext, wq, wkv, w_out.astype(bf16), wr, br


def _mixer_half(x, mod, pos, wts, p, past_lat16, past_kpe16, state0, counts0, ret_chunk, q_tiles, tm):
    w_ext, wq, wkv, w_out16, wr, br = wts
    b, s, d = x.shape
    x2d = x.reshape(b * s, d)
    t64, t256 = _rope_tables(pos)
    qlat, kvlat, kvlat16, kpe, kpe16, rq, rk, rv, rg = _inproj(
        x2d, mod, p['g_pre_mix'], w_ext, p['g_q_lat'], p['g_kv_lat'], t64, t256, s, tm)
    lat16 = kvlat16.reshape(b, s, KV_LORA)
    kpe16 = kpe16.reshape(b, s, 2 * MLA_ROPE)
    if past_lat16 is not None:
        lat16 = jnp.concatenate([past_lat16, lat16], axis=1)
        kpe16 = jnp.concatenate([past_kpe16, kpe16], axis=1)
    attn = _mla(qlat.reshape(b, s, Q_LORA), lat16, kpe16, wq, wkv, t64, q_tiles)
    log_g = jnp.log1p(-jnp.exp2(-5.0 - jnp.arange(RET_HEADS, dtype=f32)))
    log_g = jnp.broadcast_to(log_g[:, None, None], (RET_HEADS, 8, LANES))
    sh = lambda t, w: t.reshape(b, s, w)
    rn, state = _retention(sh(rq, RET_HEADS * RET_QK), sh(rk, RET_HEADS * RET_QK), sh(rv, RET_WIDTH),
                           sh(rg, RET_WIDTH), p['g_ret'], state0, log_g, ret_chunk)
    xmid, h2p, idx, gates, rank, counts = _mixout(
        attn.reshape(b * s, MLA_WIDTH), rn.reshape(b * s, RET_WIDTH), x2d, mod,
        p['g_post_mix'], p['g_pre_ffn'], w_out16, wr, br, counts0, s, tm)
    route = (idx[:, :TOP_K], gates[:, :TOP_K], rank[:, :TOP_K], counts)
    return kvlat.reshape(b, s, KV_LORA), kpe.reshape(b, s, MLA_ROPE), state, xmid, h2p, route


def _combine_order(dest):
    tm = COMBINE_TOKENS
    n_tiles = dest.shape[0] // tm
    return dest.reshape(n_tiles, tm, TOP_K).transpose(0, 2, 1).reshape(n_tiles, 1, TOP_K * tm)


def kernel(x_prompt, x_sample, cache_kv_latent, cache_k_rope, state_retention, c_prompt, c_sample, w_ada, b_ada, g_pre_mix, g_post_mix, g_pre_ffn, g_post_ffn, w_in, g_q_lat, g_kv_lat, w_uq, w_ukv, g_ret, w_out, w_router, b_router, w_gate_up, b_gate_up, w_down, b_down):
    depth = w_in.shape[0]
    assert depth == 1, "the staged problem has a single layer"
    bp, sp, d = x_prompt.shape
    bs, ss, _ = x_sample.shape
    past = cache_kv_latent.shape[2]
    l = 0
    p = dict(g_pre_mix=g_pre_mix[l], g_post_mix=g_post_mix[l], g_pre_ffn=g_pre_ffn[l], g_post_ffn=g_post_ffn[l],
             g_q_lat=g_q_lat[l], g_kv_lat=g_kv_lat[l], g_ret=g_ret[l])
    wts = _prep_weights(w_in[l], w_uq[l], w_ukv[l], w_out[l], w_router[l], b_router[l])

    mod = _ada(jnp.concatenate([c_prompt, c_sample], axis=0), w_ada[l], b_ada[l]).reshape(bp + bs, 6, d)
    mod_p, mod_s = mod[:bp], mod[bp:]

    tq = ATTN_Q_TILE
    tiles_p = tuple((q0, tq, q0 + tq, True) for q0 in range(0, sp, tq))
    zero_state = jnp.zeros((bp, RET_HEADS, RET_QK, RET_V), f32)
    lat_p, kpe_p, st_p, xmid_p, h2p_p, (idx_p, gates_p, rank_p, counts_p) = _mixer_half(
        x_prompt, mod_p, jnp.arange(sp), wts, p, None, None, zero_state, jnp.zeros((1, LANES), f32),
        RET_CHUNK_PROMPT, tiles_p, 512)

    past_lat16 = cache_kv_latent[l].astype(bf16)
    past_kpe16 = jnp.pad(cache_k_rope[l], ((0, 0), (0, 0), (0, MLA_ROPE))).astype(bf16)
    tiles_s = ((0, ss, past + ss, False),)
    lat_s, kpe_s, st_s, xmid_s, h2p_s, (idx_s, gates_s, rank_s, counts) = _mixer_half(
        x_sample, mod_s, past + jnp.arange(ss), wts, p, past_lat16, past_kpe16, state_retention[l], counts_p,
        ss, tiles_s, ss)

    n_p = bp * sp
    dest, row_start, pad_end, n_chunks = _route(
        jnp.concatenate([idx_p, idx_s], axis=0), jnp.concatenate([rank_p, rank_s], axis=0), counts)
    x_sorted = _dispatch(pad_end, n_chunks, dest, h2p_p, h2p_s)
    sched = _block_schedule(n_chunks, x_sorted.shape[0] // MOE_ROWS)
    act = _moe_up(sched, x_sorted, w_gate_up[l], b_gate_up[l])
    y_sorted = _moe_down(sched, act, w_down[l], b_down[l])
    y_p = _combine(_combine_order(dest[:n_p]), y_sorted, gates_p, xmid_p, mod_p, p['g_post_ffn'], sp)
    y_s = _combine(_combine_order(dest[n_p:]), y_sorted, gates_s, xmid_s, mod_s, p['g_post_ffn'], ss)

    return (y_p.reshape(bp, sp, d), y_s.reshape(bs, ss, d),
            lat_p[None], kpe_p[None], st_p[None].astype(state_retention.dtype),
            lat_s[None], kpe_s[None], st_s[None].astype(state_retention.dtype))
```

```python
import functools

import numpy as np
import jax
import jax.numpy as jnp
from jax import lax
from jax.experimental import pallas as pl
from jax.experimental.pallas import tpu as pltpu

CHUNK = 64
EPS = 1e-6
ROPE_THETA = 10000.0

MLA_HEADS = 8
MLA_NOPE = 128
MLA_ROPE = 64
MLA_V = 128
Q_LORA = 512
KV_LORA = 512
MLA_WIDTH = MLA_HEADS * MLA_V

RET_HEADS = 4
RET_QK = 256
RET_V = 256
RET_WIDTH = RET_HEADS * RET_V

N_EXPERTS = 32
TOP_K = 4
D_FF = 2048
SWIGLU_LIMIT = 7.0
SWIGLU_ALPHA = 1.702

LANES = 128
V7X_VMEM_BYTES = 64 * 1024 * 1024
VMEM_CAP_BYTES = V7X_VMEM_BYTES - 8 * 1024 * 1024

RET_CHUNK_PROMPT = 128
RET_UNROLL = 8
ATTN_Q_TILE = 256
MOE_ROWS = 256
UP_TILE = 1024
DOWN_TILE = 2048
COMBINE_TOKENS = 64
NEG_BIG = -1e30

bf16 = jnp.bfloat16
f32 = jnp.float32


def _params(sem, vmem_bytes):
    return pltpu.CompilerParams(dimension_semantics=sem, vmem_limit_bytes=int(min(vmem_bytes, VMEM_CAP_BYTES)))


def _const_spec(shape):
    nd = len(shape)
    return pl.BlockSpec(shape, lambda *_: (0,) * nd, pipeline_mode=pl.Buffered(1))


def _rms(x):
    return x * lax.rsqrt(jnp.mean(x * x, axis=-1, keepdims=True) + EPS)


def _silu(x):
    return x * jax.nn.sigmoid(x)


def _ada_kernel(c_ref, w_ref, b_ref, o_ref):
    s = _silu(c_ref[...]).astype(bf16)
    o_ref[...] = jnp.dot(s, w_ref[...].astype(bf16), preferred_element_type=f32) + b_ref[...]


def _ada(c, w_ada, b_ada):
    nb, d = c.shape
    n = w_ada.shape[1]
    tn = 1536
    return pl.pallas_call(
        _ada_kernel,
        grid=(n // tn,),
        in_specs=[pl.BlockSpec((nb, d), lambda j: (0, 0)),
                  pl.BlockSpec((d, tn), lambda j: (0, j)),
                  pl.BlockSpec((1, tn), lambda j: (0, j))],
        out_specs=pl.BlockSpec((nb, tn), lambda j: (0, j)),
        out_shape=jax.ShapeDtypeStruct((nb, n), f32),
        compiler_params=_params(("arbitrary",), 2 * d * tn * 4 + d * tn * 2 + (8 << 20)),
        name="ada_modulation",
    )(c, w_ada, b_ada.reshape(1, n))


_C_Q = 0
_C_KV = _C_Q + Q_LORA
_C_PE = _C_KV + KV_LORA
_C_RQ = _C_PE + 2 * MLA_ROPE
_C_RK = _C_RQ + RET_HEADS * RET_QK
_C_RV = _C_RK + RET_HEADS * RET_QK
_C_RG = _C_RV + RET_WIDTH
_C_END = _C_RG + RET_WIDTH


def _rope_pair(t):
    return t + pltpu.roll(t, MLA_ROPE, axis=1)


def _rope_heads(z, cos, sin, scale):
    outs = []
    half = RET_QK // 2
    for h in range(RET_HEADS):
        x1 = z[:, h * RET_QK:h * RET_QK + half]
        x2 = z[:, h * RET_QK + half:(h + 1) * RET_QK]
        outs.append((x1 * cos - x2 * sin) * scale)
        outs.append((x1 * sin + x2 * cos) * scale)
    return jnp.concatenate(outs, axis=-1)


def _inproj_kernel(x_ref, mod_ref, g_ref, w_ref, gq_ref, gkv_ref, t64_ref, t256_ref,
                   qlat_ref, kvlat_ref, kvlat16_ref, kpe_ref, kpe16_ref, rq_ref, rk_ref, rv_ref, rg_ref):
    m = mod_ref[0]
    h = (_rms(x_ref[...]) * g_ref[...] * (1.0 + m[1:2]) + m[0:1]).astype(bf16)

    def proj(c0, c1):
        return jnp.dot(h, w_ref[:, c0:c1], preferred_element_type=f32)

    qlat_ref[...] = (_rms(proj(_C_Q, _C_KV)) * gq_ref[...]).astype(bf16)
    kv = _rms(proj(_C_KV, _C_PE)) * gkv_ref[...]
    kvlat_ref[...] = kv
    kvlat16_ref[...] = kv.astype(bf16)
    pe = _rope_pair(proj(_C_PE, _C_RQ) * t64_ref[...])
    kpe_ref[...] = pe[:, :MLA_ROPE]
    lane = lax.broadcasted_iota(jnp.int32, pe.shape, 1)
    kpe16_ref[...] = jnp.where(lane < MLA_ROPE, pe, 0.0).astype(bf16)
    cos = t256_ref[:, :RET_QK // 2]
    sin = t256_ref[:, RET_QK // 2:]
    rq_ref[...] = _rope_heads(proj(_C_RQ, _C_RK), cos, sin, 1.0).astype(bf16)
    rk_ref[...] = _rope_heads(proj(_C_RK, _C_RV), cos, sin, RET_QK ** -0.5).astype(bf16)
    rv_ref[...] = proj(_C_RV, _C_RG).astype(bf16)
    rg_ref[...] = proj(_C_RG, _C_END).astype(bf16)


def _inproj(x2d, mod, g_pre, w_ext, g_q, g_kv, t64, t256, seq, tm):
    n, d = x2d.shape
    per_seq = seq // tm
    row = lambda w: pl.BlockSpec((tm, w), lambda i: (i, 0))
    tab = lambda w: pl.BlockSpec((tm, w), lambda i: (i % per_seq, 0))
    outs = [(Q_LORA, bf16), (KV_LORA, f32), (KV_LORA, bf16), (MLA_ROPE, f32), (2 * MLA_ROPE, bf16),
            (RET_HEADS * RET_QK, bf16), (RET_HEADS * RET_QK, bf16), (RET_WIDTH, bf16), (RET_WIDTH, bf16)]
    vmem = w_ext.size * 2 + 2 * tm * d * 4 + 4 * tm * d * 4 + sum(2 * tm * w * 4 for w, _ in outs) + (6 << 20)
    return pl.pallas_call(
        _inproj_kernel,
        grid=(n // tm,),
        in_specs=[row(d),
                  pl.BlockSpec((1, 6, d), lambda i: (i // per_seq, 0, 0)),
                  _const_spec((1, d)), _const_spec(w_ext.shape), _const_spec((1, Q_LORA)), _const_spec((1, KV_LORA)),
                  tab(2 * MLA_ROPE), tab(RET_QK)],
        out_specs=[row(w) for w, _ in outs],
        out_shape=[jax.ShapeDtypeStruct((n, w), dt) for w, dt in outs],
        compiler_params=_params(("arbitrary",), vmem),
        name="in_projection",
    )(x2d, mod, g_pre.reshape(1, d), w_ext, g_q.reshape(1, Q_LORA), g_kv.reshape(1, KV_LORA), t64, t256)


def _mla_kernel(qlat_ref, lat_ref, kpe_ref, wq_ref, wkv_ref, tab_ref, o_ref, q_s, k_s, v_s, *, q_tiles):
    scale = (MLA_NOPE + MLA_ROPE) ** -0.5
    qh = jnp.dot(qlat_ref[0], wq_ref[0], preferred_element_type=f32)
    q_s[:, :MLA_NOPE] = (qh[:, :MLA_NOPE] * scale).astype(bf16)
    q_s[:, MLA_NOPE:] = (_rope_pair(qh[:, MLA_NOPE:] * tab_ref[...]) * scale).astype(bf16)
    kv = jnp.dot(lat_ref[0], wkv_ref[0], preferred_element_type=f32)
    k_s[:, :MLA_NOPE] = kv[:, :MLA_NOPE].astype(bf16)
    k_s[:, MLA_NOPE:] = kpe_ref[0]
    v_s[...] = kv[:, MLA_NOPE:].astype(bf16)
    for q0, ql, kvl, masked in q_tiles:
        q = q_s[q0:q0 + ql, :]
        s = lax.dot_general(q, k_s[0:kvl, :], (((1,), (1,)), ((), ())), preferred_element_type=f32)
        if masked:
            assert kvl == q0 + ql and q0 % CHUNK == 0
            qc = lax.broadcasted_iota(jnp.int32, (ql, 1), 0) // CHUNK
            kc = lax.broadcasted_iota(jnp.int32, (1, ql), 1) // CHUNK
            diag = jnp.where(kc <= qc, s[:, q0:], NEG_BIG)
            s = diag if q0 == 0 else jnp.concatenate([s[:, :q0], diag], axis=1)
        p = jnp.exp(s - jnp.max(s, axis=-1, keepdims=True))
        l = jnp.sum(p, axis=-1, keepdims=True)
        o = jnp.dot(p.astype(bf16), v_s[0:kvl, :], preferred_element_type=f32)
        o_ref[0, q0:q0 + ql, :] = (o / l).astype(bf16)


def _mla(qlat, lat, kpe16, wq, wkv, tab, q_tiles):
    b, sq, _ = qlat.shape
    skv = lat.shape[1]
    kern = functools.partial(_mla_kernel, q_tiles=q_tiles)
    max_ql = max(t[1] for t in q_tiles)
    vmem = 2 * 2 * (sq * Q_LORA + skv * KV_LORA + skv * LANES) + sq * LANES * 4 + sq * 3 * LANES * 4 \
        + (sq + skv) * 3 * LANES * 2 + 4 * max_ql * skv * 4 + (8 << 20)
    return pl.pallas_call(
        kern,
        grid=(b, MLA_HEADS),
        in_specs=[pl.BlockSpec((1, sq, Q_LORA), lambda i, h: (i, 0, 0)),
                  pl.BlockSpec((1, skv, KV_LORA), lambda i, h: (i, 0, 0)),
                  pl.BlockSpec((1, skv, 2 * MLA_ROPE), lambda i, h: (i, 0, 0)),
                  pl.BlockSpec((1, Q_LORA, MLA_NOPE + 2 * MLA_ROPE), lambda i, h: (h, 0, 0)),
                  pl.BlockSpec((1, KV_LORA, MLA_NOPE + MLA_V), lambda i, h: (h, 0, 0)),
                  pl.BlockSpec((sq, 2 * MLA_ROPE), lambda i, h: (0, 0))],
        out_specs=pl.BlockSpec((1, sq, MLA_V), lambda i, h: (i, 0, h)),
        out_shape=jax.ShapeDtypeStruct((b, sq, MLA_WIDTH), bf16),
        scratch_shapes=[pltpu.VMEM((sq, MLA_NOPE + 2 * MLA_ROPE), bf16),
                        pltpu.VMEM((skv, MLA_NOPE + 2 * MLA_ROPE), bf16),
                        pltpu.VMEM((skv, MLA_V), bf16)],
        compiler_params=_params(("arbitrary", "arbitrary"), vmem),
        name="mla_attention",
    )(qlat, lat, kpe16, wq, wkv, tab)


def _ret_kernel(lg_ref, rq_ref, rk_ref, rv_ref, rg_ref, gret_ref, s0_ref, o_ref, sout_ref, state, *, chunk, n_chunks):
    c = chunk
    lg = lg_ref[0, 0:1, 0:1]
    ii = lax.broadcasted_iota(jnp.int32, (c, c), 0)
    jj = lax.broadcasted_iota(jnp.int32, (c, c), 1)
    diff = (ii - jj).astype(f32)
    dmask = jnp.where(diff >= 0.0, jnp.exp(jnp.maximum(diff, 0.0) * lg), 0.0)
    ic = lax.broadcasted_iota(jnp.int32, (c, 1), 0).astype(f32)
    q_decay = jnp.exp((ic + 1.0) * lg)
    k_decay = jnp.exp((c - 1.0 - ic) * lg)
    chunk_decay = jnp.exp(float(c) * lg)
    gret = gret_ref[...]
    state[...] = s0_ref[0, 0]

    def step(n, carry):
        sl = pl.ds(pl.multiple_of(n * c, c), c)
        q = rq_ref[0, sl, :]
        k = rk_ref[0, sl, :]
        v = rv_ref[0, sl, :]
        st = state[...]
        attn = lax.dot_general(q, k, (((1,), (1,)), ((), ())), preferred_element_type=f32) * dmask
        inner = jnp.dot(attn.astype(bf16), v, preferred_element_type=f32)
        cross = jnp.dot(q, st.astype(bf16), preferred_element_type=f32) * q_decay
        o = inner + cross
        kd_t = (k.astype(f32) * k_decay).T.astype(bf16)
        state[...] = st * chunk_decay + jnp.dot(kd_t, v, preferred_element_type=f32)
        mu = jnp.mean(o, axis=-1, keepdims=True)
        oc = o - mu
        rn = oc * lax.rsqrt(jnp.mean(oc * oc, axis=-1, keepdims=True) + EPS) * gret
        o_ref[0, sl, :] = (rn * _silu(rg_ref[0, sl, :].astype(f32))).astype(bf16)
        return carry

    lax.fori_loop(0, n_chunks, step, 0, unroll=min(n_chunks, RET_UNROLL))
    sout_ref[0, 0] = state[...]


def _retention(rq, rk, rv, rg, g_ret, state0, log_g, chunk):
    b, s, _ = rq.shape
    kern = functools.partial(_ret_kernel, chunk=chunk, n_chunks=s // chunk)
    head = lambda w: pl.BlockSpec((1, s, w), lambda i, h: (i, 0, h))
    st = pl.BlockSpec((1, 1, RET_QK, RET_V), lambda i, h: (i, h, 0, 0))
    vmem = 2 * 5 * s * RET_QK * 2 + 5 * RET_QK * RET_V * 4 + (8 << 20)
    return pl.pallas_call(
        kern,
        grid=(b, RET_HEADS),
        in_specs=[pl.BlockSpec((1, 8, LANES), lambda i, h: (h, 0, 0)),
                  head(RET_QK), head(RET_QK), head(RET_V), head(RET_V),
                  pl.BlockSpec((1, RET_V), lambda i, h: (0, h)), st],
        out_specs=[head(RET_V), st],
        out_shape=[jax.ShapeDtypeStruct((b, s, RET_WIDTH), bf16),
                   jax.ShapeDtypeStruct((b, RET_HEADS, RET_QK, RET_V), f32)],
        scratch_shapes=[pltpu.VMEM((RET_QK, RET_V), f32)],
        compiler_params=_params(("arbitrary", "arbitrary"), vmem),
        name="retention",
    )(log_g, rq, rk, rv, rg, g_ret.reshape(1, RET_WIDTH), state0)


def _pack_pairs(lo16, hi16):
    lo = lax.bitcast_convert_type(lo16.astype(f32), jnp.uint32)
    hi = lax.bitcast_convert_type(hi16.astype(f32), jnp.uint32)
    return (hi & jnp.uint32(0xFFFF0000)) | (lo >> 16)


def _unpack_pairs(u):
    lo = lax.bitcast_convert_type(u << 16, f32)
    hi = lax.bitcast_convert_type(u & jnp.uint32(0xFFFF0000), f32)
    return lo, hi


def _mixout_kernel(attn_ref, rn_ref, x_ref, mod_ref, gpost_ref, gpre_ref, wout_ref, wr_ref, br_ref, cnt0_ref,
                   xmid_ref, h2p_ref, idx_ref, gate_ref, rank_ref, cnt_ref, carry):
    @pl.when(pl.program_id(0) == 0)
    def _():
        carry[...] = cnt0_ref[...]

    mix = (jnp.dot(attn_ref[...], wout_ref[0:MLA_WIDTH, :], preferred_element_type=f32)
           + jnp.dot(rn_ref[...], wout_ref[MLA_WIDTH:, :], preferred_element_type=f32))
    m = mod_ref[0]
    x1 = x_ref[...] + m[2:3] * (_rms(mix) * gpost_ref[...])
    xmid_ref[...] = x1
    h2 = _rms(x1) * gpre_ref[...] * (1.0 + m[4:5]) + m[3:4]
    h_hi = h2.astype(bf16)
    half = h2.shape[1] // 2
    h2p_ref[...] = _pack_pairs(h_hi[:, :half], h_hi[:, half:])
    h_lo = (h2 - h_hi.astype(f32)).astype(bf16)
    hi_terms = jnp.dot(h_hi, wr_ref[...], preferred_element_type=f32)
    logits = (hi_terms[:, :LANES] + jnp.dot(h_lo, wr_ref[:, :LANES], preferred_element_type=f32)
              + hi_terms[:, LANES:]) + br_ref[...]
    tm = logits.shape[0]
    col = lax.broadcasted_iota(jnp.int32, logits.shape, 1).astype(f32)
    vals, idxs = [], []
    for _ in range(TOP_K):
        mx = jnp.max(logits, axis=-1, keepdims=True)
        ix = jnp.min(jnp.where(logits == mx, col, float(LANES)), axis=-1, keepdims=True)
        vals.append(mx)
        idxs.append(ix)
        logits = jnp.where(col == ix, -jnp.inf, logits)
    es = [jnp.exp(v - vals[0]) for v in vals]
    tot = es[0] + es[1] + es[2] + es[3]
    onehot = jnp.zeros_like(col)
    for k in range(TOP_K):
        onehot = jnp.where(col == idxs[k], 1.0, onehot)
    earlier = (lax.broadcasted_iota(jnp.int32, (tm, tm), 0) > lax.broadcasted_iota(jnp.int32, (tm, tm), 1))
    before = jnp.dot(jnp.where(earlier, 1.0, 0.0).astype(bf16), onehot.astype(bf16),
                     preferred_element_type=f32) + carry[...]
    idx_out = jnp.zeros_like(col)
    gate_out = jnp.zeros_like(col)
    rank_out = jnp.zeros_like(col)
    for k in range(TOP_K):
        slot = col == float(k)
        idx_out = jnp.where(slot, idxs[k], idx_out)
        gate_out = jnp.where(slot, es[k] / tot, gate_out)
        rank_k = jnp.sum(jnp.where(col == idxs[k], before, 0.0), axis=-1, keepdims=True)
        rank_out = jnp.where(slot, rank_k, rank_out)
    idx_ref[...] = idx_out.astype(jnp.int32)
    gate_ref[...] = gate_out
    rank_ref[...] = rank_out.astype(jnp.int32)
    carry[...] = carry[...] + jnp.sum(onehot, axis=0, keepdims=True)
    cnt_ref[...] = carry[...]


def _mixout(attn, rn, x2d, mod, g_post, g_pre, w_out16, wr, br, counts0, seq, tm):
    n, d = x2d.shape
    per_seq = seq // tm
    row = lambda w: pl.BlockSpec((tm, w), lambda i: (i, 0))
    vmem = w_out16.size * 2 + 2 * tm * d * (4 + 4 + 2 + 2) + 6 * tm * d * 4 + (8 << 20)
    return pl.pallas_call(
        _mixout_kernel,
        grid=(n // tm,),
        in_specs=[row(MLA_WIDTH), row(RET_WIDTH), row(d),
                  pl.BlockSpec((1, 6, d), lambda i: (i // per_seq, 0, 0)),
                  _const_spec((1, d)), _const_spec((1, d)), _const_spec(w_out16.shape),
                  _const_spec(wr.shape), _const_spec((1, LANES)), _const_spec((1, LANES))],
        out_specs=[row(d), row(d // 2), row(LANES), row(LANES), row(LANES),
                   pl.BlockSpec((1, LANES), lambda i: (0, 0))],
        out_shape=[jax.ShapeDtypeStruct((n, d), f32), jax.ShapeDtypeStruct((n, d // 2), jnp.uint32),
                   jax.ShapeDtypeStruct((n, LANES), jnp.int32), jax.ShapeDtypeStruct((n, LANES), f32),
                   jax.ShapeDtypeStruct((n, LANES), jnp.int32), jax.ShapeDtypeStruct((1, LANES), f32)],
        scratch_shapes=[pltpu.VMEM((1, LANES), f32)],
        compiler_params=_params(("arbitrary",), vmem),
        name="mixer_out_router",
    )(attn, rn, x2d, mod, g_post.reshape(1, d), g_pre.reshape(1, d), w_out16, wr, br, counts0)


def _route(top_idx, rank, counts):
    counts = counts[0, :N_EXPERTS].astype(jnp.int32)
    padded = (counts + MOE_ROWS - 1) // MOE_ROWS * MOE_ROWS
    pad_end = jnp.cumsum(padded).astype(jnp.int32)
    pad_start = pad_end - padded
    dest = pad_start[top_idx] + rank
    return dest, pad_start, pad_end, padded // MOE_ROWS


def _n_sorted_rows(n_tokens):
    return (n_tokens * TOP_K + N_EXPERTS * (MOE_ROWS - 1)) // MOE_ROWS * MOE_ROWS


def _fill_unused_chunks(zero_buf, dst_rows, sem, used_rows, total_rows):
    first = used_rows // MOE_ROWS

    def copy(c):
        return pltpu.make_async_copy(zero_buf, dst_rows(pl.ds(pl.multiple_of(c * MOE_ROWS, MOE_ROWS), MOE_ROWS)), sem)

    def start(c, carry):
        copy(c).start()
        return carry

    def wait(c, carry):
        copy(c).wait()
        return carry

    lax.fori_loop(first, total_rows // MOE_ROWS, start, 0)
    lax.fori_loop(first, total_rows // MOE_ROWS, wait, 0)


def _dispatch_kernel(pe_ref, nc_ref, dest_ref, h2p_a_ref, h2p_b_ref, x_hbm, zbuf, sem, zsem, *, tokens, n_first):
    i = pl.program_id(0)

    @pl.when(i == 0)
    def _():
        zbuf[...] = jnp.zeros_like(zbuf)

        def tail(e):
            start = pl.multiple_of(pe_ref[e] - MOE_ROWS, MOE_ROWS)
            return pltpu.make_async_copy(zbuf, x_hbm.at[pl.ds(start, MOE_ROWS), :], zsem)

        def start_tail(e, c):
            @pl.when(nc_ref[e] > 0)
            def _():
                tail(e).start()
            return c

        def wait_tail(e, c):
            @pl.when(nc_ref[e] > 0)
            def _():
                tail(e).wait()
            return c

        lax.fori_loop(0, N_EXPERTS, start_tail, 0)
        lax.fori_loop(0, N_EXPERTS, wait_tail, 0)
        _fill_unused_chunks(zbuf, lambda rows: x_hbm.at[rows, :], zsem, pe_ref[N_EXPERTS - 1], x_hbm.shape[0])

    def scatter(h2p_ref):
        for t in range(tokens):
            for k in range(TOP_K):
                d = dest_ref[0, 0, t * TOP_K + k]
                pltpu.make_async_copy(h2p_ref.at[pl.ds(t, 1), :], x_hbm.at[pl.ds(d, 1), :], sem).start(priority=k % 2)
        for _ in range(TOP_K):
            pltpu.make_async_copy(h2p_ref, x_hbm.at[pl.ds(0, tokens), :], sem).wait()

    @pl.when(i < n_first)
    def _():
        scatter(h2p_a_ref)

    @pl.when(i >= n_first)
    def _():
        scatter(h2p_b_ref)


def _dispatch(pad_end, n_chunks, dest, h2p_a, h2p_b):
    w = h2p_a.shape[1]
    tokens = 256
    n_a, n_b = h2p_a.shape[0] // tokens, h2p_b.shape[0] // tokens
    assert n_a * tokens == h2p_a.shape[0] and n_b * tokens == h2p_b.shape[0]
    n_steps = n_a + n_b
    grid_spec = pltpu.PrefetchScalarGridSpec(
        num_scalar_prefetch=2,
        grid=(n_steps,),
        in_specs=[pl.BlockSpec((1, 1, tokens * TOP_K), lambda i, pe, nc: (i, 0, 0), memory_space=pltpu.SMEM),
                  pl.BlockSpec((tokens, w), lambda i, pe, nc: (jnp.minimum(i, n_a - 1), 0)),
                  pl.BlockSpec((tokens, w), lambda i, pe, nc: (jnp.maximum(i - n_a, 0), 0))],
        out_specs=pl.BlockSpec(memory_space=pl.ANY),
        scratch_shapes=[pltpu.VMEM((MOE_ROWS, w), jnp.uint32), pltpu.SemaphoreType.DMA(()),
                        pltpu.SemaphoreType.DMA(())],
    )
    return pl.pallas_call(
        functools.partial(_dispatch_kernel, tokens=tokens, n_first=n_a),
        grid_spec=grid_spec,
        out_shape=jax.ShapeDtypeStruct((_n_sorted_rows(n_steps * tokens), w), jnp.uint32),
        compiler_params=_params(("arbitrary",), 24 << 20),
        name="moe_dispatch",
    )(pad_end, n_chunks, dest.reshape(n_steps, 1, tokens * TOP_K), h2p_a, h2p_b)


def _chunk_rows(r0, c):
    return pl.ds(pl.multiple_of(r0 + c * MOE_ROWS, MOE_ROWS), MOE_ROWS)


CHUNK_READ_PRIORITY = 1
READ_SLOTS = 4


def _start_first_reads(copies, n):
    for a in range(READ_SLOTS - 1):
        @pl.when(a < n)
        def _():
            for cp in copies(a, a):
                cp.start(priority=CHUNK_READ_PRIORITY)


def _next_step_rows(rs_ref, nc_ref, e, j, n_j):
    last_tile = j == n_j - 1
    has_next = jnp.logical_or(e < N_EXPERTS - 1, jnp.logical_not(last_tile))
    e_next = jnp.minimum(jnp.where(last_tile, e + 1, e), N_EXPERTS - 1)
    return rs_ref[e_next], jnp.where(has_next, nc_ref[e_next], 0)


def _read_ahead(copies, c, n):
    ahead = c + (READ_SLOTS - 1)

    @pl.when(ahead < n)
    def _():
        for cp in copies(ahead, ahead % READ_SLOTS):
            cp.start(priority=CHUNK_READ_PRIORITY)

    rslot = c % READ_SLOTS
    for cp in copies(c, rslot):
        cp.wait()
    return rslot


def _moe_up_kernel(rs_ref, nc_ref, x_hbm, wg_ref, wu_ref, bg_ref, bu_ref, act_hbm,
                   wg16, wu16, xbuf, obuf, sem_in, sem_out):
    e = pl.program_id(0)
    j = pl.program_id(1)
    n = nc_ref[e]
    r0 = rs_ref[e]

    def in_copy(c, slot, base=r0):
        return [pltpu.make_async_copy(x_hbm.at[_chunk_rows(base, c), :], xbuf.at[slot], sem_in.at[slot])]

    def out_copy(c, slot):
        return pltpu.make_async_copy(obuf.at[slot], act_hbm.at[j, _chunk_rows(r0, c), :], sem_out.at[slot])

    @pl.when(n > 0)
    def _():
        @pl.when(jnp.logical_and(e == 0, j == 0))
        def _():
            _start_first_reads(in_copy, n)

        wg16[...] = wg_ref[0].astype(bf16)
        wu16[...] = wu_ref[0].astype(bf16)
        half = wg16.shape[0] // 2

        def body(c, carry):
            slot = c % 2
            rslot = _read_ahead(in_copy, c, n)
            lo, hi = _unpack_pairs(xbuf[rslot])
            x_lo = lo.astype(bf16)
            x_hi = hi.astype(bf16)

            def proj(w16, b_ref):
                return (jnp.dot(x_lo, w16[0:half, :], preferred_element_type=f32)
                        + jnp.dot(x_hi, w16[half:, :], preferred_element_type=f32) + b_ref[0])

            gate = jnp.minimum(proj(wg16, bg_ref), SWIGLU_LIMIT)
            up = jnp.clip(proj(wu16, bu_ref), -SWIGLU_LIMIT, SWIGLU_LIMIT)
            act = (up + 1.0) * gate * jax.nn.sigmoid(SWIGLU_ALPHA * gate)

            @pl.when(c >= 2)
            def _():
                out_copy(c - 2, slot).wait()

            obuf[slot] = act.astype(bf16)
            out_copy(c, slot).start()
            return carry

        lax.fori_loop(0, n, body, 0)

        @pl.when(n >= 2)
        def _():
            out_copy(n - 2, n % 2).wait()

        out_copy(n - 1, (n - 1) % 2).wait()

    r0_next, n_next = _next_step_rows(rs_ref, nc_ref, e, j, act_hbm.shape[0])
    _start_first_reads(lambda c, slot: in_copy(c, slot, r0_next), n_next)

    @pl.when(e == N_EXPERTS - 1)
    def _():
        obuf[0] = jnp.zeros(obuf.shape[1:], obuf.dtype)
        _fill_unused_chunks(obuf.at[0], lambda rows: act_hbm.at[j, rows, :], sem_out.at[0],
                            r0 + n * MOE_ROWS, act_hbm.shape[1])


def _moe_up(row_start, n_chunks, x_sorted, w_gate_up, b_gate_up):
    e, d, f2 = w_gate_up.shape
    tf = UP_TILE
    n_j = D_FF // tf
    rows = x_sorted.shape[0]
    vmem = 2 * 2 * d * tf * 4 + 2 * d * tf * 2 + 2 * MOE_ROWS * d * 2 + 2 * MOE_ROWS * tf * 2 \
        + 6 * MOE_ROWS * tf * 4 + (6 << 20)
    grid_spec = pltpu.PrefetchScalarGridSpec(
        num_scalar_prefetch=2,
        grid=(e, n_j),
        in_specs=[pl.BlockSpec(memory_space=pl.ANY),
                  pl.BlockSpec((1, d, tf), lambda i, j, rs, nc: (i, 0, j)),
                  pl.BlockSpec((1, d, tf), lambda i, j, rs, nc: (i, 0, n_j + j)),
                  pl.BlockSpec((1, 1, tf), lambda i, j, rs, nc: (i, 0, j)),
                  pl.BlockSpec((1, 1, tf), lambda i, j, rs, nc: (i, 0, n_j + j))],
        out_specs=pl.BlockSpec(memory_space=pl.ANY),
        scratch_shapes=[pltpu.VMEM((d, tf), bf16), pltpu.VMEM((d, tf), bf16),
                        pltpu.VMEM((READ_SLOTS, MOE_ROWS, d // 2), jnp.uint32), pltpu.VMEM((2, MOE_ROWS, tf), bf16),
                        pltpu.SemaphoreType.DMA((READ_SLOTS,)), pltpu.SemaphoreType.DMA((2,))],
    )
    b3 = b_gate_up.reshape(e, 1, f2)
    return pl.pallas_call(
        _moe_up_kernel,
        grid_spec=grid_spec,
        out_shape=jax.ShapeDtypeStruct((n_j, rows, tf), bf16),
        compiler_params=_params(("arbitrary", "arbitrary"), vmem),
        name="moe_gate_up",
    )(row_start, n_chunks, x_sorted, w_gate_up, w_gate_up, b3, b3)


def _moe_down_kernel(rs_ref, nc_ref, act_hbm, wd_ref, bd_ref, y_hbm, wd16, abuf, ybuf, sem_in, sem_out):
    e = pl.program_id(0)
    nt = pl.program_id(1)
    n = nc_ref[e]
    r0 = rs_ref[e]
    n_k, tf = abuf.shape[1], abuf.shape[3]
    tw = ybuf.shape[2]
    n_t = y_hbm.shape[1] // tw

    def on_tile(fn):
        for t in range(n_t):
            @pl.when(nt == t)
            def _():
                fn(t)

    def in_copies(c, slot, base=r0):
        return [pltpu.make_async_copy(act_hbm.at[k, _chunk_rows(base, c), :], abuf.at[slot, k], sem_in.at[slot])
                for k in range(n_k)]

    def out_copy(c, slot, t):
        return pltpu.make_async_copy(ybuf.at[slot], y_hbm.at[_chunk_rows(r0, c), t * tw:(t + 1) * tw],
                                     sem_out.at[slot])

    @pl.when(n > 0)
    def _():
        @pl.when(jnp.logical_and(e == 0, nt == 0))
        def _():
            _start_first_reads(in_copies, n)

        wd16[...] = wd_ref[0].astype(bf16)

        def body(c, carry):
            slot = c % 2
            rslot = _read_ahead(in_copies, c, n)
            y = bd_ref[0]
            for k in range(n_k):
                y = y + jnp.dot(abuf[rslot, k], wd16[k * tf:(k + 1) * tf, :], preferred_element_type=f32)
            y16 = y.astype(bf16)

            @pl.when(c >= 2)
            def _():
                on_tile(lambda t: out_copy(c - 2, slot, t).wait())

            ybuf[slot] = _pack_pairs(y16[:, :tw], y16[:, tw:])
            on_tile(lambda t: out_copy(c, slot, t).start())
            return carry

        lax.fori_loop(0, n, body, 0)

        @pl.when(n >= 2)
        def _():
            on_tile(lambda t: out_copy(n - 2, n % 2, t).wait())

        on_tile(lambda t: out_copy(n - 1, (n - 1) % 2, t).wait())

    r0_next, n_next = _next_step_rows(rs_ref, nc_ref, e, nt, n_t)
    _start_first_reads(lambda c, slot: in_copies(c, slot, r0_next), n_next)

    @pl.when(e == N_EXPERTS - 1)
    def _():
        ybuf[0] = jnp.zeros(ybuf.shape[1:], ybuf.dtype)
        on_tile(lambda t: _fill_unused_chunks(ybuf.at[0], lambda rows: y_hbm.at[rows, t * tw:(t + 1) * tw],
                                              sem_out.at[0], r0 + n * MOE_ROWS, y_hbm.shape[0]))


def _moe_down(row_start, n_chunks, act, w_down, b_down):
    e, f, d = w_down.shape
    n_k, rows, tf = act.shape
    tn = DOWN_TILE
    vmem = 2 * f * tn * 4 + f * tn * 2 + 2 * MOE_ROWS * f * 2 + 2 * MOE_ROWS * tn * 2 + 4 * MOE_ROWS * tn * 4 + (6 << 20)
    grid_spec = pltpu.PrefetchScalarGridSpec(
        num_scalar_prefetch=2,
        grid=(e, d // tn),
        in_specs=[pl.BlockSpec(memory_space=pl.ANY),
                  pl.BlockSpec((1, f, tn), lambda i, t, rs, nc: (i, 0, t)),
                  pl.BlockSpec((1, 1, tn), lambda i, t, rs, nc: (i, 0, t))],
        out_specs=pl.BlockSpec(memory_space=pl.ANY),
        scratch_shapes=[pltpu.VMEM((f, tn), bf16), pltpu.VMEM((READ_SLOTS, n_k, MOE_ROWS, tf), bf16),
                        pltpu.VMEM((2, MOE_ROWS, tn // 2), jnp.uint32),
                        pltpu.SemaphoreType.DMA((READ_SLOTS,)), pltpu.SemaphoreType.DMA((2,))],
    )
    return pl.pallas_call(
        _moe_down_kernel,
        grid_spec=grid_spec,
        out_shape=jax.ShapeDtypeStruct((rows, d // 2), jnp.uint32),
        compiler_params=_params(("arbitrary", "arbitrary"), vmem),
        name="moe_down",
    )(row_start, n_chunks, act, w_down, b_down.reshape(e, 1, d))


def _block_schedule(n_chunks, n_blocks):
    ends = jnp.cumsum(n_chunks)
    n_live = ends[-1]
    blk = jnp.arange(n_blocks, dtype=jnp.int32)
    block_e = jnp.sum(jnp.minimum(blk, n_live - 1)[:, None] >= ends[None, :], axis=1).astype(jnp.int32)
    is_first = jnp.logical_and(blk == (ends - n_chunks)[block_e], blk < n_live).astype(jnp.int32)
    ids = jnp.arange(N_EXPERTS, dtype=jnp.int32)
    cand = jnp.where(n_chunks > 0, ids, N_EXPERTS)
    from_here = lax.cummin(cand[::-1])[::-1]
    after = jnp.concatenate([from_here[1:], jnp.full((1,), N_EXPERTS, jnp.int32)])
    wraps = (after >= N_EXPERTS).astype(jnp.int32)
    next_e = jnp.where(after < N_EXPERTS, after, from_here[0]).astype(jnp.int32)
    return block_e, is_first, next_e, wraps, n_live.reshape(1).astype(jnp.int32)


def _on_static(value, n, fn):
    for v in range(n):
        @pl.when(value == v)
        def _():
            fn(v)


def _moe_up_kernel(be_ref, first_ref, next_ref, wrap_ref, nl_ref, x_ref, w_hbm, bg_ref, bu_ref, act_ref,
                   stage, w16, sem, *, n_j):
    j = pl.program_id(0)
    b = pl.program_id(1)
    e = be_ref[b]
    live = b < nl_ref[0]
    tf = w16.shape[2]

    def weight_copies(ee, jj):
        return [pltpu.make_async_copy(w_hbm.at[ee, :, g * D_FF + jj * tf:g * D_FF + (jj + 1) * tf], stage.at[g], sem)
                for g in range(2)]

    def start_weights(ee, jd):
        _on_static(jd, n_j, lambda jj: [cp.start() for cp in weight_copies(ee, jj)])

    @pl.when(jnp.logical_and(j == 0, b == 0))
    def _():
        start_weights(e, j)

    @pl.when(jnp.logical_and(live, first_ref[b] == 1))
    def _():
        _on_static(j, n_j, lambda jj: [cp.wait() for cp in weight_copies(e, jj)])
        w16[...] = stage[...].astype(bf16)
        j_next = j + wrap_ref[e]

        @pl.when(j_next < n_j)
        def _():
            start_weights(next_ref[e], j_next)

    @pl.when(live)
    def _():
        lo, hi = _unpack_pairs(x_ref[...])
        x_lo = lo.astype(bf16)
        x_hi = hi.astype(bf16)
        half = x_lo.shape[1]

        def proj(g, b_ref):
            return (jnp.dot(x_lo, w16[g, 0:half, :], preferred_element_type=f32)
                    + jnp.dot(x_hi, w16[g, half:, :], preferred_element_type=f32) + b_ref[0])

        gate = jnp.minimum(proj(0, bg_ref), SWIGLU_LIMIT)
        up = jnp.clip(proj(1, bu_ref), -SWIGLU_LIMIT, SWIGLU_LIMIT)
        act_ref[0] = ((up + 1.0) * gate * jax.nn.sigmoid(SWIGLU_ALPHA * gate)).astype(bf16)

    @pl.when(jnp.logical_not(live))
    def _():
        act_ref[...] = jnp.zeros_like(act_ref)


def _moe_up(sched, x_sorted, w_gate_up, b_gate_up):
    e, d, f2 = w_gate_up.shape
    tf = UP_TILE
    n_j = D_FF // tf
    rows = x_sorted.shape[0]
    n_blocks = rows // MOE_ROWS
    vmem = 2 * d * tf * 4 + 2 * d * tf * 2 + 2 * MOE_ROWS * d * 2 + 2 * MOE_ROWS * tf * 2 \
        + 8 * MOE_ROWS * tf * 4 + (6 << 20)
    grid_spec = pltpu.PrefetchScalarGridSpec(
        num_scalar_prefetch=5,
        grid=(n_j, n_blocks),
        in_specs=[pl.BlockSpec((MOE_ROWS, d // 2), lambda j, b, be, *_: (b, 0)),
                  pl.BlockSpec(memory_space=pl.ANY),
                  pl.BlockSpec((1, 1, tf), lambda j, b, be, *_: (be[b], 0, j)),
                  pl.BlockSpec((1, 1, tf), lambda j, b, be, *_: (be[b], 0, n_j + j))],
        out_specs=pl.BlockSpec((1, MOE_ROWS, tf), lambda j, b, be, *_: (j, b, 0)),
        scratch_shapes=[pltpu.VMEM((2, d, tf), f32), pltpu.VMEM((2, d, tf), bf16), pltpu.SemaphoreType.DMA(())],
    )
    b3 = b_gate_up.reshape(e, 1, f2)
    return pl.pallas_call(
        functools.partial(_moe_up_kernel, n_j=n_j),
        grid_spec=grid_spec,
        out_shape=jax.ShapeDtypeStruct((n_j, rows, tf), bf16),
        compiler_params=_params(("arbitrary", "arbitrary"), vmem),
        name="moe_gate_up",
    )(*sched, x_sorted, w_gate_up, b3, b3)


def _moe_down_kernel(be_ref, first_ref, next_ref, wrap_ref, nl_ref, act_ref, w_hbm, bd_ref, y_ref, stage, w16, sem):
    b = pl.program_id(0)
    e = be_ref[b]
    live = b < nl_ref[0]
    n_k, _, tf = act_ref.shape

    def weight_copy(ee):
        return pltpu.make_async_copy(w_hbm.at[ee], stage, sem)

    @pl.when(b == 0)
    def _():
        weight_copy(e).start()

    @pl.when(jnp.logical_and(live, first_ref[b] == 1))
    def _():
        weight_copy(e).wait()
        w16[...] = stage[...].astype(bf16)

        @pl.when(wrap_ref[e] == 0)
        def _():
            weight_copy(next_ref[e]).start()

    @pl.when(live)
    def _():
        y = bd_ref[0]
        for k in range(n_k):
            y = y + jnp.dot(act_ref[k], w16[k * tf:(k + 1) * tf, :], preferred_element_type=f32)
        y16 = y.astype(bf16)
        half = y16.shape[1] // 2
        y_ref[...] = _pack_pairs(y16[:, :half], y16[:, half:])

    @pl.when(jnp.logical_not(live))
    def _():
        y_ref[...] = jnp.zeros_like(y_ref)


def _moe_down(sched, act, w_down, b_down):
    e, f, d = w_down.shape
    n_k, rows, tf = act.shape
    n_blocks = rows // MOE_ROWS
    vmem = f * d * 4 + f * d * 2 + 2 * MOE_ROWS * f * 2 + 2 * MOE_ROWS * d * 2 + 4 * MOE_ROWS * d * 4 + (6 << 20)
    grid_spec = pltpu.PrefetchScalarGridSpec(
        num_scalar_prefetch=5,
        grid=(n_blocks,),
        in_specs=[pl.BlockSpec((n_k, MOE_ROWS, tf), lambda b, be, *_: (0, b, 0)),
                  pl.BlockSpec(memory_space=pl.ANY),
                  pl.BlockSpec((1, 1, d), lambda b, be, *_: (be[b], 0, 0))],
        out_specs=pl.BlockSpec((MOE_ROWS, d // 2), lambda b, be, *_: (b, 0)),
        scratch_shapes=[pltpu.VMEM((f, d), f32), pltpu.VMEM((f, d), bf16), pltpu.SemaphoreType.DMA(())],
    )
    return pl.pallas_call(
        _moe_down_kernel,
        grid_spec=grid_spec,
        out_shape=jax.ShapeDtypeStruct((rows, d // 2), jnp.uint32),
        compiler_params=_params(("arbitrary",), vmem),
        name="moe_down",
    )(*sched, act, w_down, b_down.reshape(e, 1, d))


def _gather_rows(idx_ref, src_hbm, dst, sem, n_rows):
    for r in range(n_rows):
        t = idx_ref[0, 0, r]
        pltpu.make_async_copy(src_hbm.at[pl.ds(t, 1), :], dst.at[pl.ds(r, 1), :], sem).start(priority=r % 2)


def _wait_rows(src_hbm, dst, sem, n_rows):
    pltpu.make_async_copy(src_hbm.at[pl.ds(0, n_rows), :], dst, sem).wait()


def _prefetched_gather(i, n_live, cur_idx_ref, next_idx_ref, src_hbm, buf, sem, n_rows):
    slot = i % 2

    @pl.when(jnp.logical_and(i == 0, n_live > 0))
    def _():
        _gather_rows(cur_idx_ref, src_hbm, buf.at[0], sem.at[0], n_rows)

    @pl.when(i + 1 < n_live)
    def _():
        _gather_rows(next_idx_ref, src_hbm, buf.at[1 - slot], sem.at[1 - slot], n_rows)

    @pl.when(i < n_live)
    def _():
        _wait_rows(src_hbm, buf.at[slot], sem.at[slot], n_rows)

    return slot


def _combine_kernel(pos_ref, posn_ref, y_hbm, gates_ref, xmid_ref, mod_ref, gpost_ref, out_ref, ybuf, sem, *, n_steps):
    i = pl.program_id(0)
    tm = COMBINE_TOKENS
    slot = _prefetched_gather(i, n_steps, pos_ref, posn_ref, y_hbm, ybuf, sem, TOP_K * tm)
    g = gates_ref[...]
    lo = hi = None
    for k in range(TOP_K):
        l_k, h_k = _unpack_pairs(ybuf[slot, k * tm:(k + 1) * tm, :])
        gk = g[:, k:k + 1]
        lo = gk * l_k if lo is None else lo + gk * l_k
        hi = gk * h_k if hi is None else hi + gk * h_k
    half = lo.shape[1]
    inv = lax.rsqrt((jnp.sum(lo * lo, axis=-1, keepdims=True) + jnp.sum(hi * hi, axis=-1, keepdims=True))
                    / (2 * half) + EPS)
    gate2 = mod_ref[0][5:6]
    gp = gpost_ref[...]
    out_ref[:, :half] = xmid_ref[:, :half] + gate2[:, :half] * (lo * inv * gp[:, :half])
    out_ref[:, half:] = xmid_ref[:, half:] + gate2[:, half:] * (hi * inv * gp[:, half:])


def _combine(pos, y_sorted, gates, xmid, mod, g_post, seq):
    n, d = xmid.shape
    tm = COMBINE_TOKENS
    n_tiles = n // tm
    per_seq = seq // tm
    vmem = 2 * TOP_K * tm * d * 2 + 4 * tm * d * 4 + 6 * tm * d * 4 + (8 << 20)
    return pl.pallas_call(
        functools.partial(_combine_kernel, n_steps=n_tiles),
        grid=(n_tiles,),
        in_specs=[pl.BlockSpec((1, 1, TOP_K * tm), lambda i: (i, 0, 0), memory_space=pltpu.SMEM),
                  pl.BlockSpec((1, 1, TOP_K * tm), lambda i: (jnp.minimum(i + 1, n_tiles - 1), 0, 0),
                               memory_space=pltpu.SMEM),
                  pl.BlockSpec(memory_space=pl.ANY),
                  pl.BlockSpec((tm, TOP_K), lambda i: (i, 0)),
                  pl.BlockSpec((tm, d), lambda i: (i, 0)),
                  pl.BlockSpec((1, 6, d), lambda i: (i // per_seq, 0, 0)),
                  _const_spec((1, d))],
        out_specs=pl.BlockSpec((tm, d), lambda i: (i, 0)),
        out_shape=jax.ShapeDtypeStruct((n, d), f32),
        scratch_shapes=[pltpu.VMEM((2, TOP_K * tm, d // 2), jnp.uint32), pltpu.SemaphoreType.DMA((2,))],
        compiler_params=_params(("arbitrary",), vmem),
        name="moe_combine",
    )(pos, pos, y_sorted, gates, xmid, mod, g_post.reshape(1, d))


def _rope_tables(pos):
    def cs(d):
        half = d // 2
        inv_freq = 1.0 / (ROPE_THETA ** (jnp.arange(half, dtype=f32) * (2.0 / d)))
        ang = pos.astype(f32)[:, None] * inv_freq[None, :]
        return jnp.cos(ang), jnp.sin(ang)
    c64, s64 = cs(MLA_ROPE)
    c256, s256 = cs(RET_QK)
    t64 = jnp.concatenate([c64, c64, s64, s64], axis=-1)
    t256 = jnp.concatenate([c256, s256], axis=-1)
    return t64, t256


def _rot_cols(w):
    half = w.shape[-1] // 2
    return jnp.concatenate([-w[..., half:], w[..., :half]], axis=-1)


def _prep_weights(w_in, w_uq, w_ukv, w_out, w_router, b_router):
    b = np.cumsum((Q_LORA, KV_LORA, MLA_ROPE))
    w_pe = w_in[:, b[1]:b[2]]
    w_ext = jnp.concatenate([w_in[:, :b[2]], _rot_cols(w_pe), w_in[:, b[2]:]], axis=-1).astype(bf16)
    wq = w_uq.reshape(Q_LORA, MLA_HEADS, MLA_NOPE + MLA_ROPE)
    wq = jnp.concatenate([wq, _rot_cols(wq[..., MLA_NOPE:])], axis=-1).transpose(1, 0, 2).astype(bf16)
    wkv = w_ukv.reshape(KV_LORA, MLA_HEADS, MLA_NOPE + MLA_V).transpose(1, 0, 2).astype(bf16)
    r_hi = w_router.astype(bf16)
    r_lo = (w_router - r_hi.astype(f32)).astype(bf16)
    pad = ((0, 0), (0, LANES - N_EXPERTS))
    wr = jnp.concatenate([jnp.pad(r_hi, pad), jnp.pad(r_lo, pad)], axis=1)
    br = jnp.pad(b_router, (0, LANES - N_EXPERTS), constant_values=NEG_BIG).reshape(1, LANES)
    return w_ext, wq, wkv, w_out.astype(bf16), wr, br


def _mixer_half(x, mod, pos, wts, p, past_lat16, past_kpe16, state0, counts0, ret_chunk, q_tiles, tm):
    w_ext, wq, wkv, w_out16, wr, br = wts
    b, s, d = x.shape
    x2d = x.reshape(b * s, d)
    t64, t256 = _rope_tables(pos)
    qlat, kvlat, kvlat16, kpe, kpe16, rq, rk, rv, rg = _inproj(
        x2d, mod, p['g_pre_mix'], w_ext, p['g_q_lat'], p['g_kv_lat'], t64, t256, s, tm)
    lat16 = kvlat16.reshape(b, s, KV_LORA)
    kpe16 = kpe16.reshape(b, s, 2 * MLA_ROPE)
    if past_lat16 is not None:
        lat16 = jnp.concatenate([past_lat16, lat16], axis=1)
        kpe16 = jnp.concatenate([past_kpe16, kpe16], axis=1)
    attn = _mla(qlat.reshape(b, s, Q_LORA), lat16, kpe16, wq, wkv, t64, q_tiles)
    log_g = jnp.log1p(-jnp.exp2(-5.0 - jnp.arange(RET_HEADS, dtype=f32)))
    log_g = jnp.broadcast_to(log_g[:, None, None], (RET_HEADS, 8, LANES))
    sh = lambda t, w: t.reshape(b, s, w)
    rn, state = _retention(sh(rq, RET_HEADS * RET_QK), sh(rk, RET_HEADS * RET_QK), sh(rv, RET_WIDTH),
                           sh(rg, RET_WIDTH), p['g_ret'], state0, log_g, ret_chunk)
    xmid, h2p, idx, gates, rank, counts = _mixout(
        attn.reshape(b * s, MLA_WIDTH), rn.reshape(b * s, RET_WIDTH), x2d, mod,
        p['g_post_mix'], p['g_pre_ffn'], w_out16, wr, br, counts0, s, tm)
    route = (idx[:, :TOP_K], gates[:, :TOP_K], rank[:, :TOP_K], counts)
    return kvlat.reshape(b, s, KV_LORA), kpe.reshape(b, s, MLA_ROPE), state, xmid, h2p, route


def _combine_order(dest):
    tm = COMBINE_TOKENS
    n_tiles = dest.shape[0] // tm
    return dest.reshape(n_tiles, tm, TOP_K).transpose(0, 2, 1).reshape(n_tiles, 1, TOP_K * tm)


def kernel(x_prompt, x_sample, cache_kv_latent, cache_k_rope, state_retention, c_prompt, c_sample, w_ada, b_ada, g_pre_mix, g_post_mix, g_pre_ffn, g_post_ffn, w_in, g_q_lat, g_kv_lat, w_uq, w_ukv, g_ret, w_out, w_router, b_router, w_gate_up, b_gate_up, w_down, b_down):
    depth = w_in.shape[0]
    assert depth == 1, "the staged problem has a single layer"
    bp, sp, d = x_prompt.shape
    bs, ss, _ = x_sample.shape
    past = cache_kv_latent.shape[2]
    l = 0
    p = dict(g_pre_mix=g_pre_mix[l], g_post_mix=g_post_mix[l], g_pre_ffn=g_pre_ffn[l], g_post_ffn=g_post_ffn[l],
             g_q_lat=g_q_lat[l], g_kv_lat=g_kv_lat[l], g_ret=g_ret[l])
    wts = _prep_weights(w_in[l], w_uq[l], w_ukv[l], w_out[l], w_router[l], b_router[l])

    mod = _ada(jnp.concatenate([c_prompt, c_sample], axis=0), w_ada[l], b_ada[l]).reshape(bp + bs, 6, d)
    mod_p, mod_s = mod[:bp], mod[bp:]

    tq = ATTN_Q_TILE
    tiles_p = tuple((q0, tq, q0 + tq, True) for q0 in range(0, sp, tq))
    zero_state = jnp.zeros((bp, RET_HEADS, RET_QK, RET_V), f32)
    lat_p, kpe_p, st_p, xmid_p, h2p_p, (idx_p, gates_p, rank_p, counts_p) = _mixer_half(
        x_prompt, mod_p, jnp.arange(sp), wts, p, None, None, zero_state, jnp.zeros((1, LANES), f32),
        RET_CHUNK_PROMPT, tiles_p, 512)

    past_lat16 = cache_kv_latent[l].astype(bf16)
    past_kpe16 = jnp.pad(cache_k_rope[l], ((0, 0), (0, 0), (0, MLA_ROPE))).astype(bf16)
    tiles_s = ((0, ss, past + ss, False),)
    lat_s, kpe_s, st_s, xmid_s, h2p_s, (idx_s, gates_s, rank_s, counts) = _mixer_half(
        x_sample, mod_s, past + jnp.arange(ss), wts, p, past_lat16, past_kpe16, state_retention[l], counts_p,
        ss, tiles_s, ss)

    n_p = bp * sp
    dest, row_start, pad_end, n_chunks = _route(
        jnp.concatenate([idx_p, idx_s], axis=0), jnp.concatenate([rank_p, rank_s], axis=0), counts)
    x_sorted = _dispatch(pad_end, n_chunks, dest, h2p_p, h2p_s)
    sched = _block_schedule(n_chunks, x_sorted.shape[0] // MOE_ROWS)
    act = _moe_up(sched, x_sorted, w_gate_up[l], b_gate_up[l])
    y_sorted = _moe_down(sched, act, w_down[l], b_down[l])
    y_p = _combine(_combine_order(dest[:n_p]), y_sorted, gates_p, xmid_p, mod_p, p['g_post_ffn'], sp)
    y_s = _combine(_combine_order(dest[n_p:]), y_sorted, gates_s, xmid_s, mod_s, p['g_post_ffn'], ss)

    return (y_p.reshape(bp, sp, d), y_s.reshape(bs, ss, d),
            lat_p[None], kpe_p[None], st_p[None].astype(state_retention.dtype),
            lat_s[None], kpe_s[None], st_s[None].astype(state_retention.dtype))
```

```python
import functools

import numpy as np
import jax
import jax.numpy as jnp
from jax import lax
from jax.experimental import pallas as pl
from jax.experimental.pallas import tpu as pltpu

CHUNK = 64
EPS = 1e-6
ROPE_THETA = 10000.0

MLA_HEADS = 8
MLA_NOPE = 128
MLA_ROPE = 64
MLA_V = 128
Q_LORA = 512
KV_LORA = 512
MLA_WIDTH = MLA_HEADS * MLA_V

RET_HEADS = 4
RET_QK = 256
RET_V = 256
RET_WIDTH = RET_HEADS * RET_V

N_EXPERTS = 32
TOP_K = 4
D_FF = 2048
SWIGLU_LIMIT = 7.0
SWIGLU_ALPHA = 1.702

LANES = 128
SUBLANES = 8
V7X_VMEM_BYTES = 64 * 1024 * 1024
VMEM_CAP_BYTES = V7X_VMEM_BYTES - 8 * 1024 * 1024

RET_CHUNK_PROMPT = 128
RET_UNROLL = 8
ATTN_Q_TILE = 256
MOE_ROWS = 256
UP_TILE = 1024
DISPATCH_TOKENS = 256
COMBINE_TOKENS = 64
NEG_BIG = -1e30

bf16 = jnp.bfloat16
f32 = jnp.float32


def _params(sem, vmem_bytes):
    return pltpu.CompilerParams(dimension_semantics=sem, vmem_limit_bytes=int(min(vmem_bytes, VMEM_CAP_BYTES)))


def _const_spec(shape):
    nd = len(shape)
    return pl.BlockSpec(shape, lambda *_: (0,) * nd, pipeline_mode=pl.Buffered(1))


def _rms(x):
    return x * lax.rsqrt(jnp.mean(x * x, axis=-1, keepdims=True) + EPS)


def _silu(x):
    return x * jax.nn.sigmoid(x)


def _ada_kernel(c_ref, w_ref, b_ref, o_ref):
    s = _silu(c_ref[...]).astype(bf16)
    o_ref[...] = jnp.dot(s, w_ref[...].astype(bf16), preferred_element_type=f32) + b_ref[...]


def _ada(c, w_ada, b_ada):
    nb, d = c.shape
    n = w_ada.shape[1]
    tn = 1536
    return pl.pallas_call(
        _ada_kernel,
        grid=(n // tn,),
        in_specs=[pl.BlockSpec((nb, d), lambda j: (0, 0)),
                  pl.BlockSpec((d, tn), lambda j: (0, j)),
                  pl.BlockSpec((1, tn), lambda j: (0, j))],
        out_specs=pl.BlockSpec((nb, tn), lambda j: (0, j)),
        out_shape=jax.ShapeDtypeStruct((nb, n), f32),
        compiler_params=_params(("arbitrary",), 2 * d * tn * 4 + d * tn * 2 + (8 << 20)),
        name="ada_modulation",
    )(c, w_ada, b_ada.reshape(1, n))


_C_RK = RET_HEADS * RET_QK
_C_RV = _C_RK + RET_HEADS * RET_QK
_C_RG = _C_RV + RET_WIDTH
_C_END = _C_RG + RET_WIDTH


def _rope_pair(t):
    return t + pltpu.roll(t, MLA_ROPE, axis=1)


def _rope_heads(z, cos, sin, scale):
    outs = []
    half = RET_QK // 2
    for h in range(RET_HEADS):
        x1 = z[:, h * RET_QK:h * RET_QK + half]
        x2 = z[:, h * RET_QK + half:(h + 1) * RET_QK]
        outs.append((x1 * cos - x2 * sin) * scale)
        outs.append((x1 * sin + x2 * cos) * scale)
    return jnp.concatenate(outs, axis=-1)


def _inproj_kernel(x_ref, mod_ref, g_ref, wlat_ref, wpe_ref, wret_ref, gq_ref, gkv_ref, t64_ref, t256_ref,
                   qlat_ref, kvlat_ref, kvlat16_ref, kpe_ref, kpe16_ref, rq_ref, rk_ref, rv_ref, rg_ref):
    m = mod_ref[0]
    h = (_rms(x_ref[...]) * g_ref[...] * (1.0 + m[1:2]) + m[0:1]).astype(bf16)

    def proj(w_ref, c0, c1):
        return jnp.dot(h, w_ref[:, c0:c1], preferred_element_type=f32)

    qlat_ref[...] = (_rms(proj(wlat_ref, 0, Q_LORA)) * gq_ref[...]).astype(bf16)
    kv = _rms(proj(wlat_ref, Q_LORA, Q_LORA + KV_LORA)) * gkv_ref[...]
    kvlat_ref[...] = kv
    kvlat16_ref[...] = kv.astype(bf16)
    pe = _rope_pair(proj(wpe_ref, 0, 2 * MLA_ROPE) * t64_ref[...])
    kpe_ref[...] = pe[:, :MLA_ROPE]
    lane = lax.broadcasted_iota(jnp.int32, pe.shape, 1)
    kpe16_ref[...] = jnp.where(lane < MLA_ROPE, pe, 0.0).astype(bf16)
    cos = t256_ref[:, :RET_QK // 2]
    sin = t256_ref[:, RET_QK // 2:]
    rq_ref[...] = _rope_heads(proj(wret_ref, 0, _C_RK), cos, sin, 1.0).astype(bf16)
    rk_ref[...] = _rope_heads(proj(wret_ref, _C_RK, _C_RV), cos, sin, RET_QK ** -0.5).astype(bf16)
    rv_ref[...] = proj(wret_ref, _C_RV, _C_RG).astype(bf16)
    rg_ref[...] = proj(wret_ref, _C_RG, _C_END).astype(bf16)


def _inproj(x2d, mod, g_pre, w_parts, g_q, g_kv, t64, t256, seq, tm):
    n, d = x2d.shape
    per_seq = seq // tm
    row = lambda w: pl.BlockSpec((tm, w), lambda i: (i, 0))
    tab = lambda w: pl.BlockSpec((tm, w), lambda i: (i % per_seq, 0))
    outs = [(Q_LORA, bf16), (KV_LORA, f32), (KV_LORA, bf16), (MLA_ROPE, f32), (2 * MLA_ROPE, bf16),
            (RET_HEADS * RET_QK, bf16), (RET_HEADS * RET_QK, bf16), (RET_WIDTH, bf16), (RET_WIDTH, bf16)]
    vmem = sum(w.size for w in w_parts) * 2 + 2 * tm * d * 4 + 4 * tm * d * 4 + sum(2 * tm * w * 4 for w, _ in outs) + (6 << 20)
    return pl.pallas_call(
        _inproj_kernel,
        grid=(n // tm,),
        in_specs=[row(d),
                  pl.BlockSpec((1, 6, d), lambda i: (i // per_seq, 0, 0)),
                  _const_spec((1, d)), *[_const_spec(w.shape) for w in w_parts],
                  _const_spec((1, Q_LORA)), _const_spec((1, KV_LORA)),
                  tab(2 * MLA_ROPE), tab(RET_QK)],
        out_specs=[row(w) for w, _ in outs],
        out_shape=[jax.ShapeDtypeStruct((n, w), dt) for w, dt in outs],
        compiler_params=_params(("arbitrary",), vmem),
        name="in_projection",
    )(x2d, mod, g_pre.reshape(1, d), *w_parts, g_q.reshape(1, Q_LORA), g_kv.reshape(1, KV_LORA), t64, t256)


def _mla_kernel(qlat_ref, lat_ref, kpe_ref, wq_ref, wkv_ref, tab_ref, o_ref, q_s, k_s, v_s, *, q_tiles):
    scale = (MLA_NOPE + MLA_ROPE) ** -0.5
    qh = jnp.dot(qlat_ref[0], wq_ref[0], preferred_element_type=f32)
    q_s[:, :MLA_NOPE] = (qh[:, :MLA_NOPE] * scale).astype(bf16)
    q_s[:, MLA_NOPE:] = (_rope_pair(qh[:, MLA_NOPE:] * tab_ref[...]) * scale).astype(bf16)
    kv = jnp.dot(lat_ref[0], wkv_ref[0], preferred_element_type=f32)
    k_s[:, :MLA_NOPE] = kv[:, :MLA_NOPE].astype(bf16)
    k_s[:, MLA_NOPE:] = kpe_ref[0]
    v_s[...] = kv[:, MLA_NOPE:].astype(bf16)
    for q0, ql, kvl, masked in q_tiles:
        q = q_s[q0:q0 + ql, :]
        s = lax.dot_general(q, k_s[0:kvl, :], (((1,), (1,)), ((), ())), preferred_element_type=f32)
        if masked:
            assert kvl == q0 + ql and q0 % CHUNK == 0
            qc = lax.broadcasted_iota(jnp.int32, (ql, 1), 0) // CHUNK
            kc = lax.broadcasted_iota(jnp.int32, (1, ql), 1) // CHUNK
            diag = jnp.where(kc <= qc, s[:, q0:], NEG_BIG)
            s = diag if q0 == 0 else jnp.concatenate([s[:, :q0], diag], axis=1)
        p = jnp.exp(s - jnp.max(s, axis=-1, keepdims=True))
        l = jnp.sum(p, axis=-1, keepdims=True)
        o = jnp.dot(p.astype(bf16), v_s[0:kvl, :], preferred_element_type=f32)
        o_ref[0, q0:q0 + ql, :] = (o / l).astype(bf16)


def _mla(qlat, lat, kpe16, wq, wkv, tab, q_tiles):
    b, sq, _ = qlat.shape
    skv = lat.shape[1]
    kern = functools.partial(_mla_kernel, q_tiles=q_tiles)
    max_ql = max(t[1] for t in q_tiles)
    vmem = 2 * 2 * (sq * Q_LORA + skv * KV_LORA + skv * LANES) + sq * LANES * 4 + sq * 3 * LANES * 4 \
        + (sq + skv) * 3 * LANES * 2 + 4 * max_ql * skv * 4 + (8 << 20)
    return pl.pallas_call(
        kern,
        grid=(b, MLA_HEADS),
        in_specs=[pl.BlockSpec((1, sq, Q_LORA), lambda i, h: (i, 0, 0)),
                  pl.BlockSpec((1, skv, KV_LORA), lambda i, h: (i, 0, 0)),
                  pl.BlockSpec((1, skv, 2 * MLA_ROPE), lambda i, h: (i, 0, 0)),
                  pl.BlockSpec((1, Q_LORA, MLA_NOPE + 2 * MLA_ROPE), lambda i, h: (h, 0, 0)),
                  pl.BlockSpec((1, KV_LORA, MLA_NOPE + MLA_V), lambda i, h: (h, 0, 0)),
                  pl.BlockSpec((sq, 2 * MLA_ROPE), lambda i, h: (0, 0))],
        out_specs=pl.BlockSpec((1, sq, MLA_V), lambda i, h: (i, 0, h)),
        out_shape=jax.ShapeDtypeStruct((b, sq, MLA_WIDTH), bf16),
        scratch_shapes=[pltpu.VMEM((sq, MLA_NOPE + 2 * MLA_ROPE), bf16),
                        pltpu.VMEM((skv, MLA_NOPE + 2 * MLA_ROPE), bf16),
                        pltpu.VMEM((skv, MLA_V), bf16)],
        compiler_params=_params(("arbitrary", "arbitrary"), vmem),
        name="mla_attention",
    )(qlat, lat, kpe16, wq, wkv, tab)


def _ret_kernel(lg_ref, rq_ref, rk_ref, rv_ref, rg_ref, gret_ref, s0_ref, o_ref, sout_ref, state, *, chunk, n_chunks):
    c = chunk
    lg = lg_ref[0, 0:1, 0:1]
    ii = lax.broadcasted_iota(jnp.int32, (c, c), 0)
    jj = lax.broadcasted_iota(jnp.int32, (c, c), 1)
    diff = (ii - jj).astype(f32)
    dmask = jnp.where(diff >= 0.0, jnp.exp(jnp.maximum(diff, 0.0) * lg), 0.0)
    ic = lax.broadcasted_iota(jnp.int32, (c, 1), 0).astype(f32)
    q_decay = jnp.exp((ic + 1.0) * lg)
    k_decay = jnp.exp((c - 1.0 - ic) * lg)
    chunk_decay = jnp.exp(float(c) * lg)
    gret = gret_ref[...]
    state[...] = s0_ref[0, 0]

    def step(n, carry):
        sl = pl.ds(pl.multiple_of(n * c, c), c)
        q = rq_ref[0, sl, :]
        k = rk_ref[0, sl, :]
        v = rv_ref[0, sl, :]
        st = state[...]
        attn = lax.dot_general(q, k, (((1,), (1,)), ((), ())), preferred_element_type=f32) * dmask
        inner = jnp.dot(attn.astype(bf16), v, preferred_element_type=f32)
        cross = jnp.dot(q, st.astype(bf16), preferred_element_type=f32) * q_decay
        o = inner + cross
        kd_t = (k.astype(f32) * k_decay).T.astype(bf16)
        state[...] = st * chunk_decay + jnp.dot(kd_t, v, preferred_element_type=f32)
        mu = jnp.mean(o, axis=-1, keepdims=True)
        oc = o - mu
        rn = oc * lax.rsqrt(jnp.mean(oc * oc, axis=-1, keepdims=True) + EPS) * gret
        o_ref[0, sl, :] = (rn * _silu(rg_ref[0, sl, :].astype(f32))).astype(bf16)
        return carry

    lax.fori_loop(0, n_chunks, step, 0, unroll=min(n_chunks, RET_UNROLL))
    sout_ref[0, 0] = state[...]


def _retention(rq, rk, rv, rg, g_ret, state0, log_g, chunk):
    b, s, _ = rq.shape
    kern = functools.partial(_ret_kernel, chunk=chunk, n_chunks=s // chunk)
    head = lambda w: pl.BlockSpec((1, s, w), lambda i, h: (i, 0, h))
    st = pl.BlockSpec((1, 1, RET_QK, RET_V), lambda i, h: (i, h, 0, 0))
    vmem = 2 * 5 * s * RET_QK * 2 + 5 * RET_QK * RET_V * 4 + (8 << 20)
    return pl.pallas_call(
        kern,
        grid=(b, RET_HEADS),
        in_specs=[pl.BlockSpec((1, 8, LANES), lambda i, h: (h, 0, 0)),
                  head(RET_QK), head(RET_QK), head(RET_V), head(RET_V),
                  pl.BlockSpec((1, RET_V), lambda i, h: (0, h)), st],
        out_specs=[head(RET_V), st],
        out_shape=[jax.ShapeDtypeStruct((b, s, RET_WIDTH), bf16),
                   jax.ShapeDtypeStruct((b, RET_HEADS, RET_QK, RET_V), f32)],
        scratch_shapes=[pltpu.VMEM((RET_QK, RET_V), f32)],
        compiler_params=_params(("arbitrary", "arbitrary"), vmem),
        name="retention",
    )(log_g, rq, rk, rv, rg, g_ret.reshape(1, RET_WIDTH), state0)


def _pack_pairs(lo16, hi16):
    lo = lax.bitcast_convert_type(lo16.astype(f32), jnp.uint32)
    hi = lax.bitcast_convert_type(hi16.astype(f32), jnp.uint32)
    return (hi & jnp.uint32(0xFFFF0000)) | (lo >> 16)


def _unpack_pairs(u):
    lo = lax.bitcast_convert_type(u << 16, f32)
    hi = lax.bitcast_convert_type(u & jnp.uint32(0xFFFF0000), f32)
    return lo, hi


def _mixout_kernel(attn_ref, rn_ref, x_ref, mod_ref, gpost_ref, gpre_ref, wout_ref, wr_ref, br_ref, cnt0_ref,
                   xmid_ref, h2p_ref, idx_ref, gate_ref, rank_ref, cnt_ref, carry):
    @pl.when(pl.program_id(0) == 0)
    def _():
        carry[...] = cnt0_ref[...]

    mix = (jnp.dot(attn_ref[...], wout_ref[0:MLA_WIDTH, :], preferred_element_type=f32)
           + jnp.dot(rn_ref[...], wout_ref[MLA_WIDTH:, :], preferred_element_type=f32))
    m = mod_ref[0]
    x1 = x_ref[...] + m[2:3] * (_rms(mix) * gpost_ref[...])
    xmid_ref[...] = x1
    h2 = _rms(x1) * gpre_ref[...] * (1.0 + m[4:5]) + m[3:4]
    h_hi = h2.astype(bf16)
    half = h2.shape[1] // 2
    h2p_ref[...] = _pack_pairs(h_hi[:, :half], h_hi[:, half:])
    h_lo = (h2 - h_hi.astype(f32)).astype(bf16)
    hi_terms = jnp.dot(h_hi, wr_ref[...], preferred_element_type=f32)
    logits = (hi_terms[:, :LANES] + jnp.dot(h_lo, wr_ref[:, :LANES], preferred_element_type=f32)
              + hi_terms[:, LANES:]) + br_ref[...]
    tm = logits.shape[0]
    col = lax.broadcasted_iota(jnp.int32, logits.shape, 1).astype(f32)
    vals, idxs = [], []
    for _ in range(TOP_K):
        mx = jnp.max(logits, axis=-1, keepdims=True)
        ix = jnp.min(jnp.where(logits == mx, col, float(LANES)), axis=-1, keepdims=True)
        vals.append(mx)
        idxs.append(ix)
        logits = jnp.where(col == ix, -jnp.inf, logits)
    es = [jnp.exp(v - vals[0]) for v in vals]
    tot = es[0] + es[1] + es[2] + es[3]
    onehot = jnp.zeros_like(col)
    for k in range(TOP_K):
        onehot = jnp.where(col == idxs[k], 1.0, onehot)
    earlier = (lax.broadcasted_iota(jnp.int32, (tm, tm), 0) > lax.broadcasted_iota(jnp.int32, (tm, tm), 1))
    before = jnp.dot(jnp.where(earlier, 1.0, 0.0).astype(bf16), onehot.astype(bf16),
                     preferred_element_type=f32) + carry[...]
    idx_out = jnp.zeros_like(col)
    gate_out = jnp.zeros_like(col)
    rank_out = jnp.zeros_like(col)
    for k in range(TOP_K):
        slot = col == float(k)
        idx_out = jnp.where(slot, idxs[k], idx_out)
        gate_out = jnp.where(slot, es[k] / tot, gate_out)
        rank_k = jnp.sum(jnp.where(col == idxs[k], before, 0.0), axis=-1, keepdims=True)
        rank_out = jnp.where(slot, rank_k, rank_out)
    idx_ref[...] = idx_out.astype(jnp.int32)
    gate_ref[...] = gate_out
    rank_ref[...] = rank_out.astype(jnp.int32)
    carry[...] = carry[...] + jnp.sum(onehot, axis=0, keepdims=True)
    cnt_ref[...] = carry[...]


def _mixout(attn, rn, x2d, mod, g_post, g_pre, w_out16, wr, br, counts0, seq, tm):
    n, d = x2d.shape
    per_seq = seq // tm
    row = lambda w: pl.BlockSpec((tm, w), lambda i: (i, 0))
    vmem = w_out16.size * 2 + 2 * tm * d * (4 + 4 + 2 + 2) + 6 * tm * d * 4 + (8 << 20)
    return pl.pallas_call(
        _mixout_kernel,
        grid=(n // tm,),
        in_specs=[row(MLA_WIDTH), row(RET_WIDTH), row(d),
                  pl.BlockSpec((1, 6, d), lambda i: (i // per_seq, 0, 0)),
                  _const_spec((1, d)), _const_spec((1, d)), _const_spec(w_out16.shape),
                  _const_spec(wr.shape), _const_spec((1, LANES)), _const_spec((1, LANES))],
        out_specs=[row(d), row(d // 2), row(LANES), row(LANES), row(LANES),
                   pl.BlockSpec((1, LANES), lambda i: (0, 0))],
        out_shape=[jax.ShapeDtypeStruct((n, d), f32), jax.ShapeDtypeStruct((n, d // 2), jnp.uint32),
                   jax.ShapeDtypeStruct((n, LANES), jnp.int32), jax.ShapeDtypeStruct((n, LANES), f32),
                   jax.ShapeDtypeStruct((n, LANES), jnp.int32), jax.ShapeDtypeStruct((1, LANES), f32)],
        scratch_shapes=[pltpu.VMEM((1, LANES), f32)],
        compiler_params=_params(("arbitrary",), vmem),
        name="mixer_out_router",
    )(attn, rn, x2d, mod, g_post.reshape(1, d), g_pre.reshape(1, d), w_out16, wr, br, counts0)


def _route(top_idx, rank, counts):
    counts = counts[0, :N_EXPERTS].astype(jnp.int32)
    padded = (counts + MOE_ROWS - 1) // MOE_ROWS * MOE_ROWS
    pad_end = jnp.cumsum(padded).astype(jnp.int32)
    pad_start = pad_end - padded
    dest = pad_start[top_idx] + rank
    return dest, pad_end, padded // MOE_ROWS


def _n_sorted_rows(n_tokens):
    return (n_tokens * TOP_K + N_EXPERTS * (MOE_ROWS - 1)) // MOE_ROWS * MOE_ROWS


def _block_schedule(n_chunks, n_blocks):
    ends = jnp.cumsum(n_chunks)
    n_live = ends[-1]
    blk = jnp.arange(n_blocks, dtype=jnp.int32)
    block_e = jnp.sum(jnp.minimum(blk, n_live - 1)[:, None] >= ends[None, :], axis=1).astype(jnp.int32)
    is_first = jnp.logical_and(blk == (ends - n_chunks)[block_e], blk < n_live).astype(jnp.int32)
    ids = jnp.arange(N_EXPERTS, dtype=jnp.int32)
    cand = jnp.where(n_chunks > 0, ids, N_EXPERTS)
    from_here = lax.cummin(cand[::-1])[::-1]
    after = jnp.concatenate([from_here[1:], jnp.full((1,), N_EXPERTS, jnp.int32)])
    wraps = (after >= N_EXPERTS).astype(jnp.int32)
    next_e = jnp.where(after < N_EXPERTS, after, from_here[0]).astype(jnp.int32)
    return block_e, is_first, next_e, wraps, n_live.reshape(1).astype(jnp.int32)


def _dispatch_kernel(pe_ref, nc_ref, dest_ref, h2p_a_ref, h2p_b_ref, x_hbm, zbuf, sem, zsem, *, tokens, n_first):
    i = pl.program_id(0)

    @pl.when(i == 0)
    def _():
        zbuf[...] = jnp.zeros_like(zbuf)

        def tail(e):
            start = pl.multiple_of(pe_ref[e] - MOE_ROWS, MOE_ROWS)
            return pltpu.make_async_copy(zbuf, x_hbm.at[pl.ds(start, MOE_ROWS)], zsem)

        def start_tail(e, c):
            @pl.when(nc_ref[e] > 0)
            def _():
                tail(e).start()
            return c

        def wait_tail(e, c):
            @pl.when(nc_ref[e] > 0)
            def _():
                tail(e).wait()
            return c

        lax.fori_loop(0, N_EXPERTS, start_tail, 0)
        lax.fori_loop(0, N_EXPERTS, wait_tail, 0)

        def unused(c):
            return pltpu.make_async_copy(zbuf, x_hbm.at[pl.ds(pl.multiple_of(c * MOE_ROWS, MOE_ROWS), MOE_ROWS)], zsem)

        def start_unused(c, carry):
            unused(c).start()
            return carry

        def wait_unused(c, carry):
            unused(c).wait()
            return carry

        first_unused = pe_ref[N_EXPERTS - 1] // MOE_ROWS
        lax.fori_loop(first_unused, x_hbm.shape[0] // MOE_ROWS, start_unused, 0)
        lax.fori_loop(first_unused, x_hbm.shape[0] // MOE_ROWS, wait_unused, 0)

    def scatter(h2p_ref):
        for t in range(tokens):
            for k in range(TOP_K):
                d = dest_ref[0, 0, t * TOP_K + k]
                pltpu.make_async_copy(h2p_ref.at[pl.ds(t, 1)], x_hbm.at[pl.ds(d, 1)], sem).start(priority=k % 2)
        for _ in range(TOP_K):
            pltpu.make_async_copy(h2p_ref, x_hbm.at[pl.ds(0, tokens)], sem).wait()

    @pl.when(i < n_first)
    def _():
        scatter(h2p_a_ref)

    @pl.when(i >= n_first)
    def _():
        scatter(h2p_b_ref)


def _dispatch(pad_end, n_chunks, dest, h2p_a, h2p_b):
    tokens = DISPATCH_TOKENS
    tile = h2p_a.shape[1:]
    n_a, n_b = h2p_a.shape[0] // tokens, h2p_b.shape[0] // tokens
    assert n_a * tokens == h2p_a.shape[0] and n_b * tokens == h2p_b.shape[0]
    n_steps = n_a + n_b
    grid_spec = pltpu.PrefetchScalarGridSpec(
        num_scalar_prefetch=2,
        grid=(n_steps,),
        in_specs=[pl.BlockSpec((1, 1, tokens * TOP_K), lambda i, pe, nc: (i, 0, 0), memory_space=pltpu.SMEM),
                  pl.BlockSpec((tokens,) + tile, lambda i, pe, nc: (jnp.minimum(i, n_a - 1), 0)),
                  pl.BlockSpec((tokens,) + tile, lambda i, pe, nc: (jnp.maximum(i - n_a, 0), 0))],
        out_specs=pl.BlockSpec(memory_space=pl.ANY),
        scratch_shapes=[pltpu.VMEM((MOE_ROWS,) + tile, jnp.uint32), pltpu.SemaphoreType.DMA(()),
                        pltpu.SemaphoreType.DMA(())],
    )
    return pl.pallas_call(
        functools.partial(_dispatch_kernel, tokens=tokens, n_first=n_a),
        grid_spec=grid_spec,
        out_shape=jax.ShapeDtypeStruct((_n_sorted_rows(n_steps * tokens),) + tile, jnp.uint32),
        compiler_params=_params(("arbitrary",), 24 << 20),
        name="moe_dispatch",
    )(pad_end, n_chunks, dest.reshape(n_steps, 1, tokens * TOP_K), h2p_a, h2p_b)


def _on_static(value, n, fn):
    for v in range(n):
        @pl.when(value == v)
        def _():
            fn(v)


def _moe_up_kernel(be_ref, first_ref, next_ref, wrap_ref, nl_ref, x_ref, w_hbm, bg_ref, bu_ref, act_ref,
                   stage, w16, sem, *, n_j):
    j = pl.program_id(0)
    b = pl.program_id(1)
    e = be_ref[b]
    live = b < nl_ref[0]
    tf = w16.shape[2]

    def weight_copies(ee, jj):
        return [pltpu.make_async_copy(w_hbm.at[ee, :, g * D_FF + jj * tf:g * D_FF + (jj + 1) * tf], stage.at[g], sem)
                for g in range(2)]

    def start_weights(ee, jd):
        _on_static(jd, n_j, lambda jj: [cp.start() for cp in weight_copies(ee, jj)])

    @pl.when(jnp.logical_and(j == 0, b == 0))
    def _():
        start_weights(e, j)

    @pl.when(jnp.logical_and(live, first_ref[b] == 1))
    def _():
        _on_static(j, n_j, lambda jj: [cp.wait() for cp in weight_copies(e, jj)])
        w16[...] = stage[...].astype(bf16)
        j_next = j + wrap_ref[e]

        @pl.when(j_next < n_j)
        def _():
            start_weights(next_ref[e], j_next)

    @pl.when(live)
    def _():
        lo, hi = _unpack_pairs(x_ref[...])
        x_lo = lo.astype(bf16)
        x_hi = hi.astype(bf16)
        half = x_lo.shape[1]

        def proj(g, b_ref):
            return (jnp.dot(x_lo, w16[g, 0:half, :], preferred_element_type=f32)
                    + jnp.dot(x_hi, w16[g, half:, :], preferred_element_type=f32) + b_ref[0])

        gate = jnp.minimum(proj(0, bg_ref), SWIGLU_LIMIT)
        up = jnp.clip(proj(1, bu_ref), -SWIGLU_LIMIT, SWIGLU_LIMIT)
        act_ref[0] = ((up + 1.0) * gate * jax.nn.sigmoid(SWIGLU_ALPHA * gate)).astype(bf16)

    @pl.when(jnp.logical_not(live))
    def _():
        act_ref[...] = jnp.zeros_like(act_ref)


def _moe_up(sched, x_sorted, w_gate_up, b_gate_up):
    e, d, f2 = w_gate_up.shape
    tf = UP_TILE
    n_j = D_FF // tf
    rows = x_sorted.shape[0]
    n_blocks = rows // MOE_ROWS
    vmem = 2 * d * tf * 4 + 2 * d * tf * 2 + 2 * MOE_ROWS * d * 2 + 2 * MOE_ROWS * tf * 2 \
        + 8 * MOE_ROWS * tf * 4 + (6 << 20)
    grid_spec = pltpu.PrefetchScalarGridSpec(
        num_scalar_prefetch=5,
        grid=(n_j, n_blocks),
        in_specs=[pl.BlockSpec((MOE_ROWS, d // 2), lambda j, b, be, *_: (b, 0)),
                  pl.BlockSpec(memory_space=pl.ANY),
                  pl.BlockSpec((1, 1, tf), lambda j, b, be, *_: (be[b], 0, j)),
                  pl.BlockSpec((1, 1, tf), lambda j, b, be, *_: (be[b], 0, n_j + j))],
        out_specs=pl.BlockSpec((1, MOE_ROWS, tf), lambda j, b, be, *_: (j, b, 0)),
        scratch_shapes=[pltpu.VMEM((2, d, tf), f32), pltpu.VMEM((2, d, tf), bf16), pltpu.SemaphoreType.DMA(())],
    )
    b3 = b_gate_up.reshape(e, 1, f2)
    return pl.pallas_call(
        functools.partial(_moe_up_kernel, n_j=n_j),
        grid_spec=grid_spec,
        out_shape=jax.ShapeDtypeStruct((n_j, rows, tf), bf16),
        compiler_params=_params(("arbitrary", "arbitrary"), vmem),
        name="moe_gate_up",
    )(*sched, x_sorted, w_gate_up, b3, b3)


def _moe_down_kernel(be_ref, first_ref, next_ref, wrap_ref, nl_ref, act_ref, w_hbm, bd_ref, y_ref, stage, w16, sem):
    b = pl.program_id(0)
    e = be_ref[b]
    live = b < nl_ref[0]
    n_k, _, tf = act_ref.shape

    def weight_copy(ee):
        return pltpu.make_async_copy(w_hbm.at[ee], stage, sem)

    @pl.when(b == 0)
    def _():
        weight_copy(e).start()

    @pl.when(jnp.logical_and(live, first_ref[b] == 1))
    def _():
        weight_copy(e).wait()
        w16[...] = stage[...].astype(bf16)

        @pl.when(wrap_ref[e] == 0)
        def _():
            weight_copy(next_ref[e]).start()

    @pl.when(live)
    def _():
        y = bd_ref[0]
        for k in range(n_k):
            y = y + jnp.dot(act_ref[k], w16[k * tf:(k + 1) * tf, :], preferred_element_type=f32)
        y16 = y.astype(bf16)
        half = y16.shape[1] // 2
        y_ref[...] = _pack_pairs(y16[:, :half], y16[:, half:])

    @pl.when(jnp.logical_not(live))
    def _():
        y_ref[...] = jnp.zeros_like(y_ref)


def _moe_down(sched, act, w_down, b_down):
    e, f, d = w_down.shape
    n_k, rows, tf = act.shape
    n_blocks = rows // MOE_ROWS
    vmem = f * d * 4 + f * d * 2 + 2 * MOE_ROWS * f * 2 + 2 * MOE_ROWS * d * 2 + 4 * MOE_ROWS * d * 4 + (6 << 20)
    grid_spec = pltpu.PrefetchScalarGridSpec(
        num_scalar_prefetch=5,
        grid=(n_blocks,),
        in_specs=[pl.BlockSpec((n_k, MOE_ROWS, tf), lambda b, be, *_: (0, b, 0)),
                  pl.BlockSpec(memory_space=pl.ANY),
                  pl.BlockSpec((1, 1, d), lambda b, be, *_: (be[b], 0, 0))],
        out_specs=pl.BlockSpec((MOE_ROWS, d // 2), lambda b, be, *_: (b, 0)),
        scratch_shapes=[pltpu.VMEM((f, d), f32), pltpu.VMEM((f, d), bf16), pltpu.SemaphoreType.DMA(())],
    )
    return pl.pallas_call(
        _moe_down_kernel,
        grid_spec=grid_spec,
        out_shape=jax.ShapeDtypeStruct((rows, d // 2), jnp.uint32),
        compiler_params=_params(("arbitrary",), vmem),
        name="moe_down",
    )(*sched, act, w_down, b_down.reshape(e, 1, d))


def _gather_rows(idx_ref, src_hbm, dst, sem, n_rows):
    for r in range(n_rows):
        t = idx_ref[0, 0, r]
        pltpu.make_async_copy(src_hbm.at[pl.ds(t, 1)], dst.at[pl.ds(r, 1)], sem).start(priority=r % 2)


def _wait_rows(src_hbm, dst, sem, n_rows):
    pltpu.make_async_copy(src_hbm.at[pl.ds(0, n_rows)], dst, sem).wait()


def _combine_kernel(pos_ref, posn_ref, y_hbm, gates_ref, xmid_ref, mod_ref, gpost_ref, out_ref, ybuf, sem, *, n_steps):
    i = pl.program_id(0)
    tm = COMBINE_TOKENS
    n_rows = TOP_K * tm
    slot = i % 2

    @pl.when(i == 0)
    def _():
        _gather_rows(pos_ref, y_hbm, ybuf.at[0], sem.at[0], n_rows)

    _wait_rows(y_hbm, ybuf.at[slot], sem.at[slot], n_rows)
    _gather_rows(posn_ref, y_hbm, ybuf.at[1 - slot], sem.at[1 - slot], n_rows)
    g = gates_ref[...]
    lo = hi = None
    for k in range(TOP_K):
        l_k, h_k = _unpack_pairs(ybuf[slot, k * tm:(k + 1) * tm, :])
        gk = g[:, k:k + 1]
        lo = gk * l_k if lo is None else lo + gk * l_k
        hi = gk * h_k if hi is None else hi + gk * h_k
    half = lo.shape[1]
    inv = lax.rsqrt((jnp.sum(lo * lo, axis=-1, keepdims=True) + jnp.sum(hi * hi, axis=-1, keepdims=True))
                    / (2 * half) + EPS)
    gate2 = mod_ref[0][5:6]
    gp = gpost_ref[...]
    out_ref[:, :half] = xmid_ref[:, :half] + gate2[:, :half] * (lo * inv * gp[:, :half])
    out_ref[:, half:] = xmid_ref[:, half:] + gate2[:, half:] * (hi * inv * gp[:, half:])

    @pl.when(i == n_steps - 1)
    def _():
        _wait_rows(y_hbm, ybuf.at[1 - slot], sem.at[1 - slot], n_rows)


def _combine(pos, y_sorted, gates, xmid, mod, g_post, seq):
    n, d = xmid.shape
    tm = COMBINE_TOKENS
    n_tiles = n // tm
    per_seq = seq // tm
    vmem = 2 * TOP_K * tm * d * 2 + 4 * tm * d * 4 + 6 * tm * d * 4 + (8 << 20)
    return pl.pallas_call(
        functools.partial(_combine_kernel, n_steps=n_tiles),
        grid=(n_tiles,),
        in_specs=[pl.BlockSpec((1, 1, TOP_K * tm), lambda i: (i, 0, 0), memory_space=pltpu.SMEM),
                  pl.BlockSpec((1, 1, TOP_K * tm), lambda i: (jnp.minimum(i + 1, n_tiles - 1), 0, 0),
                               memory_space=pltpu.SMEM),
                  pl.BlockSpec(memory_space=pl.ANY),
                  pl.BlockSpec((tm, TOP_K), lambda i: (i, 0)),
                  pl.BlockSpec((tm, d), lambda i: (i, 0)),
                  pl.BlockSpec((1, 6, d), lambda i: (i // per_seq, 0, 0)),
                  _const_spec((1, d))],
        out_specs=pl.BlockSpec((tm, d), lambda i: (i, 0)),
        out_shape=jax.ShapeDtypeStruct((n, d), f32),
        scratch_shapes=[pltpu.VMEM((2, TOP_K * tm) + y_sorted.shape[1:], jnp.uint32), pltpu.SemaphoreType.DMA((2,))],
        compiler_params=_params(("arbitrary",), vmem),
        name="moe_combine",
    )(pos, pos, y_sorted, gates, xmid, mod, g_post.reshape(1, d))


def _rope_tables(pos):
    def cs(d):
        half = d // 2
        inv_freq = 1.0 / (ROPE_THETA ** (jnp.arange(half, dtype=f32) * (2.0 / d)))
        ang = pos.astype(f32)[:, None] * inv_freq[None, :]
        return jnp.cos(ang), jnp.sin(ang)
    c64, s64 = cs(MLA_ROPE)
    c256, s256 = cs(RET_QK)
    t64 = jnp.concatenate([c64, c64, s64, s64], axis=-1)
    t256 = jnp.concatenate([c256, s256], axis=-1)
    return t64, t256


def _rot_cols(w):
    half = w.shape[-1] // 2
    return jnp.concatenate([-w[..., half:], w[..., :half]], axis=-1)


def _prep_weights(w_in, w_uq, w_ukv, w_out, w_router, b_router):
    b = np.cumsum((Q_LORA, KV_LORA, MLA_ROPE))
    w_pe = w_in[:, b[1]:b[2]]
    w_parts = (w_in[:, :b[1]].astype(bf16), jnp.concatenate([w_pe, _rot_cols(w_pe)], axis=-1).astype(bf16),
               w_in[:, b[2]:].astype(bf16))
    wq = w_uq.reshape(Q_LORA, MLA_HEADS, MLA_NOPE + MLA_ROPE)
    wq = jnp.concatenate([wq, _rot_cols(wq[..., MLA_NOPE:])], axis=-1).transpose(1, 0, 2).astype(bf16)
    wkv = w_ukv.reshape(KV_LORA, MLA_HEADS, MLA_NOPE + MLA_V).transpose(1, 0, 2).astype(bf16)
    r_hi = w_router.astype(bf16)
    r_lo = (w_router - r_hi.astype(f32)).astype(bf16)
    pad = ((0, 0), (0, LANES - N_EXPERTS))
    wr = jnp.concatenate([jnp.pad(r_hi, pad), jnp.pad(r_lo, pad)], axis=1)
    br = jnp.pad(b_router, (0, LANES - N_EXPERTS), constant_values=NEG_BIG).reshape(1, LANES)
    return w_parts, wq, wkv, w_out.astype(bf16), wr, br


def _mixer_half(x, mod, pos, wts, p, past_lat16, past_kpe16, state0, counts0, ret_chunk, q_tiles, tm):
    w_parts, wq, wkv, w_out16, wr, br = wts
    b, s, d = x.shape
    x2d = x.reshape(b * s, d)
    t64, t256 = _rope_tables(pos)
    qlat, kvlat, kvlat16, kpe, kpe16, rq, rk, rv, rg = _inproj(
        x2d, mod, p['g_pre_mix'], w_parts, p['g_q_lat'], p['g_kv_lat'], t64, t256, s, tm)
    lat16 = kvlat16.reshape(b, s, KV_LORA)
    kpe16 = kpe16.reshape(b, s, 2 * MLA_ROPE)
    if past_lat16 is not None:
        lat16 = jnp.concatenate([past_lat16, lat16], axis=1)
        kpe16 = jnp.concatenate([past_kpe16, kpe16], axis=1)
    attn = _mla(qlat.reshape(b, s, Q_LORA), lat16, kpe16, wq, wkv, t64, q_tiles)
    log_g = jnp.log1p(-jnp.exp2(-5.0 - jnp.arange(RET_HEADS, dtype=f32)))
    log_g = jnp.broadcast_to(log_g[:, None, None], (RET_HEADS, 8, LANES))
    sh = lambda t, w: t.reshape(b, s, w)
    rn, state = _retention(sh(rq, RET_HEADS * RET_QK), sh(rk, RET_HEADS * RET_QK), sh(rv, RET_WIDTH),
                           sh(rg, RET_WIDTH), p['g_ret'], state0, log_g, ret_chunk)
    xmid, h2p, idx, gates, rank, counts = _mixout(
        attn.reshape(b * s, MLA_WIDTH), rn.reshape(b * s, RET_WIDTH), x2d, mod,
        p['g_post_mix'], p['g_pre_ffn'], w_out16, wr, br, counts0, s, tm)
    route = (idx[:, :TOP_K], gates[:, :TOP_K], rank[:, :TOP_K], counts)
    return kvlat.reshape(b, s, KV_LORA), kpe.reshape(b, s, MLA_ROPE), state, xmid, h2p, route


def _combine_order(dest):
    tm = COMBINE_TOKENS
    n_tiles = dest.shape[0] // tm
    return dest.reshape(n_tiles, tm, TOP_K).transpose(0, 2, 1).reshape(n_tiles, 1, TOP_K * tm)


def kernel(x_prompt, x_sample, cache_kv_latent, cache_k_rope, state_retention, c_prompt, c_sample, w_ada, b_ada, g_pre_mix, g_post_mix, g_pre_ffn, g_post_ffn, w_in, g_q_lat, g_kv_lat, w_uq, w_ukv, g_ret, w_out, w_router, b_router, w_gate_up, b_gate_up, w_down, b_down):
    depth = w_in.shape[0]
    assert depth == 1, "the staged problem has a single layer"
    bp, sp, d = x_prompt.shape
    bs, ss, _ = x_sample.shape
    past = cache_kv_latent.shape[2]
    l = 0
    p = dict(g_pre_mix=g_pre_mix[l], g_post_mix=g_post_mix[l], g_pre_ffn=g_pre_ffn[l], g_post_ffn=g_post_ffn[l],
             g_q_lat=g_q_lat[l], g_kv_lat=g_kv_lat[l], g_ret=g_ret[l])
    wts = _prep_weights(w_in[l], w_uq[l], w_ukv[l], w_out[l], w_router[l], b_router[l])

    mod = _ada(jnp.concatenate([c_prompt, c_sample], axis=0), w_ada[l], b_ada[l]).reshape(bp + bs, 6, d)
    mod_p, mod_s = mod[:bp], mod[bp:]

    tq = ATTN_Q_TILE
    tiles_p = tuple((q0, tq, q0 + tq, True) for q0 in range(0, sp, tq))
    zero_state = jnp.zeros((bp, RET_HEADS, RET_QK, RET_V), f32)
    lat_p, kpe_p, st_p, xmid_p, h2p_p, (idx_p, gates_p, rank_p, counts_p) = _mixer_half(
        x_prompt, mod_p, jnp.arange(sp), wts, p, None, None, zero_state, jnp.zeros((1, LANES), f32),
        RET_CHUNK_PROMPT, tiles_p, 512)

    past_lat16 = cache_kv_latent[l].astype(bf16)
    past_kpe16 = jnp.pad(cache_k_rope[l], ((0, 0), (0, 0), (0, MLA_ROPE))).astype(bf16)
    tiles_s = ((0, ss, past + ss, False),)
    lat_s, kpe_s, st_s, xmid_s, h2p_s, (idx_s, gates_s, rank_s, counts) = _mixer_half(
        x_sample, mod_s, past + jnp.arange(ss), wts, p, past_lat16, past_kpe16, state_retention[l], counts_p,
        ss, tiles_s, ss)

    n_p = bp * sp
    dest, pad_end, n_chunks = _route(
        jnp.concatenate([idx_p, idx_s], axis=0), jnp.concatenate([rank_p, rank_s], axis=0), counts)
    x_sorted = _dispatch(pad_end, n_chunks, dest, h2p_p, h2p_s)
    sched = _block_schedule(n_chunks, x_sorted.shape[0] // MOE_ROWS)
    act = _moe_up(sched, x_sorted, w_gate_up[l], b_gate_up[l])
    y_sorted = _moe_down(sched, act, w_down[l], b_down[l])
    y_p = _combine(_combine_order(dest[:n_p]), y_sorted, gates_p, xmid_p, mod_p, p['g_post_ffn'], sp)
    y_s = _combine(_combine_order(dest[n_p:]), y_sorted, gates_s, xmid_s, mod_s, p['g_post_ffn'], ss)

    return (y_p.reshape(bp, sp, d), y_s.reshape(bs, ss, d),
            lat_p[None], kpe_p[None], st_p[None].astype(state_retention.dtype),
            lat_s[None], kpe_s[None], st_s[None].astype(state_retention.dtype))
```

```python
import functools

import numpy as np
import jax
import jax.numpy as jnp
from jax import lax
from jax.experimental import pallas as pl
from jax.experimental.pallas import tpu as pltpu

CHUNK = 64
EPS = 1e-6
ROPE_THETA = 10000.0

MLA_HEADS = 8
MLA_NOPE = 128
MLA_ROPE = 64
MLA_V = 128
Q_LORA = 512
KV_LORA = 512
MLA_WIDTH = MLA_HEADS * MLA_V

RET_HEADS = 4
RET_QK = 256
RET_V = 256
RET_WIDTH = RET_HEADS * RET_V

N_EXPERTS = 32
TOP_K = 4
D_FF = 2048
SWIGLU_LIMIT = 7.0
SWIGLU_ALPHA = 1.702

LANES = 128
SUBLANES = 8
V7X_VMEM_BYTES = 64 * 1024 * 1024
VMEM_CAP_BYTES = V7X_VMEM_BYTES - 8 * 1024 * 1024

RET_CHUNK_PROMPT = 128
RET_UNROLL = 16
ATTN_Q_TILE = 256
MOE_ROWS = 256
UP_TILE = 1024
DISPATCH_TOKENS = 256
COMBINE_TOKENS = 64
NEG_BIG = -1e30

bf16 = jnp.bfloat16
f32 = jnp.float32


def _params(sem, vmem_bytes):
    return pltpu.CompilerParams(dimension_semantics=sem, vmem_limit_bytes=int(min(vmem_bytes, VMEM_CAP_BYTES)))


def _const_spec(shape):
    nd = len(shape)
    return pl.BlockSpec(shape, lambda *_: (0,) * nd, pipeline_mode=pl.Buffered(1))


def _rms(x):
    return x * lax.rsqrt(jnp.mean(x * x, axis=-1, keepdims=True) + EPS)


def _silu(x):
    return x * jax.nn.sigmoid(x)


def _ada_kernel(c_ref, w_ref, b_ref, o_ref):
    s = _silu(c_ref[...]).astype(bf16)
    o_ref[...] = jnp.dot(s, w_ref[...].astype(bf16), preferred_element_type=f32) + b_ref[...]


def _ada(c, w_ada, b_ada):
    nb, d = c.shape
    n = w_ada.shape[1]
    tn = 1536
    return pl.pallas_call(
        _ada_kernel,
        grid=(n // tn,),
        in_specs=[pl.BlockSpec((nb, d), lambda j: (0, 0)),
                  pl.BlockSpec((d, tn), lambda j: (0, j)),
                  pl.BlockSpec((1, tn), lambda j: (0, j))],
        out_specs=pl.BlockSpec((nb, tn), lambda j: (0, j)),
        out_shape=jax.ShapeDtypeStruct((nb, n), f32),
        compiler_params=_params(("arbitrary",), 2 * d * tn * 4 + d * tn * 2 + (8 << 20)),
        name="ada_modulation",
    )(c, w_ada, b_ada.reshape(1, n))


_C_RK = RET_HEADS * RET_QK
_C_RV = _C_RK + RET_HEADS * RET_QK
_C_RG = _C_RV + RET_WIDTH
_C_END = _C_RG + RET_WIDTH


def _rope_pair(t):
    return t + pltpu.roll(t, MLA_ROPE, axis=1)


def _rope_heads(z, cos, sin, scale):
    outs = []
    half = RET_QK // 2
    for h in range(RET_HEADS):
        x1 = z[:, h * RET_QK:h * RET_QK + half]
        x2 = z[:, h * RET_QK + half:(h + 1) * RET_QK]
        outs.append((x1 * cos - x2 * sin) * scale)
        outs.append((x1 * sin + x2 * cos) * scale)
    return jnp.concatenate(outs, axis=-1)


def _inproj_kernel(x_ref, mod_ref, g_ref, wlat_ref, wpe_ref, wret_ref, gq_ref, gkv_ref, t64_ref, t256_ref,
                   qlat_ref, kvlat_ref, kvlat16_ref, kpe_ref, kpe16_ref, rq_ref, rk_ref, rv_ref, rg_ref):
    m = mod_ref[0]
    h = (_rms(x_ref[...]) * g_ref[...] * (1.0 + m[1:2]) + m[0:1]).astype(bf16)

    def proj(w_ref, c0, c1):
        return jnp.dot(h, w_ref[:, c0:c1], preferred_element_type=f32)

    qlat_ref[...] = (_rms(proj(wlat_ref, 0, Q_LORA)) * gq_ref[...]).astype(bf16)
    kv = _rms(proj(wlat_ref, Q_LORA, Q_LORA + KV_LORA)) * gkv_ref[...]
    kvlat_ref[...] = kv
    kvlat16_ref[...] = kv.astype(bf16)
    pe = _rope_pair(proj(wpe_ref, 0, 2 * MLA_ROPE) * t64_ref[...])
    kpe_ref[...] = pe[:, :MLA_ROPE]
    lane = lax.broadcasted_iota(jnp.int32, pe.shape, 1)
    kpe16_ref[...] = jnp.where(lane < MLA_ROPE, pe, 0.0).astype(bf16)
    cos = t256_ref[:, :RET_QK // 2]
    sin = t256_ref[:, RET_QK // 2:]
    rq_ref[...] = _rope_heads(proj(wret_ref, 0, _C_RK), cos, sin, 1.0).astype(bf16)
    rk_ref[...] = _rope_heads(proj(wret_ref, _C_RK, _C_RV), cos, sin, RET_QK ** -0.5).astype(bf16)
    rv_ref[...] = proj(wret_ref, _C_RV, _C_RG).astype(bf16)
    rg_ref[...] = proj(wret_ref, _C_RG, _C_END).astype(bf16)


def _inproj(x2d, mod, g_pre, w_parts, g_q, g_kv, t64, t256, seq, tm):
    n, d = x2d.shape
    per_seq = seq // tm
    row = lambda w: pl.BlockSpec((tm, w), lambda i: (i, 0))
    tab = lambda w: pl.BlockSpec((tm, w), lambda i: (i % per_seq, 0))
    outs = [(Q_LORA, bf16), (KV_LORA, f32), (KV_LORA, bf16), (MLA_ROPE, f32), (2 * MLA_ROPE, bf16),
            (RET_HEADS * RET_QK, bf16), (RET_HEADS * RET_QK, bf16), (RET_WIDTH, bf16), (RET_WIDTH, bf16)]
    vmem = sum(w.size for w in w_parts) * 2 + 2 * tm * d * 4 + 4 * tm * d * 4 + sum(2 * tm * w * 4 for w, _ in outs) + (6 << 20)
    return pl.pallas_call(
        _inproj_kernel,
        grid=(n // tm,),
        in_specs=[row(d),
                  pl.BlockSpec((1, 6, d), lambda i: (i // per_seq, 0, 0)),
                  _const_spec((1, d)), *[_const_spec(w.shape) for w in w_parts],
                  _const_spec((1, Q_LORA)), _const_spec((1, KV_LORA)),
                  tab(2 * MLA_ROPE), tab(RET_QK)],
        out_specs=[row(w) for w, _ in outs],
        out_shape=[jax.ShapeDtypeStruct((n, w), dt) for w, dt in outs],
        compiler_params=_params(("arbitrary",), vmem),
        name="in_projection",
    )(x2d, mod, g_pre.reshape(1, d), *w_parts, g_q.reshape(1, Q_LORA), g_kv.reshape(1, KV_LORA), t64, t256)


def _mla_kernel(qlat_ref, lat_ref, kpe_ref, wq_ref, wkv_ref, tab_ref, o_ref, q_s, k_s, v_s, *, q_tiles):
    scale = (MLA_NOPE + MLA_ROPE) ** -0.5
    qh = jnp.dot(qlat_ref[0], wq_ref[0], preferred_element_type=f32)
    q_s[:, :MLA_NOPE] = (qh[:, :MLA_NOPE] * scale).astype(bf16)
    q_s[:, MLA_NOPE:] = (_rope_pair(qh[:, MLA_NOPE:] * tab_ref[...]) * scale).astype(bf16)
    kv = jnp.dot(lat_ref[0], wkv_ref[0], preferred_element_type=f32)
    k_s[:, :MLA_NOPE] = kv[:, :MLA_NOPE].astype(bf16)
    k_s[:, MLA_NOPE:] = kpe_ref[0]
    v_s[...] = kv[:, MLA_NOPE:].astype(bf16)
    for q0, ql, kvl, masked in q_tiles:
        q = q_s[q0:q0 + ql, :]
        s = lax.dot_general(q, k_s[0:kvl, :], (((1,), (1,)), ((), ())), preferred_element_type=f32)
        if masked:
            assert kvl == q0 + ql and q0 % CHUNK == 0
            qc = lax.broadcasted_iota(jnp.int32, (ql, 1), 0) // CHUNK
            kc = lax.broadcasted_iota(jnp.int32, (1, ql), 1) // CHUNK
            diag = jnp.where(kc <= qc, s[:, q0:], NEG_BIG)
            s = diag if q0 == 0 else jnp.concatenate([s[:, :q0], diag], axis=1)
        p = jnp.exp(s - jnp.max(s, axis=-1, keepdims=True))
        l = jnp.sum(p, axis=-1, keepdims=True)
        o = jnp.dot(p.astype(bf16), v_s[0:kvl, :], preferred_element_type=f32)
        o_ref[0, q0:q0 + ql, :] = (o / l).astype(bf16)


def _mla(qlat, lat, kpe16, wq, wkv, tab, q_tiles):
    b, sq, _ = qlat.shape
    skv = lat.shape[1]
    kern = functools.partial(_mla_kernel, q_tiles=q_tiles)
    max_ql = max(t[1] for t in q_tiles)
    vmem = 2 * 2 * (sq * Q_LORA + skv * KV_LORA + skv * LANES) + sq * LANES * 4 + sq * 3 * LANES * 4 \
        + (sq + skv) * 3 * LANES * 2 + 4 * max_ql * skv * 4 + (8 << 20)
    return pl.pallas_call(
        kern,
        grid=(b, MLA_HEADS),
        in_specs=[pl.BlockSpec((1, sq, Q_LORA), lambda i, h: (i, 0, 0)),
                  pl.BlockSpec((1, skv, KV_LORA), lambda i, h: (i, 0, 0)),
                  pl.BlockSpec((1, skv, 2 * MLA_ROPE), lambda i, h: (i, 0, 0)),
                  pl.BlockSpec((1, Q_LORA, MLA_NOPE + 2 * MLA_ROPE), lambda i, h: (h, 0, 0)),
                  pl.BlockSpec((1, KV_LORA, MLA_NOPE + MLA_V), lambda i, h: (h, 0, 0)),
                  pl.BlockSpec((sq, 2 * MLA_ROPE), lambda i, h: (0, 0))],
        out_specs=pl.BlockSpec((1, sq, MLA_V), lambda i, h: (i, 0, h)),
        out_shape=jax.ShapeDtypeStruct((b, sq, MLA_WIDTH), bf16),
        scratch_shapes=[pltpu.VMEM((sq, MLA_NOPE + 2 * MLA_ROPE), bf16),
                        pltpu.VMEM((skv, MLA_NOPE + 2 * MLA_ROPE), bf16),
                        pltpu.VMEM((skv, MLA_V), bf16)],
        compiler_params=_params(("arbitrary", "arbitrary"), vmem),
        name="mla_attention",
    )(qlat, lat, kpe16, wq, wkv, tab)


def _ret_kernel(lg_ref, rq_ref, rk_ref, rv_ref, rg_ref, gret_ref, s0_ref, o_ref, sout_ref, state, *, chunk, n_chunks):
    c = chunk
    lg = lg_ref[0, 0:1, 0:1]
    ii = lax.broadcasted_iota(jnp.int32, (c, c), 0)
    jj = lax.broadcasted_iota(jnp.int32, (c, c), 1)
    diff = (ii - jj).astype(f32)
    dmask = jnp.where(diff >= 0.0, jnp.exp(jnp.maximum(diff, 0.0) * lg), 0.0)
    ic = lax.broadcasted_iota(jnp.int32, (c, 1), 0).astype(f32)
    q_decay = jnp.exp((ic + 1.0) * lg)
    k_decay = jnp.exp((c - 1.0 - ic) * lg)
    chunk_decay = jnp.exp(float(c) * lg)
    gret = gret_ref[...]
    state[...] = s0_ref[0, 0]

    def step(n, carry):
        sl = pl.ds(pl.multiple_of(n * c, c), c)
        q = rq_ref[0, sl, :]
        k = rk_ref[0, sl, :]
        v = rv_ref[0, sl, :]
        st = state[...]
        attn = lax.dot_general(q, k, (((1,), (1,)), ((), ())), preferred_element_type=f32) * dmask
        inner = jnp.dot(attn.astype(bf16), v, preferred_element_type=f32)
        cross = jnp.dot(q, st.astype(bf16), preferred_element_type=f32) * q_decay
        o = inner + cross
        kd_t = (k.astype(f32) * k_decay).T.astype(bf16)
        state[...] = st * chunk_decay + jnp.dot(kd_t, v, preferred_element_type=f32)
        mu = jnp.mean(o, axis=-1, keepdims=True)
        oc = o - mu
        rn = oc * lax.rsqrt(jnp.mean(oc * oc, axis=-1, keepdims=True) + EPS) * gret
        o_ref[0, sl, :] = (rn * _silu(rg_ref[0, sl, :].astype(f32))).astype(bf16)
        return carry

    lax.fori_loop(0, n_chunks, step, 0, unroll=min(n_chunks, RET_UNROLL))
    sout_ref[0, 0] = state[...]


def _retention(rq, rk, rv, rg, g_ret, state0, log_g, chunk):
    b, s, _ = rq.shape
    kern = functools.partial(_ret_kernel, chunk=chunk, n_chunks=s // chunk)
    head = lambda w: pl.BlockSpec((1, s, w), lambda i, h: (i, 0, h))
    st = pl.BlockSpec((1, 1, RET_QK, RET_V), lambda i, h: (i, h, 0, 0))
    vmem = 2 * 5 * s * RET_QK * 2 + 5 * RET_QK * RET_V * 4 + (8 << 20)
    return pl.pallas_call(
        kern,
        grid=(b, RET_HEADS),
        in_specs=[pl.BlockSpec((1, 8, LANES), lambda i, h: (h, 0, 0)),
                  head(RET_QK), head(RET_QK), head(RET_V), head(RET_V),
                  pl.BlockSpec((1, RET_V), lambda i, h: (0, h)), st],
        out_specs=[head(RET_V), st],
        out_shape=[jax.ShapeDtypeStruct((b, s, RET_WIDTH), bf16),
                   jax.ShapeDtypeStruct((b, RET_HEADS, RET_QK, RET_V), f32)],
        scratch_shapes=[pltpu.VMEM((RET_QK, RET_V), f32)],
        compiler_params=_params(("arbitrary", "arbitrary"), vmem),
        name="retention",
    )(log_g, rq, rk, rv, rg, g_ret.reshape(1, RET_WIDTH), state0)


def _pack_pairs(lo16, hi16):
    lo = lax.bitcast_convert_type(lo16.astype(f32), jnp.uint32)
    hi = lax.bitcast_convert_type(hi16.astype(f32), jnp.uint32)
    return (hi & jnp.uint32(0xFFFF0000)) | (lo >> 16)


def _unpack_pairs(u):
    lo = lax.bitcast_convert_type(u << 16, f32)
    hi = lax.bitcast_convert_type(u & jnp.uint32(0xFFFF0000), f32)
    return lo, hi


def _mixout_kernel(attn_ref, rn_ref, x_ref, mod_ref, gpost_ref, gpre_ref, wout_ref, wr_ref, br_ref, cnt0_ref,
                   xmid_ref, h2p_ref, idx_ref, gate_ref, rank_ref, cnt_ref, carry):
    @pl.when(pl.program_id(0) == 0)
    def _():
        carry[...] = cnt0_ref[...]

    mix = (jnp.dot(attn_ref[...], wout_ref[0:MLA_WIDTH, :], preferred_element_type=f32)
           + jnp.dot(rn_ref[...], wout_ref[MLA_WIDTH:, :], preferred_element_type=f32))
    m = mod_ref[0]
    x1 = x_ref[...] + m[2:3] * (_rms(mix) * gpost_ref[...])
    xmid_ref[...] = x1
    h2 = _rms(x1) * gpre_ref[...] * (1.0 + m[4:5]) + m[3:4]
    h_hi = h2.astype(bf16)
    half = h2.shape[1] // 2
    h2p_ref[...] = _pack_pairs(h_hi[:, :half], h_hi[:, half:])
    h_lo = (h2 - h_hi.astype(f32)).astype(bf16)
    hi_terms = jnp.dot(h_hi, wr_ref[...], preferred_element_type=f32)
    logits = (hi_terms[:, :LANES] + jnp.dot(h_lo, wr_ref[:, :LANES], preferred_element_type=f32)
              + hi_terms[:, LANES:]) + br_ref[...]
    tm = logits.shape[0]
    col = lax.broadcasted_iota(jnp.int32, logits.shape, 1).astype(f32)
    vals, idxs = [], []
    for _ in range(TOP_K):
        mx = jnp.max(logits, axis=-1, keepdims=True)
        ix = jnp.min(jnp.where(logits == mx, col, float(LANES)), axis=-1, keepdims=True)
        vals.append(mx)
        idxs.append(ix)
        logits = jnp.where(col == ix, -jnp.inf, logits)
    es = [jnp.exp(v - vals[0]) for v in vals]
    tot = es[0] + es[1] + es[2] + es[3]
    onehot = jnp.zeros_like(col)
    for k in range(TOP_K):
        onehot = jnp.where(col == idxs[k], 1.0, onehot)
    earlier = (lax.broadcasted_iota(jnp.int32, (tm, tm), 0) > lax.broadcasted_iota(jnp.int32, (tm, tm), 1))
    before = jnp.dot(jnp.where(earlier, 1.0, 0.0).astype(bf16), onehot.astype(bf16),
                     preferred_element_type=f32) + carry[...]
    idx_out = jnp.zeros_like(col)
    gate_out = jnp.zeros_like(col)
    rank_out = jnp.zeros_like(col)
    for k in range(TOP_K):
        slot = col == float(k)
        idx_out = jnp.where(slot, idxs[k], idx_out)
        gate_out = jnp.where(slot, es[k] / tot, gate_out)
        rank_k = jnp.sum(jnp.where(col == idxs[k], before, 0.0), axis=-1, keepdims=True)
        rank_out = jnp.where(slot, rank_k, rank_out)
    idx_ref[...] = idx_out.astype(jnp.int32)
    gate_ref[...] = gate_out
    rank_ref[...] = rank_out.astype(jnp.int32)
    carry[...] = carry[...] + jnp.sum(onehot, axis=0, keepdims=True)
    cnt_ref[...] = carry[...]


def _mixout(attn, rn, x2d, mod, g_post, g_pre, w_out16, wr, br, counts0, seq, tm):
    n, d = x2d.shape
    per_seq = seq // tm
    row = lambda w: pl.BlockSpec((tm, w), lambda i: (i, 0))
    vmem = w_out16.size * 2 + 2 * tm * d * (4 + 4 + 2 + 2) + 6 * tm * d * 4 + (8 << 20)
    return pl.pallas_call(
        _mixout_kernel,
        grid=(n // tm,),
        in_specs=[row(MLA_WIDTH), row(RET_WIDTH), row(d),
                  pl.BlockSpec((1, 6, d), lambda i: (i // per_seq, 0, 0)),
                  _const_spec((1, d)), _const_spec((1, d)), _const_spec(w_out16.shape),
                  _const_spec(wr.shape), _const_spec((1, LANES)), _const_spec((1, LANES))],
        out_specs=[row(d), row(d // 2), row(LANES), row(LANES), row(LANES),
                   pl.BlockSpec((1, LANES), lambda i: (0, 0))],
        out_shape=[jax.ShapeDtypeStruct((n, d), f32), jax.ShapeDtypeStruct((n, d // 2), jnp.uint32),
                   jax.ShapeDtypeStruct((n, LANES), jnp.int32), jax.ShapeDtypeStruct((n, LANES), f32),
                   jax.ShapeDtypeStruct((n, LANES), jnp.int32), jax.ShapeDtypeStruct((1, LANES), f32)],
        scratch_shapes=[pltpu.VMEM((1, LANES), f32)],
        compiler_params=_params(("arbitrary",), vmem),
        name="mixer_out_router",
    )(attn, rn, x2d, mod, g_post.reshape(1, d), g_pre.reshape(1, d), w_out16, wr, br, counts0)


def _route(top_idx, rank, counts):
    counts = counts[0, :N_EXPERTS].astype(jnp.int32)
    padded = (counts + MOE_ROWS - 1) // MOE_ROWS * MOE_ROWS
    pad_end = jnp.cumsum(padded).astype(jnp.int32)
    pad_start = pad_end - padded
    dest = pad_start[top_idx] + rank
    return dest, pad_end, padded // MOE_ROWS


def _n_sorted_rows(n_tokens):
    return (n_tokens * TOP_K + N_EXPERTS * (MOE_ROWS - 1)) // MOE_ROWS * MOE_ROWS


def _block_schedule(n_chunks, n_blocks):
    ends = jnp.cumsum(n_chunks)
    n_live = ends[-1]
    blk = jnp.arange(n_blocks, dtype=jnp.int32)
    block_e = jnp.sum(jnp.minimum(blk, n_live - 1)[:, None] >= ends[None, :], axis=1).astype(jnp.int32)
    is_first = jnp.logical_and(blk == (ends - n_chunks)[block_e], blk < n_live).astype(jnp.int32)
    ids = jnp.arange(N_EXPERTS, dtype=jnp.int32)
    cand = jnp.where(n_chunks > 0, ids, N_EXPERTS)
    from_here = lax.cummin(cand[::-1])[::-1]
    after = jnp.concatenate([from_here[1:], jnp.full((1,), N_EXPERTS, jnp.int32)])
    wraps = (after >= N_EXPERTS).astype(jnp.int32)
    next_e = jnp.where(after < N_EXPERTS, after, from_here[0]).astype(jnp.int32)
    return block_e, is_first, next_e, wraps, n_live.reshape(1).astype(jnp.int32)


def _dispatch_kernel(pe_ref, nc_ref, dest_ref, h2p_a_ref, h2p_b_ref, x_hbm, zbuf, sem, zsem, *, tokens, n_first):
    i = pl.program_id(0)

    @pl.when(i == 0)
    def _():
        zbuf[...] = jnp.zeros_like(zbuf)

        def tail(e):
            start = pl.multiple_of(pe_ref[e] - MOE_ROWS, MOE_ROWS)
            return pltpu.make_async_copy(zbuf, x_hbm.at[pl.ds(start, MOE_ROWS)], zsem)

        def start_tail(e, c):
            @pl.when(nc_ref[e] > 0)
            def _():
                tail(e).start()
            return c

        def wait_tail(e, c):
            @pl.when(nc_ref[e] > 0)
            def _():
                tail(e).wait()
            return c

        lax.fori_loop(0, N_EXPERTS, start_tail, 0)
        lax.fori_loop(0, N_EXPERTS, wait_tail, 0)

        def unused(c):
            return pltpu.make_async_copy(zbuf, x_hbm.at[pl.ds(pl.multiple_of(c * MOE_ROWS, MOE_ROWS), MOE_ROWS)], zsem)

        def start_unused(c, carry):
            unused(c).start()
            return carry

        def wait_unused(c, carry):
            unused(c).wait()
            return carry

        first_unused = pe_ref[N_EXPERTS - 1] // MOE_ROWS
        lax.fori_loop(first_unused, x_hbm.shape[0] // MOE_ROWS, start_unused, 0)
        lax.fori_loop(first_unused, x_hbm.shape[0] // MOE_ROWS, wait_unused, 0)

    def scatter(h2p_ref):
        for t in range(tokens):
            for k in range(TOP_K):
                d = dest_ref[0, 0, t * TOP_K + k]
                pltpu.make_async_copy(h2p_ref.at[pl.ds(t, 1)], x_hbm.at[pl.ds(d, 1)], sem).start(priority=k % 2)
        for _ in range(TOP_K):
            pltpu.make_async_copy(h2p_ref, x_hbm.at[pl.ds(0, tokens)], sem).wait()

    @pl.when(i < n_first)
    def _():
        scatter(h2p_a_ref)

    @pl.when(i >= n_first)
    def _():
        scatter(h2p_b_ref)


def _dispatch(pad_end, n_chunks, dest, h2p_a, h2p_b):
    tokens = DISPATCH_TOKENS
    tile = h2p_a.shape[1:]
    n_a, n_b = h2p_a.shape[0] // tokens, h2p_b.shape[0] // tokens
    assert n_a * tokens == h2p_a.shape[0] and n_b * tokens == h2p_b.shape[0]
    n_steps = n_a + n_b
    grid_spec = pltpu.PrefetchScalarGridSpec(
        num_scalar_prefetch=2,
        grid=(n_steps,),
        in_specs=[pl.BlockSpec((1, 1, tokens * TOP_K), lambda i, pe, nc: (i, 0, 0), memory_space=pltpu.SMEM),
                  pl.BlockSpec((tokens,) + tile, lambda i, pe, nc: (jnp.minimum(i, n_a - 1), 0)),
                  pl.BlockSpec((tokens,) + tile, lambda i, pe, nc: (jnp.maximum(i - n_a, 0), 0))],
        out_specs=pl.BlockSpec(memory_space=pl.ANY),
        scratch_shapes=[pltpu.VMEM((MOE_ROWS,) + tile, jnp.uint32), pltpu.SemaphoreType.DMA(()),
                        pltpu.SemaphoreType.DMA(())],
    )
    return pl.pallas_call(
        functools.partial(_dispatch_kernel, tokens=tokens, n_first=n_a),
        grid_spec=grid_spec,
        out_shape=jax.ShapeDtypeStruct((_n_sorted_rows(n_steps * tokens),) + tile, jnp.uint32),
        compiler_params=_params(("arbitrary",), 24 << 20),
        name="moe_dispatch",
    )(pad_end, n_chunks, dest.reshape(n_steps, 1, tokens * TOP_K), h2p_a, h2p_b)


def _on_static(value, n, fn):
    for v in range(n):
        @pl.when(value == v)
        def _():
            fn(v)


def _moe_up_kernel(be_ref, first_ref, next_ref, wrap_ref, nl_ref, x_ref, w_hbm, bg_ref, bu_ref, act_ref,
                   stage, w16, sem, *, n_j):
    j = pl.program_id(0)
    b = pl.program_id(1)
    e = be_ref[b]
    live = b < nl_ref[0]
    tf = w16.shape[2]

    def weight_copies(ee, jj):
        return [pltpu.make_async_copy(w_hbm.at[ee, :, g * D_FF + jj * tf:g * D_FF + (jj + 1) * tf], stage.at[g], sem)
                for g in range(2)]

    def start_weights(ee, jd):
        _on_static(jd, n_j, lambda jj: [cp.start() for cp in weight_copies(ee, jj)])

    @pl.when(jnp.logical_and(j == 0, b == 0))
    def _():
        start_weights(e, j)

    @pl.when(jnp.logical_and(live, first_ref[b] == 1))
    def _():
        _on_static(j, n_j, lambda jj: [cp.wait() for cp in weight_copies(e, jj)])
        w16[...] = stage[...].astype(bf16)
        j_next = j + wrap_ref[e]

        @pl.when(j_next < n_j)
        def _():
            start_weights(next_ref[e], j_next)

    @pl.when(live)
    def _():
        lo, hi = _unpack_pairs(x_ref[...])
        x_lo = lo.astype(bf16)
        x_hi = hi.astype(bf16)
        half = x_lo.shape[1]

        def proj(g, b_ref):
            return (jnp.dot(x_lo, w16[g, 0:half, :], preferred_element_type=f32)
                    + jnp.dot(x_hi, w16[g, half:, :], preferred_element_type=f32) + b_ref[0])

        gate = jnp.minimum(proj(0, bg_ref), SWIGLU_LIMIT)
        up = jnp.clip(proj(1, bu_ref), -SWIGLU_LIMIT, SWIGLU_LIMIT)
        act_ref[0] = ((up + 1.0) * gate * jax.nn.sigmoid(SWIGLU_ALPHA * gate)).astype(bf16)

    @pl.when(jnp.logical_not(live))
    def _():
        act_ref[...] = jnp.zeros_like(act_ref)


def _moe_up(sched, x_sorted, w_gate_up, b_gate_up):
    e, d, f2 = w_gate_up.shape
    tf = UP_TILE
    n_j = D_FF // tf
    rows = x_sorted.shape[0]
    n_blocks = rows // MOE_ROWS
    vmem = 2 * d * tf * 4 + 2 * d * tf * 2 + 2 * MOE_ROWS * d * 2 + 2 * MOE_ROWS * tf * 2 \
        + 8 * MOE_ROWS * tf * 4 + (6 << 20)
    grid_spec = pltpu.PrefetchScalarGridSpec(
        num_scalar_prefetch=5,
        grid=(n_j, n_blocks),
        in_specs=[pl.BlockSpec((MOE_ROWS, d // 2), lambda j, b, be, *_: (b, 0)),
                  pl.BlockSpec(memory_space=pl.ANY),
                  pl.BlockSpec((1, 1, tf), lambda j, b, be, *_: (be[b], 0, j)),
                  pl.BlockSpec((1, 1, tf), lambda j, b, be, *_: (be[b], 0, n_j + j))],
        out_specs=pl.BlockSpec((1, MOE_ROWS, tf), lambda j, b, be, *_: (j, b, 0)),
        scratch_shapes=[pltpu.VMEM((2, d, tf), f32), pltpu.VMEM((2, d, tf), bf16), pltpu.SemaphoreType.DMA(())],
    )
    b3 = b_gate_up.reshape(e, 1, f2)
    return pl.pallas_call(
        functools.partial(_moe_up_kernel, n_j=n_j),
        grid_spec=grid_spec,
        out_shape=jax.ShapeDtypeStruct((n_j, rows, tf), bf16),
        compiler_params=_params(("arbitrary", "arbitrary"), vmem),
        name="moe_gate_up",
    )(*sched, x_sorted, w_gate_up, b3, b3)


def _moe_down_kernel(be_ref, first_ref, next_ref, wrap_ref, nl_ref, act_ref, w_hbm, bd_ref, y_ref, stage, w16, sem):
    b = pl.program_id(0)
    e = be_ref[b]
    live = b < nl_ref[0]
    n_k, _, tf = act_ref.shape

    def weight_copy(ee):
        return pltpu.make_async_copy(w_hbm.at[ee], stage, sem)

    @pl.when(b == 0)
    def _():
        weight_copy(e).start()

    @pl.when(jnp.logical_and(live, first_ref[b] == 1))
    def _():
        weight_copy(e).wait()
        w16[...] = stage[...].astype(bf16)

        @pl.when(wrap_ref[e] == 0)
        def _():
            weight_copy(next_ref[e]).start()

    @pl.when(live)
    def _():
        y = bd_ref[0]
        for k in range(n_k):
            y = y + jnp.dot(act_ref[k], w16[k * tf:(k + 1) * tf, :], preferred_element_type=f32)
        y16 = y.astype(bf16)
        half = y16.shape[1] // 2
        y_ref[...] = _pack_pairs(y16[:, :half], y16[:, half:])

    @pl.when(jnp.logical_not(live))
    def _():
        y_ref[...] = jnp.zeros_like(y_ref)


def _moe_down(sched, act, w_down, b_down):
    e, f, d = w_down.shape
    n_k, rows, tf = act.shape
    n_blocks = rows // MOE_ROWS
    vmem = f * d * 4 + f * d * 2 + 2 * MOE_ROWS * f * 2 + 2 * MOE_ROWS * d * 2 + 4 * MOE_ROWS * d * 4 + (6 << 20)
    grid_spec = pltpu.PrefetchScalarGridSpec(
        num_scalar_prefetch=5,
        grid=(n_blocks,),
        in_specs=[pl.BlockSpec((n_k, MOE_ROWS, tf), lambda b, be, *_: (0, b, 0)),
                  pl.BlockSpec(memory_space=pl.ANY),
                  pl.BlockSpec((1, 1, d), lambda b, be, *_: (be[b], 0, 0))],
        out_specs=pl.BlockSpec((MOE_ROWS, d // 2), lambda b, be, *_: (b, 0)),
        scratch_shapes=[pltpu.VMEM((f, d), f32), pltpu.VMEM((f, d), bf16), pltpu.SemaphoreType.DMA(())],
    )
    return pl.pallas_call(
        _moe_down_kernel,
        grid_spec=grid_spec,
        out_shape=jax.ShapeDtypeStruct((rows, d // 2), jnp.uint32),
        compiler_params=_params(("arbitrary",), vmem),
        name="moe_down",
    )(*sched, act, w_down, b_down.reshape(e, 1, d))


def _gather_rows(idx_ref, src_hbm, dst, sem, n_rows):
    for r in range(n_rows):
        t = idx_ref[0, 0, r]
        pltpu.make_async_copy(src_hbm.at[pl.ds(t, 1)], dst.at[pl.ds(r, 1)], sem).start(priority=r % 2)


def _wait_rows(src_hbm, dst, sem, n_rows):
    pltpu.make_async_copy(src_hbm.at[pl.ds(0, n_rows)], dst, sem).wait()


def _combine_kernel(pos_ref, posn_ref, y_hbm, gates_ref, xmid_ref, mod_ref, gpost_ref, out_ref, ybuf, sem, *, n_steps):
    i = pl.program_id(0)
    tm = COMBINE_TOKENS
    n_rows = TOP_K * tm
    slot = i % 2

    @pl.when(i == 0)
    def _():
        _gather_rows(pos_ref, y_hbm, ybuf.at[0], sem.at[0], n_rows)

    @pl.when(i + 1 < n_steps)
    def _():
        _gather_rows(posn_ref, y_hbm, ybuf.at[1 - slot], sem.at[1 - slot], n_rows)

    _wait_rows(y_hbm, ybuf.at[slot], sem.at[slot], n_rows)
    g = gates_ref[...]
    lo = hi = None
    for k in range(TOP_K):
        l_k, h_k = _unpack_pairs(ybuf[slot, k * tm:(k + 1) * tm, :])
        gk = g[:, k:k + 1]
        lo = gk * l_k if lo is None else lo + gk * l_k
        hi = gk * h_k if hi is None else hi + gk * h_k
    half = lo.shape[1]
    inv = lax.rsqrt((jnp.sum(lo * lo, axis=-1, keepdims=True) + jnp.sum(hi * hi, axis=-1, keepdims=True))
                    / (2 * half) + EPS)
    gate2 = mod_ref[0][5:6]
    gp = gpost_ref[...]
    out_ref[:, :half] = xmid_ref[:, :half] + gate2[:, :half] * (lo * inv * gp[:, :half])
    out_ref[:, half:] = xmid_ref[:, half:] + gate2[:, half:] * (hi * inv * gp[:, half:])


def _combine(pos, y_sorted, gates, xmid, mod, g_post, seq):
    n, d = xmid.shape
    tm = COMBINE_TOKENS
    n_tiles = n // tm
    per_seq = seq // tm
    vmem = 2 * TOP_K * tm * d * 2 + 4 * tm * d * 4 + 6 * tm * d * 4 + (8 << 20)
    return pl.pallas_call(
        functools.partial(_combine_kernel, n_steps=n_tiles),
        grid=(n_tiles,),
        in_specs=[pl.BlockSpec((1, 1, TOP_K * tm), lambda i: (i, 0, 0), memory_space=pltpu.SMEM),
                  pl.BlockSpec((1, 1, TOP_K * tm), lambda i: (jnp.minimum(i + 1, n_tiles - 1), 0, 0),
                               memory_space=pltpu.SMEM),
                  pl.BlockSpec(memory_space=pl.ANY),
                  pl.BlockSpec((tm, TOP_K), lambda i: (i, 0)),
                  pl.BlockSpec((tm, d), lambda i: (i, 0)),
                  pl.BlockSpec((1, 6, d), lambda i: (i // per_seq, 0, 0)),
                  _const_spec((1, d))],
        out_specs=pl.BlockSpec((tm, d), lambda i: (i, 0)),
        out_shape=jax.ShapeDtypeStruct((n, d), f32),
        scratch_shapes=[pltpu.VMEM((2, TOP_K * tm) + y_sorted.shape[1:], jnp.uint32), pltpu.SemaphoreType.DMA((2,))],
        compiler_params=_params(("arbitrary",), vmem),
        name="moe_combine",
    )(pos, pos, y_sorted, gates, xmid, mod, g_post.reshape(1, d))


def _rope_tables(pos):
    def cs(d):
        half = d // 2
        inv_freq = 1.0 / (ROPE_THETA ** (jnp.arange(half, dtype=f32) * (2.0 / d)))
        ang = pos.astype(f32)[:, None] * inv_freq[None, :]
        return jnp.cos(ang), jnp.sin(ang)
    c64, s64 = cs(MLA_ROPE)
    c256, s256 = cs(RET_QK)
    t64 = jnp.concatenate([c64, c64, s64, s64], axis=-1)
    t256 = jnp.concatenate([c256, s256], axis=-1)
    return t64, t256


def _rot_cols(w):
    half = w.shape[-1] // 2
    return jnp.concatenate([-w[..., half:], w[..., :half]], axis=-1)


def _prep_weights(w_in, w_uq, w_ukv, w_out, w_router, b_router):
    b = np.cumsum((Q_LORA, KV_LORA, MLA_ROPE))
    w_pe = w_in[:, b[1]:b[2]]
    w_parts = (w_in[:, :b[1]].astype(bf16), jnp.concatenate([w_pe, _rot_cols(w_pe)], axis=-1).astype(bf16),
               w_in[:, b[2]:].astype(bf16))
    wq = w_uq.reshape(Q_LORA, MLA_HEADS, MLA_NOPE + MLA_ROPE)
    wq = jnp.concatenate([wq, _rot_cols(wq[..., MLA_NOPE:])], axis=-1).transpose(1, 0, 2).astype(bf16)
    wkv = w_ukv.reshape(KV_LORA, MLA_HEADS, MLA_NOPE + MLA_V).transpose(1, 0, 2).astype(bf16)
    r_hi = w_router.astype(bf16)
    r_lo = (w_router - r_hi.astype(f32)).astype(bf16)
    pad = ((0, 0), (0, LANES - N_EXPERTS))
    wr = jnp.concatenate([jnp.pad(r_hi, pad), jnp.pad(r_lo, pad)], axis=1)
    br = jnp.pad(b_router, (0, LANES - N_EXPERTS), constant_values=NEG_BIG).reshape(1, LANES)
    return w_parts, wq, wkv, w_out.astype(bf16), wr, br


def _mixer_half(x, mod, pos, wts, p, past_lat16, past_kpe16, state0, counts0, ret_chunk, q_tiles, tm):
    w_parts, wq, wkv, w_out16, wr, br = wts
    b, s, d = x.shape
    x2d = x.reshape(b * s, d)
    t64, t256 = _rope_tables(pos)
    qlat, kvlat, kvlat16, kpe, kpe16, rq, rk, rv, rg = _inproj(
        x2d, mod, p['g_pre_mix'], w_parts, p['g_q_lat'], p['g_kv_lat'], t64, t256, s, tm)
    lat16 = kvlat16.reshape(b, s, KV_LORA)
    kpe16 = kpe16.reshape(b, s, 2 * MLA_ROPE)
    if past_lat16 is not None:
        lat16 = jnp.concatenate([past_lat16, lat16], axis=1)
        kpe16 = jnp.concatenate([past_kpe16, kpe16], axis=1)
    attn = _mla(qlat.reshape(b, s, Q_LORA), lat16, kpe16, wq, wkv, t64, q_tiles)
    log_g = jnp.log1p(-jnp.exp2(-5.0 - jnp.arange(RET_HEADS, dtype=f32)))
    log_g = jnp.broadcast_to(log_g[:, None, None], (RET_HEADS, 8, LANES))
    sh = lambda t, w: t.reshape(b, s, w)
    rn, state = _retention(sh(rq, RET_HEADS * RET_QK), sh(rk, RET_HEADS * RET_QK), sh(rv, RET_WIDTH),
                           sh(rg, RET_WIDTH), p['g_ret'], state0, log_g, ret_chunk)
    xmid, h2p, idx, gates, rank, counts = _mixout(
        attn.reshape(b * s, MLA_WIDTH), rn.reshape(b * s, RET_WIDTH), x2d, mod,
        p['g_post_mix'], p['g_pre_ffn'], w_out16, wr, br, counts0, s, tm)
    route = (idx[:, :TOP_K], gates[:, :TOP_K], rank[:, :TOP_K], counts)
    return kvlat.reshape(b, s, KV_LORA), kpe.reshape(b, s, MLA_ROPE), state, xmid, h2p, route


def _combine_order(dest):
    tm = COMBINE_TOKENS
    n_tiles = dest.shape[0] // tm
    return dest.reshape(n_tiles, tm, TOP_K).transpose(0, 2, 1).reshape(n_tiles, 1, TOP_K * tm)


def kernel(x_prompt, x_sample, cache_kv_latent, cache_k_rope, state_retention, c_prompt, c_sample, w_ada, b_ada, g_pre_mix, g_post_mix, g_pre_ffn, g_post_ffn, w_in, g_q_lat, g_kv_lat, w_uq, w_ukv, g_ret, w_out, w_router, b_router, w_gate_up, b_gate_up, w_down, b_down):
    depth = w_in.shape[0]
    assert depth == 1, "the staged problem has a single layer"
    bp, sp, d = x_prompt.shape
    bs, ss, _ = x_sample.shape
    past = cache_kv_latent.shape[2]
    l = 0
    p = dict(g_pre_mix=g_pre_mix[l], g_post_mix=g_post_mix[l], g_pre_ffn=g_pre_ffn[l], g_post_ffn=g_post_ffn[l],
             g_q_lat=g_q_lat[l], g_kv_lat=g_kv_lat[l], g_ret=g_ret[l])
    wts = _prep_weights(w_in[l], w_uq[l], w_ukv[l], w_out[l], w_router[l], b_router[l])

    mod = _ada(jnp.concatenate([c_prompt, c_sample], axis=0), w_ada[l], b_ada[l]).reshape(bp + bs, 6, d)
    mod_p, mod_s = mod[:bp], mod[bp:]

    tq = ATTN_Q_TILE
    tiles_p = tuple((q0, tq, q0 + tq, True) for q0 in range(0, sp, tq))
    zero_state = jnp.zeros((bp, RET_HEADS, RET_QK, RET_V), f32)
    lat_p, kpe_p, st_p, xmid_p, h2p_p, (idx_p, gates_p, rank_p, counts_p) = _mixer_half(
        x_prompt, mod_p, jnp.arange(sp), wts, p, None, None, zero_state, jnp.zeros((1, LANES), f32),
        RET_CHUNK_PROMPT, tiles_p, 512)

    past_lat16 = cache_kv_latent[l].astype(bf16)
    past_kpe16 = jnp.pad(cache_k_rope[l], ((0, 0), (0, 0), (0, MLA_ROPE))).astype(bf16)
    tiles_s = ((0, ss, past + ss, False),)
    lat_s, kpe_s, st_s, xmid_s, h2p_s, (idx_s, gates_s, rank_s, counts) = _mixer_half(
        x_sample, mod_s, past + jnp.arange(ss), wts, p, past_lat16, past_kpe16, state_retention[l], counts_p,
        ss, tiles_s, ss)

    n_p = bp * sp
    dest, pad_end, n_chunks = _route(
        jnp.concatenate([idx_p, idx_s], axis=0), jnp.concatenate([rank_p, rank_s], axis=0), counts)
    x_sorted = _dispatch(pad_end, n_chunks, dest, h2p_p, h2p_s)
    sched = _block_schedule(n_chunks, x_sorted.shape[0] // MOE_ROWS)
    act = _moe_up(sched, x_sorted, w_gate_up[l], b_gate_up[l])
    y_sorted = _moe_down(sched, act, w_down[l], b_down[l])
    y_p = _combine(_combine_order(dest[:n_p]), y_sorted, gates_p, xmid_p, mod_p, p['g_post_ffn'], sp)
    y_s = _combine(_combine_order(dest[n_p:]), y_sorted, gates_s, xmid_s, mod_s, p['g_post_ffn'], ss)

    return (y_p.reshape(bp, sp, d), y_s.reshape(bs, ss, d),
            lat_p[None], kpe_p[None], st_p[None].astype(state_retention.dtype),
            lat_s[None], kpe_s[None], st_s[None].astype(state_retention.dtype))
```

```python
import functools

import numpy as np
import jax
import jax.numpy as jnp
from jax import lax
from jax.experimental import pallas as pl
from jax.experimental.pallas import tpu as pltpu

CHUNK = 64
EPS = 1e-6
ROPE_THETA = 10000.0

MLA_HEADS = 8
MLA_NOPE = 128
MLA_ROPE = 64
MLA_V = 128
Q_LORA = 512
KV_LORA = 512
MLA_WIDTH = MLA_HEADS * MLA_V

RET_HEADS = 4
RET_QK = 256
RET_V = 256
RET_WIDTH = RET_HEADS * RET_V

N_EXPERTS = 32
TOP_K = 4
D_FF = 2048
SWIGLU_LIMIT = 7.0
SWIGLU_ALPHA = 1.702

LANES = 128
SUBLANES = 8
V7X_VMEM_BYTES = 64 * 1024 * 1024
VMEM_CAP_BYTES = V7X_VMEM_BYTES - 8 * 1024 * 1024

RET_CHUNK_PROMPT = 128
RET_UNROLL = 16
ATTN_Q_TILE = 256
MOE_ROWS = 256
UP_TILE = 1024
DISPATCH_TOKENS = 256
COMBINE_TOKENS = 64
NEG_BIG = -1e30

bf16 = jnp.bfloat16
f32 = jnp.float32


def _params(sem, vmem_bytes):
    return pltpu.CompilerParams(dimension_semantics=sem, vmem_limit_bytes=int(min(vmem_bytes, VMEM_CAP_BYTES)))


def _const_spec(shape):
    nd = len(shape)
    return pl.BlockSpec(shape, lambda *_: (0,) * nd, pipeline_mode=pl.Buffered(1))


def _rms(x):
    return x * lax.rsqrt(jnp.mean(x * x, axis=-1, keepdims=True) + EPS)


def _silu(x):
    return x * jax.nn.sigmoid(x)


def _ada_kernel(c_ref, w_ref, b_ref, o_ref):
    s = _silu(c_ref[...]).astype(bf16)
    o_ref[...] = jnp.dot(s, w_ref[...].astype(bf16), preferred_element_type=f32) + b_ref[...]


def _ada(c, w_ada, b_ada):
    nb, d = c.shape
    n = w_ada.shape[1]
    tn = 1536
    return pl.pallas_call(
        _ada_kernel,
        grid=(n // tn,),
        in_specs=[pl.BlockSpec((nb, d), lambda j: (0, 0)),
                  pl.BlockSpec((d, tn), lambda j: (0, j)),
                  pl.BlockSpec((1, tn), lambda j: (0, j))],
        out_specs=pl.BlockSpec((nb, tn), lambda j: (0, j)),
        out_shape=jax.ShapeDtypeStruct((nb, n), f32),
        compiler_params=_params(("arbitrary",), 2 * d * tn * 4 + d * tn * 2 + (8 << 20)),
        name="ada_modulation",
    )(c, w_ada, b_ada.reshape(1, n))


_C_RK = RET_HEADS * RET_QK
_C_RV = _C_RK + RET_HEADS * RET_QK
_C_RG = _C_RV + RET_WIDTH
_C_END = _C_RG + RET_WIDTH


def _rope_pair(t):
    return t + pltpu.roll(t, MLA_ROPE, axis=1)


def _rope_heads(z, cos, sin, scale):
    outs = []
    half = RET_QK // 2
    for h in range(RET_HEADS):
        x1 = z[:, h * RET_QK:h * RET_QK + half]
        x2 = z[:, h * RET_QK + half:(h + 1) * RET_QK]
        outs.append((x1 * cos - x2 * sin) * scale)
        outs.append((x1 * sin + x2 * cos) * scale)
    return jnp.concatenate(outs, axis=-1)


def _inproj_kernel(x_ref, mod_ref, g_ref, wlat_ref, wpe_ref, wret_ref, gq_ref, gkv_ref, t64_ref, t256_ref,
                   qlat_ref, kvlat_ref, kvlat16_ref, kpe_ref, kpe16_ref, rq_ref, rk_ref, rv_ref, rg_ref):
    m = mod_ref[0]
    h = (_rms(x_ref[...]) * g_ref[...] * (1.0 + m[1:2]) + m[0:1]).astype(bf16)

    def proj(w_ref, c0, c1):
        return lax.dot_general(h, w_ref[c0:c1, :], (((1,), (1,)), ((), ())), preferred_element_type=f32)

    qlat_ref[...] = (_rms(proj(wlat_ref, 0, Q_LORA)) * gq_ref[...]).astype(bf16)
    kv = _rms(proj(wlat_ref, Q_LORA, Q_LORA + KV_LORA)) * gkv_ref[...]
    kvlat_ref[...] = kv
    kvlat16_ref[...] = kv.astype(bf16)
    pe = _rope_pair(proj(wpe_ref, 0, 2 * MLA_ROPE) * t64_ref[...])
    kpe_ref[...] = pe[:, :MLA_ROPE]
    lane = lax.broadcasted_iota(jnp.int32, pe.shape, 1)
    kpe16_ref[...] = jnp.where(lane < MLA_ROPE, pe, 0.0).astype(bf16)
    cos = t256_ref[:, :RET_QK // 2]
    sin = t256_ref[:, RET_QK // 2:]
    rq_ref[...] = _rope_heads(proj(wret_ref, 0, _C_RK), cos, sin, 1.0).astype(bf16)
    rk_ref[...] = _rope_heads(proj(wret_ref, _C_RK, _C_RV), cos, sin, RET_QK ** -0.5).astype(bf16)
    rv_ref[...] = proj(wret_ref, _C_RV, _C_RG).astype(bf16)
    rg_ref[...] = proj(wret_ref, _C_RG, _C_END).astype(bf16)


def _inproj(x2d, mod, g_pre, w_parts, g_q, g_kv, t64, t256, seq, tm):
    n, d = x2d.shape
    per_seq = seq // tm
    row = lambda w: pl.BlockSpec((tm, w), lambda i: (i, 0))
    tab = lambda w: pl.BlockSpec((tm, w), lambda i: (i % per_seq, 0))
    outs = [(Q_LORA, bf16), (KV_LORA, f32), (KV_LORA, bf16), (MLA_ROPE, f32), (2 * MLA_ROPE, bf16),
            (RET_HEADS * RET_QK, bf16), (RET_HEADS * RET_QK, bf16), (RET_WIDTH, bf16), (RET_WIDTH, bf16)]
    vmem = sum(w.size for w in w_parts) * 2 + 2 * tm * d * 4 + 4 * tm * d * 4 + sum(2 * tm * w * 4 for w, _ in outs) + (6 << 20)
    return pl.pallas_call(
        _inproj_kernel,
        grid=(n // tm,),
        in_specs=[row(d),
                  pl.BlockSpec((1, 6, d), lambda i: (i // per_seq, 0, 0)),
                  _const_spec((1, d)), *[_const_spec(w.shape) for w in w_parts],
                  _const_spec((1, Q_LORA)), _const_spec((1, KV_LORA)),
                  tab(2 * MLA_ROPE), tab(RET_QK)],
        out_specs=[row(w) for w, _ in outs],
        out_shape=[jax.ShapeDtypeStruct((n, w), dt) for w, dt in outs],
        compiler_params=_params(("arbitrary",), vmem),
        name="in_projection",
    )(x2d, mod, g_pre.reshape(1, d), *w_parts, g_q.reshape(1, Q_LORA), g_kv.reshape(1, KV_LORA), t64, t256)


def _mla_kernel(qlat_ref, lat_ref, kpe_ref, wq_ref, wkv_ref, tab_ref, o_ref, q_s, k_s, v_s, *, q_tiles):
    scale = (MLA_NOPE + MLA_ROPE) ** -0.5
    qh = jnp.dot(qlat_ref[0], wq_ref[0], preferred_element_type=f32)
    q_s[:, :MLA_NOPE] = (qh[:, :MLA_NOPE] * scale).astype(bf16)
    q_s[:, MLA_NOPE:] = (_rope_pair(qh[:, MLA_NOPE:] * tab_ref[...]) * scale).astype(bf16)
    kv = jnp.dot(lat_ref[0], wkv_ref[0], preferred_element_type=f32)
    k_s[:, :MLA_NOPE] = kv[:, :MLA_NOPE].astype(bf16)
    k_s[:, MLA_NOPE:] = kpe_ref[0]
    v_s[...] = kv[:, MLA_NOPE:].astype(bf16)
    for q0, ql, kvl, masked in q_tiles:
        q = q_s[q0:q0 + ql, :]
        s = lax.dot_general(q, k_s[0:kvl, :], (((1,), (1,)), ((), ())), preferred_element_type=f32)
        if masked:
            assert kvl == q0 + ql and q0 % CHUNK == 0
            qc = lax.broadcasted_iota(jnp.int32, (ql, 1), 0) // CHUNK
            kc = lax.broadcasted_iota(jnp.int32, (1, ql), 1) // CHUNK
            diag = jnp.where(kc <= qc, s[:, q0:], NEG_BIG)
            s = diag if q0 == 0 else jnp.concatenate([s[:, :q0], diag], axis=1)
        p = jnp.exp(s - jnp.max(s, axis=-1, keepdims=True))
        l = jnp.sum(p, axis=-1, keepdims=True)
        o = jnp.dot(p.astype(bf16), v_s[0:kvl, :], preferred_element_type=f32)
        o_ref[0, q0:q0 + ql, :] = (o / l).astype(bf16)


def _mla(qlat, lat, kpe16, wq, wkv, tab, q_tiles):
    b, sq, _ = qlat.shape
    skv = lat.shape[1]
    kern = functools.partial(_mla_kernel, q_tiles=q_tiles)
    max_ql = max(t[1] for t in q_tiles)
    vmem = 2 * 2 * (sq * Q_LORA + skv * KV_LORA + skv * LANES) + sq * LANES * 4 + sq * 3 * LANES * 4 \
        + (sq + skv) * 3 * LANES * 2 + 4 * max_ql * skv * 4 + (8 << 20)
    return pl.pallas_call(
        kern,
        grid=(b, MLA_HEADS),
        in_specs=[pl.BlockSpec((1, sq, Q_LORA), lambda i, h: (i, 0, 0)),
                  pl.BlockSpec((1, skv, KV_LORA), lambda i, h: (i, 0, 0)),
                  pl.BlockSpec((1, skv, 2 * MLA_ROPE), lambda i, h: (i, 0, 0)),
                  pl.BlockSpec((1, Q_LORA, MLA_NOPE + 2 * MLA_ROPE), lambda i, h: (h, 0, 0)),
                  pl.BlockSpec((1, KV_LORA, MLA_NOPE + MLA_V), lambda i, h: (h, 0, 0)),
                  pl.BlockSpec((sq, 2 * MLA_ROPE), lambda i, h: (0, 0))],
        out_specs=pl.BlockSpec((1, sq, MLA_V), lambda i, h: (i, 0, h)),
        out_shape=jax.ShapeDtypeStruct((b, sq, MLA_WIDTH), bf16),
        scratch_shapes=[pltpu.VMEM((sq, MLA_NOPE + 2 * MLA_ROPE), bf16),
                        pltpu.VMEM((skv, MLA_NOPE + 2 * MLA_ROPE), bf16),
                        pltpu.VMEM((skv, MLA_V), bf16)],
        compiler_params=_params(("arbitrary", "arbitrary"), vmem),
        name="mla_attention",
    )(qlat, lat, kpe16, wq, wkv, tab)


def _ret_kernel(lg_ref, rq_ref, rk_ref, rv_ref, rg_ref, gret_ref, s0_ref, o_ref, sout_ref, state, *, chunk, n_chunks):
    c = chunk
    lg = lg_ref[0, 0:1, 0:1]
    ii = lax.broadcasted_iota(jnp.int32, (c, c), 0)
    jj = lax.broadcasted_iota(jnp.int32, (c, c), 1)
    diff = (ii - jj).astype(f32)
    dmask = jnp.where(diff >= 0.0, jnp.exp(jnp.maximum(diff, 0.0) * lg), 0.0)
    ic = lax.broadcasted_iota(jnp.int32, (c, 1), 0).astype(f32)
    q_decay = jnp.exp((ic + 1.0) * lg)
    k_decay = jnp.exp((c - 1.0 - ic) * lg)
    chunk_decay = jnp.exp(float(c) * lg)
    gret = gret_ref[...]
    state[...] = s0_ref[0, 0]

    def step(n, carry):
        sl = pl.ds(pl.multiple_of(n * c, c), c)
        q = rq_ref[0, sl, :]
        k = rk_ref[0, sl, :]
        v = rv_ref[0, sl, :]
        st = state[...]
        attn = lax.dot_general(q, k, (((1,), (1,)), ((), ())), preferred_element_type=f32) * dmask
        inner = jnp.dot(attn.astype(bf16), v, preferred_element_type=f32)
        cross = jnp.dot(q, st.astype(bf16), preferred_element_type=f32) * q_decay
        o = inner + cross
        kd_t = (k.astype(f32) * k_decay).T.astype(bf16)
        state[...] = st * chunk_decay + jnp.dot(kd_t, v, preferred_element_type=f32)
        mu = jnp.mean(o, axis=-1, keepdims=True)
        oc = o - mu
        rn = oc * lax.rsqrt(jnp.mean(oc * oc, axis=-1, keepdims=True) + EPS) * gret
        o_ref[0, sl, :] = (rn * _silu(rg_ref[0, sl, :].astype(f32))).astype(bf16)
        return carry

    lax.fori_loop(0, n_chunks, step, 0, unroll=min(n_chunks, RET_UNROLL))
    sout_ref[0, 0] = state[...]


def _retention(rq, rk, rv, rg, g_ret, state0, log_g, chunk):
    b, s, _ = rq.shape
    kern = functools.partial(_ret_kernel, chunk=chunk, n_chunks=s // chunk)
    head = lambda w: pl.BlockSpec((1, s, w), lambda i, h: (i, 0, h))
    st = pl.BlockSpec((1, 1, RET_QK, RET_V), lambda i, h: (i, h, 0, 0))
    vmem = 2 * 5 * s * RET_QK * 2 + 5 * RET_QK * RET_V * 4 + (8 << 20)
    return pl.pallas_call(
        kern,
        grid=(b, RET_HEADS),
        in_specs=[pl.BlockSpec((1, 8, LANES), lambda i, h: (h, 0, 0)),
                  head(RET_QK), head(RET_QK), head(RET_V), head(RET_V),
                  pl.BlockSpec((1, RET_V), lambda i, h: (0, h)), st],
        out_specs=[head(RET_V), st],
        out_shape=[jax.ShapeDtypeStruct((b, s, RET_WIDTH), bf16),
                   jax.ShapeDtypeStruct((b, RET_HEADS, RET_QK, RET_V), f32)],
        scratch_shapes=[pltpu.VMEM((RET_QK, RET_V), f32)],
        compiler_params=_params(("arbitrary", "arbitrary"), vmem),
        name="retention",
    )(log_g, rq, rk, rv, rg, g_ret.reshape(1, RET_WIDTH), state0)


def _pack_pairs(lo16, hi16):
    lo = lax.bitcast_convert_type(lo16.astype(f32), jnp.uint32)
    hi = lax.bitcast_convert_type(hi16.astype(f32), jnp.uint32)
    return (hi & jnp.uint32(0xFFFF0000)) | (lo >> 16)


def _unpack_pairs(u):
    lo = lax.bitcast_convert_type(u << 16, f32)
    hi = lax.bitcast_convert_type(u & jnp.uint32(0xFFFF0000), f32)
    return lo, hi


def _mixout_kernel(attn_ref, rn_ref, x_ref, mod_ref, gpost_ref, gpre_ref, wout_ref, wr_ref, br_ref, cnt0_ref, tri_ref,
                   xmid_ref, h2p_ref, idx_ref, gate_ref, rank_ref, cnt_ref, carry):
    @pl.when(pl.program_id(0) == 0)
    def _():
        carry[...] = cnt0_ref[...]

    mix = (jnp.dot(attn_ref[...], wout_ref[0:MLA_WIDTH, :], preferred_element_type=f32)
           + jnp.dot(rn_ref[...], wout_ref[MLA_WIDTH:, :], preferred_element_type=f32))
    m = mod_ref[0]
    x1 = x_ref[...] + _rms(mix) * (m[2:3] * gpost_ref[...])
    xmid_ref[...] = x1
    h2 = _rms(x1) * (gpre_ref[...] * (1.0 + m[4:5])) + m[3:4]
    h_hi = h2.astype(bf16)
    half = h2.shape[1] // 2
    h2p_ref[...] = _pack_pairs(h_hi[:, :half], h_hi[:, half:])
    h_lo = (h2 - h_hi.astype(f32)).astype(bf16)
    hi_terms = jnp.dot(h_hi, wr_ref[...], preferred_element_type=f32)
    logits = (hi_terms[:, :LANES] + jnp.dot(h_lo, wr_ref[:, :LANES], preferred_element_type=f32)
              + hi_terms[:, LANES:]) + br_ref[...]
    tm = logits.shape[0]
    col = lax.broadcasted_iota(jnp.int32, logits.shape, 1).astype(f32)
    vals, idxs = [], []
    for _ in range(TOP_K):
        mx = jnp.max(logits, axis=-1, keepdims=True)
        ix = jnp.min(jnp.where(logits == mx, col, float(LANES)), axis=-1, keepdims=True)
        vals.append(mx)
        idxs.append(ix)
        logits = jnp.where(col == ix, -jnp.inf, logits)
    es = [jnp.exp(v - vals[0]) for v in vals]
    tot = es[0] + es[1] + es[2] + es[3]
    onehot = jnp.zeros_like(col)
    for k in range(TOP_K):
        onehot = jnp.where(col == idxs[k], 1.0, onehot)
    before = jnp.dot(tri_ref[...], onehot.astype(bf16), preferred_element_type=f32) + carry[...]
    idx_out = jnp.zeros_like(col)
    gate_out = jnp.zeros_like(col)
    rank_out = jnp.zeros_like(col)
    for k in range(TOP_K):
        slot = col == float(k)
        idx_out = jnp.where(slot, idxs[k], idx_out)
        gate_out = jnp.where(slot, es[k] / tot, gate_out)
        rank_k = jnp.sum(jnp.where(col == idxs[k], before, 0.0), axis=-1, keepdims=True)
        rank_out = jnp.where(slot, rank_k, rank_out)
    idx_ref[...] = idx_out.astype(jnp.int32)
    gate_ref[...] = gate_out
    rank_ref[...] = rank_out.astype(jnp.int32)
    carry[...] = carry[...] + jnp.sum(onehot, axis=0, keepdims=True)
    cnt_ref[...] = carry[...]


def _mixout(attn, rn, x2d, mod, g_post, g_pre, w_out16, wr, br, counts0, seq, tm):
    n, d = x2d.shape
    per_seq = seq // tm
    row = lambda w: pl.BlockSpec((tm, w), lambda i: (i, 0))
    tri = jnp.tril(jnp.ones((tm, tm), bf16), -1)
    vmem = w_out16.size * 2 + 2 * tm * d * (4 + 4 + 2 + 2) + 6 * tm * d * 4 + (8 << 20)
    return pl.pallas_call(
        _mixout_kernel,
        grid=(n // tm,),
        in_specs=[row(MLA_WIDTH), row(RET_WIDTH), row(d),
                  pl.BlockSpec((1, 6, d), lambda i: (i // per_seq, 0, 0)),
                  _const_spec((1, d)), _const_spec((1, d)), _const_spec(w_out16.shape),
                  _const_spec(wr.shape), _const_spec((1, LANES)), _const_spec((1, LANES)), _const_spec((tm, tm))],
        out_specs=[row(d), row(d // 2), row(LANES), row(LANES), row(LANES),
                   pl.BlockSpec((1, LANES), lambda i: (0, 0))],
        out_shape=[jax.ShapeDtypeStruct((n, d), f32), jax.ShapeDtypeStruct((n, d // 2), jnp.uint32),
                   jax.ShapeDtypeStruct((n, LANES), jnp.int32), jax.ShapeDtypeStruct((n, LANES), f32),
                   jax.ShapeDtypeStruct((n, LANES), jnp.int32), jax.ShapeDtypeStruct((1, LANES), f32)],
        scratch_shapes=[pltpu.VMEM((1, LANES), f32)],
        compiler_params=_params(("arbitrary",), vmem),
        name="mixer_out_router",
    )(attn, rn, x2d, mod, g_post.reshape(1, d), g_pre.reshape(1, d), w_out16, wr, br, counts0, tri)


def _route(top_idx, rank, counts):
    counts = counts[0, :N_EXPERTS].astype(jnp.int32)
    padded = (counts + MOE_ROWS - 1) // MOE_ROWS * MOE_ROWS
    pad_end = jnp.cumsum(padded).astype(jnp.int32)
    pad_start = pad_end - padded
    dest = pad_start[top_idx] + rank
    return dest, pad_end, padded // MOE_ROWS


def _n_sorted_rows(n_tokens):
    return (n_tokens * TOP_K + N_EXPERTS * (MOE_ROWS - 1)) // MOE_ROWS * MOE_ROWS


def _block_schedule(n_chunks, n_blocks):
    ends = jnp.cumsum(n_chunks)
    n_live = ends[-1]
    blk = jnp.arange(n_blocks, dtype=jnp.int32)
    block_e = jnp.sum(jnp.minimum(blk, n_live - 1)[:, None] >= ends[None, :], axis=1).astype(jnp.int32)
    is_first = jnp.logical_and(blk == (ends - n_chunks)[block_e], blk < n_live).astype(jnp.int32)
    ids = jnp.arange(N_EXPERTS, dtype=jnp.int32)
    cand = jnp.where(n_chunks > 0, ids, N_EXPERTS)
    from_here = lax.cummin(cand[::-1])[::-1]
    after = jnp.concatenate([from_here[1:], jnp.full((1,), N_EXPERTS, jnp.int32)])
    wraps = (after >= N_EXPERTS).astype(jnp.int32)
    next_e = jnp.where(after < N_EXPERTS, after, from_here[0]).astype(jnp.int32)
    return block_e, is_first, next_e, wraps, n_live.reshape(1).astype(jnp.int32)


def _dispatch_kernel(pe_ref, nc_ref, dest_ref, h2p_a_ref, h2p_b_ref, x_hbm, zbuf, sem, zsem, *, tokens, n_first):
    i = pl.program_id(0)

    @pl.when(i == 0)
    def _():
        zbuf[...] = jnp.zeros_like(zbuf)

        def tail(e):
            start = pl.multiple_of(pe_ref[e] - MOE_ROWS, MOE_ROWS)
            return pltpu.make_async_copy(zbuf, x_hbm.at[pl.ds(start, MOE_ROWS)], zsem)

        def start_tail(e, c):
            @pl.when(nc_ref[e] > 0)
            def _():
                tail(e).start()
            return c

        def wait_tail(e, c):
            @pl.when(nc_ref[e] > 0)
            def _():
                tail(e).wait()
            return c

        lax.fori_loop(0, N_EXPERTS, start_tail, 0)
        lax.fori_loop(0, N_EXPERTS, wait_tail, 0)

        def unused(c):
            return pltpu.make_async_copy(zbuf, x_hbm.at[pl.ds(pl.multiple_of(c * MOE_ROWS, MOE_ROWS), MOE_ROWS)], zsem)

        def start_unused(c, carry):
            unused(c).start()
            return carry

        def wait_unused(c, carry):
            unused(c).wait()
            return carry

        first_unused = pe_ref[N_EXPERTS - 1] // MOE_ROWS
        lax.fori_loop(first_unused, x_hbm.shape[0] // MOE_ROWS, start_unused, 0)
        lax.fori_loop(first_unused, x_hbm.shape[0] // MOE_ROWS, wait_unused, 0)

    def scatter(h2p_ref):
        for t in range(tokens):
            for k in range(TOP_K):
                d = dest_ref[0, 0, t * TOP_K + k]
                pltpu.make_async_copy(h2p_ref.at[pl.ds(t, 1)], x_hbm.at[pl.ds(d, 1)], sem).start(priority=k % 2)
        for _ in range(TOP_K):
            pltpu.make_async_copy(h2p_ref, x_hbm.at[pl.ds(0, tokens)], sem).wait()

    @pl.when(i < n_first)
    def _():
        scatter(h2p_a_ref)

    @pl.when(i >= n_first)
    def _():
        scatter(h2p_b_ref)


def _dispatch(pad_end, n_chunks, dest, h2p_a, h2p_b):
    tokens = DISPATCH_TOKENS
    tile = h2p_a.shape[1:]
    n_a, n_b = h2p_a.shape[0] // tokens, h2p_b.shape[0] // tokens
    assert n_a * tokens == h2p_a.shape[0] and n_b * tokens == h2p_b.shape[0]
    n_steps = n_a + n_b
    grid_spec = pltpu.PrefetchScalarGridSpec(
        num_scalar_prefetch=2,
        grid=(n_steps,),
        in_specs=[pl.BlockSpec((1, 1, tokens * TOP_K), lambda i, pe, nc: (i, 0, 0), memory_space=pltpu.SMEM),
                  pl.BlockSpec((tokens,) + tile, lambda i, pe, nc: (jnp.minimum(i, n_a - 1), 0)),
                  pl.BlockSpec((tokens,) + tile, lambda i, pe, nc: (jnp.maximum(i - n_a, 0), 0))],
        out_specs=pl.BlockSpec(memory_space=pl.ANY),
        scratch_shapes=[pltpu.VMEM((MOE_ROWS,) + tile, jnp.uint32), pltpu.SemaphoreType.DMA(()),
                        pltpu.SemaphoreType.DMA(())],
    )
    return pl.pallas_call(
        functools.partial(_dispatch_kernel, tokens=tokens, n_first=n_a),
        grid_spec=grid_spec,
        out_shape=jax.ShapeDtypeStruct((_n_sorted_rows(n_steps * tokens),) + tile, jnp.uint32),
        compiler_params=_params(("arbitrary",), 24 << 20),
        name="moe_dispatch",
    )(pad_end, n_chunks, dest.reshape(n_steps, 1, tokens * TOP_K), h2p_a, h2p_b)


def _on_static(value, n, fn):
    for v in range(n):
        @pl.when(value == v)
        def _():
            fn(v)


def _moe_up_kernel(be_ref, first_ref, next_ref, wrap_ref, nl_ref, x_ref, w_hbm, bg_ref, bu_ref, act_ref,
                   stage, w16, sem, *, n_j):
    j = pl.program_id(0)
    b = pl.program_id(1)
    e = be_ref[b]
    live = b < nl_ref[0]
    tf = w16.shape[2]

    def weight_copies(ee, jj):
        return [pltpu.make_async_copy(w_hbm.at[ee, :, g * D_FF + jj * tf:g * D_FF + (jj + 1) * tf], stage.at[g], sem)
                for g in range(2)]

    def start_weights(ee, jd):
        _on_static(jd, n_j, lambda jj: [cp.start() for cp in weight_copies(ee, jj)])

    @pl.when(jnp.logical_and(j == 0, b == 0))
    def _():
        start_weights(e, j)

    @pl.when(jnp.logical_and(live, first_ref[b] == 1))
    def _():
        _on_static(j, n_j, lambda jj: [cp.wait() for cp in weight_copies(e, jj)])
        w16[...] = stage[...].astype(bf16)
        j_next = j + wrap_ref[e]

        @pl.when(j_next < n_j)
        def _():
            start_weights(next_ref[e], j_next)

    @pl.when(live)
    def _():
        lo, hi = _unpack_pairs(x_ref[...])
        x_lo = lo.astype(bf16)
        x_hi = hi.astype(bf16)
        half = x_lo.shape[1]

        def proj(g, b_ref):
            return (jnp.dot(x_lo, w16[g, 0:half, :], preferred_element_type=f32)
                    + jnp.dot(x_hi, w16[g, half:, :], preferred_element_type=f32) + b_ref[0])

        gate = jnp.minimum(proj(0, bg_ref), SWIGLU_LIMIT)
        up = jnp.clip(proj(1, bu_ref), -SWIGLU_LIMIT, SWIGLU_LIMIT)
        act_ref[0] = ((up + 1.0) * gate * jax.nn.sigmoid(SWIGLU_ALPHA * gate)).astype(bf16)

    @pl.when(jnp.logical_not(live))
    def _():
        act_ref[...] = jnp.zeros_like(act_ref)


def _moe_up(sched, x_sorted, w_gate_up, b_gate_up):
    e, d, f2 = w_gate_up.shape
    tf = UP_TILE
    n_j = D_FF // tf
    rows = x_sorted.shape[0]
    n_blocks = rows // MOE_ROWS
    vmem = 2 * d * tf * 4 + 2 * d * tf * 2 + 2 * MOE_ROWS * d * 2 + 2 * MOE_ROWS * tf * 2 \
        + 8 * MOE_ROWS * tf * 4 + (6 << 20)
    grid_spec = pltpu.PrefetchScalarGridSpec(
        num_scalar_prefetch=5,
        grid=(n_j, n_blocks),
        in_specs=[pl.BlockSpec((MOE_ROWS, d // 2), lambda j, b, be, *_: (b, 0)),
                  pl.BlockSpec(memory_space=pl.ANY),
                  pl.BlockSpec((1, 1, tf), lambda j, b, be, *_: (be[b], 0, j)),
                  pl.BlockSpec((1, 1, tf), lambda j, b, be, *_: (be[b], 0, n_j + j))],
        out_specs=pl.BlockSpec((1, MOE_ROWS, tf), lambda j, b, be, *_: (j, b, 0)),
        scratch_shapes=[pltpu.VMEM((2, d, tf), f32), pltpu.VMEM((2, d, tf), bf16), pltpu.SemaphoreType.DMA(())],
    )
    b3 = b_gate_up.reshape(e, 1, f2)
    return pl.pallas_call(
        functools.partial(_moe_up_kernel, n_j=n_j),
        grid_spec=grid_spec,
        out_shape=jax.ShapeDtypeStruct((n_j, rows, tf), bf16),
        compiler_params=_params(("arbitrary", "arbitrary"), vmem),
        name="moe_gate_up",
    )(*sched, x_sorted, w_gate_up, b3, b3)


def _moe_down_kernel(be_ref, first_ref, next_ref, wrap_ref, nl_ref, act_ref, w_hbm, bd_ref, y_ref, stage, w16, sem):
    b = pl.program_id(0)
    e = be_ref[b]
    live = b < nl_ref[0]
    n_k, _, tf = act_ref.shape

    def weight_copy(ee):
        return pltpu.make_async_copy(w_hbm.at[ee], stage, sem)

    @pl.when(b == 0)
    def _():
        weight_copy(e).start()

    @pl.when(jnp.logical_and(live, first_ref[b] == 1))
    def _():
        weight_copy(e).wait()
        w16[...] = stage[...].astype(bf16)

        @pl.when(wrap_ref[e] == 0)
        def _():
            weight_copy(next_ref[e]).start()

    @pl.when(live)
    def _():
        y = bd_ref[0]
        for k in range(n_k):
            y = y + jnp.dot(act_ref[k], w16[k * tf:(k + 1) * tf, :], preferred_element_type=f32)
        y16 = y.astype(bf16)
        half = y16.shape[1] // 2
        y_ref[...] = _pack_pairs(y16[:, :half], y16[:, half:])

    @pl.when(jnp.logical_not(live))
    def _():
        y_ref[...] = jnp.zeros_like(y_ref)


def _moe_down(sched, act, w_down, b_down):
    e, f, d = w_down.shape
    n_k, rows, tf = act.shape
    n_blocks = rows // MOE_ROWS
    vmem = f * d * 4 + f * d * 2 + 2 * MOE_ROWS * f * 2 + 2 * MOE_ROWS * d * 2 + 4 * MOE_ROWS * d * 4 + (6 << 20)
    grid_spec = pltpu.PrefetchScalarGridSpec(
        num_scalar_prefetch=5,
        grid=(n_blocks,),
        in_specs=[pl.BlockSpec((n_k, MOE_ROWS, tf), lambda b, be, *_: (0, b, 0)),
                  pl.BlockSpec(memory_space=pl.ANY),
                  pl.BlockSpec((1, 1, d), lambda b, be, *_: (be[b], 0, 0))],
        out_specs=pl.BlockSpec((MOE_ROWS, d // 2), lambda b, be, *_: (b, 0)),
        scratch_shapes=[pltpu.VMEM((f, d), f32), pltpu.VMEM((f, d), bf16), pltpu.SemaphoreType.DMA(())],
    )
    return pl.pallas_call(
        _moe_down_kernel,
        grid_spec=grid_spec,
        out_shape=jax.ShapeDtypeStruct((rows, d // 2), jnp.uint32),
        compiler_params=_params(("arbitrary",), vmem),
        name="moe_down",
    )(*sched, act, w_down, b_down.reshape(e, 1, d))


def _gather_rows(idx_ref, src_hbm, dst, sem, n_rows):
    for r in range(n_rows):
        t = idx_ref[0, 0, r]
        pltpu.make_async_copy(src_hbm.at[pl.ds(t, 1)], dst.at[pl.ds(r, 1)], sem).start(priority=r % 2)


def _wait_rows(src_hbm, dst, sem, n_rows):
    pltpu.make_async_copy(src_hbm.at[pl.ds(0, n_rows)], dst, sem).wait()


def _combine_kernel(pos_ref, posn_ref, y_hbm, gates_ref, xmid_ref, mod_ref, gpost_ref, out_ref, ybuf, sem, *, n_steps):
    i = pl.program_id(0)
    tm = COMBINE_TOKENS
    n_rows = TOP_K * tm
    slot = i % 2

    @pl.when(i == 0)
    def _():
        _gather_rows(pos_ref, y_hbm, ybuf.at[0], sem.at[0], n_rows)

    @pl.when(i + 1 < n_steps)
    def _():
        _gather_rows(posn_ref, y_hbm, ybuf.at[1 - slot], sem.at[1 - slot], n_rows)

    _wait_rows(y_hbm, ybuf.at[slot], sem.at[slot], n_rows)
    g = gates_ref[...]
    lo = hi = None
    for k in range(TOP_K):
        l_k, h_k = _unpack_pairs(ybuf[slot, k * tm:(k + 1) * tm, :])
        gk = g[:, k:k + 1]
        lo = gk * l_k if lo is None else lo + gk * l_k
        hi = gk * h_k if hi is None else hi + gk * h_k
    half = lo.shape[1]
    inv = lax.rsqrt((jnp.sum(lo * lo, axis=-1, keepdims=True) + jnp.sum(hi * hi, axis=-1, keepdims=True))
                    / (2 * half) + EPS)
    gate2 = mod_ref[0][5:6]
    gp = gpost_ref[...]
    out_ref[:, :half] = xmid_ref[:, :half] + gate2[:, :half] * (lo * inv * gp[:, :half])
    out_ref[:, half:] = xmid_ref[:, half:] + gate2[:, half:] * (hi * inv * gp[:, half:])


def _combine(pos, y_sorted, gates, xmid, mod, g_post, seq):
    n, d = xmid.shape
    tm = COMBINE_TOKENS
    n_tiles = n // tm
    per_seq = seq // tm
    vmem = 2 * TOP_K * tm * d * 2 + 4 * tm * d * 4 + 6 * tm * d * 4 + (8 << 20)
    return pl.pallas_call(
        functools.partial(_combine_kernel, n_steps=n_tiles),
        grid=(n_tiles,),
        in_specs=[pl.BlockSpec((1, 1, TOP_K * tm), lambda i: (i, 0, 0), memory_space=pltpu.SMEM),
                  pl.BlockSpec((1, 1, TOP_K * tm), lambda i: (jnp.minimum(i + 1, n_tiles - 1), 0, 0),
                               memory_space=pltpu.SMEM),
                  pl.BlockSpec(memory_space=pl.ANY),
                  pl.BlockSpec((tm, TOP_K), lambda i: (i, 0)),
                  pl.BlockSpec((tm, d), lambda i: (i, 0)),
                  pl.BlockSpec((1, 6, d), lambda i: (i // per_seq, 0, 0)),
                  _const_spec((1, d))],
        out_specs=pl.BlockSpec((tm, d), lambda i: (i, 0)),
        out_shape=jax.ShapeDtypeStruct((n, d), f32),
        scratch_shapes=[pltpu.VMEM((2, TOP_K * tm) + y_sorted.shape[1:], jnp.uint32), pltpu.SemaphoreType.DMA((2,))],
        compiler_params=_params(("arbitrary",), vmem),
        name="moe_combine",
    )(pos, pos, y_sorted, gates, xmid, mod, g_post.reshape(1, d))


def _rope_tables(pos):
    def cs(d):
        half = d // 2
        inv_freq = 1.0 / (ROPE_THETA ** (jnp.arange(half, dtype=f32) * (2.0 / d)))
        ang = pos.astype(f32)[:, None] * inv_freq[None, :]
        return jnp.cos(ang), jnp.sin(ang)
    c64, s64 = cs(MLA_ROPE)
    c256, s256 = cs(RET_QK)
    t64 = jnp.concatenate([c64, c64, s64, s64], axis=-1)
    t256 = jnp.concatenate([c256, s256], axis=-1)
    return t64, t256


def _rot_cols(w):
    half = w.shape[-1] // 2
    return jnp.concatenate([-w[..., half:], w[..., :half]], axis=-1)


def _prep_weights(w_in, w_uq, w_ukv, w_out, w_router, b_router):
    b = np.cumsum((Q_LORA, KV_LORA, MLA_ROPE))
    w_t = w_in.T
    w_pe = w_in[:, b[1]:b[2]]
    w_parts = (w_t[:b[1]].astype(bf16), jnp.concatenate([w_pe, _rot_cols(w_pe)], axis=-1).T.astype(bf16),
               w_t[b[2]:].astype(bf16))
    wq = w_uq.reshape(Q_LORA, MLA_HEADS, MLA_NOPE + MLA_ROPE)
    wq = jnp.concatenate([wq, _rot_cols(wq[..., MLA_NOPE:])], axis=-1).transpose(1, 0, 2).astype(bf16)
    wkv = w_ukv.reshape(KV_LORA, MLA_HEADS, MLA_NOPE + MLA_V).transpose(1, 0, 2).astype(bf16)
    r_hi = w_router.astype(bf16)
    r_lo = (w_router - r_hi.astype(f32)).astype(bf16)
    pad = ((0, 0), (0, LANES - N_EXPERTS))
    wr = jnp.concatenate([jnp.pad(r_hi, pad), jnp.pad(r_lo, pad)], axis=1)
    br = jnp.pad(b_router, (0, LANES - N_EXPERTS), constant_values=NEG_BIG).reshape(1, LANES)
    return w_parts, wq, wkv, w_out.astype(bf16), wr, br


def _mixer_half(x, mod, pos, wts, p, past_lat16, past_kpe16, state0, counts0, ret_chunk, q_tiles, tm):
    w_parts, wq, wkv, w_out16, wr, br = wts
    b, s, d = x.shape
    x2d = x.reshape(b * s, d)
    t64, t256 = _rope_tables(pos)
    qlat, kvlat, kvlat16, kpe, kpe16, rq, rk, rv, rg = _inproj(
        x2d, mod, p['g_pre_mix'], w_parts, p['g_q_lat'], p['g_kv_lat'], t64, t256, s, tm)
    lat16 = kvlat16.reshape(b, s, KV_LORA)
    kpe16 = kpe16.reshape(b, s, 2 * MLA_ROPE)
    if past_lat16 is not None:
        lat16 = jnp.concatenate([past_lat16, lat16], axis=1)
        kpe16 = jnp.concatenate([past_kpe16, kpe16], axis=1)
    attn = _mla(qlat.reshape(b, s, Q_LORA), lat16, kpe16, wq, wkv, t64, q_tiles)
    log_g = jnp.log1p(-jnp.exp2(-5.0 - jnp.arange(RET_HEADS, dtype=f32)))
    log_g = jnp.broadcast_to(log_g[:, None, None], (RET_HEADS, 8, LANES))
    sh = lambda t, w: t.reshape(b, s, w)
    rn, state = _retention(sh(rq, RET_HEADS * RET_QK), sh(rk, RET_HEADS * RET_QK), sh(rv, RET_WIDTH),
                           sh(rg, RET_WIDTH), p['g_ret'], state0, log_g, ret_chunk)
    xmid, h2p, idx, gates, rank, counts = _mixout(
        attn.reshape(b * s, MLA_WIDTH), rn.reshape(b * s, RET_WIDTH), x2d, mod,
        p['g_post_mix'], p['g_pre_ffn'], w_out16, wr, br, counts0, s, tm)
    route = (idx[:, :TOP_K], gates[:, :TOP_K], rank[:, :TOP_K], counts)
    return kvlat.reshape(b, s, KV_LORA), kpe.reshape(b, s, MLA_ROPE), state, xmid, h2p, route


def _combine_order(dest):
    tm = COMBINE_TOKENS
    n_tiles = dest.shape[0] // tm
    return dest.reshape(n_tiles, tm, TOP_K).transpose(0, 2, 1).reshape(n_tiles, 1, TOP_K * tm)


def kernel(x_prompt, x_sample, cache_kv_latent, cache_k_rope, state_retention, c_prompt, c_sample, w_ada, b_ada, g_pre_mix, g_post_mix, g_pre_ffn, g_post_ffn, w_in, g_q_lat, g_kv_lat, w_uq, w_ukv, g_ret, w_out, w_router, b_router, w_gate_up, b_gate_up, w_down, b_down):
    depth = w_in.shape[0]
    assert depth == 1, "the staged problem has a single layer"
    bp, sp, d = x_prompt.shape
    bs, ss, _ = x_sample.shape
    past = cache_kv_latent.shape[2]
    l = 0
    p = dict(g_pre_mix=g_pre_mix[l], g_post_mix=g_post_mix[l], g_pre_ffn=g_pre_ffn[l], g_post_ffn=g_post_ffn[l],
             g_q_lat=g_q_lat[l], g_kv_lat=g_kv_lat[l], g_ret=g_ret[l])
    wts = _prep_weights(w_in[l], w_uq[l], w_ukv[l], w_out[l], w_router[l], b_router[l])

    mod = _ada(jnp.concatenate([c_prompt, c_sample], axis=0), w_ada[l], b_ada[l]).reshape(bp + bs, 6, d)
    mod_p, mod_s = mod[:bp], mod[bp:]

    tq = ATTN_Q_TILE
    tiles_p = tuple((q0, tq, q0 + tq, True) for q0 in range(0, sp, tq))
    zero_state = jnp.zeros((bp, RET_HEADS, RET_QK, RET_V), f32)
    lat_p, kpe_p, st_p, xmid_p, h2p_p, (idx_p, gates_p, rank_p, counts_p) = _mixer_half(
        x_prompt, mod_p, jnp.arange(sp), wts, p, None, None, zero_state, jnp.zeros((1, LANES), f32),
        RET_CHUNK_PROMPT, tiles_p, 512)

    past_lat16 = cache_kv_latent[l].astype(bf16)
    past_kpe16 = jnp.pad(cache_k_rope[l], ((0, 0), (0, 0), (0, MLA_ROPE))).astype(bf16)
    tiles_s = ((0, ss, past + ss, False),)
    lat_s, kpe_s, st_s, xmid_s, h2p_s, (idx_s, gates_s, rank_s, counts) = _mixer_half(
        x_sample, mod_s, past + jnp.arange(ss), wts, p, past_lat16, past_kpe16, state_retention[l], counts_p,
        ss, tiles_s, ss)

    n_p = bp * sp
    dest, pad_end, n_chunks = _route(
        jnp.concatenate([idx_p, idx_s], axis=0), jnp.concatenate([rank_p, rank_s], axis=0), counts)
    x_sorted = _dispatch(pad_end, n_chunks, dest, h2p_p, h2p_s)
    sched = _block_schedule(n_chunks, x_sorted.shape[0] // MOE_ROWS)
    act = _moe_up(sched, x_sorted, w_gate_up[l], b_gate_up[l])
    y_sorted = _moe_down(sched, act, w_down[l], b_down[l])
    y_p = _combine(_combine_order(dest[:n_p]), y_sorted, gates_p, xmid_p, mod_p, p['g_post_ffn'], sp)
    y_s = _combine(_combine_order(dest[n_p:]), y_sorted, gates_s, xmid_s, mod_s, p['g_post_ffn'], ss)

    return (y_p.reshape(bp, sp, d), y_s.reshape(bs, ss, d),
            lat_p[None], kpe_p[None], st_p[None].astype(state_retention.dtype),
            lat_s[None], kpe_s[None], st_s[None].astype(state_retention.dtype))
```

```python
import functools

import numpy as np
import jax
import jax.numpy as jnp
from jax import lax
from jax.experimental import pallas as pl
from jax.experimental.pallas import tpu as pltpu

CHUNK = 64
EPS = 1e-6
ROPE_THETA = 10000.0

MLA_HEADS = 8
MLA_NOPE = 128
MLA_ROPE = 64
MLA_V = 128
Q_LORA = 512
KV_LORA = 512
MLA_WIDTH = MLA_HEADS * MLA_V

RET_HEADS = 4
RET_QK = 256
RET_V = 256
RET_WIDTH = RET_HEADS * RET_V

N_EXPERTS = 32
TOP_K = 4
D_FF = 2048
SWIGLU_LIMIT = 7.0
SWIGLU_ALPHA = 1.702

LANES = 128
V7X_VMEM_BYTES = 64 * 1024 * 1024
VMEM_CAP_BYTES = V7X_VMEM_BYTES - 8 * 1024 * 1024

RET_CHUNK_PROMPT = 128
RET_UNROLL = 16
ATTN_Q_TILE = 256
MOE_ROWS = 256
UP_TILE = 1024
DISPATCH_TOKENS = 256
COMBINE_TOKENS = 64
NEG_BIG = -1e30

bf16 = jnp.bfloat16
f32 = jnp.float32


def _params(sem, vmem_bytes):
    return pltpu.CompilerParams(dimension_semantics=sem, vmem_limit_bytes=int(min(vmem_bytes, VMEM_CAP_BYTES)))


def _const_spec(shape):
    nd = len(shape)
    return pl.BlockSpec(shape, lambda *_: (0,) * nd, pipeline_mode=pl.Buffered(1))


def _rms(x):
    return x * lax.rsqrt(jnp.mean(x * x, axis=-1, keepdims=True) + EPS)


def _silu(x):
    return x * jax.nn.sigmoid(x)


def _ada_kernel(c_ref, w_ref, b_ref, o_ref):
    s = _silu(c_ref[...]).astype(bf16)
    o_ref[...] = jnp.dot(s, w_ref[...].astype(bf16), preferred_element_type=f32) + b_ref[...]


def _ada(c, w_ada, b_ada):
    nb, d = c.shape
    n = w_ada.shape[1]
    tn = 1536
    return pl.pallas_call(
        _ada_kernel,
        grid=(n // tn,),
        in_specs=[pl.BlockSpec((nb, d), lambda j: (0, 0)),
                  pl.BlockSpec((d, tn), lambda j: (0, j)),
                  pl.BlockSpec((1, tn), lambda j: (0, j))],
        out_specs=pl.BlockSpec((nb, tn), lambda j: (0, j)),
        out_shape=jax.ShapeDtypeStruct((nb, n), f32),
        compiler_params=_params(("arbitrary",), 2 * d * tn * 4 + d * tn * 2 + (8 << 20)),
        name="ada_modulation",
    )(c, w_ada, b_ada.reshape(1, n))


_C_RK = RET_HEADS * RET_QK
_C_RV = _C_RK + RET_HEADS * RET_QK
_C_RG = _C_RV + RET_WIDTH
_C_END = _C_RG + RET_WIDTH


def _rope_pair(t):
    return t + pltpu.roll(t, MLA_ROPE, axis=1)


def _rope_heads(z, cos, sin, scale):
    outs = []
    half = RET_QK // 2
    for h in range(RET_HEADS):
        x1 = z[:, h * RET_QK:h * RET_QK + half]
        x2 = z[:, h * RET_QK + half:(h + 1) * RET_QK]
        outs.append((x1 * cos - x2 * sin) * scale)
        outs.append((x1 * sin + x2 * cos) * scale)
    return jnp.concatenate(outs, axis=-1)


def _per_sequence(tile_rows, mod_ref):
    n_seq = mod_ref.shape[0]
    rows = tile_rows // n_seq
    return [(slice(i * rows, (i + 1) * rows), mod_ref[i]) for i in range(n_seq)]


def _tile_geometry(seq, tm):
    assert seq % tm == 0 or tm % seq == 0
    return max(seq // tm, 1), max(tm // seq, 1)


def _inproj_kernel(x_ref, mod_ref, g_ref, wlat_ref, wpe_ref, wret_ref, gq_ref, gkv_ref, t64_ref, t256_ref,
                   qlat_ref, kvlat_ref, kvlat16_ref, kpe_ref, kpe16_ref, rq_ref, rk_ref, rv_ref, rg_ref):
    hs = []
    for rows, m in _per_sequence(x_ref.shape[0], mod_ref):
        hs.append((_rms(x_ref[rows, :]) * (g_ref[...] * (1.0 + m[1:2])) + m[0:1]).astype(bf16))
    h = hs[0] if len(hs) == 1 else jnp.concatenate(hs, axis=0)

    def proj(w_ref, c0, c1):
        return lax.dot_general(h, w_ref[c0:c1, :], (((1,), (1,)), ((), ())), preferred_element_type=f32)

    qlat_ref[...] = (_rms(proj(wlat_ref, 0, Q_LORA)) * gq_ref[...]).astype(bf16)
    kv = _rms(proj(wlat_ref, Q_LORA, Q_LORA + KV_LORA)) * gkv_ref[...]
    kvlat_ref[...] = kv
    kvlat16_ref[...] = kv.astype(bf16)
    pe = _rope_pair(proj(wpe_ref, 0, 2 * MLA_ROPE) * t64_ref[...])
    kpe_ref[...] = pe[:, :MLA_ROPE]
    lane = lax.broadcasted_iota(jnp.int32, pe.shape, 1)
    kpe16_ref[...] = jnp.where(lane < MLA_ROPE, pe, 0.0).astype(bf16)
    cos = t256_ref[:, :RET_QK // 2]
    sin = t256_ref[:, RET_QK // 2:]
    rq_ref[...] = _rope_heads(proj(wret_ref, 0, _C_RK), cos, sin, 1.0).astype(bf16)
    rk_ref[...] = _rope_heads(proj(wret_ref, _C_RK, _C_RV), cos, sin, RET_QK ** -0.5).astype(bf16)
    rv_ref[...] = proj(wret_ref, _C_RV, _C_RG).astype(bf16)
    rg_ref[...] = proj(wret_ref, _C_RG, _C_END).astype(bf16)


def _inproj(x2d, mod, g_pre, w_parts, g_q, g_kv, t64, t256, seq, tm):
    n, d = x2d.shape
    per_seq, n_seq = _tile_geometry(seq, tm)
    t64, t256 = jnp.tile(t64, (n_seq, 1)), jnp.tile(t256, (n_seq, 1))
    row = lambda w: pl.BlockSpec((tm, w), lambda i: (i, 0))
    tab = lambda w: pl.BlockSpec((tm, w), lambda i: (i % per_seq, 0))
    outs = [(Q_LORA, bf16), (KV_LORA, f32), (KV_LORA, bf16), (MLA_ROPE, f32), (2 * MLA_ROPE, bf16),
            (RET_HEADS * RET_QK, bf16), (RET_HEADS * RET_QK, bf16), (RET_WIDTH, bf16), (RET_WIDTH, bf16)]
    vmem = sum(w.size for w in w_parts) * 2 + 2 * tm * d * 4 + 4 * tm * d * 4 + sum(2 * tm * w * 4 for w, _ in outs) + (6 << 20)
    return pl.pallas_call(
        _inproj_kernel,
        grid=(n // tm,),
        in_specs=[row(d),
                  pl.BlockSpec((n_seq, 6, d), lambda i: (i // per_seq, 0, 0)),
                  _const_spec((1, d)), *[_const_spec(w.shape) for w in w_parts],
                  _const_spec((1, Q_LORA)), _const_spec((1, KV_LORA)),
                  tab(2 * MLA_ROPE), tab(RET_QK)],
        out_specs=[row(w) for w, _ in outs],
        out_shape=[jax.ShapeDtypeStruct((n, w), dt) for w, dt in outs],
        compiler_params=_params(("arbitrary",), vmem),
        name="in_projection",
    )(x2d, mod, g_pre.reshape(1, d), *w_parts, g_q.reshape(1, Q_LORA), g_kv.reshape(1, KV_LORA), t64, t256)


def _mla_kernel(qlat_ref, lat_ref, kpe_ref, wq_ref, wkv_ref, tab_ref, o_ref, q_s, k_s, v_s, *, q_tiles):
    scale = (MLA_NOPE + MLA_ROPE) ** -0.5
    qh = jnp.dot(qlat_ref[0], wq_ref[0], preferred_element_type=f32)
    q_s[:, :MLA_NOPE] = (qh[:, :MLA_NOPE] * scale).astype(bf16)
    q_s[:, MLA_NOPE:] = (_rope_pair(qh[:, MLA_NOPE:] * tab_ref[...]) * scale).astype(bf16)
    kv = jnp.dot(lat_ref[0], wkv_ref[0], preferred_element_type=f32)
    k_s[:, :MLA_NOPE] = kv[:, :MLA_NOPE].astype(bf16)
    k_s[:, MLA_NOPE:] = kpe_ref[0]
    v_s[...] = kv[:, MLA_NOPE:].astype(bf16)
    for q0, ql, kvl, masked in q_tiles:
        q = q_s[q0:q0 + ql, :]
        s = lax.dot_general(q, k_s[0:kvl, :], (((1,), (1,)), ((), ())), preferred_element_type=f32)
        if masked:
            assert kvl == q0 + ql and q0 % CHUNK == 0
            qc = lax.broadcasted_iota(jnp.int32, (ql, 1), 0) // CHUNK
            kc = lax.broadcasted_iota(jnp.int32, (1, ql), 1) // CHUNK
            diag = jnp.where(kc <= qc, s[:, q0:], NEG_BIG)
            s = diag if q0 == 0 else jnp.concatenate([s[:, :q0], diag], axis=1)
        p = jnp.exp(s - jnp.max(s, axis=-1, keepdims=True))
        l = jnp.sum(p, axis=-1, keepdims=True)
        o = jnp.dot(p.astype(bf16), v_s[0:kvl, :], preferred_element_type=f32)
        o_ref[0, q0:q0 + ql, :] = (o / l).astype(bf16)


def _mla(qlat, lat, kpe16, wq, wkv, tab, q_tiles):
    b, sq, _ = qlat.shape
    skv = lat.shape[1]
    kern = functools.partial(_mla_kernel, q_tiles=q_tiles)
    max_ql = max(t[1] for t in q_tiles)
    vmem = 2 * 2 * (sq * Q_LORA + skv * KV_LORA + skv * LANES) + sq * LANES * 4 + sq * 3 * LANES * 4 \
        + (sq + skv) * 3 * LANES * 2 + 4 * max_ql * skv * 4 + (8 << 20)
    return pl.pallas_call(
        kern,
        grid=(b, MLA_HEADS),
        in_specs=[pl.BlockSpec((1, sq, Q_LORA), lambda i, h: (i, 0, 0)),
                  pl.BlockSpec((1, skv, KV_LORA), lambda i, h: (i, 0, 0)),
                  pl.BlockSpec((1, skv, 2 * MLA_ROPE), lambda i, h: (i, 0, 0)),
                  pl.BlockSpec((1, Q_LORA, MLA_NOPE + 2 * MLA_ROPE), lambda i, h: (h, 0, 0)),
                  pl.BlockSpec((1, KV_LORA, MLA_NOPE + MLA_V), lambda i, h: (h, 0, 0)),
                  pl.BlockSpec((sq, 2 * MLA_ROPE), lambda i, h: (0, 0))],
        out_specs=pl.BlockSpec((1, sq, MLA_V), lambda i, h: (i, 0, h)),
        out_shape=jax.ShapeDtypeStruct((b, sq, MLA_WIDTH), bf16),
        scratch_shapes=[pltpu.VMEM((sq, MLA_NOPE + 2 * MLA_ROPE), bf16),
                        pltpu.VMEM((skv, MLA_NOPE + 2 * MLA_ROPE), bf16),
                        pltpu.VMEM((skv, MLA_V), bf16)],
        compiler_params=_params(("arbitrary", "arbitrary"), vmem),
        name="mla_attention",
    )(qlat, lat, kpe16, wq, wkv, tab)


def _ret_kernel(lg_ref, rq_ref, rk_ref, rv_ref, rg_ref, gret_ref, s0_ref, o_ref, sout_ref, state, *, chunk, n_chunks):
    c = chunk
    lg = lg_ref[0, 0:1, 0:1]
    ii = lax.broadcasted_iota(jnp.int32, (c, c), 0)
    jj = lax.broadcasted_iota(jnp.int32, (c, c), 1)
    diff = (ii - jj).astype(f32)
    dmask = jnp.where(diff >= 0.0, jnp.exp(jnp.maximum(diff, 0.0) * lg), 0.0)
    ic = lax.broadcasted_iota(jnp.int32, (c, 1), 0).astype(f32)
    q_decay = jnp.exp((ic + 1.0) * lg)
    k_decay = jnp.exp((c - 1.0 - ic) * lg)
    chunk_decay = jnp.exp(float(c) * lg)
    gret = gret_ref[...]
    state[...] = s0_ref[0, 0]

    def step(n, carry):
        sl = pl.ds(pl.multiple_of(n * c, c), c)
        q = rq_ref[0, sl, :]
        k = rk_ref[0, sl, :]
        v = rv_ref[0, sl, :]
        st = state[...]
        attn = lax.dot_general(q, k, (((1,), (1,)), ((), ())), preferred_element_type=f32) * dmask
        inner = jnp.dot(attn.astype(bf16), v, preferred_element_type=f32)
        cross = jnp.dot(q, st.astype(bf16), preferred_element_type=f32) * q_decay
        o = inner + cross
        kd_t = (k.astype(f32) * k_decay).T.astype(bf16)
        state[...] = st * chunk_decay + jnp.dot(kd_t, v, preferred_element_type=f32)
        mu = jnp.mean(o, axis=-1, keepdims=True)
        oc = o - mu
        rn = oc * lax.rsqrt(jnp.mean(oc * oc, axis=-1, keepdims=True) + EPS) * gret
        o_ref[0, sl, :] = (rn * _silu(rg_ref[0, sl, :].astype(f32))).astype(bf16)
        return carry

    lax.fori_loop(0, n_chunks, step, 0, unroll=min(n_chunks, RET_UNROLL))
    sout_ref[0, 0] = state[...]


def _retention(rq, rk, rv, rg, g_ret, state0, log_g, chunk):
    b, s, _ = rq.shape
    kern = functools.partial(_ret_kernel, chunk=chunk, n_chunks=s // chunk)
    head = lambda w: pl.BlockSpec((1, s, w), lambda i, h: (i, 0, h))
    st = pl.BlockSpec((1, 1, RET_QK, RET_V), lambda i, h: (i, h, 0, 0))
    vmem = 2 * 5 * s * RET_QK * 2 + 5 * RET_QK * RET_V * 4 + (8 << 20)
    return pl.pallas_call(
        kern,
        grid=(b, RET_HEADS),
        in_specs=[pl.BlockSpec((1, 8, LANES), lambda i, h: (h, 0, 0)),
                  head(RET_QK), head(RET_QK), head(RET_V), head(RET_V),
                  pl.BlockSpec((1, RET_V), lambda i, h: (0, h)), st],
        out_specs=[head(RET_V), st],
        out_shape=[jax.ShapeDtypeStruct((b, s, RET_WIDTH), bf16),
                   jax.ShapeDtypeStruct((b, RET_HEADS, RET_QK, RET_V), f32)],
        scratch_shapes=[pltpu.VMEM((RET_QK, RET_V), f32)],
        compiler_params=_params(("arbitrary", "arbitrary"), vmem),
        name="retention",
    )(log_g, rq, rk, rv, rg, g_ret.reshape(1, RET_WIDTH), state0)


def _pack_pairs(lo16, hi16):
    lo = lax.bitcast_convert_type(lo16.astype(f32), jnp.uint32)
    hi = lax.bitcast_convert_type(hi16.astype(f32), jnp.uint32)
    return (hi & jnp.uint32(0xFFFF0000)) | (lo >> 16)


def _unpack_pairs(u):
    lo = lax.bitcast_convert_type(u << 16, f32)
    hi = lax.bitcast_convert_type(u & jnp.uint32(0xFFFF0000), f32)
    return lo, hi


def _mixout_kernel(attn_ref, rn_ref, x_ref, mod_ref, gpost_ref, gpre_ref, wout_ref, wr_ref, br_ref, cnt0_ref, tri_ref,
                   xmid_ref, h2p_ref, idx_ref, gate_ref, rank_ref, cnt_ref, carry):
    @pl.when(pl.program_id(0) == 0)
    def _():
        carry[...] = cnt0_ref[...]

    mix = (jnp.dot(attn_ref[...], wout_ref[0:MLA_WIDTH, :], preferred_element_type=f32)
           + jnp.dot(rn_ref[...], wout_ref[MLA_WIDTH:, :], preferred_element_type=f32))
    h2s = []
    for rows, m in _per_sequence(mix.shape[0], mod_ref):
        x1 = x_ref[rows, :] + _rms(mix[rows]) * (m[2:3] * gpost_ref[...])
        xmid_ref[rows, :] = x1
        h2s.append(_rms(x1) * (gpre_ref[...] * (1.0 + m[4:5])) + m[3:4])
    h2 = h2s[0] if len(h2s) == 1 else jnp.concatenate(h2s, axis=0)
    h_hi = h2.astype(bf16)
    half = h2.shape[1] // 2
    h2p_ref[...] = _pack_pairs(h_hi[:, :half], h_hi[:, half:])
    h_lo = (h2 - h_hi.astype(f32)).astype(bf16)
    hi_terms = jnp.dot(h_hi, wr_ref[...], preferred_element_type=f32)
    logits = (hi_terms[:, :LANES] + jnp.dot(h_lo, wr_ref[:, :LANES], preferred_element_type=f32)
              + hi_terms[:, LANES:]) + br_ref[...]
    col = lax.broadcasted_iota(jnp.int32, logits.shape, 1).astype(f32)
    vals, idxs = [], []
    for _ in range(TOP_K):
        mx = jnp.max(logits, axis=-1, keepdims=True)
        ix = jnp.min(jnp.where(logits == mx, col, float(LANES)), axis=-1, keepdims=True)
        vals.append(mx)
        idxs.append(ix)
        logits = jnp.where(col == ix, -jnp.inf, logits)
    es = [jnp.exp(v - vals[0]) for v in vals]
    tot = es[0] + es[1] + es[2] + es[3]
    onehot = jnp.zeros_like(col)
    for k in range(TOP_K):
        onehot = jnp.where(col == idxs[k], 1.0, onehot)
    before = jnp.dot(tri_ref[...], onehot.astype(bf16), preferred_element_type=f32) + carry[...]
    idx_out = jnp.zeros_like(col)
    gate_out = jnp.zeros_like(col)
    rank_out = jnp.zeros_like(col)
    for k in range(TOP_K):
        slot = col == float(k)
        idx_out = jnp.where(slot, idxs[k], idx_out)
        gate_out = jnp.where(slot, es[k] / tot, gate_out)
        rank_k = jnp.sum(jnp.where(col == idxs[k], before, 0.0), axis=-1, keepdims=True)
        rank_out = jnp.where(slot, rank_k, rank_out)
    idx_ref[...] = idx_out.astype(jnp.int32)
    gate_ref[...] = gate_out
    rank_ref[...] = rank_out.astype(jnp.int32)
    carry[...] = carry[...] + jnp.sum(onehot, axis=0, keepdims=True)
    cnt_ref[...] = carry[...]


def _mixout(attn, rn, x2d, mod, g_post, g_pre, w_out16, wr, br, counts0, seq, tm):
    n, d = x2d.shape
    per_seq, n_seq = _tile_geometry(seq, tm)
    row = lambda w: pl.BlockSpec((tm, w), lambda i: (i, 0))
    tri = jnp.tril(jnp.ones((tm, tm), bf16), -1)
    vmem = w_out16.size * 2 + 2 * tm * d * (4 + 4 + 2 + 2) + 6 * tm * d * 4 + (8 << 20)
    return pl.pallas_call(
        _mixout_kernel,
        grid=(n // tm,),
        in_specs=[row(MLA_WIDTH), row(RET_WIDTH), row(d),
                  pl.BlockSpec((n_seq, 6, d), lambda i: (i // per_seq, 0, 0)),
                  _const_spec((1, d)), _const_spec((1, d)), _const_spec(w_out16.shape),
                  _const_spec(wr.shape), _const_spec((1, LANES)), _const_spec((1, LANES)), _const_spec((tm, tm))],
        out_specs=[row(d), row(d // 2), row(LANES), row(LANES), row(LANES),
                   pl.BlockSpec((1, LANES), lambda i: (0, 0))],
        out_shape=[jax.ShapeDtypeStruct((n, d), f32), jax.ShapeDtypeStruct((n, d // 2), jnp.uint32),
                   jax.ShapeDtypeStruct((n, LANES), jnp.int32), jax.ShapeDtypeStruct((n, LANES), f32),
                   jax.ShapeDtypeStruct((n, LANES), jnp.int32), jax.ShapeDtypeStruct((1, LANES), f32)],
        scratch_shapes=[pltpu.VMEM((1, LANES), f32)],
        compiler_params=_params(("arbitrary",), vmem),
        name="mixer_out_router",
    )(attn, rn, x2d, mod, g_post.reshape(1, d), g_pre.reshape(1, d), w_out16, wr, br, counts0, tri)


def _route(top_idx, rank, counts):
    counts = counts[0, :N_EXPERTS].astype(jnp.int32)
    padded = (counts + MOE_ROWS - 1) // MOE_ROWS * MOE_ROWS
    pad_end = jnp.cumsum(padded).astype(jnp.int32)
    pad_start = pad_end - padded
    dest = pad_start[top_idx] + rank
    return dest, pad_end, padded // MOE_ROWS


def _n_sorted_rows(n_tokens):
    return (n_tokens * TOP_K + N_EXPERTS * (MOE_ROWS - 1)) // MOE_ROWS * MOE_ROWS


def _block_schedule(n_chunks, n_blocks):
    ends = jnp.cumsum(n_chunks)
    n_live = ends[-1]
    blk = jnp.arange(n_blocks, dtype=jnp.int32)
    block_e = jnp.sum(jnp.minimum(blk, n_live - 1)[:, None] >= ends[None, :], axis=1).astype(jnp.int32)
    is_first = jnp.logical_and(blk == (ends - n_chunks)[block_e], blk < n_live).astype(jnp.int32)
    ids = jnp.arange(N_EXPERTS, dtype=jnp.int32)
    cand = jnp.where(n_chunks > 0, ids, N_EXPERTS)
    from_here = lax.cummin(cand[::-1])[::-1]
    after = jnp.concatenate([from_here[1:], jnp.full((1,), N_EXPERTS, jnp.int32)])
    wraps = (after >= N_EXPERTS).astype(jnp.int32)
    next_e = jnp.where(after < N_EXPERTS, after, from_here[0]).astype(jnp.int32)
    return block_e, is_first, next_e, wraps, n_live.reshape(1).astype(jnp.int32)


def _dispatch_kernel(pe_ref, nc_ref, dest_ref, h2p_a_ref, h2p_b_ref, x_hbm, zbuf, sem, zsem, *, tokens, n_first):
    i = pl.program_id(0)

    @pl.when(i == 0)
    def _():
        zbuf[...] = jnp.zeros_like(zbuf)

        def tail(e):
            start = pl.multiple_of(pe_ref[e] - MOE_ROWS, MOE_ROWS)
            return pltpu.make_async_copy(zbuf, x_hbm.at[pl.ds(start, MOE_ROWS)], zsem)

        def start_tail(e, c):
            @pl.when(nc_ref[e] > 0)
            def _():
                tail(e).start()
            return c

        def wait_tail(e, c):
            @pl.when(nc_ref[e] > 0)
            def _():
                tail(e).wait()
            return c

        lax.fori_loop(0, N_EXPERTS, start_tail, 0)
        lax.fori_loop(0, N_EXPERTS, wait_tail, 0)

        def unused(c):
            return pltpu.make_async_copy(zbuf, x_hbm.at[pl.ds(pl.multiple_of(c * MOE_ROWS, MOE_ROWS), MOE_ROWS)], zsem)

        def start_unused(c, carry):
            unused(c).start()
            return carry

        def wait_unused(c, carry):
            unused(c).wait()
            return carry

        first_unused = pe_ref[N_EXPERTS - 1] // MOE_ROWS
        lax.fori_loop(first_unused, x_hbm.shape[0] // MOE_ROWS, start_unused, 0)
        lax.fori_loop(first_unused, x_hbm.shape[0] // MOE_ROWS, wait_unused, 0)

    def scatter(h2p_ref):
        for t in range(tokens):
            for k in range(TOP_K):
                d = dest_ref[0, 0, t * TOP_K + k]
                pltpu.make_async_copy(h2p_ref.at[pl.ds(t, 1)], x_hbm.at[pl.ds(d, 1)], sem).start(priority=k % 2)
        for _ in range(TOP_K):
            pltpu.make_async_copy(h2p_ref, x_hbm.at[pl.ds(0, tokens)], sem).wait()

    @pl.when(i < n_first)
    def _():
        scatter(h2p_a_ref)

    @pl.when(i >= n_first)
    def _():
        scatter(h2p_b_ref)


def _dispatch(pad_end, n_chunks, dest, h2p_a, h2p_b):
    tokens = DISPATCH_TOKENS
    tile = h2p_a.shape[1:]
    n_a, n_b = h2p_a.shape[0] // tokens, h2p_b.shape[0] // tokens
    assert n_a * tokens == h2p_a.shape[0] and n_b * tokens == h2p_b.shape[0]
    n_steps = n_a + n_b
    grid_spec = pltpu.PrefetchScalarGridSpec(
        num_scalar_prefetch=2,
        grid=(n_steps,),
        in_specs=[pl.BlockSpec((1, 1, tokens * TOP_K), lambda i, pe, nc: (i, 0, 0), memory_space=pltpu.SMEM),
                  pl.BlockSpec((tokens,) + tile, lambda i, pe, nc: (jnp.minimum(i, n_a - 1), 0)),
                  pl.BlockSpec((tokens,) + tile, lambda i, pe, nc: (jnp.maximum(i - n_a, 0), 0))],
        out_specs=pl.BlockSpec(memory_space=pl.ANY),
        scratch_shapes=[pltpu.VMEM((MOE_ROWS,) + tile, jnp.uint32), pltpu.SemaphoreType.DMA(()),
                        pltpu.SemaphoreType.DMA(())],
    )
    return pl.pallas_call(
        functools.partial(_dispatch_kernel, tokens=tokens, n_first=n_a),
        grid_spec=grid_spec,
        out_shape=jax.ShapeDtypeStruct((_n_sorted_rows(n_steps * tokens),) + tile, jnp.uint32),
        compiler_params=_params(("arbitrary",), 24 << 20),
        name="moe_dispatch",
    )(pad_end, n_chunks, dest.reshape(n_steps, 1, tokens * TOP_K), h2p_a, h2p_b)


def _on_static(value, n, fn):
    for v in range(n):
        @pl.when(value == v)
        def _():
            fn(v)


def _moe_up_kernel(be_ref, first_ref, next_ref, wrap_ref, nl_ref, x_ref, w_hbm, bg_ref, bu_ref, act_ref,
                   stage, w16, sem, *, n_j):
    j = pl.program_id(0)
    b = pl.program_id(1)
    e = be_ref[b]
    live = b < nl_ref[0]
    tf = w16.shape[2]

    def weight_copies(ee, jj):
        return [pltpu.make_async_copy(w_hbm.at[ee, :, g * D_FF + jj * tf:g * D_FF + (jj + 1) * tf], stage.at[g], sem)
                for g in range(2)]

    def start_weights(ee, jd):
        _on_static(jd, n_j, lambda jj: [cp.start() for cp in weight_copies(ee, jj)])

    @pl.when(jnp.logical_and(j == 0, b == 0))
    def _():
        start_weights(e, j)

    @pl.when(jnp.logical_and(live, first_ref[b] == 1))
    def _():
        _on_static(j, n_j, lambda jj: [cp.wait() for cp in weight_copies(e, jj)])
        w16[...] = stage[...].astype(bf16)
        j_next = j + wrap_ref[e]

        @pl.when(j_next < n_j)
        def _():
            start_weights(next_ref[e], j_next)

    @pl.when(live)
    def _():
        lo, hi = _unpack_pairs(x_ref[...])
        x_lo = lo.astype(bf16)
        x_hi = hi.astype(bf16)
        half = x_lo.shape[1]

        def proj(g, b_ref):
            return (jnp.dot(x_lo, w16[g, 0:half, :], preferred_element_type=f32)
                    + jnp.dot(x_hi, w16[g, half:, :], preferred_element_type=f32) + b_ref[0])

        gate = jnp.minimum(proj(0, bg_ref), SWIGLU_LIMIT)
        up = jnp.clip(proj(1, bu_ref), -SWIGLU_LIMIT, SWIGLU_LIMIT)
        act_ref[0] = ((up + 1.0) * gate * jax.nn.sigmoid(SWIGLU_ALPHA * gate)).astype(bf16)

    @pl.when(jnp.logical_not(live))
    def _():
        act_ref[...] = jnp.zeros_like(act_ref)


def _moe_up(sched, x_sorted, w_gate_up, b_gate_up):
    e, d, f2 = w_gate_up.shape
    tf = UP_TILE
    n_j = D_FF // tf
    rows = x_sorted.shape[0]
    n_blocks = rows // MOE_ROWS
    vmem = 2 * d * tf * 4 + 2 * d * tf * 2 + 2 * MOE_ROWS * d * 2 + 2 * MOE_ROWS * tf * 2 \
        + 8 * MOE_ROWS * tf * 4 + (6 << 20)
    grid_spec = pltpu.PrefetchScalarGridSpec(
        num_scalar_prefetch=5,
        grid=(n_j, n_blocks),
        in_specs=[pl.BlockSpec((MOE_ROWS, d // 2), lambda j, b, be, *_: (b, 0)),
                  pl.BlockSpec(memory_space=pl.ANY),
                  pl.BlockSpec((1, 1, tf), lambda j, b, be, *_: (be[b], 0, j)),
                  pl.BlockSpec((1, 1, tf), lambda j, b, be, *_: (be[b], 0, n_j + j))],
        out_specs=pl.BlockSpec((1, MOE_ROWS, tf), lambda j, b, be, *_: (j, b, 0)),
        scratch_shapes=[pltpu.VMEM((2, d, tf), f32), pltpu.VMEM((2, d, tf), bf16), pltpu.SemaphoreType.DMA(())],
    )
    b3 = b_gate_up.reshape(e, 1, f2)
    return pl.pallas_call(
        functools.partial(_moe_up_kernel, n_j=n_j),
        grid_spec=grid_spec,
        out_shape=jax.ShapeDtypeStruct((n_j, rows, tf), bf16),
        compiler_params=_params(("arbitrary", "arbitrary"), vmem),
        name="moe_gate_up",
    )(*sched, x_sorted, w_gate_up, b3, b3)


def _moe_down_kernel(be_ref, first_ref, next_ref, wrap_ref, nl_ref, act_ref, w_hbm, bd_ref, y_ref, stage, w16, sem):
    b = pl.program_id(0)
    e = be_ref[b]
    live = b < nl_ref[0]
    n_k, _, tf = act_ref.shape

    def weight_copy(ee):
        return pltpu.make_async_copy(w_hbm.at[ee], stage, sem)

    @pl.when(b == 0)
    def _():
        weight_copy(e).start()

    @pl.when(jnp.logical_and(live, first_ref[b] == 1))
    def _():
        weight_copy(e).wait()
        w16[...] = stage[...].astype(bf16)

        @pl.when(wrap_ref[e] == 0)
        def _():
            weight_copy(next_ref[e]).start()

    @pl.when(live)
    def _():
        y = bd_ref[0]
        for k in range(n_k):
            y = y + jnp.dot(act_ref[k], w16[k * tf:(k + 1) * tf, :], preferred_element_type=f32)
        y16 = y.astype(bf16)
        half = y16.shape[1] // 2
        y_ref[...] = _pack_pairs(y16[:, :half], y16[:, half:])

    @pl.when(jnp.logical_not(live))
    def _():
        y_ref[...] = jnp.zeros_like(y_ref)


def _moe_down(sched, act, w_down, b_down):
    e, f, d = w_down.shape
    n_k, rows, tf = act.shape
    n_blocks = rows // MOE_ROWS
    vmem = f * d * 4 + f * d * 2 + 2 * MOE_ROWS * f * 2 + 2 * MOE_ROWS * d * 2 + 4 * MOE_ROWS * d * 4 + (6 << 20)
    grid_spec = pltpu.PrefetchScalarGridSpec(
        num_scalar_prefetch=5,
        grid=(n_blocks,),
        in_specs=[pl.BlockSpec((n_k, MOE_ROWS, tf), lambda b, be, *_: (0, b, 0)),
                  pl.BlockSpec(memory_space=pl.ANY),
                  pl.BlockSpec((1, 1, d), lambda b, be, *_: (be[b], 0, 0))],
        out_specs=pl.BlockSpec((MOE_ROWS, d // 2), lambda b, be, *_: (b, 0)),
        scratch_shapes=[pltpu.VMEM((f, d), f32), pltpu.VMEM((f, d), bf16), pltpu.SemaphoreType.DMA(())],
    )
    return pl.pallas_call(
        _moe_down_kernel,
        grid_spec=grid_spec,
        out_shape=jax.ShapeDtypeStruct((rows, d // 2), jnp.uint32),
        compiler_params=_params(("arbitrary",), vmem),
        name="moe_down",
    )(*sched, act, w_down, b_down.reshape(e, 1, d))


def _gather_rows(idx_ref, src_hbm, dst, sem, n_rows):
    for r in range(n_rows):
        t = idx_ref[0, 0, r]
        pltpu.make_async_copy(src_hbm.at[pl.ds(t, 1)], dst.at[pl.ds(r, 1)], sem).start(priority=r % 2)


def _wait_rows(src_hbm, dst, sem, n_rows):
    pltpu.make_async_copy(src_hbm.at[pl.ds(0, n_rows)], dst, sem).wait()


def _combine_kernel(pos_ref, posn_ref, y_hbm, gates_ref, xmid_ref, mod_ref, gpost_ref, out_ref, ybuf, sem, *, n_steps):
    i = pl.program_id(0)
    tm = COMBINE_TOKENS
    n_rows = TOP_K * tm
    slot = i % 2

    @pl.when(i == 0)
    def _():
        _gather_rows(pos_ref, y_hbm, ybuf.at[0], sem.at[0], n_rows)

    @pl.when(i + 1 < n_steps)
    def _():
        _gather_rows(posn_ref, y_hbm, ybuf.at[1 - slot], sem.at[1 - slot], n_rows)

    _wait_rows(y_hbm, ybuf.at[slot], sem.at[slot], n_rows)
    g = gates_ref[...]
    lo = hi = None
    for k in range(TOP_K):
        l_k, h_k = _unpack_pairs(ybuf[slot, k * tm:(k + 1) * tm, :])
        gk = g[:, k:k + 1]
        lo = gk * l_k if lo is None else lo + gk * l_k
        hi = gk * h_k if hi is None else hi + gk * h_k
    half = lo.shape[1]
    inv = lax.rsqrt((jnp.sum(lo * lo, axis=-1, keepdims=True) + jnp.sum(hi * hi, axis=-1, keepdims=True))
                    / (2 * half) + EPS)
    gate2 = mod_ref[0][5:6]
    gp = gpost_ref[...]
    out_ref[:, :half] = xmid_ref[:, :half] + gate2[:, :half] * (lo * inv * gp[:, :half])
    out_ref[:, half:] = xmid_ref[:, half:] + gate2[:, half:] * (hi * inv * gp[:, half:])


def _combine(pos, y_sorted, gates, xmid, mod, g_post, seq):
    n, d = xmid.shape
    tm = COMBINE_TOKENS
    n_tiles = n // tm
    per_seq = seq // tm
    vmem = 2 * TOP_K * tm * d * 2 + 4 * tm * d * 4 + 6 * tm * d * 4 + (8 << 20)
    return pl.pallas_call(
        functools.partial(_combine_kernel, n_steps=n_tiles),
        grid=(n_tiles,),
        in_specs=[pl.BlockSpec((1, 1, TOP_K * tm), lambda i: (i, 0, 0), memory_space=pltpu.SMEM),
                  pl.BlockSpec((1, 1, TOP_K * tm), lambda i: (jnp.minimum(i + 1, n_tiles - 1), 0, 0),
                               memory_space=pltpu.SMEM),
                  pl.BlockSpec(memory_space=pl.ANY),
                  pl.BlockSpec((tm, TOP_K), lambda i: (i, 0)),
                  pl.BlockSpec((tm, d), lambda i: (i, 0)),
                  pl.BlockSpec((1, 6, d), lambda i: (i // per_seq, 0, 0)),
                  _const_spec((1, d))],
        out_specs=pl.BlockSpec((tm, d), lambda i: (i, 0)),
        out_shape=jax.ShapeDtypeStruct((n, d), f32),
        scratch_shapes=[pltpu.VMEM((2, TOP_K * tm) + y_sorted.shape[1:], jnp.uint32), pltpu.SemaphoreType.DMA((2,))],
        compiler_params=_params(("arbitrary",), vmem),
        name="moe_combine",
    )(pos, pos, y_sorted, gates, xmid, mod, g_post.reshape(1, d))


def _rope_tables(pos):
    def cs(d):
        half = d // 2
        inv_freq = 1.0 / (ROPE_THETA ** (jnp.arange(half, dtype=f32) * (2.0 / d)))
        ang = pos.astype(f32)[:, None] * inv_freq[None, :]
        return jnp.cos(ang), jnp.sin(ang)
    c64, s64 = cs(MLA_ROPE)
    c256, s256 = cs(RET_QK)
    t64 = jnp.concatenate([c64, c64, s64, s64], axis=-1)
    t256 = jnp.concatenate([c256, s256], axis=-1)
    return t64, t256


def _rot_cols(w):
    half = w.shape[-1] // 2
    return jnp.concatenate([-w[..., half:], w[..., :half]], axis=-1)


def _prep_weights(w_in, w_uq, w_ukv, w_out, w_router, b_router):
    b = np.cumsum((Q_LORA, KV_LORA, MLA_ROPE))
    w_t = w_in.T
    w_pe = w_in[:, b[1]:b[2]]
    w_parts = (w_t[:b[1]].astype(bf16), jnp.concatenate([w_pe, _rot_cols(w_pe)], axis=-1).T.astype(bf16),
               w_t[b[2]:].astype(bf16))
    wq = w_uq.reshape(Q_LORA, MLA_HEADS, MLA_NOPE + MLA_ROPE)
    wq = jnp.concatenate([wq, _rot_cols(wq[..., MLA_NOPE:])], axis=-1).transpose(1, 0, 2).astype(bf16)
    wkv = w_ukv.reshape(KV_LORA, MLA_HEADS, MLA_NOPE + MLA_V).transpose(1, 0, 2).astype(bf16)
    r_hi = w_router.astype(bf16)
    r_lo = (w_router - r_hi.astype(f32)).astype(bf16)
    pad = ((0, 0), (0, LANES - N_EXPERTS))
    wr = jnp.concatenate([jnp.pad(r_hi, pad), jnp.pad(r_lo, pad)], axis=1)
    br = jnp.pad(b_router, (0, LANES - N_EXPERTS), constant_values=NEG_BIG).reshape(1, LANES)
    return w_parts, wq, wkv, w_out.astype(bf16), wr, br


def _mixer_half(x, mod, pos, wts, p, past_lat16, past_kpe16, state0, counts0, ret_chunk, q_tiles, tm):
    w_parts, wq, wkv, w_out16, wr, br = wts
    b, s, d = x.shape
    x2d = x.reshape(b * s, d)
    t64, t256 = _rope_tables(pos)
    qlat, kvlat, kvlat16, kpe, kpe16, rq, rk, rv, rg = _inproj(
        x2d, mod, p['g_pre_mix'], w_parts, p['g_q_lat'], p['g_kv_lat'], t64, t256, s, tm)
    lat16 = kvlat16.reshape(b, s, KV_LORA)
    kpe16 = kpe16.reshape(b, s, 2 * MLA_ROPE)
    if past_lat16 is not None:
        lat16 = jnp.concatenate([past_lat16, lat16], axis=1)
        kpe16 = jnp.concatenate([past_kpe16, kpe16], axis=1)
    attn = _mla(qlat.reshape(b, s, Q_LORA), lat16, kpe16, wq, wkv, t64, q_tiles)
    log_g = jnp.log1p(-jnp.exp2(-5.0 - jnp.arange(RET_HEADS, dtype=f32)))
    log_g = jnp.broadcast_to(log_g[:, None, None], (RET_HEADS, 8, LANES))
    sh = lambda t, w: t.reshape(b, s, w)
    rn, state = _retention(sh(rq, RET_HEADS * RET_QK), sh(rk, RET_HEADS * RET_QK), sh(rv, RET_WIDTH),
                           sh(rg, RET_WIDTH), p['g_ret'], state0, log_g, ret_chunk)
    xmid, h2p, idx, gates, rank, counts = _mixout(
        attn.reshape(b * s, MLA_WIDTH), rn.reshape(b * s, RET_WIDTH), x2d, mod,
        p['g_post_mix'], p['g_pre_ffn'], w_out16, wr, br, counts0, s, tm)
    route = (idx[:, :TOP_K], gates[:, :TOP_K], rank[:, :TOP_K], counts)
    return kvlat.reshape(b, s, KV_LORA), kpe.reshape(b, s, MLA_ROPE), state, xmid, h2p, route


def _combine_order(dest):
    tm = COMBINE_TOKENS
    n_tiles = dest.shape[0] // tm
    return dest.reshape(n_tiles, tm, TOP_K).transpose(0, 2, 1).reshape(n_tiles, 1, TOP_K * tm)


def kernel(x_prompt, x_sample, cache_kv_latent, cache_k_rope, state_retention, c_prompt, c_sample, w_ada, b_ada, g_pre_mix, g_post_mix, g_pre_ffn, g_post_ffn, w_in, g_q_lat, g_kv_lat, w_uq, w_ukv, g_ret, w_out, w_router, b_router, w_gate_up, b_gate_up, w_down, b_down):
    depth = w_in.shape[0]
    assert depth == 1, "the staged problem has a single layer"
    bp, sp, d = x_prompt.shape
    bs, ss, _ = x_sample.shape
    past = cache_kv_latent.shape[2]
    l = 0
    p = dict(g_pre_mix=g_pre_mix[l], g_post_mix=g_post_mix[l], g_pre_ffn=g_pre_ffn[l], g_post_ffn=g_post_ffn[l],
             g_q_lat=g_q_lat[l], g_kv_lat=g_kv_lat[l], g_ret=g_ret[l])
    wts = _prep_weights(w_in[l], w_uq[l], w_ukv[l], w_out[l], w_router[l], b_router[l])

    mod = _ada(jnp.concatenate([c_prompt, c_sample], axis=0), w_ada[l], b_ada[l]).reshape(bp + bs, 6, d)
    mod_p, mod_s = mod[:bp], mod[bp:]

    tq = ATTN_Q_TILE
    tiles_p = tuple((q0, tq, q0 + tq, True) for q0 in range(0, sp, tq))
    zero_state = jnp.zeros((bp, RET_HEADS, RET_QK, RET_V), f32)
    lat_p, kpe_p, st_p, xmid_p, h2p_p, (idx_p, gates_p, rank_p, counts_p) = _mixer_half(
        x_prompt, mod_p, jnp.arange(sp), wts, p, None, None, zero_state, jnp.zeros((1, LANES), f32),
        RET_CHUNK_PROMPT, tiles_p, 512)

    past_lat16 = cache_kv_latent[l].astype(bf16)
    past_kpe16 = jnp.pad(cache_k_rope[l], ((0, 0), (0, 0), (0, MLA_ROPE))).astype(bf16)
    tiles_s = ((0, ss, past + ss, False),)
    lat_s, kpe_s, st_s, xmid_s, h2p_s, (idx_s, gates_s, rank_s, counts) = _mixer_half(
        x_sample, mod_s, past + jnp.arange(ss), wts, p, past_lat16, past_kpe16, state_retention[l], counts_p,
        ss, tiles_s, bs * ss)

    n_p = bp * sp
    dest, pad_end, n_chunks = _route(
        jnp.concatenate([idx_p, idx_s], axis=0), jnp.concatenate([rank_p, rank_s], axis=0), counts)
    x_sorted = _dispatch(pad_end, n_chunks, dest, h2p_p, h2p_s)
    sched = _block_schedule(n_chunks, x_sorted.shape[0] // MOE_ROWS)
    act = _moe_up(sched, x_sorted, w_gate_up[l], b_gate_up[l])
    y_sorted = _moe_down(sched, act, w_down[l], b_down[l])
    y_p = _combine(_combine_order(dest[:n_p]), y_sorted, gates_p, xmid_p, mod_p, p['g_post_ffn'], sp)
    y_s = _combine(_combine_order(dest[n_p:]), y_sorted, gates_s, xmid_s, mod_s, p['g_post_ffn'], ss)

    return (y_p.reshape(bp, sp, d), y_s.reshape(bs, ss, d),
            lat_p[None], kpe_p[None], st_p[None].astype(state_retention.dtype),
            lat_s[None], kpe_s[None], st_s[None].astype(state_retention.dtype))
```

```python
import functools

import numpy as np
import jax
import jax.numpy as jnp
from jax import lax
from jax.experimental import pallas as pl
from jax.experimental.pallas import tpu as pltpu

CHUNK = 64
EPS = 1e-6
ROPE_THETA = 10000.0

MLA_HEADS = 8
MLA_NOPE = 128
MLA_ROPE = 64
MLA_V = 128
Q_LORA = 512
KV_LORA = 512
MLA_WIDTH = MLA_HEADS * MLA_V

RET_HEADS = 4
RET_QK = 256
RET_V = 256
RET_WIDTH = RET_HEADS * RET_V

N_EXPERTS = 32
TOP_K = 4
D_FF = 2048
SWIGLU_LIMIT = 7.0
SWIGLU_ALPHA = 1.702

LANES = 128
V7X_VMEM_BYTES = 64 * 1024 * 1024
VMEM_CAP_BYTES = V7X_VMEM_BYTES - 8 * 1024 * 1024

RET_CHUNK_PROMPT = 128
RET_UNROLL = 16
ATTN_Q_TILE = 256
MOE_ROWS = 256
UP_TILE = 1024
DISPATCH_TOKENS = 512
COMBINE_TOKENS = 64
NEG_BIG = -1e30

bf16 = jnp.bfloat16
f32 = jnp.float32


def _params(sem, vmem_bytes):
    return pltpu.CompilerParams(dimension_semantics=sem, vmem_limit_bytes=int(min(vmem_bytes, VMEM_CAP_BYTES)))


def _const_spec(shape):
    nd = len(shape)
    return pl.BlockSpec(shape, lambda *_: (0,) * nd, pipeline_mode=pl.Buffered(1))


def _rms(x):
    return x * lax.rsqrt(jnp.mean(x * x, axis=-1, keepdims=True) + EPS)


def _silu(x):
    return x * jax.nn.sigmoid(x)


def _ada_kernel(c_ref, w_ref, b_ref, o_ref):
    s = _silu(c_ref[...]).astype(bf16)
    o_ref[...] = jnp.dot(s, w_ref[...].astype(bf16), preferred_element_type=f32) + b_ref[...]


def _ada(c, w_ada, b_ada):
    nb, d = c.shape
    n = w_ada.shape[1]
    tn = 1536
    return pl.pallas_call(
        _ada_kernel,
        grid=(n // tn,),
        in_specs=[pl.BlockSpec((nb, d), lambda j: (0, 0)),
                  pl.BlockSpec((d, tn), lambda j: (0, j)),
                  pl.BlockSpec((1, tn), lambda j: (0, j))],
        out_specs=pl.BlockSpec((nb, tn), lambda j: (0, j)),
        out_shape=jax.ShapeDtypeStruct((nb, n), f32),
        compiler_params=_params(("arbitrary",), 2 * d * tn * 4 + d * tn * 2 + (8 << 20)),
        name="ada_modulation",
    )(c, w_ada, b_ada.reshape(1, n))


_C_RK = RET_HEADS * RET_QK
_C_RV = _C_RK + RET_HEADS * RET_QK
_C_RG = _C_RV + RET_WIDTH
_C_END = _C_RG + RET_WIDTH


def _rope_pair(t):
    return t + pltpu.roll(t, MLA_ROPE, axis=1)


def _rope_heads(z, cos, sin, scale):
    outs = []
    half = RET_QK // 2
    for h in range(RET_HEADS):
        x1 = z[:, h * RET_QK:h * RET_QK + half]
        x2 = z[:, h * RET_QK + half:(h + 1) * RET_QK]
        outs.append((x1 * cos - x2 * sin) * scale)
        outs.append((x1 * sin + x2 * cos) * scale)
    return jnp.concatenate(outs, axis=-1)


def _per_sequence(tile_rows, mod_ref):
    n_seq = mod_ref.shape[0]
    rows = tile_rows // n_seq
    return [(slice(i * rows, (i + 1) * rows), mod_ref[i]) for i in range(n_seq)]


def _tile_geometry(seq, tm):
    assert seq % tm == 0 or tm % seq == 0
    return max(seq // tm, 1), max(tm // seq, 1)


def _inproj_kernel(x_ref, mod_ref, g_ref, wlat_ref, wpe_ref, wret_ref, gq_ref, gkv_ref, t64_ref, t256_ref,
                   qlat_ref, kvlat_ref, kvlat16_ref, kpe_ref, kpe16_ref, rq_ref, rk_ref, rv_ref, rg_ref):
    hs = []
    for rows, m in _per_sequence(x_ref.shape[0], mod_ref):
        hs.append((_rms(x_ref[rows, :]) * (g_ref[...] * (1.0 + m[1:2])) + m[0:1]).astype(bf16))
    h = hs[0] if len(hs) == 1 else jnp.concatenate(hs, axis=0)

    def proj(w_ref, c0, c1):
        return lax.dot_general(h, w_ref[c0:c1, :], (((1,), (1,)), ((), ())), preferred_element_type=f32)

    qlat_ref[...] = (_rms(proj(wlat_ref, 0, Q_LORA)) * gq_ref[...]).astype(bf16)
    kv = _rms(proj(wlat_ref, Q_LORA, Q_LORA + KV_LORA)) * gkv_ref[...]
    kvlat_ref[...] = kv
    kvlat16_ref[...] = kv.astype(bf16)
    pe = _rope_pair(proj(wpe_ref, 0, 2 * MLA_ROPE) * t64_ref[...])
    kpe_ref[...] = pe[:, :MLA_ROPE]
    lane = lax.broadcasted_iota(jnp.int32, pe.shape, 1)
    kpe16_ref[...] = jnp.where(lane < MLA_ROPE, pe, 0.0).astype(bf16)
    cos = t256_ref[:, :RET_QK // 2]
    sin = t256_ref[:, RET_QK // 2:]
    rq_ref[...] = _rope_heads(proj(wret_ref, 0, _C_RK), cos, sin, 1.0).astype(bf16)
    rk_ref[...] = _rope_heads(proj(wret_ref, _C_RK, _C_RV), cos, sin, RET_QK ** -0.5).astype(bf16)
    rv_ref[...] = proj(wret_ref, _C_RV, _C_RG).astype(bf16)
    rg_ref[...] = proj(wret_ref, _C_RG, _C_END).astype(bf16)


def _inproj(x2d, mod, g_pre, w_parts, g_q, g_kv, t64, t256, seq, tm):
    n, d = x2d.shape
    per_seq, n_seq = _tile_geometry(seq, tm)
    t64, t256 = jnp.tile(t64, (n_seq, 1)), jnp.tile(t256, (n_seq, 1))
    row = lambda w: pl.BlockSpec((tm, w), lambda i: (i, 0))
    tab = lambda w: pl.BlockSpec((tm, w), lambda i: (i % per_seq, 0))
    outs = [(Q_LORA, bf16), (KV_LORA, f32), (KV_LORA, bf16), (MLA_ROPE, f32), (2 * MLA_ROPE, bf16),
            (RET_HEADS * RET_QK, bf16), (RET_HEADS * RET_QK, bf16), (RET_WIDTH, bf16), (RET_WIDTH, bf16)]
    vmem = sum(w.size for w in w_parts) * 2 + 2 * tm * d * 4 + 4 * tm * d * 4 + sum(2 * tm * w * 4 for w, _ in outs) + (6 << 20)
    return pl.pallas_call(
        _inproj_kernel,
        grid=(n // tm,),
        in_specs=[row(d),
                  pl.BlockSpec((n_seq, 6, d), lambda i: (i // per_seq, 0, 0)),
                  _const_spec((1, d)), *[_const_spec(w.shape) for w in w_parts],
                  _const_spec((1, Q_LORA)), _const_spec((1, KV_LORA)),
                  tab(2 * MLA_ROPE), tab(RET_QK)],
        out_specs=[row(w) for w, _ in outs],
        out_shape=[jax.ShapeDtypeStruct((n, w), dt) for w, dt in outs],
        compiler_params=_params(("arbitrary",), vmem),
        name="in_projection",
    )(x2d, mod, g_pre.reshape(1, d), *w_parts, g_q.reshape(1, Q_LORA), g_kv.reshape(1, KV_LORA), t64, t256)


def _mla_kernel(qlat_ref, lat_ref, kpe_ref, wq_ref, wkv_ref, tab_ref, o_ref, q_s, k_s, v_s, *, q_tiles):
    scale = (MLA_NOPE + MLA_ROPE) ** -0.5
    for hh in range(wq_ref.shape[0]):
        qh = jnp.dot(qlat_ref[0], wq_ref[hh], preferred_element_type=f32)
        q_s[:, :MLA_NOPE] = (qh[:, :MLA_NOPE] * scale).astype(bf16)
        q_s[:, MLA_NOPE:] = (_rope_pair(qh[:, MLA_NOPE:] * tab_ref[...]) * scale).astype(bf16)
        kv = jnp.dot(lat_ref[0], wkv_ref[hh], preferred_element_type=f32)
        k_s[:, :MLA_NOPE] = kv[:, :MLA_NOPE].astype(bf16)
        k_s[:, MLA_NOPE:] = kpe_ref[0]
        v_s[...] = kv[:, MLA_NOPE:].astype(bf16)
        for q0, ql, kvl, masked in q_tiles:
            q = q_s[q0:q0 + ql, :]
            s = lax.dot_general(q, k_s[0:kvl, :], (((1,), (1,)), ((), ())), preferred_element_type=f32)
            if masked:
                assert kvl == q0 + ql and q0 % CHUNK == 0
                qc = lax.broadcasted_iota(jnp.int32, (ql, 1), 0) // CHUNK
                kc = lax.broadcasted_iota(jnp.int32, (1, ql), 1) // CHUNK
                diag = jnp.where(kc <= qc, s[:, q0:], NEG_BIG)
                s = diag if q0 == 0 else jnp.concatenate([s[:, :q0], diag], axis=1)
            p = jnp.exp(s - jnp.max(s, axis=-1, keepdims=True))
            l = jnp.sum(p, axis=-1, keepdims=True)
            o = jnp.dot(p.astype(bf16), v_s[0:kvl, :], preferred_element_type=f32)
            o_ref[0, q0:q0 + ql, hh * MLA_V:(hh + 1) * MLA_V] = (o / l).astype(bf16)


def _mla(qlat, lat, kpe16, wq, wkv, tab, q_tiles, heads_per_step):
    b, sq, _ = qlat.shape
    skv = lat.shape[1]
    hs = heads_per_step
    kern = functools.partial(_mla_kernel, q_tiles=q_tiles)
    max_ql = max(t[1] for t in q_tiles)
    vmem = 2 * 2 * (sq * Q_LORA + skv * KV_LORA + skv * LANES) + sq * LANES * 4 + sq * 3 * LANES * 4 \
        + (sq + skv) * 3 * LANES * 2 + 4 * max_ql * skv * 4 + 8 * hs * Q_LORA * LANES * 2 + (8 << 20)
    return pl.pallas_call(
        kern,
        grid=(b, MLA_HEADS // hs),
        in_specs=[pl.BlockSpec((1, sq, Q_LORA), lambda i, h: (i, 0, 0)),
                  pl.BlockSpec((1, skv, KV_LORA), lambda i, h: (i, 0, 0)),
                  pl.BlockSpec((1, skv, 2 * MLA_ROPE), lambda i, h: (i, 0, 0)),
                  pl.BlockSpec((hs, Q_LORA, MLA_NOPE + 2 * MLA_ROPE), lambda i, h: (h, 0, 0)),
                  pl.BlockSpec((hs, KV_LORA, MLA_NOPE + MLA_V), lambda i, h: (h, 0, 0)),
                  pl.BlockSpec((sq, 2 * MLA_ROPE), lambda i, h: (0, 0))],
        out_specs=pl.BlockSpec((1, sq, hs * MLA_V), lambda i, h: (i, 0, h)),
        out_shape=jax.ShapeDtypeStruct((b, sq, MLA_WIDTH), bf16),
        scratch_shapes=[pltpu.VMEM((sq, MLA_NOPE + 2 * MLA_ROPE), bf16),
                        pltpu.VMEM((skv, MLA_NOPE + 2 * MLA_ROPE), bf16),
                        pltpu.VMEM((skv, MLA_V), bf16)],
        compiler_params=_params(("arbitrary", "arbitrary"), vmem),
        name="mla_attention",
    )(qlat, lat, kpe16, wq, wkv, tab)


def _ret_kernel(lg_ref, rq_ref, rk_ref, rv_ref, rg_ref, gret_ref, s0_ref, o_ref, sout_ref, state, *, chunk, n_chunks):
    c = chunk
    lg = lg_ref[0, 0:1, 0:1]
    ii = lax.broadcasted_iota(jnp.int32, (c, c), 0)
    jj = lax.broadcasted_iota(jnp.int32, (c, c), 1)
    diff = (ii - jj).astype(f32)
    dmask = jnp.where(diff >= 0.0, jnp.exp(jnp.maximum(diff, 0.0) * lg), 0.0)
    ic = lax.broadcasted_iota(jnp.int32, (c, 1), 0).astype(f32)
    q_decay = jnp.exp((ic + 1.0) * lg)
    k_decay = jnp.exp((c - 1.0 - ic) * lg)
    chunk_decay = jnp.exp(float(c) * lg)
    gret = gret_ref[...]
    state[...] = s0_ref[0, 0]

    def step(n, carry):
        sl = pl.ds(pl.multiple_of(n * c, c), c)
        q = rq_ref[0, sl, :]
        k = rk_ref[0, sl, :]
        v = rv_ref[0, sl, :]
        st = state[...]
        attn = lax.dot_general(q, k, (((1,), (1,)), ((), ())), preferred_element_type=f32) * dmask
        inner = jnp.dot(attn.astype(bf16), v, preferred_element_type=f32)
        cross = jnp.dot(q, st.astype(bf16), preferred_element_type=f32) * q_decay
        o = inner + cross
        kd_t = (k.astype(f32) * k_decay).T.astype(bf16)
        state[...] = st * chunk_decay + jnp.dot(kd_t, v, preferred_element_type=f32)
        mu = jnp.mean(o, axis=-1, keepdims=True)
        oc = o - mu
        rn = oc * lax.rsqrt(jnp.mean(oc * oc, axis=-1, keepdims=True) + EPS) * gret
        o_ref[0, sl, :] = (rn * _silu(rg_ref[0, sl, :].astype(f32))).astype(bf16)
        return carry

    lax.fori_loop(0, n_chunks, step, 0, unroll=min(n_chunks, RET_UNROLL))
    sout_ref[0, 0] = state[...]


def _retention(rq, rk, rv, rg, g_ret, state0, log_g, chunk):
    b, s, _ = rq.shape
    kern = functools.partial(_ret_kernel, chunk=chunk, n_chunks=s // chunk)
    head = lambda w: pl.BlockSpec((1, s, w), lambda i, h: (i, 0, h))
    st = pl.BlockSpec((1, 1, RET_QK, RET_V), lambda i, h: (i, h, 0, 0))
    vmem = 2 * 5 * s * RET_QK * 2 + 5 * RET_QK * RET_V * 4 + (8 << 20)
    return pl.pallas_call(
        kern,
        grid=(b, RET_HEADS),
        in_specs=[pl.BlockSpec((1, 8, LANES), lambda i, h: (h, 0, 0)),
                  head(RET_QK), head(RET_QK), head(RET_V), head(RET_V),
                  pl.BlockSpec((1, RET_V), lambda i, h: (0, h)), st],
        out_specs=[head(RET_V), st],
        out_shape=[jax.ShapeDtypeStruct((b, s, RET_WIDTH), bf16),
                   jax.ShapeDtypeStruct((b, RET_HEADS, RET_QK, RET_V), f32)],
        scratch_shapes=[pltpu.VMEM((RET_QK, RET_V), f32)],
        compiler_params=_params(("arbitrary", "arbitrary"), vmem),
        name="retention",
    )(log_g, rq, rk, rv, rg, g_ret.reshape(1, RET_WIDTH), state0)


def _pack_pairs(lo16, hi16):
    lo = lax.bitcast_convert_type(lo16.astype(f32), jnp.uint32)
    hi = lax.bitcast_convert_type(hi16.astype(f32), jnp.uint32)
    return (hi & jnp.uint32(0xFFFF0000)) | (lo >> 16)


def _unpack_pairs(u):
    lo = lax.bitcast_convert_type(u << 16, f32)
    hi = lax.bitcast_convert_type(u & jnp.uint32(0xFFFF0000), f32)
    return lo, hi


def _mixout_kernel(attn_ref, rn_ref, x_ref, mod_ref, gpost_ref, gpre_ref, wout_ref, wr_ref, br_ref, cnt0_ref, tri_ref,
                   xmid_ref, h2p_ref, idx_ref, gate_ref, rank_ref, cnt_ref, carry):
    @pl.when(pl.program_id(0) == 0)
    def _():
        carry[...] = cnt0_ref[...]

    mix = (jnp.dot(attn_ref[...], wout_ref[0:MLA_WIDTH, :], preferred_element_type=f32)
           + jnp.dot(rn_ref[...], wout_ref[MLA_WIDTH:, :], preferred_element_type=f32))
    h2s = []
    for rows, m in _per_sequence(mix.shape[0], mod_ref):
        x1 = x_ref[rows, :] + _rms(mix[rows]) * (m[2:3] * gpost_ref[...])
        xmid_ref[rows, :] = x1
        h2s.append(_rms(x1) * (gpre_ref[...] * (1.0 + m[4:5])) + m[3:4])
    h2 = h2s[0] if len(h2s) == 1 else jnp.concatenate(h2s, axis=0)
    h_hi = h2.astype(bf16)
    half = h2.shape[1] // 2
    h2p_ref[...] = _pack_pairs(h_hi[:, :half], h_hi[:, half:])
    h_lo = (h2 - h_hi.astype(f32)).astype(bf16)
    hi_terms = jnp.dot(h_hi, wr_ref[...], preferred_element_type=f32)
    logits = (hi_terms[:, :LANES] + jnp.dot(h_lo, wr_ref[:, :LANES], preferred_element_type=f32)
              + hi_terms[:, LANES:]) + br_ref[...]
    col = lax.broadcasted_iota(jnp.int32, logits.shape, 1).astype(f32)
    vals, idxs = [], []
    for _ in range(TOP_K):
        mx = jnp.max(logits, axis=-1, keepdims=True)
        ix = jnp.min(jnp.where(logits == mx, col, float(LANES)), axis=-1, keepdims=True)
        vals.append(mx)
        idxs.append(ix)
        logits = jnp.where(col == ix, -jnp.inf, logits)
    es = [jnp.exp(v - vals[0]) for v in vals]
    tot = es[0] + es[1] + es[2] + es[3]
    onehot = jnp.zeros_like(col)
    for k in range(TOP_K):
        onehot = jnp.where(col == idxs[k], 1.0, onehot)
    before = jnp.dot(tri_ref[...], onehot.astype(bf16), preferred_element_type=f32) + carry[...]
    idx_out = jnp.zeros_like(col)
    gate_out = jnp.zeros_like(col)
    rank_out = jnp.zeros_like(col)
    for k in range(TOP_K):
        slot = col == float(k)
        idx_out = jnp.where(slot, idxs[k], idx_out)
        gate_out = jnp.where(slot, es[k] / tot, gate_out)
        rank_k = jnp.sum(jnp.where(col == idxs[k], before, 0.0), axis=-1, keepdims=True)
        rank_out = jnp.where(slot, rank_k, rank_out)
    idx_ref[...] = idx_out.astype(jnp.int32)
    gate_ref[...] = gate_out
    rank_ref[...] = rank_out.astype(jnp.int32)
    carry[...] = carry[...] + jnp.sum(onehot, axis=0, keepdims=True)
    cnt_ref[...] = carry[...]


def _mixout(attn, rn, x2d, mod, g_post, g_pre, w_out16, wr, br, counts0, seq, tm):
    n, d = x2d.shape
    per_seq, n_seq = _tile_geometry(seq, tm)
    row = lambda w: pl.BlockSpec((tm, w), lambda i: (i, 0))
    tri = jnp.tril(jnp.ones((tm, tm), bf16), -1)
    vmem = w_out16.size * 2 + 2 * tm * d * (4 + 4 + 2 + 2) + 6 * tm * d * 4 + (8 << 20)
    return pl.pallas_call(
        _mixout_kernel,
        grid=(n // tm,),
        in_specs=[row(MLA_WIDTH), row(RET_WIDTH), row(d),
                  pl.BlockSpec((n_seq, 6, d), lambda i: (i // per_seq, 0, 0)),
                  _const_spec((1, d)), _const_spec((1, d)), _const_spec(w_out16.shape),
                  _const_spec(wr.shape), _const_spec((1, LANES)), _const_spec((1, LANES)), _const_spec((tm, tm))],
        out_specs=[row(d), row(d // 2), row(LANES), row(LANES), row(LANES),
                   pl.BlockSpec((1, LANES), lambda i: (0, 0))],
        out_shape=[jax.ShapeDtypeStruct((n, d), f32), jax.ShapeDtypeStruct((n, d // 2), jnp.uint32),
                   jax.ShapeDtypeStruct((n, LANES), jnp.int32), jax.ShapeDtypeStruct((n, LANES), f32),
                   jax.ShapeDtypeStruct((n, LANES), jnp.int32), jax.ShapeDtypeStruct((1, LANES), f32)],
        scratch_shapes=[pltpu.VMEM((1, LANES), f32)],
        compiler_params=_params(("arbitrary",), vmem),
        name="mixer_out_router",
    )(attn, rn, x2d, mod, g_post.reshape(1, d), g_pre.reshape(1, d), w_out16, wr, br, counts0, tri)


def _route(top_idx, rank, counts):
    counts = counts[0, :N_EXPERTS].astype(jnp.int32)
    padded = (counts + MOE_ROWS - 1) // MOE_ROWS * MOE_ROWS
    pad_end = jnp.cumsum(padded).astype(jnp.int32)
    pad_start = pad_end - padded
    dest = pad_start[top_idx] + rank
    return dest, pad_end, padded // MOE_ROWS


def _n_sorted_rows(n_tokens):
    return (n_tokens * TOP_K + N_EXPERTS * (MOE_ROWS - 1)) // MOE_ROWS * MOE_ROWS


def _block_schedule(n_chunks, n_blocks):
    ends = jnp.cumsum(n_chunks)
    n_live = ends[-1]
    blk = jnp.arange(n_blocks, dtype=jnp.int32)
    block_e = jnp.sum(jnp.minimum(blk, n_live - 1)[:, None] >= ends[None, :], axis=1).astype(jnp.int32)
    is_first = jnp.logical_and(blk == (ends - n_chunks)[block_e], blk < n_live).astype(jnp.int32)
    ids = jnp.arange(N_EXPERTS, dtype=jnp.int32)
    cand = jnp.where(n_chunks > 0, ids, N_EXPERTS)
    from_here = lax.cummin(cand[::-1])[::-1]
    after = jnp.concatenate([from_here[1:], jnp.full((1,), N_EXPERTS, jnp.int32)])
    wraps = (after >= N_EXPERTS).astype(jnp.int32)
    next_e = jnp.where(after < N_EXPERTS, after, from_here[0]).astype(jnp.int32)
    return block_e, is_first, next_e, wraps, n_live.reshape(1).astype(jnp.int32)


def _dispatch_kernel(pe_ref, nc_ref, dest_ref, h2p_a_ref, h2p_b_ref, x_hbm, zbuf, sem, zsem, *, tokens, n_first):
    i = pl.program_id(0)

    @pl.when(i == 0)
    def _():
        zbuf[...] = jnp.zeros_like(zbuf)

        def tail(e):
            start = pl.multiple_of(pe_ref[e] - MOE_ROWS, MOE_ROWS)
            return pltpu.make_async_copy(zbuf, x_hbm.at[pl.ds(start, MOE_ROWS)], zsem)

        def start_tail(e, c):
            @pl.when(nc_ref[e] > 0)
            def _():
                tail(e).start()
            return c

        def wait_tail(e, c):
            @pl.when(nc_ref[e] > 0)
            def _():
                tail(e).wait()
            return c

        lax.fori_loop(0, N_EXPERTS, start_tail, 0)
        lax.fori_loop(0, N_EXPERTS, wait_tail, 0)

        def unused(c):
            return pltpu.make_async_copy(zbuf, x_hbm.at[pl.ds(pl.multiple_of(c * MOE_ROWS, MOE_ROWS), MOE_ROWS)], zsem)

        def start_unused(c, carry):
            unused(c).start()
            return carry

        def wait_unused(c, carry):
            unused(c).wait()
            return carry

        first_unused = pe_ref[N_EXPERTS - 1] // MOE_ROWS
        lax.fori_loop(first_unused, x_hbm.shape[0] // MOE_ROWS, start_unused, 0)
        lax.fori_loop(first_unused, x_hbm.shape[0] // MOE_ROWS, wait_unused, 0)

    def scatter(h2p_ref):
        for t in range(tokens):
            for k in range(TOP_K):
                d = dest_ref[0, 0, t * TOP_K + k]
                pltpu.make_async_copy(h2p_ref.at[pl.ds(t, 1)], x_hbm.at[pl.ds(d, 1)], sem).start(priority=k % 2)
        for _ in range(TOP_K):
            pltpu.make_async_copy(h2p_ref, x_hbm.at[pl.ds(0, tokens)], sem).wait()

    @pl.when(i < n_first)
    def _():
        scatter(h2p_a_ref)

    @pl.when(i >= n_first)
    def _():
        scatter(h2p_b_ref)


def _dispatch(pad_end, n_chunks, dest, h2p_a, h2p_b):
    tokens = DISPATCH_TOKENS
    tile = h2p_a.shape[1:]
    n_a, n_b = h2p_a.shape[0] // tokens, h2p_b.shape[0] // tokens
    assert n_a * tokens == h2p_a.shape[0] and n_b * tokens == h2p_b.shape[0]
    n_steps = n_a + n_b
    grid_spec = pltpu.PrefetchScalarGridSpec(
        num_scalar_prefetch=2,
        grid=(n_steps,),
        in_specs=[pl.BlockSpec((1, 1, tokens * TOP_K), lambda i, pe, nc: (i, 0, 0), memory_space=pltpu.SMEM),
                  pl.BlockSpec((tokens,) + tile, lambda i, pe, nc: (jnp.minimum(i, n_a - 1), 0)),
                  pl.BlockSpec((tokens,) + tile, lambda i, pe, nc: (jnp.maximum(i - n_a, 0), 0))],
        out_specs=pl.BlockSpec(memory_space=pl.ANY),
        scratch_shapes=[pltpu.VMEM((MOE_ROWS,) + tile, jnp.uint32), pltpu.SemaphoreType.DMA(()),
                        pltpu.SemaphoreType.DMA(())],
    )
    return pl.pallas_call(
        functools.partial(_dispatch_kernel, tokens=tokens, n_first=n_a),
        grid_spec=grid_spec,
        out_shape=jax.ShapeDtypeStruct((_n_sorted_rows(n_steps * tokens),) + tile, jnp.uint32),
        compiler_params=_params(("arbitrary",), 24 << 20),
        name="moe_dispatch",
    )(pad_end, n_chunks, dest.reshape(n_steps, 1, tokens * TOP_K), h2p_a, h2p_b)


def _on_static(value, n, fn):
    for v in range(n):
        @pl.when(value == v)
        def _():
            fn(v)


def _moe_up_kernel(be_ref, first_ref, next_ref, wrap_ref, nl_ref, x_ref, w_hbm, bg_ref, bu_ref, act_ref,
                   stage, w16, sem, *, n_j):
    j = pl.program_id(0)
    b = pl.program_id(1)
    e = be_ref[b]
    live = b < nl_ref[0]
    tf = w16.shape[2]

    def weight_copies(ee, jj):
        return [pltpu.make_async_copy(w_hbm.at[ee, :, g * D_FF + jj * tf:g * D_FF + (jj + 1) * tf], stage.at[g], sem)
                for g in range(2)]

    def start_weights(ee, jd):
        _on_static(jd, n_j, lambda jj: [cp.start() for cp in weight_copies(ee, jj)])

    @pl.when(jnp.logical_and(j == 0, b == 0))
    def _():
        start_weights(e, j)

    @pl.when(jnp.logical_and(live, first_ref[b] == 1))
    def _():
        _on_static(j, n_j, lambda jj: [cp.wait() for cp in weight_copies(e, jj)])
        w16[...] = stage[...].astype(bf16)
        j_next = j + wrap_ref[e]

        @pl.when(j_next < n_j)
        def _():
            start_weights(next_ref[e], j_next)

    @pl.when(live)
    def _():
        lo, hi = _unpack_pairs(x_ref[...])
        x_lo = lo.astype(bf16)
        x_hi = hi.astype(bf16)
        half = x_lo.shape[1]

        def proj(g, b_ref):
            return (jnp.dot(x_lo, w16[g, 0:half, :], preferred_element_type=f32)
                    + jnp.dot(x_hi, w16[g, half:, :], preferred_element_type=f32) + b_ref[0])

        gate = jnp.minimum(proj(0, bg_ref), SWIGLU_LIMIT)
        up = jnp.clip(proj(1, bu_ref), -SWIGLU_LIMIT, SWIGLU_LIMIT)
        act_ref[0] = ((up + 1.0) * gate * jax.nn.sigmoid(SWIGLU_ALPHA * gate)).astype(bf16)

    @pl.when(jnp.logical_not(live))
    def _():
        act_ref[...] = jnp.zeros_like(act_ref)


def _moe_up(sched, x_sorted, w_gate_up, b_gate_up):
    e, d, f2 = w_gate_up.shape
    tf = UP_TILE
    n_j = D_FF // tf
    rows = x_sorted.shape[0]
    n_blocks = rows // MOE_ROWS
    vmem = 2 * d * tf * 4 + 2 * d * tf * 2 + 2 * MOE_ROWS * d * 2 + 2 * MOE_ROWS * tf * 2 \
        + 8 * MOE_ROWS * tf * 4 + (6 << 20)
    grid_spec = pltpu.PrefetchScalarGridSpec(
        num_scalar_prefetch=5,
        grid=(n_j, n_blocks),
        in_specs=[pl.BlockSpec((MOE_ROWS, d // 2), lambda j, b, be, *_: (b, 0)),
                  pl.BlockSpec(memory_space=pl.ANY),
                  pl.BlockSpec((1, 1, tf), lambda j, b, be, *_: (be[b], 0, j)),
                  pl.BlockSpec((1, 1, tf), lambda j, b, be, *_: (be[b], 0, n_j + j))],
        out_specs=pl.BlockSpec((1, MOE_ROWS, tf), lambda j, b, be, *_: (j, b, 0)),
        scratch_shapes=[pltpu.VMEM((2, d, tf), f32), pltpu.VMEM((2, d, tf), bf16), pltpu.SemaphoreType.DMA(())],
    )
    b3 = b_gate_up.reshape(e, 1, f2)
    return pl.pallas_call(
        functools.partial(_moe_up_kernel, n_j=n_j),
        grid_spec=grid_spec,
        out_shape=jax.ShapeDtypeStruct((n_j, rows, tf), bf16),
        compiler_params=_params(("arbitrary", "arbitrary"), vmem),
        name="moe_gate_up",
    )(*sched, x_sorted, w_gate_up, b3, b3)


def _moe_down_kernel(be_ref, first_ref, next_ref, wrap_ref, nl_ref, act_ref, w_hbm, bd_ref, y_ref, stage, w16, sem):
    b = pl.program_id(0)
    e = be_ref[b]
    live = b < nl_ref[0]
    n_k, _, tf = act_ref.shape

    def weight_copy(ee):
        return pltpu.make_async_copy(w_hbm.at[ee], stage, sem)

    @pl.when(b == 0)
    def _():
        weight_copy(e).start()

    @pl.when(jnp.logical_and(live, first_ref[b] == 1))
    def _():
        weight_copy(e).wait()
        w16[...] = stage[...].astype(bf16)

        @pl.when(wrap_ref[e] == 0)
        def _():
            weight_copy(next_ref[e]).start()

    @pl.when(live)
    def _():
        y = bd_ref[0]
        for k in range(n_k):
            y = y + jnp.dot(act_ref[k], w16[k * tf:(k + 1) * tf, :], preferred_element_type=f32)
        y16 = y.astype(bf16)
        half = y16.shape[1] // 2
        y_ref[...] = _pack_pairs(y16[:, :half], y16[:, half:])

    @pl.when(jnp.logical_not(live))
    def _():
        y_ref[...] = jnp.zeros_like(y_ref)


def _moe_down(sched, act, w_down, b_down):
    e, f, d = w_down.shape
    n_k, rows, tf = act.shape
    n_blocks = rows // MOE_ROWS
    vmem = f * d * 4 + f * d * 2 + 2 * MOE_ROWS * f * 2 + 2 * MOE_ROWS * d * 2 + 4 * MOE_ROWS * d * 4 + (6 << 20)
    grid_spec = pltpu.PrefetchScalarGridSpec(
        num_scalar_prefetch=5,
        grid=(n_blocks,),
        in_specs=[pl.BlockSpec((n_k, MOE_ROWS, tf), lambda b, be, *_: (0, b, 0)),
                  pl.BlockSpec(memory_space=pl.ANY),
                  pl.BlockSpec((1, 1, d), lambda b, be, *_: (be[b], 0, 0))],
        out_specs=pl.BlockSpec((MOE_ROWS, d // 2), lambda b, be, *_: (b, 0)),
        scratch_shapes=[pltpu.VMEM((f, d), f32), pltpu.VMEM((f, d), bf16), pltpu.SemaphoreType.DMA(())],
    )
    return pl.pallas_call(
        _moe_down_kernel,
        grid_spec=grid_spec,
        out_shape=jax.ShapeDtypeStruct((rows, d // 2), jnp.uint32),
        compiler_params=_params(("arbitrary",), vmem),
        name="moe_down",
    )(*sched, act, w_down, b_down.reshape(e, 1, d))


def _gather_rows(idx_ref, src_hbm, dst, sem, n_rows):
    for r in range(n_rows):
        t = idx_ref[0, 0, r]
        pltpu.make_async_copy(src_hbm.at[pl.ds(t, 1)], dst.at[pl.ds(r, 1)], sem).start(priority=r % 2)


def _wait_rows(src_hbm, dst, sem, n_rows):
    pltpu.make_async_copy(src_hbm.at[pl.ds(0, n_rows)], dst, sem).wait()


def _combine_kernel(pos_ref, posn_ref, y_hbm, gates_ref, xmid_ref, mod_ref, gpost_ref, out_ref, ybuf, sem, *, n_steps):
    i = pl.program_id(0)
    tm = COMBINE_TOKENS
    n_rows = TOP_K * tm
    slot = i % 2

    @pl.when(i == 0)
    def _():
        _gather_rows(pos_ref, y_hbm, ybuf.at[0], sem.at[0], n_rows)

    @pl.when(i + 1 < n_steps)
    def _():
        _gather_rows(posn_ref, y_hbm, ybuf.at[1 - slot], sem.at[1 - slot], n_rows)

    _wait_rows(y_hbm, ybuf.at[slot], sem.at[slot], n_rows)
    g = gates_ref[...]
    lo = hi = None
    for k in range(TOP_K):
        l_k, h_k = _unpack_pairs(ybuf[slot, k * tm:(k + 1) * tm, :])
        gk = g[:, k:k + 1]
        lo = gk * l_k if lo is None else lo + gk * l_k
        hi = gk * h_k if hi is None else hi + gk * h_k
    half = lo.shape[1]
    inv = lax.rsqrt((jnp.sum(lo * lo, axis=-1, keepdims=True) + jnp.sum(hi * hi, axis=-1, keepdims=True))
                    / (2 * half) + EPS)
    gate2 = mod_ref[0][5:6]
    gp = gpost_ref[...]
    out_ref[:, :half] = xmid_ref[:, :half] + gate2[:, :half] * (lo * inv * gp[:, :half])
    out_ref[:, half:] = xmid_ref[:, half:] + gate2[:, half:] * (hi * inv * gp[:, half:])


def _combine(pos, y_sorted, gates, xmid, mod, g_post, seq):
    n, d = xmid.shape
    tm = COMBINE_TOKENS
    n_tiles = n // tm
    per_seq = seq // tm
    vmem = 2 * TOP_K * tm * d * 2 + 4 * tm * d * 4 + 6 * tm * d * 4 + (8 << 20)
    return pl.pallas_call(
        functools.partial(_combine_kernel, n_steps=n_tiles),
        grid=(n_tiles,),
        in_specs=[pl.BlockSpec((1, 1, TOP_K * tm), lambda i: (i, 0, 0), memory_space=pltpu.SMEM),
                  pl.BlockSpec((1, 1, TOP_K * tm), lambda i: (jnp.minimum(i + 1, n_tiles - 1), 0, 0),
                               memory_space=pltpu.SMEM),
                  pl.BlockSpec(memory_space=pl.ANY),
                  pl.BlockSpec((tm, TOP_K), lambda i: (i, 0)),
                  pl.BlockSpec((tm, d), lambda i: (i, 0)),
                  pl.BlockSpec((1, 6, d), lambda i: (i // per_seq, 0, 0)),
                  _const_spec((1, d))],
        out_specs=pl.BlockSpec((tm, d), lambda i: (i, 0)),
        out_shape=jax.ShapeDtypeStruct((n, d), f32),
        scratch_shapes=[pltpu.VMEM((2, TOP_K * tm) + y_sorted.shape[1:], jnp.uint32), pltpu.SemaphoreType.DMA((2,))],
        compiler_params=_params(("arbitrary",), vmem),
        name="moe_combine",
    )(pos, pos, y_sorted, gates, xmid, mod, g_post.reshape(1, d))


def _rope_tables(pos):
    def cs(d):
        half = d // 2
        inv_freq = 1.0 / (ROPE_THETA ** (jnp.arange(half, dtype=f32) * (2.0 / d)))
        ang = pos.astype(f32)[:, None] * inv_freq[None, :]
        return jnp.cos(ang), jnp.sin(ang)
    c64, s64 = cs(MLA_ROPE)
    c256, s256 = cs(RET_QK)
    t64 = jnp.concatenate([c64, c64, s64, s64], axis=-1)
    t256 = jnp.concatenate([c256, s256], axis=-1)
    return t64, t256


def _rot_cols(w):
    half = w.shape[-1] // 2
    return jnp.concatenate([-w[..., half:], w[..., :half]], axis=-1)


def _prep_weights(w_in, w_uq, w_ukv, w_out, w_router, b_router):
    b = np.cumsum((Q_LORA, KV_LORA, MLA_ROPE))
    w_t = w_in.T
    w_pe = w_in[:, b[1]:b[2]]
    w_parts = (w_t[:b[1]].astype(bf16), jnp.concatenate([w_pe, _rot_cols(w_pe)], axis=-1).T.astype(bf16),
               w_t[b[2]:].astype(bf16))
    wq = w_uq.reshape(Q_LORA, MLA_HEADS, MLA_NOPE + MLA_ROPE)
    wq = jnp.concatenate([wq, _rot_cols(wq[..., MLA_NOPE:])], axis=-1).transpose(1, 0, 2).astype(bf16)
    wkv = w_ukv.reshape(KV_LORA, MLA_HEADS, MLA_NOPE + MLA_V).transpose(1, 0, 2).astype(bf16)
    r_hi = w_router.astype(bf16)
    r_lo = (w_router - r_hi.astype(f32)).astype(bf16)
    pad = ((0, 0), (0, LANES - N_EXPERTS))
    wr = jnp.concatenate([jnp.pad(r_hi, pad), jnp.pad(r_lo, pad)], axis=1)
    br = jnp.pad(b_router, (0, LANES - N_EXPERTS), constant_values=NEG_BIG).reshape(1, LANES)
    return w_parts, wq, wkv, w_out.astype(bf16), wr, br


def _mixer_half(x, mod, pos, wts, p, past_lat16, past_kpe16, state0, counts0, ret_chunk, q_tiles, tm, heads_per_step):
    w_parts, wq, wkv, w_out16, wr, br = wts
    b, s, d = x.shape
    x2d = x.reshape(b * s, d)
    t64, t256 = _rope_tables(pos)
    qlat, kvlat, kvlat16, kpe, kpe16, rq, rk, rv, rg = _inproj(
        x2d, mod, p['g_pre_mix'], w_parts, p['g_q_lat'], p['g_kv_lat'], t64, t256, s, tm)
    lat16 = kvlat16.reshape(b, s, KV_LORA)
    kpe16 = kpe16.reshape(b, s, 2 * MLA_ROPE)
    if past_lat16 is not None:
        lat16 = jnp.concatenate([past_lat16, lat16], axis=1)
        kpe16 = jnp.concatenate([past_kpe16, kpe16], axis=1)
    attn = _mla(qlat.reshape(b, s, Q_LORA), lat16, kpe16, wq, wkv, t64, q_tiles, heads_per_step)
    log_g = jnp.log1p(-jnp.exp2(-5.0 - jnp.arange(RET_HEADS, dtype=f32)))
    log_g = jnp.broadcast_to(log_g[:, None, None], (RET_HEADS, 8, LANES))
    sh = lambda t, w: t.reshape(b, s, w)
    rn, state = _retention(sh(rq, RET_HEADS * RET_QK), sh(rk, RET_HEADS * RET_QK), sh(rv, RET_WIDTH),
                           sh(rg, RET_WIDTH), p['g_ret'], state0, log_g, ret_chunk)
    xmid, h2p, idx, gates, rank, counts = _mixout(
        attn.reshape(b * s, MLA_WIDTH), rn.reshape(b * s, RET_WIDTH), x2d, mod,
        p['g_post_mix'], p['g_pre_ffn'], w_out16, wr, br, counts0, s, tm)
    route = (idx[:, :TOP_K], gates[:, :TOP_K], rank[:, :TOP_K], counts)
    return kvlat.reshape(b, s, KV_LORA), kpe.reshape(b, s, MLA_ROPE), state, xmid, h2p, route


def _combine_order(dest):
    tm = COMBINE_TOKENS
    n_tiles = dest.shape[0] // tm
    return dest.reshape(n_tiles, tm, TOP_K).transpose(0, 2, 1).reshape(n_tiles, 1, TOP_K * tm)


def kernel(x_prompt, x_sample, cache_kv_latent, cache_k_rope, state_retention, c_prompt, c_sample, w_ada, b_ada, g_pre_mix, g_post_mix, g_pre_ffn, g_post_ffn, w_in, g_q_lat, g_kv_lat, w_uq, w_ukv, g_ret, w_out, w_router, b_router, w_gate_up, b_gate_up, w_down, b_down):
    depth = w_in.shape[0]
    assert depth == 1, "the staged problem has a single layer"
    bp, sp, d = x_prompt.shape
    bs, ss, _ = x_sample.shape
    past = cache_kv_latent.shape[2]
    l = 0
    p = dict(g_pre_mix=g_pre_mix[l], g_post_mix=g_post_mix[l], g_pre_ffn=g_pre_ffn[l], g_post_ffn=g_post_ffn[l],
             g_q_lat=g_q_lat[l], g_kv_lat=g_kv_lat[l], g_ret=g_ret[l])
    wts = _prep_weights(w_in[l], w_uq[l], w_ukv[l], w_out[l], w_router[l], b_router[l])

    mod = _ada(jnp.concatenate([c_prompt, c_sample], axis=0), w_ada[l], b_ada[l]).reshape(bp + bs, 6, d)
    mod_p, mod_s = mod[:bp], mod[bp:]

    tq = ATTN_Q_TILE
    tiles_p = tuple((q0, tq, q0 + tq, True) for q0 in range(0, sp, tq))
    zero_state = jnp.zeros((bp, RET_HEADS, RET_QK, RET_V), f32)
    lat_p, kpe_p, st_p, xmid_p, h2p_p, (idx_p, gates_p, rank_p, counts_p) = _mixer_half(
        x_prompt, mod_p, jnp.arange(sp), wts, p, None, None, zero_state, jnp.zeros((1, LANES), f32),
        RET_CHUNK_PROMPT, tiles_p, 512, 1)

    past_lat16 = cache_kv_latent[l].astype(bf16)
    past_kpe16 = jnp.pad(cache_k_rope[l], ((0, 0), (0, 0), (0, MLA_ROPE))).astype(bf16)
    tiles_s = ((0, ss, past + ss, False),)
    lat_s, kpe_s, st_s, xmid_s, h2p_s, (idx_s, gates_s, rank_s, counts) = _mixer_half(
        x_sample, mod_s, past + jnp.arange(ss), wts, p, past_lat16, past_kpe16, state_retention[l], counts_p,
        ss, tiles_s, bs * ss, MLA_HEADS)

    n_p = bp * sp
    dest, pad_end, n_chunks = _route(
        jnp.concatenate([idx_p, idx_s], axis=0), jnp.concatenate([rank_p, rank_s], axis=0), counts)
    x_sorted = _dispatch(pad_end, n_chunks, dest, h2p_p, h2p_s)
    sched = _block_schedule(n_chunks, x_sorted.shape[0] // MOE_ROWS)
    act = _moe_up(sched, x_sorted, w_gate_up[l], b_gate_up[l])
    y_sorted = _moe_down(sched, act, w_down[l], b_down[l])
    y_p = _combine(_combine_order(dest[:n_p]), y_sorted, gates_p, xmid_p, mod_p, p['g_post_ffn'], sp)
    y_s = _combine(_combine_order(dest[n_p:]), y_sorted, gates_s, xmid_s, mod_s, p['g_post_ffn'], ss)

    return (y_p.reshape(bp, sp, d), y_s.reshape(bs, ss, d),
            lat_p[None], kpe_p[None], st_p[None].astype(state_retention.dtype),
            lat_s[None], kpe_s[None], st_s[None].astype(state_retention.dtype))
```

```python
import functools

import numpy as np
import jax
import jax.numpy as jnp
from jax import lax
from jax.experimental import pallas as pl
from jax.experimental.pallas import tpu as pltpu

CHUNK = 64
EPS = 1e-6
ROPE_THETA = 10000.0

MLA_HEADS = 8
MLA_NOPE = 128
MLA_ROPE = 64
MLA_V = 128
Q_LORA = 512
KV_LORA = 512
MLA_WIDTH = MLA_HEADS * MLA_V

RET_HEADS = 4
RET_QK = 256
RET_V = 256
RET_WIDTH = RET_HEADS * RET_V

N_EXPERTS = 32
TOP_K = 4
D_FF = 2048
SWIGLU_LIMIT = 7.0
SWIGLU_ALPHA = 1.702

LANES = 128
V7X_VMEM_BYTES = 64 * 1024 * 1024
VMEM_CAP_BYTES = V7X_VMEM_BYTES - 8 * 1024 * 1024

RET_CHUNK_PROMPT = 128
RET_UNROLL = 16
ATTN_Q_TILE = 256
MOE_ROWS = 256
UP_TILE = 1024
DISPATCH_TOKENS = 512
COMBINE_TOKENS = 128
NEG_BIG = -1e30

bf16 = jnp.bfloat16
f32 = jnp.float32


def _params(sem, vmem_bytes):
    return pltpu.CompilerParams(dimension_semantics=sem, vmem_limit_bytes=int(min(vmem_bytes, VMEM_CAP_BYTES)))


def _const_spec(shape):
    nd = len(shape)
    return pl.BlockSpec(shape, lambda *_: (0,) * nd, pipeline_mode=pl.Buffered(1))


def _rms(x):
    return x * lax.rsqrt(jnp.mean(x * x, axis=-1, keepdims=True) + EPS)


def _silu(x):
    return x * jax.nn.sigmoid(x)


def _ada_kernel(c_ref, w_ref, b_ref, o_ref):
    s = _silu(c_ref[...]).astype(bf16)
    o_ref[...] = jnp.dot(s, w_ref[...].astype(bf16), preferred_element_type=f32) + b_ref[...]


def _ada(c, w_ada, b_ada):
    nb, d = c.shape
    n = w_ada.shape[1]
    tn = 1536
    return pl.pallas_call(
        _ada_kernel,
        grid=(n // tn,),
        in_specs=[pl.BlockSpec((nb, d), lambda j: (0, 0)),
                  pl.BlockSpec((d, tn), lambda j: (0, j)),
                  pl.BlockSpec((1, tn), lambda j: (0, j))],
        out_specs=pl.BlockSpec((nb, tn), lambda j: (0, j)),
        out_shape=jax.ShapeDtypeStruct((nb, n), f32),
        compiler_params=_params(("arbitrary",), 2 * d * tn * 4 + d * tn * 2 + (8 << 20)),
        name="ada_modulation",
    )(c, w_ada, b_ada.reshape(1, n))


_C_RK = RET_HEADS * RET_QK
_C_RV = _C_RK + RET_HEADS * RET_QK
_C_RG = _C_RV + RET_WIDTH
_C_END = _C_RG + RET_WIDTH


def _rope_pair(t):
    return t + pltpu.roll(t, MLA_ROPE, axis=1)


def _rope_heads(z, cos, sin, scale):
    outs = []
    half = RET_QK // 2
    for h in range(RET_HEADS):
        x1 = z[:, h * RET_QK:h * RET_QK + half]
        x2 = z[:, h * RET_QK + half:(h + 1) * RET_QK]
        outs.append((x1 * cos - x2 * sin) * scale)
        outs.append((x1 * sin + x2 * cos) * scale)
    return jnp.concatenate(outs, axis=-1)


def _per_sequence(tile_rows, mod_ref):
    n_seq = mod_ref.shape[0]
    rows = tile_rows // n_seq
    return [(slice(i * rows, (i + 1) * rows), mod_ref[i]) for i in range(n_seq)]


def _tile_geometry(seq, tm):
    assert seq % tm == 0 or tm % seq == 0
    return max(seq // tm, 1), max(tm // seq, 1)


def _inproj_kernel(x_ref, mod_ref, g_ref, wlat_ref, wpe_ref, wret_ref, gq_ref, gkv_ref, t64_ref, t256_ref,
                   qlat_ref, kvlat_ref, kvlat16_ref, kpe_ref, kpe16_ref, rq_ref, rk_ref, rv_ref, rg_ref):
    hs = []
    for rows, m in _per_sequence(x_ref.shape[0], mod_ref):
        hs.append((_rms(x_ref[rows, :]) * (g_ref[...] * (1.0 + m[1:2])) + m[0:1]).astype(bf16))
    h = hs[0] if len(hs) == 1 else jnp.concatenate(hs, axis=0)

    def proj(w_ref, c0, c1):
        return lax.dot_general(h, w_ref[c0:c1, :], (((1,), (1,)), ((), ())), preferred_element_type=f32)

    qlat_ref[...] = (_rms(proj(wlat_ref, 0, Q_LORA)) * gq_ref[...]).astype(bf16)
    kv = _rms(proj(wlat_ref, Q_LORA, Q_LORA + KV_LORA)) * gkv_ref[...]
    kvlat_ref[...] = kv
    kvlat16_ref[...] = kv.astype(bf16)
    pe = _rope_pair(proj(wpe_ref, 0, 2 * MLA_ROPE) * t64_ref[...])
    kpe_ref[...] = pe[:, :MLA_ROPE]
    lane = lax.broadcasted_iota(jnp.int32, pe.shape, 1)
    kpe16_ref[...] = jnp.where(lane < MLA_ROPE, pe, 0.0).astype(bf16)
    cos = t256_ref[:, :RET_QK // 2]
    sin = t256_ref[:, RET_QK // 2:]
    rq_ref[...] = _rope_heads(proj(wret_ref, 0, _C_RK), cos, sin, 1.0).astype(bf16)
    rk_ref[...] = _rope_heads(proj(wret_ref, _C_RK, _C_RV), cos, sin, RET_QK ** -0.5).astype(bf16)
    rv_ref[...] = proj(wret_ref, _C_RV, _C_RG).astype(bf16)
    rg_ref[...] = proj(wret_ref, _C_RG, _C_END).astype(bf16)


def _inproj(x2d, mod, g_pre, w_parts, g_q, g_kv, t64, t256, seq, tm):
    n, d = x2d.shape
    per_seq, n_seq = _tile_geometry(seq, tm)
    t64, t256 = jnp.tile(t64, (n_seq, 1)), jnp.tile(t256, (n_seq, 1))
    row = lambda w: pl.BlockSpec((tm, w), lambda i: (i, 0))
    tab = lambda w: pl.BlockSpec((tm, w), lambda i: (i % per_seq, 0))
    outs = [(Q_LORA, bf16), (KV_LORA, f32), (KV_LORA, bf16), (MLA_ROPE, f32), (2 * MLA_ROPE, bf16),
            (RET_HEADS * RET_QK, bf16), (RET_HEADS * RET_QK, bf16), (RET_WIDTH, bf16), (RET_WIDTH, bf16)]
    vmem = sum(w.size for w in w_parts) * 2 + 2 * tm * d * 4 + 4 * tm * d * 4 + sum(2 * tm * w * 4 for w, _ in outs) + (6 << 20)
    return pl.pallas_call(
        _inproj_kernel,
        grid=(n // tm,),
        in_specs=[row(d),
                  pl.BlockSpec((n_seq, 6, d), lambda i: (i // per_seq, 0, 0)),
                  _const_spec((1, d)), *[_const_spec(w.shape) for w in w_parts],
                  _const_spec((1, Q_LORA)), _const_spec((1, KV_LORA)),
                  tab(2 * MLA_ROPE), tab(RET_QK)],
        out_specs=[row(w) for w, _ in outs],
        out_shape=[jax.ShapeDtypeStruct((n, w), dt) for w, dt in outs],
        compiler_params=_params(("arbitrary",), vmem),
        name="in_projection",
    )(x2d, mod, g_pre.reshape(1, d), *w_parts, g_q.reshape(1, Q_LORA), g_kv.reshape(1, KV_LORA), t64, t256)


def _mla_kernel(qlat_ref, lat_ref, kpe_ref, wq_ref, wkv_ref, tab_ref, o_ref, q_s, k_s, v_s, *, q_tiles):
    scale = (MLA_NOPE + MLA_ROPE) ** -0.5
    for hh in range(wq_ref.shape[0]):
        qh = jnp.dot(qlat_ref[0], wq_ref[hh], preferred_element_type=f32)
        q_s[:, :MLA_NOPE] = (qh[:, :MLA_NOPE] * scale).astype(bf16)
        q_s[:, MLA_NOPE:] = (_rope_pair(qh[:, MLA_NOPE:] * tab_ref[...]) * scale).astype(bf16)
        kv = jnp.dot(lat_ref[0], wkv_ref[hh], preferred_element_type=f32)
        k_s[:, :MLA_NOPE] = kv[:, :MLA_NOPE].astype(bf16)
        k_s[:, MLA_NOPE:] = kpe_ref[0]
        v_s[...] = kv[:, MLA_NOPE:].astype(bf16)
        for q0, ql, kvl, masked in q_tiles:
            q = q_s[q0:q0 + ql, :]
            s = lax.dot_general(q, k_s[0:kvl, :], (((1,), (1,)), ((), ())), preferred_element_type=f32)
            if masked:
                assert kvl == q0 + ql and q0 % CHUNK == 0
                qc = lax.broadcasted_iota(jnp.int32, (ql, 1), 0) // CHUNK
                kc = lax.broadcasted_iota(jnp.int32, (1, ql), 1) // CHUNK
                diag = jnp.where(kc <= qc, s[:, q0:], NEG_BIG)
                s = diag if q0 == 0 else jnp.concatenate([s[:, :q0], diag], axis=1)
            p = jnp.exp(s - jnp.max(s, axis=-1, keepdims=True))
            l = jnp.sum(p, axis=-1, keepdims=True)
            o = jnp.dot(p.astype(bf16), v_s[0:kvl, :], preferred_element_type=f32)
            o_ref[0, q0:q0 + ql, hh * MLA_V:(hh + 1) * MLA_V] = (o / l).astype(bf16)


def _mla(qlat, lat, kpe16, wq, wkv, tab, q_tiles, heads_per_step):
    b, sq, _ = qlat.shape
    skv = lat.shape[1]
    hs = heads_per_step
    kern = functools.partial(_mla_kernel, q_tiles=q_tiles)
    max_ql = max(t[1] for t in q_tiles)
    vmem = 2 * 2 * (sq * Q_LORA + skv * KV_LORA + skv * LANES) + sq * LANES * 4 + sq * 3 * LANES * 4 \
        + (sq + skv) * 3 * LANES * 2 + 4 * max_ql * skv * 4 + 8 * hs * Q_LORA * LANES * 2 + (8 << 20)
    return pl.pallas_call(
        kern,
        grid=(b, MLA_HEADS // hs),
        in_specs=[pl.BlockSpec((1, sq, Q_LORA), lambda i, h: (i, 0, 0)),
                  pl.BlockSpec((1, skv, KV_LORA), lambda i, h: (i, 0, 0)),
                  pl.BlockSpec((1, skv, 2 * MLA_ROPE), lambda i, h: (i, 0, 0)),
                  pl.BlockSpec((hs, Q_LORA, MLA_NOPE + 2 * MLA_ROPE), lambda i, h: (h, 0, 0)),
                  pl.BlockSpec((hs, KV_LORA, MLA_NOPE + MLA_V), lambda i, h: (h, 0, 0)),
                  pl.BlockSpec((sq, 2 * MLA_ROPE), lambda i, h: (0, 0))],
        out_specs=pl.BlockSpec((1, sq, hs * MLA_V), lambda i, h: (i, 0, h)),
        out_shape=jax.ShapeDtypeStruct((b, sq, MLA_WIDTH), bf16),
        scratch_shapes=[pltpu.VMEM((sq, MLA_NOPE + 2 * MLA_ROPE), bf16),
                        pltpu.VMEM((skv, MLA_NOPE + 2 * MLA_ROPE), bf16),
                        pltpu.VMEM((skv, MLA_V), bf16)],
        compiler_params=_params(("arbitrary", "arbitrary"), vmem),
        name="mla_attention",
    )(qlat, lat, kpe16, wq, wkv, tab)


def _ret_kernel(lg_ref, rq_ref, rk_ref, rv_ref, rg_ref, gret_ref, s0_ref, o_ref, sout_ref, state, *, chunk, n_chunks):
    c = chunk
    lg = lg_ref[0, 0:1, 0:1]
    ii = lax.broadcasted_iota(jnp.int32, (c, c), 0)
    jj = lax.broadcasted_iota(jnp.int32, (c, c), 1)
    diff = (ii - jj).astype(f32)
    dmask = jnp.where(diff >= 0.0, jnp.exp(jnp.maximum(diff, 0.0) * lg), 0.0)
    ic = lax.broadcasted_iota(jnp.int32, (c, 1), 0).astype(f32)
    q_decay = jnp.exp((ic + 1.0) * lg)
    k_decay = jnp.exp((c - 1.0 - ic) * lg)
    chunk_decay = jnp.exp(float(c) * lg)
    gret = gret_ref[...]
    state[...] = s0_ref[0, 0]

    def step(n, carry):
        sl = pl.ds(pl.multiple_of(n * c, c), c)
        q = rq_ref[0, sl, :]
        k = rk_ref[0, sl, :]
        v = rv_ref[0, sl, :]
        st = state[...]
        attn = lax.dot_general(q, k, (((1,), (1,)), ((), ())), preferred_element_type=f32) * dmask
        inner = jnp.dot(attn.astype(bf16), v, preferred_element_type=f32)
        cross = jnp.dot(q, st.astype(bf16), preferred_element_type=f32) * q_decay
        o = inner + cross
        kd_t = (k.astype(f32) * k_decay).T.astype(bf16)
        state[...] = st * chunk_decay + jnp.dot(kd_t, v, preferred_element_type=f32)
        mu = jnp.mean(o, axis=-1, keepdims=True)
        oc = o - mu
        rn = oc * lax.rsqrt(jnp.mean(oc * oc, axis=-1, keepdims=True) + EPS) * gret
        o_ref[0, sl, :] = (rn * _silu(rg_ref[0, sl, :].astype(f32))).astype(bf16)
        return carry

    lax.fori_loop(0, n_chunks, step, 0, unroll=min(n_chunks, RET_UNROLL))
    sout_ref[0, 0] = state[...]


def _retention(rq, rk, rv, rg, g_ret, state0, log_g, chunk):
    b, s, _ = rq.shape
    kern = functools.partial(_ret_kernel, chunk=chunk, n_chunks=s // chunk)
    head = lambda w: pl.BlockSpec((1, s, w), lambda i, h: (i, 0, h))
    st = pl.BlockSpec((1, 1, RET_QK, RET_V), lambda i, h: (i, h, 0, 0))
    vmem = 2 * 5 * s * RET_QK * 2 + 5 * RET_QK * RET_V * 4 + (8 << 20)
    return pl.pallas_call(
        kern,
        grid=(b, RET_HEADS),
        in_specs=[pl.BlockSpec((1, 8, LANES), lambda i, h: (h, 0, 0)),
                  head(RET_QK), head(RET_QK), head(RET_V), head(RET_V),
                  pl.BlockSpec((1, RET_V), lambda i, h: (0, h)), st],
        out_specs=[head(RET_V), st],
        out_shape=[jax.ShapeDtypeStruct((b, s, RET_WIDTH), bf16),
                   jax.ShapeDtypeStruct((b, RET_HEADS, RET_QK, RET_V), f32)],
        scratch_shapes=[pltpu.VMEM((RET_QK, RET_V), f32)],
        compiler_params=_params(("arbitrary", "arbitrary"), vmem),
        name="retention",
    )(log_g, rq, rk, rv, rg, g_ret.reshape(1, RET_WIDTH), state0)


def _pack_pairs(lo16, hi16):
    lo = lax.bitcast_convert_type(lo16.astype(f32), jnp.uint32)
    hi = lax.bitcast_convert_type(hi16.astype(f32), jnp.uint32)
    return (hi & jnp.uint32(0xFFFF0000)) | (lo >> 16)


def _unpack_pairs(u):
    lo = lax.bitcast_convert_type(u << 16, f32)
    hi = lax.bitcast_convert_type(u & jnp.uint32(0xFFFF0000), f32)
    return lo, hi


def _mixout_kernel(attn_ref, rn_ref, x_ref, mod_ref, gpost_ref, gpre_ref, wout_ref, wr_ref, br_ref, cnt0_ref, tri_ref,
                   xmid_ref, h2p_ref, idx_ref, gate_ref, rank_ref, cnt_ref, carry):
    @pl.when(pl.program_id(0) == 0)
    def _():
        carry[...] = cnt0_ref[...]

    mix = (jnp.dot(attn_ref[...], wout_ref[0:MLA_WIDTH, :], preferred_element_type=f32)
           + jnp.dot(rn_ref[...], wout_ref[MLA_WIDTH:, :], preferred_element_type=f32))
    h2s = []
    for rows, m in _per_sequence(mix.shape[0], mod_ref):
        x1 = x_ref[rows, :] + _rms(mix[rows]) * (m[2:3] * gpost_ref[...])
        xmid_ref[rows, :] = x1
        h2s.append(_rms(x1) * (gpre_ref[...] * (1.0 + m[4:5])) + m[3:4])
    h2 = h2s[0] if len(h2s) == 1 else jnp.concatenate(h2s, axis=0)
    h_hi = h2.astype(bf16)
    half = h2.shape[1] // 2
    h2p_ref[...] = _pack_pairs(h_hi[:, :half], h_hi[:, half:])
    h_lo = (h2 - h_hi.astype(f32)).astype(bf16)
    hi_terms = jnp.dot(h_hi, wr_ref[...], preferred_element_type=f32)
    logits = (hi_terms[:, :LANES] + jnp.dot(h_lo, wr_ref[:, :LANES], preferred_element_type=f32)
              + hi_terms[:, LANES:]) + br_ref[...]
    col = lax.broadcasted_iota(jnp.int32, logits.shape, 1).astype(f32)
    vals, idxs = [], []
    for _ in range(TOP_K):
        mx = jnp.max(logits, axis=-1, keepdims=True)
        ix = jnp.min(jnp.where(logits == mx, col, float(LANES)), axis=-1, keepdims=True)
        vals.append(mx)
        idxs.append(ix)
        logits = jnp.where(col == ix, -jnp.inf, logits)
    es = [jnp.exp(v - vals[0]) for v in vals]
    tot = es[0] + es[1] + es[2] + es[3]
    onehot = jnp.zeros_like(col)
    for k in range(TOP_K):
        onehot = jnp.where(col == idxs[k], 1.0, onehot)
    before = jnp.dot(tri_ref[...], onehot.astype(bf16), preferred_element_type=f32) + carry[...]
    idx_out = jnp.zeros_like(col)
    gate_out = jnp.zeros_like(col)
    rank_out = jnp.zeros_like(col)
    for k in range(TOP_K):
        slot = col == float(k)
        idx_out = jnp.where(slot, idxs[k], idx_out)
        gate_out = jnp.where(slot, es[k] / tot, gate_out)
        rank_k = jnp.sum(jnp.where(col == idxs[k], before, 0.0), axis=-1, keepdims=True)
        rank_out = jnp.where(slot, rank_k, rank_out)
    idx_ref[...] = idx_out.astype(jnp.int32)
    gate_ref[...] = gate_out
    rank_ref[...] = rank_out.astype(jnp.int32)
    carry[...] = carry[...] + jnp.sum(onehot, axis=0, keepdims=True)
    cnt_ref[...] = carry[...]


def _mixout(attn, rn, x2d, mod, g_post, g_pre, w_out16, wr, br, counts0, seq, tm):
    n, d = x2d.shape
    per_seq, n_seq = _tile_geometry(seq, tm)
    row = lambda w: pl.BlockSpec((tm, w), lambda i: (i, 0))
    tri = jnp.tril(jnp.ones((tm, tm), bf16), -1)
    vmem = w_out16.size * 2 + 2 * tm * d * (4 + 4 + 2 + 2) + 6 * tm * d * 4 + (8 << 20)
    return pl.pallas_call(
        _mixout_kernel,
        grid=(n // tm,),
        in_specs=[row(MLA_WIDTH), row(RET_WIDTH), row(d),
                  pl.BlockSpec((n_seq, 6, d), lambda i: (i // per_seq, 0, 0)),
                  _const_spec((1, d)), _const_spec((1, d)), _const_spec(w_out16.shape),
                  _const_spec(wr.shape), _const_spec((1, LANES)), _const_spec((1, LANES)), _const_spec((tm, tm))],
        out_specs=[row(d), row(d // 2), row(LANES), row(LANES), row(LANES),
                   pl.BlockSpec((1, LANES), lambda i: (0, 0))],
        out_shape=[jax.ShapeDtypeStruct((n, d), f32), jax.ShapeDtypeStruct((n, d // 2), jnp.uint32),
                   jax.ShapeDtypeStruct((n, LANES), jnp.int32), jax.ShapeDtypeStruct((n, LANES), f32),
                   jax.ShapeDtypeStruct((n, LANES), jnp.int32), jax.ShapeDtypeStruct((1, LANES), f32)],
        scratch_shapes=[pltpu.VMEM((1, LANES), f32)],
        compiler_params=_params(("arbitrary",), vmem),
        name="mixer_out_router",
    )(attn, rn, x2d, mod, g_post.reshape(1, d), g_pre.reshape(1, d), w_out16, wr, br, counts0, tri)


def _route(top_idx, rank, counts):
    counts = counts[0, :N_EXPERTS].astype(jnp.int32)
    padded = (counts + MOE_ROWS - 1) // MOE_ROWS * MOE_ROWS
    pad_end = jnp.cumsum(padded).astype(jnp.int32)
    pad_start = pad_end - padded
    dest = pad_start[top_idx] + rank
    return dest, pad_end, padded // MOE_ROWS


def _n_sorted_rows(n_tokens):
    return (n_tokens * TOP_K + N_EXPERTS * (MOE_ROWS - 1)) // MOE_ROWS * MOE_ROWS


def _block_schedule(n_chunks, n_blocks):
    ends = jnp.cumsum(n_chunks)
    n_live = ends[-1]
    blk = jnp.arange(n_blocks, dtype=jnp.int32)
    block_e = jnp.sum(jnp.minimum(blk, n_live - 1)[:, None] >= ends[None, :], axis=1).astype(jnp.int32)
    is_first = jnp.logical_and(blk == (ends - n_chunks)[block_e], blk < n_live).astype(jnp.int32)
    ids = jnp.arange(N_EXPERTS, dtype=jnp.int32)
    cand = jnp.where(n_chunks > 0, ids, N_EXPERTS)
    from_here = lax.cummin(cand[::-1])[::-1]
    after = jnp.concatenate([from_here[1:], jnp.full((1,), N_EXPERTS, jnp.int32)])
    wraps = (after >= N_EXPERTS).astype(jnp.int32)
    next_e = jnp.where(after < N_EXPERTS, after, from_here[0]).astype(jnp.int32)
    return block_e, is_first, next_e, wraps, n_live.reshape(1).astype(jnp.int32)


def _dispatch_kernel(pe_ref, nc_ref, dest_ref, h2p_a_ref, h2p_b_ref, x_hbm, zbuf, sem, zsem, *, tokens, n_first):
    i = pl.program_id(0)

    @pl.when(i == 0)
    def _():
        zbuf[...] = jnp.zeros_like(zbuf)

        def tail(e):
            start = pl.multiple_of(pe_ref[e] - MOE_ROWS, MOE_ROWS)
            return pltpu.make_async_copy(zbuf, x_hbm.at[pl.ds(start, MOE_ROWS)], zsem)

        def start_tail(e, c):
            @pl.when(nc_ref[e] > 0)
            def _():
                tail(e).start()
            return c

        def wait_tail(e, c):
            @pl.when(nc_ref[e] > 0)
            def _():
                tail(e).wait()
            return c

        lax.fori_loop(0, N_EXPERTS, start_tail, 0)
        lax.fori_loop(0, N_EXPERTS, wait_tail, 0)

        def unused(c):
            return pltpu.make_async_copy(zbuf, x_hbm.at[pl.ds(pl.multiple_of(c * MOE_ROWS, MOE_ROWS), MOE_ROWS)], zsem)

        def start_unused(c, carry):
            unused(c).start()
            return carry

        def wait_unused(c, carry):
            unused(c).wait()
            return carry

        first_unused = pe_ref[N_EXPERTS - 1] // MOE_ROWS
        lax.fori_loop(first_unused, x_hbm.shape[0] // MOE_ROWS, start_unused, 0)
        lax.fori_loop(first_unused, x_hbm.shape[0] // MOE_ROWS, wait_unused, 0)

    def scatter(h2p_ref):
        for t in range(tokens):
            for k in range(TOP_K):
                d = dest_ref[0, 0, t * TOP_K + k]
                pltpu.make_async_copy(h2p_ref.at[pl.ds(t, 1)], x_hbm.at[pl.ds(d, 1)], sem).start(priority=k % 2)
        for _ in range(TOP_K):
            pltpu.make_async_copy(h2p_ref, x_hbm.at[pl.ds(0, tokens)], sem).wait()

    @pl.when(i < n_first)
    def _():
        scatter(h2p_a_ref)

    @pl.when(i >= n_first)
    def _():
        scatter(h2p_b_ref)


def _dispatch(pad_end, n_chunks, dest, h2p_a, h2p_b):
    tokens = DISPATCH_TOKENS
    tile = h2p_a.shape[1:]
    n_a, n_b = h2p_a.shape[0] // tokens, h2p_b.shape[0] // tokens
    assert n_a * tokens == h2p_a.shape[0] and n_b * tokens == h2p_b.shape[0]
    n_steps = n_a + n_b
    grid_spec = pltpu.PrefetchScalarGridSpec(
        num_scalar_prefetch=2,
        grid=(n_steps,),
        in_specs=[pl.BlockSpec((1, 1, tokens * TOP_K), lambda i, pe, nc: (i, 0, 0), memory_space=pltpu.SMEM),
                  pl.BlockSpec((tokens,) + tile, lambda i, pe, nc: (jnp.minimum(i, n_a - 1), 0)),
                  pl.BlockSpec((tokens,) + tile, lambda i, pe, nc: (jnp.maximum(i - n_a, 0), 0))],
        out_specs=pl.BlockSpec(memory_space=pl.ANY),
        scratch_shapes=[pltpu.VMEM((MOE_ROWS,) + tile, jnp.uint32), pltpu.SemaphoreType.DMA(()),
                        pltpu.SemaphoreType.DMA(())],
    )
    return pl.pallas_call(
        functools.partial(_dispatch_kernel, tokens=tokens, n_first=n_a),
        grid_spec=grid_spec,
        out_shape=jax.ShapeDtypeStruct((_n_sorted_rows(n_steps * tokens),) + tile, jnp.uint32),
        compiler_params=_params(("arbitrary",), 24 << 20),
        name="moe_dispatch",
    )(pad_end, n_chunks, dest.reshape(n_steps, 1, tokens * TOP_K), h2p_a, h2p_b)


def _on_static(value, n, fn):
    for v in range(n):
        @pl.when(value == v)
        def _():
            fn(v)


def _moe_up_kernel(be_ref, first_ref, next_ref, wrap_ref, nl_ref, x_ref, w_hbm, bg_ref, bu_ref, act_ref,
                   stage, w16, sem, *, n_j):
    j = pl.program_id(0)
    b = pl.program_id(1)
    e = be_ref[b]
    live = b < nl_ref[0]
    tf = w16.shape[2]

    def weight_copies(ee, jj):
        return [pltpu.make_async_copy(w_hbm.at[ee, :, g * D_FF + jj * tf:g * D_FF + (jj + 1) * tf], stage.at[g], sem)
                for g in range(2)]

    def start_weights(ee, jd):
        _on_static(jd, n_j, lambda jj: [cp.start() for cp in weight_copies(ee, jj)])

    @pl.when(jnp.logical_and(j == 0, b == 0))
    def _():
        start_weights(e, j)

    @pl.when(jnp.logical_and(live, first_ref[b] == 1))
    def _():
        _on_static(j, n_j, lambda jj: [cp.wait() for cp in weight_copies(e, jj)])
        w16[...] = stage[...].astype(bf16)
        j_next = j + wrap_ref[e]

        @pl.when(j_next < n_j)
        def _():
            start_weights(next_ref[e], j_next)

    @pl.when(live)
    def _():
        lo, hi = _unpack_pairs(x_ref[...])
        x_lo = lo.astype(bf16)
        x_hi = hi.astype(bf16)
        half = x_lo.shape[1]

        def proj(g, b_ref):
            return (jnp.dot(x_lo, w16[g, 0:half, :], preferred_element_type=f32)
                    + jnp.dot(x_hi, w16[g, half:, :], preferred_element_type=f32) + b_ref[0])

        gate = jnp.minimum(proj(0, bg_ref), SWIGLU_LIMIT)
        up = jnp.clip(proj(1, bu_ref), -SWIGLU_LIMIT, SWIGLU_LIMIT)
        act_ref[0] = ((up + 1.0) * gate * jax.nn.sigmoid(SWIGLU_ALPHA * gate)).astype(bf16)

    @pl.when(jnp.logical_not(live))
    def _():
        act_ref[...] = jnp.zeros_like(act_ref)


def _moe_up(sched, x_sorted, w_gate_up, b_gate_up):
    e, d, f2 = w_gate_up.shape
    tf = UP_TILE
    n_j = D_FF // tf
    rows = x_sorted.shape[0]
    n_blocks = rows // MOE_ROWS
    vmem = 2 * d * tf * 4 + 2 * d * tf * 2 + 2 * MOE_ROWS * d * 2 + 2 * MOE_ROWS * tf * 2 \
        + 8 * MOE_ROWS * tf * 4 + (6 << 20)
    grid_spec = pltpu.PrefetchScalarGridSpec(
        num_scalar_prefetch=5,
        grid=(n_j, n_blocks),
        in_specs=[pl.BlockSpec((MOE_ROWS, d // 2), lambda j, b, be, *_: (b, 0)),
                  pl.BlockSpec(memory_space=pl.ANY),
                  pl.BlockSpec((1, 1, tf), lambda j, b, be, *_: (be[b], 0, j)),
                  pl.BlockSpec((1, 1, tf), lambda j, b, be, *_: (be[b], 0, n_j + j))],
        out_specs=pl.BlockSpec((1, MOE_ROWS, tf), lambda j, b, be, *_: (j, b, 0)),
        scratch_shapes=[pltpu.VMEM((2, d, tf), f32), pltpu.VMEM((2, d, tf), bf16), pltpu.SemaphoreType.DMA(())],
    )
    b3 = b_gate_up.reshape(e, 1, f2)
    return pl.pallas_call(
        functools.partial(_moe_up_kernel, n_j=n_j),
        grid_spec=grid_spec,
        out_shape=jax.ShapeDtypeStruct((n_j, rows, tf), bf16),
        compiler_params=_params(("arbitrary", "arbitrary"), vmem),
        name="moe_gate_up",
    )(*sched, x_sorted, w_gate_up, b3, b3)


def _moe_down_kernel(be_ref, first_ref, next_ref, wrap_ref, nl_ref, act_ref, w_hbm, bd_ref, y_ref, stage, w16, sem):
    b = pl.program_id(0)
    e = be_ref[b]
    live = b < nl_ref[0]
    n_k, _, tf = act_ref.shape

    def weight_copy(ee):
        return pltpu.make_async_copy(w_hbm.at[ee], stage, sem)

    @pl.when(b == 0)
    def _():
        weight_copy(e).start()

    @pl.when(jnp.logical_and(live, first_ref[b] == 1))
    def _():
        weight_copy(e).wait()
        w16[...] = stage[...].astype(bf16)

        @pl.when(wrap_ref[e] == 0)
        def _():
            weight_copy(next_ref[e]).start()

    @pl.when(live)
    def _():
        y = bd_ref[0]
        for k in range(n_k):
            y = y + jnp.dot(act_ref[k], w16[k * tf:(k + 1) * tf, :], preferred_element_type=f32)
        y16 = y.astype(bf16)
        half = y16.shape[1] // 2
        y_ref[...] = _pack_pairs(y16[:, :half], y16[:, half:])

    @pl.when(jnp.logical_not(live))
    def _():
        y_ref[...] = jnp.zeros_like(y_ref)


def _moe_down(sched, act, w_down, b_down):
    e, f, d = w_down.shape
    n_k, rows, tf = act.shape
    n_blocks = rows // MOE_ROWS
    vmem = f * d * 4 + f * d * 2 + 2 * MOE_ROWS * f * 2 + 2 * MOE_ROWS * d * 2 + 4 * MOE_ROWS * d * 4 + (6 << 20)
    grid_spec = pltpu.PrefetchScalarGridSpec(
        num_scalar_prefetch=5,
        grid=(n_blocks,),
        in_specs=[pl.BlockSpec((n_k, MOE_ROWS, tf), lambda b, be, *_: (0, b, 0)),
                  pl.BlockSpec(memory_space=pl.ANY),
                  pl.BlockSpec((1, 1, d), lambda b, be, *_: (be[b], 0, 0))],
        out_specs=pl.BlockSpec((MOE_ROWS, d // 2), lambda b, be, *_: (b, 0)),
        scratch_shapes=[pltpu.VMEM((f, d), f32), pltpu.VMEM((f, d), bf16), pltpu.SemaphoreType.DMA(())],
    )
    return pl.pallas_call(
        _moe_down_kernel,
        grid_spec=grid_spec,
        out_shape=jax.ShapeDtypeStruct((rows, d // 2), jnp.uint32),
        compiler_params=_params(("arbitrary",), vmem),
        name="moe_down",
    )(*sched, act, w_down, b_down.reshape(e, 1, d))


def _gather_rows(idx_ref, src_hbm, dst, sem, n_rows):
    for r in range(n_rows):
        t = idx_ref[0, 0, r]
        pltpu.make_async_copy(src_hbm.at[pl.ds(t, 1)], dst.at[pl.ds(r, 1)], sem).start(priority=r % 2)


def _wait_rows(src_hbm, dst, sem, n_rows):
    pltpu.make_async_copy(src_hbm.at[pl.ds(0, n_rows)], dst, sem).wait()


def _combine_kernel(pos_ref, posn_ref, y_hbm, gates_ref, xmid_ref, mod_ref, gpost_ref, out_ref, ybuf, sem, *, n_steps):
    i = pl.program_id(0)
    tm = COMBINE_TOKENS
    n_rows = TOP_K * tm
    slot = i % 2

    @pl.when(i == 0)
    def _():
        _gather_rows(pos_ref, y_hbm, ybuf.at[0], sem.at[0], n_rows)

    @pl.when(i + 1 < n_steps)
    def _():
        _gather_rows(posn_ref, y_hbm, ybuf.at[1 - slot], sem.at[1 - slot], n_rows)

    _wait_rows(y_hbm, ybuf.at[slot], sem.at[slot], n_rows)
    g = gates_ref[...]
    lo = hi = None
    for k in range(TOP_K):
        l_k, h_k = _unpack_pairs(ybuf[slot, k * tm:(k + 1) * tm, :])
        gk = g[:, k:k + 1]
        lo = gk * l_k if lo is None else lo + gk * l_k
        hi = gk * h_k if hi is None else hi + gk * h_k
    half = lo.shape[1]
    inv = lax.rsqrt((jnp.sum(lo * lo, axis=-1, keepdims=True) + jnp.sum(hi * hi, axis=-1, keepdims=True))
                    / (2 * half) + EPS)
    gp = gpost_ref[...]
    for rows, m in _per_sequence(tm, mod_ref):
        gate2 = m[5:6]
        out_ref[rows, :half] = xmid_ref[rows, :half] + gate2[:, :half] * (lo[rows] * inv[rows] * gp[:, :half])
        out_ref[rows, half:] = xmid_ref[rows, half:] + gate2[:, half:] * (hi[rows] * inv[rows] * gp[:, half:])


def _combine(pos, y_sorted, gates, xmid, mod, g_post, seq):
    n, d = xmid.shape
    tm = COMBINE_TOKENS
    n_tiles = n // tm
    per_seq, n_seq = _tile_geometry(seq, tm)
    vmem = 2 * TOP_K * tm * d * 2 + 4 * tm * d * 4 + 6 * tm * d * 4 + (8 << 20)
    return pl.pallas_call(
        functools.partial(_combine_kernel, n_steps=n_tiles),
        grid=(n_tiles,),
        in_specs=[pl.BlockSpec((1, 1, TOP_K * tm), lambda i: (i, 0, 0), memory_space=pltpu.SMEM),
                  pl.BlockSpec((1, 1, TOP_K * tm), lambda i: (jnp.minimum(i + 1, n_tiles - 1), 0, 0),
                               memory_space=pltpu.SMEM),
                  pl.BlockSpec(memory_space=pl.ANY),
                  pl.BlockSpec((tm, TOP_K), lambda i: (i, 0)),
                  pl.BlockSpec((tm, d), lambda i: (i, 0)),
                  pl.BlockSpec((n_seq, 6, d), lambda i: (i // per_seq, 0, 0)),
                  _const_spec((1, d))],
        out_specs=pl.BlockSpec((tm, d), lambda i: (i, 0)),
        out_shape=jax.ShapeDtypeStruct((n, d), f32),
        scratch_shapes=[pltpu.VMEM((2, TOP_K * tm) + y_sorted.shape[1:], jnp.uint32), pltpu.SemaphoreType.DMA((2,))],
        compiler_params=_params(("arbitrary",), vmem),
        name="moe_combine",
    )(pos, pos, y_sorted, gates, xmid, mod, g_post.reshape(1, d))


def _rope_tables(pos):
    def cs(d):
        half = d // 2
        inv_freq = 1.0 / (ROPE_THETA ** (jnp.arange(half, dtype=f32) * (2.0 / d)))
        ang = pos.astype(f32)[:, None] * inv_freq[None, :]
        return jnp.cos(ang), jnp.sin(ang)
    c64, s64 = cs(MLA_ROPE)
    c256, s256 = cs(RET_QK)
    t64 = jnp.concatenate([c64, c64, s64, s64], axis=-1)
    t256 = jnp.concatenate([c256, s256], axis=-1)
    return t64, t256


def _rot_cols(w):
    half = w.shape[-1] // 2
    return jnp.concatenate([-w[..., half:], w[..., :half]], axis=-1)


def _prep_weights(w_in, w_uq, w_ukv, w_out, w_router, b_router):
    b = np.cumsum((Q_LORA, KV_LORA, MLA_ROPE))
    w_t = w_in.T
    w_pe = w_in[:, b[1]:b[2]]
    w_parts = (w_t[:b[1]].astype(bf16), jnp.concatenate([w_pe, _rot_cols(w_pe)], axis=-1).T.astype(bf16),
               w_t[b[2]:].astype(bf16))
    wq = w_uq.reshape(Q_LORA, MLA_HEADS, MLA_NOPE + MLA_ROPE)
    wq = jnp.concatenate([wq, _rot_cols(wq[..., MLA_NOPE:])], axis=-1).transpose(1, 0, 2).astype(bf16)
    wkv = w_ukv.reshape(KV_LORA, MLA_HEADS, MLA_NOPE + MLA_V).transpose(1, 0, 2).astype(bf16)
    r_hi = w_router.astype(bf16)
    r_lo = (w_router - r_hi.astype(f32)).astype(bf16)
    pad = ((0, 0), (0, LANES - N_EXPERTS))
    wr = jnp.concatenate([jnp.pad(r_hi, pad), jnp.pad(r_lo, pad)], axis=1)
    br = jnp.pad(b_router, (0, LANES - N_EXPERTS), constant_values=NEG_BIG).reshape(1, LANES)
    return w_parts, wq, wkv, w_out.astype(bf16), wr, br


def _mixer_half(x, mod, pos, wts, p, past_lat16, past_kpe16, state0, counts0, ret_chunk, q_tiles, tm, heads_per_step):
    w_parts, wq, wkv, w_out16, wr, br = wts
    b, s, d = x.shape
    x2d = x.reshape(b * s, d)
    t64, t256 = _rope_tables(pos)
    qlat, kvlat, kvlat16, kpe, kpe16, rq, rk, rv, rg = _inproj(
        x2d, mod, p['g_pre_mix'], w_parts, p['g_q_lat'], p['g_kv_lat'], t64, t256, s, tm)
    lat16 = kvlat16.reshape(b, s, KV_LORA)
    kpe16 = kpe16.reshape(b, s, 2 * MLA_ROPE)
    if past_lat16 is not None:
        lat16 = jnp.concatenate([past_lat16, lat16], axis=1)
        kpe16 = jnp.concatenate([past_kpe16, kpe16], axis=1)
    attn = _mla(qlat.reshape(b, s, Q_LORA), lat16, kpe16, wq, wkv, t64, q_tiles, heads_per_step)
    log_g = jnp.log1p(-jnp.exp2(-5.0 - jnp.arange(RET_HEADS, dtype=f32)))
    log_g = jnp.broadcast_to(log_g[:, None, None], (RET_HEADS, 8, LANES))
    sh = lambda t, w: t.reshape(b, s, w)
    rn, state = _retention(sh(rq, RET_HEADS * RET_QK), sh(rk, RET_HEADS * RET_QK), sh(rv, RET_WIDTH),
                           sh(rg, RET_WIDTH), p['g_ret'], state0, log_g, ret_chunk)
    xmid, h2p, idx, gates, rank, counts = _mixout(
        attn.reshape(b * s, MLA_WIDTH), rn.reshape(b * s, RET_WIDTH), x2d, mod,
        p['g_post_mix'], p['g_pre_ffn'], w_out16, wr, br, counts0, s, tm)
    route = (idx[:, :TOP_K], gates[:, :TOP_K], rank[:, :TOP_K], counts)
    return kvlat.reshape(b, s, KV_LORA), kpe.reshape(b, s, MLA_ROPE), state, xmid, h2p, route


def _combine_order(dest):
    tm = COMBINE_TOKENS
    n_tiles = dest.shape[0] // tm
    return dest.reshape(n_tiles, tm, TOP_K).transpose(0, 2, 1).reshape(n_tiles, 1, TOP_K * tm)


def kernel(x_prompt, x_sample, cache_kv_latent, cache_k_rope, state_retention, c_prompt, c_sample, w_ada, b_ada, g_pre_mix, g_post_mix, g_pre_ffn, g_post_ffn, w_in, g_q_lat, g_kv_lat, w_uq, w_ukv, g_ret, w_out, w_router, b_router, w_gate_up, b_gate_up, w_down, b_down):
    depth = w_in.shape[0]
    assert depth == 1, "the staged problem has a single layer"
    bp, sp, d = x_prompt.shape
    bs, ss, _ = x_sample.shape
    past = cache_kv_latent.shape[2]
    l = 0
    p = dict(g_pre_mix=g_pre_mix[l], g_post_mix=g_post_mix[l], g_pre_ffn=g_pre_ffn[l], g_post_ffn=g_post_ffn[l],
             g_q_lat=g_q_lat[l], g_kv_lat=g_kv_lat[l], g_ret=g_ret[l])
    wts = _prep_weights(w_in[l], w_uq[l], w_ukv[l], w_out[l], w_router[l], b_router[l])

    mod = _ada(jnp.concatenate([c_prompt, c_sample], axis=0), w_ada[l], b_ada[l]).reshape(bp + bs, 6, d)
    mod_p, mod_s = mod[:bp], mod[bp:]

    tq = ATTN_Q_TILE
    tiles_p = tuple((q0, tq, q0 + tq, True) for q0 in range(0, sp, tq))
    zero_state = jnp.zeros((bp, RET_HEADS, RET_QK, RET_V), f32)
    lat_p, kpe_p, st_p, xmid_p, h2p_p, (idx_p, gates_p, rank_p, counts_p) = _mixer_half(
        x_prompt, mod_p, jnp.arange(sp), wts, p, None, None, zero_state, jnp.zeros((1, LANES), f32),
        RET_CHUNK_PROMPT, tiles_p, 512, 1)

    past_lat16 = cache_kv_latent[l].astype(bf16)
    past_kpe16 = jnp.pad(cache_k_rope[l], ((0, 0), (0, 0), (0, MLA_ROPE))).astype(bf16)
    tiles_s = ((0, ss, past + ss, False),)
    lat_s, kpe_s, st_s, xmid_s, h2p_s, (idx_s, gates_s, rank_s, counts) = _mixer_half(
        x_sample, mod_s, past + jnp.arange(ss), wts, p, past_lat16, past_kpe16, state_retention[l], counts_p,
        ss, tiles_s, bs * ss, MLA_HEADS)

    n_p = bp * sp
    dest, pad_end, n_chunks = _route(
        jnp.concatenate([idx_p, idx_s], axis=0), jnp.concatenate([rank_p, rank_s], axis=0), counts)
    x_sorted = _dispatch(pad_end, n_chunks, dest, h2p_p, h2p_s)
    sched = _block_schedule(n_chunks, x_sorted.shape[0] // MOE_ROWS)
    act = _moe_up(sched, x_sorted, w_gate_up[l], b_gate_up[l])
    y_sorted = _moe_down(sched, act, w_down[l], b_down[l])
    y_p = _combine(_combine_order(dest[:n_p]), y_sorted, gates_p, xmid_p, mod_p, p['g_post_ffn'], sp)
    y_s = _combine(_combine_order(dest[n_p:]), y_sorted, gates_s, xmid_s, mod_s, p['g_post_ffn'], ss)

    return (y_p.reshape(bp, sp, d), y_s.reshape(bs, ss, d),
            lat_p[None], kpe_p[None], st_p[None].astype(state_retention.dtype),
            lat_s[None], kpe_s[None], st_s[None].astype(state_retention.dtype))
```
